```python
import math
import jax
import jax.numpy as jnp
from jax import lax
import numpy as np

D_MODEL = 2048
BATCH = 4
SEQ = 2048
DEPTH = 1
DEC_BATCH = 128
DEC_SEQ = 4
PAST_LEN = 2048
PAGE_SIZE = 128

HEAD_DIM = 128
N_HEADS = D_MODEL // HEAD_DIM
H_A = N_HEADS // 2
H_M = N_HEADS - H_A
W_A = H_A * HEAD_DIM
W_M = H_M * HEAD_DIM
MIX_WIDTH = W_A + W_M
QK_M = 2 * W_M
N_IN = 3 * W_A + H_A + QK_M + 2 * W_M + 2 * H_M
D_FF = 128 * ((8 * D_MODEL // 3 + 127) // 128)
CONV_W = 4
MLSTM_CHUNK = 64
Q_BLOCK = 128
EPS = 1e-6
ATTN_SCALE = HEAD_DIM ** -0.5

kernel_name = 'fox_mlstm_parallel_heads_macaron_step'


def _rmsnorm(x, g):
    xf = x.astype(jnp.float32)
    y = xf * lax.rsqrt(jnp.mean(xf * xf, axis=-1, keepdims=True) + EPS)
    return (y * g.astype(jnp.float32)).astype(x.dtype)


def _swiglu(x, w_gate, w_up, w_down):
    return (jax.nn.silu(x @ w_gate) * (x @ w_up)) @ w_down


def _fox_attend(q, cq, qpos, k, v, ck, kpos):
    s = jnp.einsum('bqhd,bkhd->bhqk', q, k).astype(jnp.float32) * ATTN_SCALE
    bias = jnp.swapaxes(cq, 1, 2)[:, :, :, None] - jnp.swapaxes(ck, 1, 2)[:, :, None, :]
    causal = (qpos[:, None] >= kpos[None, :])[None, None]
    p = jax.nn.softmax(jnp.where(causal, s + bias, -jnp.inf), axis=-1)
    return jnp.einsum('bhqk,bkhd->bqhd', p.astype(v.dtype), v)


def _fox_prompt(q, k, v, logf):
    B, T, H, D = q.shape
    c = jnp.cumsum(logf.astype(jnp.float32), axis=1)
    pos = jnp.arange(T)
    nb = T // Q_BLOCK
    qb = jnp.swapaxes(q.reshape(B, nb, Q_BLOCK, H, D), 0, 1)
    cb = jnp.swapaxes(c.reshape(B, nb, Q_BLOCK, H), 0, 1)
    pb = pos.reshape(nb, Q_BLOCK)
    out = lax.map(lambda blk: _fox_attend(blk[0], blk[1], blk[2], k, v, c, pos), (qb, cb, pb))
    return jnp.swapaxes(out, 0, 1).reshape(B, T, H, D)


def _fox_sample(q, k, v, logf, k_past, v_past, lf_past):
    T = q.shape[1]
    P = k_past.shape[1]
    c_all = jnp.cumsum(jnp.concatenate([lf_past.astype(jnp.float32), logf.astype(jnp.float32)], axis=1), axis=1)
    c_past = jnp.swapaxes(c_all[:, :P], 1, 2)
    c_new = jnp.swapaxes(c_all[:, P:], 1, 2)
    s_past = jnp.einsum('bqhd,bkhd->bhqk', q, k_past.astype(q.dtype)).astype(jnp.float32) * ATTN_SCALE
    s_past = s_past + c_new[:, :, :, None] - c_past[:, :, None, :]
    s_new = jnp.einsum('bqhd,bkhd->bhqk', q, k).astype(jnp.float32) * ATTN_SCALE
    s_new = s_new + c_new[:, :, :, None] - c_new[:, :, None, :]
    tri = jnp.tril(jnp.ones((T, T), dtype=bool))
    s_new = jnp.where(tri[None, None], s_new, -jnp.inf)
    p = jax.nn.softmax(jnp.concatenate([s_past, s_new], axis=-1), axis=-1).astype(v.dtype)
    return (jnp.einsum('bhqk,bkhd->bqhd', p[..., :P], v_past.astype(v.dtype))
            + jnp.einsum('bhqk,bkhd->bqhd', p[..., P:], v))


def _causal_conv(u, prev, w, b):
    T = u.shape[1]
    ext = jnp.concatenate([prev.astype(u.dtype), u], axis=1)
    out = b
    for j in range(CONV_W):
        out = out + w[j] * ext[:, j:j + T]
    return out, ext[:, T:]


def _mlstm(q, k, v, i_pre, log_f, C0, n0, m0):
    B, T, H, D = q.shape
    L = math.gcd(T, MLSTM_CHUNK)
    nc = T // L

    def vec_chunks(a):
        return a.astype(jnp.float32).reshape(B, nc, L, H, D).transpose(1, 0, 3, 2, 4)

    def gate_chunks(a):
        return a.astype(jnp.float32).reshape(B, nc, L, H).transpose(1, 0, 3, 2)

    tril = jnp.tril(jnp.ones((L, L), dtype=bool))

    def step(carry, xs):
        C, n, m = carry
        qc, kc, vc, ic, fc = xs
        b = jnp.cumsum(fc, axis=-1)
        a = b + m[..., None]
        dmat = jnp.where(tril, b[..., :, None] - b[..., None, :] + ic[..., None, :], -jnp.inf)
        mt = jnp.maximum(a, jnp.max(dmat, axis=-1))
        w_intra = jnp.exp(dmat - mt[..., None]) * jnp.einsum('bhtd,bhsd->bhts', qc, kc)
        w_inter = jnp.exp(a - mt)
        num = (w_inter[..., None] * jnp.einsum('bhtd,bhde->bhte', qc, C)
               + jnp.einsum('bhts,bhse->bhte', w_intra, vc))
        den = w_inter * jnp.einsum('bhtd,bhd->bht', qc, n) + jnp.sum(w_intra, axis=-1)
        h = num / jnp.maximum(jnp.abs(den), jnp.exp(-mt))[..., None]
        m_new = mt[..., -1]
        w_state = jnp.exp(b[..., -1:] - b + ic - m_new[..., None])
        decay = jnp.exp(b[..., -1] + m - m_new)
        C_new = decay[..., None, None] * C + jnp.einsum('bhs,bhsd,bhse->bhde', w_state, kc, vc)
        n_new = decay[..., None] * n + jnp.einsum('bhs,bhsd->bhd', w_state, kc)
        return (C_new, n_new, m_new), h

    carry0 = (C0.astype(jnp.float32), n0.astype(jnp.float32), m0.astype(jnp.float32))
    xs = (vec_chunks(q), vec_chunks(k), vec_chunks(v), gate_chunks(i_pre), gate_chunks(log_f))
    (C, n, m), h = lax.scan(step, carry0, xs)
    h = h.transpose(1, 0, 3, 2, 4).reshape(B, T, H, D)
    return h, C, n, m


def _layer(x, conv_prev, C0, n0, m0, attend, lw):
    B, T, _ = x.shape
    x = x + 0.5 * _swiglu(_rmsnorm(x, lw['g_ffn1']), lw['w1_gate'], lw['w1_up'], lw['w1_down'])
    h = _rmsnorm(x, lw['g_mix'])
    proj = h @ lw['w_in']
    cuts = []
    acc = 0
    for wdt in (W_A, W_A, W_A, H_A, QK_M, W_M, W_M, H_M):
        acc += wdt
        cuts.append(acc)
    q_a, k_a, v_a, f_a, qk_m, v_m, o_m, i_m, f_m = jnp.split(proj, cuts, axis=-1)
    q_a = _rmsnorm(q_a.reshape(B, T, H_A, HEAD_DIM), lw['g_q'])
    k_a = _rmsnorm(k_a.reshape(B, T, H_A, HEAD_DIM), lw['g_k'])
    v_a = v_a.reshape(B, T, H_A, HEAD_DIM)
    logf_a = jax.nn.log_sigmoid((f_a + lw['b_fox_f']).astype(jnp.float32))
    y_a = _rmsnorm(attend(q_a, k_a, v_a, logf_a), lw['g_out_a'])
    qk_c, conv_new = _causal_conv(qk_m, conv_prev, lw['conv_w'], lw['conv_b'])
    qk_c = jax.nn.silu(qk_c)
    q_m = qk_c[..., :W_M].reshape(B, T, H_M, HEAD_DIM)
    k_m = qk_c[..., W_M:].reshape(B, T, H_M, HEAD_DIM) * (HEAD_DIM ** -0.5)
    v_m = v_m.reshape(B, T, H_M, HEAD_DIM)
    i_pre = (i_m + lw['b_m_i']).astype(jnp.float32)
    log_f_m = jax.nn.log_sigmoid((f_m + lw['b_m_f']).astype(jnp.float32))
    h_m, C, n, m = _mlstm(q_m, k_m, v_m, i_pre, log_f_m, C0, n0, m0)
    y_m = jax.nn.sigmoid(o_m).reshape(B, T, H_M, HEAD_DIM) * _rmsnorm(h_m, lw['g_out_m']).astype(x.dtype)
    y = jnp.concatenate([y_a.reshape(B, T, W_A), y_m.reshape(B, T, W_M)], axis=-1) @ lw['w_out']
    x = x + y
    x = x + 0.5 * _swiglu(_rmsnorm(x, lw['g_ffn2']), lw['w2_gate'], lw['w2_up'], lw['w2_down'])
    return x, k_a, v_a, logf_a, conv_new, C, n, m


def setup_inputs(seed: int = 0) -> dict:
    key = jax.random.key(seed)
    ks = jax.random.split(key, 32)
    n_pages = PAST_LEN // PAGE_SIZE
    n_used = DEC_BATCH * n_pages
    n_pool = n_used + n_used // 4

    def nrm(k, shape, scale):
        return scale * jax.random.normal(k, shape, jnp.float32)

    x_prompt = nrm(ks[0], (BATCH, SEQ, D_MODEL), 1.0)
    x_sample = nrm(ks[1], (DEC_BATCH, DEC_SEQ, D_MODEL), 1.0)
    cache_k = nrm(ks[2], (DEPTH, n_pool, PAGE_SIZE, H_A, HEAD_DIM), 1.0)
    cache_v = nrm(ks[3], (DEPTH, n_pool, PAGE_SIZE, H_A, HEAD_DIM), 1.0)
    cache_logf = jax.nn.log_sigmoid(4.0 + nrm(ks[4], (DEPTH, n_pool, PAGE_SIZE, H_A), 1.0))
    state_conv = nrm(ks[5], (DEPTH, DEC_BATCH, CONV_W - 1, QK_M), 1.0)
    state_C = nrm(ks[6], (DEPTH, DEC_BATCH, H_M, HEAD_DIM, HEAD_DIM), 0.1)
    state_n = nrm(ks[7], (DEPTH, DEC_BATCH, H_M, HEAD_DIM), 0.1)
    state_m = 1.0 + nrm(ks[8], (DEPTH, DEC_BATCH, H_M), 1.0)
    page_table = jax.random.permutation(ks[9], n_pool)[:n_used].reshape(DEC_BATCH, n_pages).astype(jnp.int32)
    return {
        'x_prompt': x_prompt,
        'x_sample': x_sample,
        'cache_k': cache_k,
        'cache_v': cache_v,
        'cache_logf': cache_logf,
        'state_conv': state_conv,
        'state_C': state_C,
        'state_n': state_n,
        'state_m': state_m,
        'page_table': page_table,
        'g_ffn1': 1.0 + nrm(ks[10], (DEPTH, D_MODEL), 0.02),
        'w1_gate': nrm(ks[11], (DEPTH, D_MODEL, D_FF), D_MODEL ** -0.5),
        'w1_up': nrm(ks[12], (DEPTH, D_MODEL, D_FF), D_MODEL ** -0.5),
        'w1_down': nrm(ks[13], (DEPTH, D_FF, D_MODEL), D_FF ** -0.5),
        'g_mix': 1.0 + nrm(ks[14], (DEPTH, D_MODEL), 0.02),
        'w_in': nrm(ks[15], (DEPTH, D_MODEL, N_IN), D_MODEL ** -0.5),
        'b_fox_f': 4.0 + nrm(ks[16], (DEPTH, H_A), 0.5),
        'b_m_i': nrm(ks[17], (DEPTH, H_M), 0.1),
        'b_m_f': 4.0 + nrm(ks[18], (DEPTH, H_M), 0.5),
        'conv_w': nrm(ks[19], (DEPTH, CONV_W, QK_M), CONV_W ** -0.5),
        'conv_b': nrm(ks[20], (DEPTH, QK_M), 0.02),
        'g_q': 1.0 + nrm(ks[21], (DEPTH, HEAD_DIM), 0.02),
        'g_k': 1.0 + nrm(ks[22], (DEPTH, HEAD_DIM), 0.02),
        'g_out_a': 1.0 + nrm(ks[23], (DEPTH, H_A, HEAD_DIM), 0.02),
        'g_out_m': 1.0 + nrm(ks[24], (DEPTH, H_M, HEAD_DIM), 0.02),
        'w_out': nrm(ks[25], (DEPTH, MIX_WIDTH, D_MODEL), MIX_WIDTH ** -0.5),
        'g_ffn2': 1.0 + nrm(ks[26], (DEPTH, D_MODEL), 0.02),
        'w2_gate': nrm(ks[27], (DEPTH, D_MODEL, D_FF), D_MODEL ** -0.5),
        'w2_up': nrm(ks[28], (DEPTH, D_MODEL, D_FF), D_MODEL ** -0.5),
        'w2_down': nrm(ks[29], (DEPTH, D_FF, D_MODEL), D_FF ** -0.5),
    }


def reference(x_prompt, x_sample, cache_k, cache_v, cache_logf, state_conv, state_C, state_n, state_m,
              page_table, g_ffn1, w1_gate, w1_up, w1_down, g_mix, w_in, b_fox_f, b_m_i, b_m_f,
              conv_w, conv_b, g_q, g_k, g_out_a, g_out_m, w_out, g_ffn2, w2_gate, w2_up, w2_down):
    n_seq, n_pages = page_table.shape
    past_len = n_pages * cache_k.shape[2]
    bp = x_prompt.shape[0]
    yp = x_prompt
    ys = x_sample
    kp_l, vp_l, lfp_l, convp_l, Cp_l, np_l, mp_l = [], [], [], [], [], [], []
    ks_l, vs_l, lfs_l, convs_l, Cs_l, ns_l, ms_l = [], [], [], [], [], [], []
    for l in range(DEPTH):
        lw = {
            'g_ffn1': g_ffn1[l], 'w1_gate': w1_gate[l], 'w1_up': w1_up[l], 'w1_down': w1_down[l],
            'g_mix': g_mix[l], 'w_in': w_in[l], 'b_fox_f': b_fox_f[l], 'b_m_i': b_m_i[l], 'b_m_f': b_m_f[l],
            'conv_w': conv_w[l], 'conv_b': conv_b[l], 'g_q': g_q[l], 'g_k': g_k[l],
            'g_out_a': g_out_a[l], 'g_out_m': g_out_m[l], 'w_out': w_out[l],
            'g_ffn2': g_ffn2[l], 'w2_gate': w2_gate[l], 'w2_up': w2_up[l], 'w2_down': w2_down[l],
        }
        conv0 = jnp.zeros((bp, CONV_W - 1, QK_M), x_prompt.dtype)
        C0 = jnp.zeros((bp, H_M, HEAD_DIM, HEAD_DIM), jnp.float32)
        n0 = jnp.zeros((bp, H_M, HEAD_DIM), jnp.float32)
        m0 = jnp.zeros((bp, H_M), jnp.float32)
        yp, kp, vp, lfp, convp, Cp, np_, mp = _layer(yp, conv0, C0, n0, m0, _fox_prompt, lw)
        k_past = cache_k[l][page_table].reshape(n_seq, past_len, H_A, HEAD_DIM)
        v_past = cache_v[l][page_table].reshape(n_seq, past_len, H_A, HEAD_DIM)
        lf_past = cache_logf[l][page_table].reshape(n_seq, past_len, H_A)
        attend_s = lambda q, k, v, lf, kp_=k_past, vp_=v_past, lp_=lf_past: _fox_sample(q, k, v, lf, kp_, vp_, lp_)
        ys, k_s, v_s, lfs, convs, Cs, ns, ms = _layer(ys, state_conv[l], state_C[l], state_n[l], state_m[l], attend_s, lw)
        kp_l.append(kp); vp_l.append(vp); lfp_l.append(lfp); convp_l.append(convp)
        Cp_l.append(Cp); np_l.append(np_); mp_l.append(mp)
        ks_l.append(k_s); vs_l.append(v_s); lfs_l.append(lfs); convs_l.append(convs)
        Cs_l.append(Cs); ns_l.append(ns); ms_l.append(ms)
    return (yp, ys,
            jnp.stack(kp_l), jnp.stack(vp_l), jnp.stack(lfp_l), jnp.stack(convp_l),
            jnp.stack(Cp_l), jnp.stack(np_l), jnp.stack(mp_l),
            jnp.stack(ks_l), jnp.stack(vs_l), jnp.stack(lfs_l), jnp.stack(convs_l),
            jnp.stack(Cs_l), jnp.stack(ns_l), jnp.stack(ms_l))
```

```python
import functools
import math

import jax
import jax.numpy as jnp
from jax import lax
from jax.experimental import pallas as pl
from jax.experimental.pallas import tpu as pltpu

F32 = jnp.float32
BF16 = jnp.bfloat16
HIGHEST = lax.Precision.HIGHEST

EPS = 1e-6
HEAD_DIM = 128
CONV_W = 4
MLSTM_CHUNK = 64
LANES = 128
SUBLANES = 8
VMEM_LIMIT_BYTES = 56 * 1024 * 1024
NEG_INF = float("-inf")


def _cparams(sem):
    return pltpu.CompilerParams(dimension_semantics=sem, vmem_limit_bytes=VMEM_LIMIT_BYTES)


def _rms(x, g):
    y = x * lax.rsqrt(jnp.mean(x * x, axis=-1, keepdims=True) + EPS)
    return y * g


def _log_sigmoid(x):
    return jnp.minimum(x, 0.0) - jnp.log1p(jnp.exp(-jnp.abs(x)))


def _dot(a, b, precision=None):
    return jnp.dot(a, b, preferred_element_type=F32, precision=precision)


def _dot_nt(a, b, precision=None):
    return lax.dot_general(a, b, (((1,), (1,)), ((), ())), preferred_element_type=F32, precision=precision)


def _pick_tile(n, pref):
    t = min(n, pref)
    while n % t:
        t //= 2
    return t


def _log2(n):
    k = int(math.log2(n))
    assert 1 << k == n
    return k


def _ffn_body(x_ref, g_ref, wg_ref, wu_ref, wd_ref, gn_ref, *rest, with_next):
    if with_next:
        o_ref, on_ref, xn_ref, acc_ref = rest
    else:
        o_ref, xn_ref, acc_ref = rest
    j = pl.program_id(1)

    @pl.when(j == 0)
    def _():
        xn_ref[...] = _rms(x_ref[...], g_ref[...]).astype(BF16)
        acc_ref[...] = jnp.zeros_like(acc_ref)

    xn = xn_ref[...]
    gate = _dot(xn, wg_ref[...])
    up = _dot(xn, wu_ref[...])
    h = (gate * jax.nn.sigmoid(gate)) * up
    acc_ref[...] += _dot(h.astype(BF16), wd_ref[...])

    @pl.when(j == pl.num_programs(1) - 1)
    def _():
        y = x_ref[...] + 0.5 * acc_ref[...]
        o_ref[...] = y
        if with_next:
            on_ref[...] = _rms(y, gn_ref[...]).astype(BF16)


def _ffn(x, g, wg, wu, wd, tf, g_next=None):
    n, d = x.shape
    fp = wg.shape[1]
    tm = _pick_tile(n, 512)
    with_next = g_next is not None
    row = pl.BlockSpec((tm, d), lambda i, j: (i, 0))
    vec = pl.BlockSpec((1, d), lambda i, j: (0, 0))
    out_shape = jax.ShapeDtypeStruct((n, d), F32)
    return pl.pallas_call(
        functools.partial(_ffn_body, with_next=with_next),
        out_shape=(out_shape, jax.ShapeDtypeStruct((n, d), BF16)) if with_next else out_shape,
        grid=(n // tm, fp // tf),
        in_specs=[
            row, vec,
            pl.BlockSpec((d, tf), lambda i, j: (0, j)),
            pl.BlockSpec((d, tf), lambda i, j: (0, j)),
            pl.BlockSpec((tf, d), lambda i, j: (j, 0)),
            vec,
        ],
        out_specs=(row, row) if with_next else row,
        scratch_shapes=[pltpu.VMEM((tm, d), BF16), pltpu.VMEM((tm, d), F32)],
        compiler_params=_cparams(("parallel", "arbitrary")),
        name="ffn",
    )(x, g, wg, wu, wd, g_next if with_next else g)


def _inproj_body(xn_ref, w_ref, wgt_ref, gq_ref, gk_ref,
                 q_ref, k_ref, kb_ref, v_ref, vb_ref, qkm_ref, vm_ref, om_ref, gt_ref, *, n_heads):
    j = pl.program_id(1)
    xn = xn_ref[...]
    acc = _dot(xn, w_ref[...])
    wa = n_heads * HEAD_DIM

    @pl.when(j == 0)
    def _():
        for h in range(n_heads):
            sl = slice(h * HEAD_DIM, (h + 1) * HEAD_DIM)
            q_ref[:, sl] = _rms(acc[:, sl], gq_ref[...]).astype(BF16)
        gt_ref[...] = _dot(xn, wgt_ref[...])

    @pl.when(j == 1)
    def _():
        for h in range(n_heads):
            sl = slice(h * HEAD_DIM, (h + 1) * HEAD_DIM)
            kn = _rms(acc[:, sl], gk_ref[...])
            k_ref[:, sl] = kn
            kb_ref[:, sl] = kn.astype(BF16)

    @pl.when(j == 2)
    def _():
        v_ref[...] = acc
        vb_ref[...] = acc.astype(BF16)

    @pl.when(j == 3)
    def _():
        qkm_ref[:, 0:wa] = acc

    @pl.when(j == 4)
    def _():
        qkm_ref[:, wa:2 * wa] = acc

    @pl.when(j == 5)
    def _():
        vm_ref[...] = acc

    @pl.when(j == 6)
    def _():
        om_ref[...] = acc


def _inproj(xn, w_main, w_gates, gq, gk, n_heads):
    n, d = xn.shape
    wa = n_heads * HEAD_DIM
    assert w_main.shape[1] == 7 * wa
    tm = _pick_tile(n, 512)
    row = lambda i, j: (i, 0)
    const = lambda i, j: (0, 0)
    out_shape = (
        jax.ShapeDtypeStruct((n, wa), BF16),
        jax.ShapeDtypeStruct((n, wa), F32),
        jax.ShapeDtypeStruct((n, wa), BF16),
        jax.ShapeDtypeStruct((n, wa), F32),
        jax.ShapeDtypeStruct((n, wa), BF16),
        jax.ShapeDtypeStruct((n, 2 * wa), F32),
        jax.ShapeDtypeStruct((n, wa), F32),
        jax.ShapeDtypeStruct((n, wa), F32),
        jax.ShapeDtypeStruct((n, LANES), F32),
    )
    out_specs = (
        pl.BlockSpec((tm, wa), row), pl.BlockSpec((tm, wa), row), pl.BlockSpec((tm, wa), row),
        pl.BlockSpec((tm, wa), row), pl.BlockSpec((tm, wa), row), pl.BlockSpec((tm, 2 * wa), row),
        pl.BlockSpec((tm, wa), row), pl.BlockSpec((tm, wa), row), pl.BlockSpec((tm, LANES), row),
    )
    return pl.pallas_call(
        functools.partial(_inproj_body, n_heads=n_heads),
        out_shape=out_shape,
        grid=(n // tm, 7),
        in_specs=[
            pl.BlockSpec((tm, d), row),
            pl.BlockSpec((d, wa), lambda i, j: (0, j)),
            pl.BlockSpec((d, LANES), const),
            pl.BlockSpec((1, HEAD_DIM), const),
            pl.BlockSpec((1, HEAD_DIM), const),
        ],
        out_specs=out_specs,
        compiler_params=_cparams(("parallel", "arbitrary")),
        name="inproj",
    )(xn, w_main, w_gates, gq, gk)


def _gates_body(g_ref, b_ref, act_ref, cg_ref, cl_ref, crow_ref, *, seg, lc, rows, n_heads):
    ch = min(rows, 256)
    sg = min(seg, ch)
    lane = lax.broadcasted_iota(jnp.int32, (1, LANES), 1)
    is_logsig = (lane < n_heads) | ((lane >= 2 * n_heads) & (lane < 3 * n_heads))
    r = lax.broadcasted_iota(jnp.int32, (ch, ch), 0)
    c = lax.broadcasted_iota(jnp.int32, (ch, ch), 1)
    sh_g = _log2(sg)
    sh_l = _log2(lc)
    tri_g = ((r >= c) & ((r >> sh_g) == (c >> sh_g))).astype(F32)
    tri_l = ((r >= c) & ((r >> sh_l) == (c >> sh_l))).astype(F32)
    sel = (lax.broadcasted_iota(jnp.int32, (SUBLANES, LANES), 0)
           == lax.broadcasted_iota(jnp.int32, (SUBLANES, LANES), 1)).astype(F32)
    carry = jnp.zeros((1, LANES), F32)
    for ci in range(rows // ch):
        sl = slice(ci * ch, (ci + 1) * ch)
        x = g_ref[sl, :] + b_ref[...]
        act = jnp.where(is_logsig, _log_sigmoid(x), x)
        act_ref[sl, :] = act
        cg = _dot(tri_g, act, precision=HIGHEST)
        if seg > ch:
            cg = cg + carry
            carry = cg[ch - 1:ch, :]
        cg_ref[sl, :] = cg
        cl_ref[sl, :] = _dot(tri_l, act, precision=HIGHEST)
        crow_ref[:, sl] = _dot_nt(sel, cg, precision=HIGHEST)


def _gates(gates, bias, seg, lc, n_heads):
    n = gates.shape[0]
    assert n_heads == SUBLANES
    rows = seg if seg >= 256 else _pick_tile(n, 256)
    nb = n // rows
    blk = pl.BlockSpec((rows, LANES), lambda b: (b, 0))
    return pl.pallas_call(
        functools.partial(_gates_body, seg=seg, lc=lc, rows=rows, n_heads=n_heads),
        out_shape=(jax.ShapeDtypeStruct((n, LANES), F32),) * 3 + (jax.ShapeDtypeStruct((nb, SUBLANES, rows), F32),),
        grid=(nb,),
        in_specs=[blk, pl.BlockSpec((1, LANES), lambda b: (0, 0))],
        out_specs=(blk, blk, blk, pl.BlockSpec((None, SUBLANES, rows), lambda b: (b, 0, 0))),
        compiler_params=_cparams(("parallel",)),
        name="gates",
    )(gates, bias)


def _fox_prompt_body(q_ref, k_ref, v_ref, ccol_ref, crow_ref, g_ref, o_ref,
                     cq_ref, m_ref, l_ref, acc_ref, *, tq, tk, scale):
    h = pl.program_id(1)
    qi = pl.program_id(2)
    ki = pl.program_id(3)

    @pl.when(ki == 0)
    def _():
        lane = lax.broadcasted_iota(jnp.int32, (1, LANES), 1)
        cq_ref[...] = jnp.sum(jnp.where(lane == h, ccol_ref[...], 0.0), axis=1, keepdims=True)
        m_ref[...] = jnp.full_like(m_ref, NEG_INF)
        l_ref[...] = jnp.zeros_like(l_ref)
        acc_ref[...] = jnp.zeros_like(acc_ref)

    @pl.when(ki * tk <= qi * tq + (tq - 1))
    def _():
        s = _dot_nt(q_ref[...], k_ref[...]) * scale
        ck = crow_ref[pl.ds(h, 1), :]
        s = s + (cq_ref[...] - ck)
        qpos = qi * tq + lax.broadcasted_iota(jnp.int32, (tq, tk), 0)
        kpos = ki * tk + lax.broadcasted_iota(jnp.int32, (tq, tk), 1)
        s = jnp.where(qpos >= kpos, s, NEG_INF)
        m_old = m_ref[...]
        m_new = jnp.maximum(m_old, jnp.max(s, axis=1, keepdims=True))
        alpha = jnp.exp(m_old - m_new)
        p = jnp.exp(s - m_new)
        l_ref[...] = alpha * l_ref[...] + jnp.sum(p, axis=1, keepdims=True)
        acc_ref[...] = alpha * acc_ref[...] + _dot(p.astype(BF16), v_ref[...])
        m_ref[...] = m_new

    @pl.when(ki == pl.num_programs(3) - 1)
    def _():
        o = acc_ref[...] / l_ref[...]
        o_ref[...] = _rms(o, g_ref[...]).astype(o_ref.dtype)


def _fox_prompt(q, kb, vb, cg, crow, g_out, batch, seq, n_heads):
    n, wa = q.shape
    tq = _pick_tile(seq, 512)
    tk = _pick_tile(seq, 512)
    nq, nk = seq // tq, seq // tk
    scale = HEAD_DIM ** -0.5

    def kv_blk(qi, ki):
        return jnp.minimum(ki, (qi * tq + tq - 1) // tk)

    return pl.pallas_call(
        functools.partial(_fox_prompt_body, tq=tq, tk=tk, scale=scale),
        out_shape=jax.ShapeDtypeStruct((n, wa), BF16),
        grid=(batch, n_heads, nq, nk),
        in_specs=[
            pl.BlockSpec((tq, HEAD_DIM), lambda b, h, qi, ki: (b * nq + qi, h)),
            pl.BlockSpec((tk, HEAD_DIM), lambda b, h, qi, ki: (b * nk + kv_blk(qi, ki), h)),
            pl.BlockSpec((tk, HEAD_DIM), lambda b, h, qi, ki: (b * nk + kv_blk(qi, ki), h)),
            pl.BlockSpec((tq, LANES), lambda b, h, qi, ki: (b * nq + qi, 0)),
            pl.BlockSpec((None, SUBLANES, tk), lambda b, h, qi, ki: (b, 0, kv_blk(qi, ki))),
            pl.BlockSpec((None, 1, HEAD_DIM), lambda b, h, qi, ki: (h, 0, 0)),
        ],
        out_specs=pl.BlockSpec((tq, HEAD_DIM), lambda b, h, qi, ki: (b * nq + qi, h)),
        scratch_shapes=[
            pltpu.VMEM((tq, 1), F32),
            pltpu.VMEM((tq, 1), F32),
            pltpu.VMEM((tq, 1), F32),
            pltpu.VMEM((tq, HEAD_DIM), F32),
        ],
        compiler_params=_cparams(("parallel", "parallel", "parallel", "arbitrary")),
        name="fox_prompt",
    )(q, kb, vb, cg, crow, g_out)


def _pool_cumsum_body(x_ref, o_ref, *, n_heads, width):
    x = x_ref[...]
    lane = lax.broadcasted_iota(jnp.int32, (1, width), 1)
    sh = n_heads
    while sh < width:
        x = x + jnp.where(lane >= sh, pltpu.roll(x, sh, axis=1), 0.0)
        sh *= 2
    o_ref[...] = x


def _pool_cumsum(lf, n_heads):
    n_pool, width = lf.shape
    rows = _pick_tile(n_pool, 256)
    blk = pl.BlockSpec((rows, width), lambda i: (i, 0))
    return pl.pallas_call(
        functools.partial(_pool_cumsum_body, n_heads=n_heads, width=width),
        out_shape=jax.ShapeDtypeStruct((n_pool, width), F32),
        grid=(n_pool // rows,),
        in_specs=[blk],
        out_specs=blk,
        compiler_params=_cparams(("parallel",)),
        name="pool_cumsum",
    )(lf)


def _fox_sample_body(pt_ref, q_ref, kn_ref, vn_ref, cl_ref, g_ref, *rest,
                     npg, n_pages, page, n_heads, t_new, scale):
    k_refs = rest[0:npg]
    c_refs = rest[npg:2 * npg]
    v_refs = rest[2 * npg:3 * npg]
    o_ref = rest[3 * npg]
    qx_ref, qxb_ref, pad_ref, cpad_ref, s_ref, carry_ref, acc_ref = rest[3 * npg + 1:]
    del pt_ref
    s_id = pl.program_id(1)
    n_ksteps = n_pages // npg
    past = n_pages * page
    wa = n_heads * HEAD_DIM
    row_h = lax.broadcasted_iota(jnp.int32, (n_heads, wa), 0)
    col_h = lax.broadcasted_iota(jnp.int32, (n_heads, wa), 1) >> _log2(HEAD_DIM)
    head_diag = row_h == col_h

    def expand(c):
        out = c
        for t in range(1, t_new):
            out = out + pltpu.roll(c, t * n_heads, axis=1)
        return out

    @pl.when(s_id == 0)
    def _():
        qx_ref[...] = jnp.zeros_like(qx_ref)
        q = q_ref[...]
        for t in range(t_new):
            qx_ref[t * n_heads:(t + 1) * n_heads, :] = jnp.where(head_diag, q[t:t + 1, :], 0.0)
        qxb_ref[...] = qx_ref[...].astype(BF16)
        cpad_ref[...] = jnp.zeros_like(cpad_ref)
        carry_ref[...] = jnp.zeros_like(carry_ref)
        acc_ref[...] = jnp.zeros_like(acc_ref)

    @pl.when(s_id < n_ksteps)
    def _():
        for r in range(npg):
            st = _dot_nt(k_refs[r][...].astype(BF16), qxb_ref[...]) * scale
            cpad_ref[:, 0:n_heads] = c_refs[r][...]
            cx = expand(cpad_ref[...]) + carry_ref[...]
            carry_ref[...] = cx[page - 1:page, :]
            off = pl.multiple_of((s_id * npg + r) * page, page)
            s_ref[pl.ds(off, page), :] = st - cx

    @pl.when(s_id == n_ksteps - 1)
    def _():
        pad_ref[...] = jnp.zeros_like(pad_ref)
        pad_ref[0:t_new, :] = kn_ref[...]
        st = _dot_nt(pad_ref[...].astype(BF16), qxb_ref[...]) * scale
        lane = lax.broadcasted_iota(jnp.int32, (1, LANES), 1)
        cpad_ref[...] = jnp.zeros_like(cpad_ref)
        cpad_ref[0:t_new, :] = jnp.where(lane < n_heads, cl_ref[...], 0.0)
        cl_x = expand(cpad_ref[...])
        u = lax.broadcasted_iota(jnp.int32, (page, LANES), 0)
        t = lax.broadcasted_iota(jnp.int32, (page, LANES), 1) >> _log2(n_heads)
        st = jnp.where((u < t_new) & (u <= t), st - cl_x, NEG_INF)
        s_past = s_ref[0:past, :] + carry_ref[...]
        m = jnp.maximum(jnp.max(s_past, axis=0, keepdims=True), jnp.max(st, axis=0, keepdims=True))
        p_past = jnp.exp(s_past - m)
        p_new = jnp.exp(st - m)
        inv = 1.0 / (jnp.sum(p_past, axis=0, keepdims=True) + jnp.sum(p_new, axis=0, keepdims=True))
        s_ref[0:past, :] = p_past * inv
        s_ref[past:past + page, :] = p_new * inv

    @pl.when(s_id >= n_ksteps)
    def _():
        for r in range(npg):
            off = pl.multiple_of(((s_id - n_ksteps) * npg + r) * page, page)
            p = s_ref[pl.ds(off, page), :].T.astype(BF16)
            acc_ref[...] += _dot(p, v_refs[r][...].astype(BF16))

    @pl.when(s_id == 2 * n_ksteps - 1)
    def _():
        pad_ref[...] = jnp.zeros_like(pad_ref)
        pad_ref[0:t_new, :] = vn_ref[...]
        p = s_ref[past:past + page, :].T.astype(BF16)
        acc = acc_ref[...] + _dot(p, pad_ref[...].astype(BF16))
        for t in range(t_new):
            blk = acc[t * n_heads:(t + 1) * n_heads, :]
            o_t = jnp.sum(jnp.where(head_diag, blk, 0.0), axis=0, keepdims=True)
            for h in range(n_heads):
                sl = slice(h * HEAD_DIM, (h + 1) * HEAD_DIM)
                o_ref[t:t + 1, sl] = _rms(o_t[:, sl], g_ref[h:h + 1, :])


def _fox_sample(page_table, q, k_new, v_new, cl, g_out, cache_k, cache_v, cache_c, n_heads):
    bsz, t_new, wa = q.shape
    n_pages = page_table.shape[1]
    page = cache_k.shape[1]
    assert page == LANES and n_heads == SUBLANES and n_heads * t_new <= LANES
    npg = _pick_tile(n_pages, 4)
    n_ksteps = n_pages // npg
    scale = HEAD_DIM ** -0.5

    def k_map(r):
        return lambda b, s, pt: (pt[b, jnp.minimum(s, n_ksteps - 1) * npg + r], 0, 0)

    def v_map(r):
        return lambda b, s, pt: (pt[b, jnp.maximum(s - n_ksteps, 0) * npg + r], 0, 0)

    seq3 = lambda b, s, pt: (b, 0, 0)
    in_specs = [
        pl.BlockSpec((None, t_new, wa), seq3),
        pl.BlockSpec((None, t_new, wa), seq3),
        pl.BlockSpec((None, t_new, wa), seq3),
        pl.BlockSpec((None, t_new, LANES), seq3),
        pl.BlockSpec((n_heads, HEAD_DIM), lambda b, s, pt: (0, 0)),
    ]
    in_specs += [pl.BlockSpec((None, page, wa), k_map(r)) for r in range(npg)]
    in_specs += [pl.BlockSpec((None, page, n_heads), k_map(r)) for r in range(npg)]
    in_specs += [pl.BlockSpec((None, page, wa), v_map(r)) for r in range(npg)]
    grid_spec = pltpu.PrefetchScalarGridSpec(
        num_scalar_prefetch=1,
        grid=(bsz, 2 * n_ksteps),
        in_specs=in_specs,
        out_specs=pl.BlockSpec((None, t_new, wa), seq3),
        scratch_shapes=[
            pltpu.VMEM((LANES, wa), F32),
            pltpu.VMEM((LANES, wa), BF16),
            pltpu.VMEM((page, wa), F32),
            pltpu.VMEM((page, LANES), F32),
            pltpu.VMEM((n_pages * page + page, LANES), F32),
            pltpu.VMEM((1, LANES), F32),
            pltpu.VMEM((LANES, wa), F32),
        ],
    )
    return pl.pallas_call(
        functools.partial(_fox_sample_body, npg=npg, n_pages=n_pages, page=page, n_heads=n_heads,
                          t_new=t_new, scale=scale),
        out_shape=jax.ShapeDtypeStruct((bsz, t_new, wa), F32),
        grid_spec=grid_spec,
        compiler_params=_cparams(("parallel", "arbitrary")),
        name="fox_sample",
    )(page_table, q, k_new, v_new, cl, g_out, *([cache_k] * npg), *([cache_c] * npg), *([cache_v] * npg))


def _col_to_row(col, n):
    eye = lax.broadcasted_iota(jnp.int32, (n, n), 0) == lax.broadcasted_iota(jnp.int32, (n, n), 1)
    return jnp.sum(jnp.where(eye, col, 0.0), axis=0, keepdims=True)


def _mlstm_chunk(qc, kc, vc, i_col, b_col, c_st, n_st, m_st, n_valid):
    ln = qc.shape[0]
    i_row = _col_to_row(i_col, ln)
    b_row = _col_to_row(b_col, ln)
    r = lax.broadcasted_iota(jnp.int32, (ln, ln), 0)
    c = lax.broadcasted_iota(jnp.int32, (ln, ln), 1)
    a_col = b_col + m_st
    dmat = jnp.where(r >= c, b_col - b_row + i_row, NEG_INF)
    mt = jnp.maximum(a_col, jnp.max(dmat, axis=1, keepdims=True))
    qb = qc.astype(BF16)
    kb = kc.astype(BF16)
    vb = vc.astype(BF16)
    w_intra = jnp.exp(dmat - mt) * _dot_nt(qb, kb)
    w_inter = jnp.exp(a_col - mt)
    num = w_inter * _dot(qb, c_st.astype(BF16)) + _dot(w_intra.astype(BF16), vb)
    den = w_inter * jnp.sum(qc * n_st, axis=1, keepdims=True) + jnp.sum(w_intra, axis=1, keepdims=True)
    h = num / jnp.maximum(jnp.abs(den), jnp.exp(-mt))
    last = n_valid - 1
    m_new = mt[last:last + 1, :]
    b_last = b_col[last:last + 1, :]
    w_state = jnp.exp(b_last - b_col + i_col - m_new)
    if n_valid < ln:
        row = lax.broadcasted_iota(jnp.int32, (ln, 1), 0)
        w_state = jnp.where(row < n_valid, w_state, 0.0)
    decay = jnp.exp(b_last + m_st - m_new)
    ks = w_state * kc
    c_new = decay * c_st + _dot(ks.T.astype(BF16), vb)
    n_new = decay * n_st + jnp.sum(ks, axis=0, keepdims=True)
    return h, c_new, n_new, m_new


def _pick_lane(x, idx):
    lane = lax.broadcasted_iota(jnp.int32, (1, LANES), 1)
    return jnp.sum(jnp.where(lane == idx, x, 0.0), axis=1, keepdims=True)


def _mlstm_prompt_body(qr_ref, kr_ref, wq_ref, wk_ref, bq_ref, bk_ref, v_ref, o_ref, act_ref, cl_ref, g_ref,
                       y_ref, c_out_ref, n_out_ref, m_out_ref,
                       ext_ref, qc_ref, kc_ref, c_ref, n_ref, m_ref, *, seq, chunk, n_heads):
    h = pl.program_id(1)
    pad = SUBLANES
    kscale = HEAD_DIM ** -0.5

    def conv(raw_ref, w_ref, b_ref, dst_ref, post_scale):
        ext_ref[0:pad, :] = jnp.zeros((pad, HEAD_DIM), F32)
        ext_ref[pad:pad + seq, :] = raw_ref[...]
        acc = b_ref[...] + w_ref[CONV_W - 1:CONV_W, :] * raw_ref[...]
        for j in range(CONV_W - 1):
            off = pad - (CONV_W - 1) + j
            acc = acc + w_ref[j:j + 1, :] * ext_ref[off:off + seq, :]
        acc = acc * jax.nn.sigmoid(acc)
        dst_ref[...] = acc * post_scale if post_scale is not None else acc

    conv(qr_ref, wq_ref, bq_ref, qc_ref, None)
    conv(kr_ref, wk_ref, bk_ref, kc_ref, kscale)
    c_ref[...] = jnp.zeros_like(c_ref)
    n_ref[...] = jnp.zeros_like(n_ref)
    m_ref[...] = jnp.zeros_like(m_ref)

    def step(ci, carry):
        rows = pl.ds(pl.multiple_of(ci * chunk, chunk), chunk)
        i_col = _pick_lane(act_ref[rows, :], n_heads + h)
        b_col = _pick_lane(cl_ref[rows, :], 2 * n_heads + h)
        hh, c_new, n_new, m_new = _mlstm_chunk(qc_ref[rows, :], kc_ref[rows, :], v_ref[rows, :], i_col, b_col,
                                               c_ref[...], n_ref[...], m_ref[...], chunk)
        c_ref[...] = c_new
        n_ref[...] = n_new
        m_ref[...] = m_new
        y = jax.nn.sigmoid(o_ref[rows, :]) * _rms(hh, g_ref[...])
        y_ref[rows, :] = y.astype(y_ref.dtype)
        return carry

    lax.fori_loop(0, seq // chunk, step, 0)
    c_out_ref[...] = c_ref[...]
    n_out_ref[...] = n_ref[...]
    m_out_ref[...] = m_ref[...]


def _mlstm_prompt(qkm, conv_w, conv_b, vm, om, act, cl, g_out, batch, seq, n_heads):
    n = qkm.shape[0]
    wm = n_heads * HEAD_DIM
    chunk = math.gcd(seq, MLSTM_CHUNK)
    tok = lambda off: (lambda b, h: (b, off + h))
    hd = pl.BlockSpec((seq, HEAD_DIM), tok(0))
    out_shape = (
        jax.ShapeDtypeStruct((n, wm), BF16),
        jax.ShapeDtypeStruct((batch, n_heads, HEAD_DIM, HEAD_DIM), F32),
        jax.ShapeDtypeStruct((batch, n_heads, 1, HEAD_DIM), F32),
        jax.ShapeDtypeStruct((batch, n_heads, 1, 1), F32),
    )
    st = lambda b, h: (b, h, 0, 0)
    return pl.pallas_call(
        functools.partial(_mlstm_prompt_body, seq=seq, chunk=chunk, n_heads=n_heads),
        out_shape=out_shape,
        grid=(batch, n_heads),
        in_specs=[
            hd,
            pl.BlockSpec((seq, HEAD_DIM), tok(n_heads)),
            pl.BlockSpec((CONV_W, HEAD_DIM), lambda b, h: (0, h)),
            pl.BlockSpec((CONV_W, HEAD_DIM), lambda b, h: (0, n_heads + h)),
            pl.BlockSpec((1, HEAD_DIM), lambda b, h: (0, h)),
            pl.BlockSpec((1, HEAD_DIM), lambda b, h: (0, n_heads + h)),
            hd,
            hd,
            pl.BlockSpec((seq, LANES), lambda b, h: (b, 0)),
            pl.BlockSpec((seq, LANES), lambda b, h: (b, 0)),
            pl.BlockSpec((None, 1, HEAD_DIM), lambda b, h: (h, 0, 0)),
        ],
        out_specs=(
            hd,
            pl.BlockSpec((None, None, HEAD_DIM, HEAD_DIM), st),
            pl.BlockSpec((None, None, 1, HEAD_DIM), st),
            pl.BlockSpec((None, None, 1, 1), st),
        ),
        scratch_shapes=[
            pltpu.VMEM((seq + SUBLANES, HEAD_DIM), F32),
            pltpu.VMEM((seq, HEAD_DIM), F32),
            pltpu.VMEM((seq, HEAD_DIM), F32),
            pltpu.VMEM((HEAD_DIM, HEAD_DIM), F32),
            pltpu.VMEM((1, HEAD_DIM), F32),
            pltpu.VMEM((1, 1), F32),
        ],
        compiler_params=_cparams(("parallel", "parallel")),
        name="mlstm_prompt",
    )(qkm, qkm, conv_w, conv_w, conv_b, conv_b, vm, om, act, cl, g_out)


def _mlstm_sample_body(qk_ref, prev_ref, w_ref, b_ref, v_ref, o_ref, act_ref, cl_ref, g_ref, c0_ref, n0_ref, m0_ref,
                       y_ref, c_out_ref, n_out_ref, m_out_ref, ext_ref, pad_ref, *, t_new, n_heads):
    wm = n_heads * HEAD_DIM
    rows = SUBLANES
    kscale = HEAD_DIM ** -0.5
    ext_ref[...] = jnp.zeros_like(ext_ref)
    ext_ref[0:CONV_W - 1, :] = prev_ref[...]
    ext_ref[CONV_W - 1:CONV_W - 1 + t_new, :] = qk_ref[...]
    acc = b_ref[...] + w_ref[0:1, :] * ext_ref[0:rows, :]
    for j in range(1, CONV_W):
        acc = acc + w_ref[j:j + 1, :] * ext_ref[j:j + rows, :]
    qk = acc * jax.nn.sigmoid(acc)
    valid = lax.broadcasted_iota(jnp.int32, (rows, 1), 0) < t_new

    def padded(ref, width):
        pad_ref[:, 0:width] = jnp.zeros((rows, width), F32)
        pad_ref[0:t_new, 0:width] = ref[...]
        return pad_ref[:, 0:width]

    v = padded(v_ref, wm)
    o = padded(o_ref, wm)
    act = padded(act_ref, LANES)
    cl = padded(cl_ref, LANES)
    for h in range(n_heads):
        sl = slice(h * HEAD_DIM, (h + 1) * HEAD_DIM)
        qc = jnp.where(valid, qk[:, sl], 0.0)
        kc = jnp.where(valid, qk[:, wm + h * HEAD_DIM: wm + (h + 1) * HEAD_DIM] * kscale, 0.0)
        i_col = jnp.where(valid, act[:, n_heads + h:n_heads + h + 1], NEG_INF)
        b_col = cl[:, 2 * n_heads + h:2 * n_heads + h + 1]
        b_col = jnp.where(valid, b_col, b_col[t_new - 1:t_new, :])
        hh, c_new, n_new, m_new = _mlstm_chunk(qc, kc, v[:, sl], i_col, b_col,
                                               c0_ref[h], n0_ref[h:h + 1, :], m0_ref[:, h:h + 1], t_new)
        y = jax.nn.sigmoid(o[:, sl]) * _rms(hh, g_ref[h:h + 1, :])
        y_ref[:, sl] = y[0:t_new, :]
        c_out_ref[h] = c_new
        n_out_ref[h:h + 1, :] = n_new
        m_out_ref[:, h:h + 1] = m_new


def _mlstm_sample(qkm, conv_prev, conv_w, conv_b, vm, om, act, cl, g_out, c0, n0, m0, n_heads):
    bsz, t_new, wm2 = qkm.shape
    wm = wm2 // 2
    assert t_new + CONV_W - 1 <= SUBLANES
    seq3 = lambda b: (b, 0, 0)
    const = lambda b: (0, 0)
    out_shape = (
        jax.ShapeDtypeStruct((bsz, t_new, wm), F32),
        jax.ShapeDtypeStruct((bsz, n_heads, HEAD_DIM, HEAD_DIM), F32),
        jax.ShapeDtypeStruct((bsz, n_heads, HEAD_DIM), F32),
        jax.ShapeDtypeStruct((bsz, 1, n_heads), F32),
    )
    return pl.pallas_call(
        functools.partial(_mlstm_sample_body, t_new=t_new, n_heads=n_heads),
        out_shape=out_shape,
        grid=(bsz,),
        in_specs=[
            pl.BlockSpec((None, t_new, wm2), seq3),
            pl.BlockSpec((None, CONV_W - 1, wm2), seq3),
            pl.BlockSpec((CONV_W, wm2), const),
            pl.BlockSpec((1, wm2), const),
            pl.BlockSpec((None, t_new, wm), seq3),
            pl.BlockSpec((None, t_new, wm), seq3),
            pl.BlockSpec((None, t_new, LANES), seq3),
            pl.BlockSpec((None, t_new, LANES), seq3),
            pl.BlockSpec((n_heads, HEAD_DIM), const),
            pl.BlockSpec((None, n_heads, HEAD_DIM, HEAD_DIM), lambda b: (b, 0, 0, 0)),
            pl.BlockSpec((None, n_heads, HEAD_DIM), seq3),
            pl.BlockSpec((None, 1, n_heads), seq3),
        ],
        out_specs=(
            pl.BlockSpec((None, t_new, wm), seq3),
            pl.BlockSpec((None, n_heads, HEAD_DIM, HEAD_DIM), lambda b: (b, 0, 0, 0)),
            pl.BlockSpec((None, n_heads, HEAD_DIM), seq3),
            pl.BlockSpec((None, 1, n_heads), seq3),
        ),
        scratch_shapes=[pltpu.VMEM((2 * SUBLANES, wm2), F32), pltpu.VMEM((SUBLANES, wm), F32)],
        compiler_params=_cparams(("parallel",)),
        name="mlstm_sample",
    )(qkm, conv_prev, conv_w, conv_b, vm, om, act, cl, g_out, c0, n0, m0)


def _outproj_body(ya_ref, ym_ref, w_ref, x_ref, o_ref, *, wa):
    y = _dot(ya_ref[...].astype(BF16), w_ref[0:wa, :]) + _dot(ym_ref[...].astype(BF16), w_ref[wa:, :])
    o_ref[...] = x_ref[...] + y


def _outproj(ya, ym, w, x):
    n, d = x.shape
    wa = ya.shape[1]
    wm = ym.shape[1]
    tm = _pick_tile(n, 512)
    row = lambda i: (i, 0)
    return pl.pallas_call(
        functools.partial(_outproj_body, wa=wa),
        out_shape=jax.ShapeDtypeStruct((n, d), F32),
        grid=(n // tm,),
        in_specs=[
            pl.BlockSpec((tm, wa), row),
            pl.BlockSpec((tm, wm), row),
            pl.BlockSpec((wa + wm, d), lambda i: (0, 0)),
            pl.BlockSpec((tm, d), row),
        ],
        out_specs=pl.BlockSpec((tm, d), row),
        compiler_params=_cparams(("parallel",)),
        name="outproj",
    )(ya, ym, w, x)


def _prep_weights(lw, n_heads_a, n_heads_m):
    wa = n_heads_a * HEAD_DIM
    wm = n_heads_m * HEAD_DIM
    f = lw['w1_gate'].shape[1]
    tf = 512
    fp = tf * ((f + tf - 1) // tf)

    def ffn_w(wg, wu, wd):
        padc = ((0, 0), (0, fp - f))
        return (jnp.pad(wg.astype(BF16), padc), jnp.pad(wu.astype(BF16), padc),
                jnp.pad(wd.astype(BF16), ((0, fp - f), (0, 0))))

    w_in = lw['w_in']
    o_f = 3 * wa
    o_qk = o_f + n_heads_a
    o_i = o_qk + 2 * wm + 2 * wm
    o_fm = o_i + n_heads_m
    w_main = jnp.concatenate([w_in[:, :o_f], w_in[:, o_qk:o_i]], axis=1).astype(BF16)
    ng = 2 * n_heads_m + n_heads_a
    w_gates = jnp.concatenate(
        [w_in[:, o_f:o_qk], w_in[:, o_i:o_fm], w_in[:, o_fm:o_fm + n_heads_m],
         jnp.zeros((w_in.shape[0], LANES - ng), w_in.dtype)], axis=1).astype(BF16)
    bias = jnp.concatenate([lw['b_fox_f'], lw['b_m_i'], lw['b_m_f'], jnp.zeros((LANES - ng,), F32)])[None, :]
    return {
        'ffn1': ffn_w(lw['w1_gate'], lw['w1_up'], lw['w1_down']),
        'ffn2': ffn_w(lw['w2_gate'], lw['w2_up'], lw['w2_down']),
        'tf': tf,
        'w_main': w_main,
        'w_gates': w_gates,
        'gate_bias': bias,
        'w_out': lw['w_out'].astype(BF16),
    }


def kernel(x_prompt, x_sample, cache_k, cache_v, cache_logf, state_conv, state_C, state_n, state_m, page_table,
           g_ffn1, w1_gate, w1_up, w1_down, g_mix, w_in, b_fox_f, b_m_i, b_m_f, conv_w, conv_b, g_q, g_k,
           g_out_a, g_out_m, w_out, g_ffn2, w2_gate, w2_up, w2_down):
    depth = w_in.shape[0]
    bp, seq, d = x_prompt.shape
    bs, t_new, _ = x_sample.shape
    n_heads_a = g_out_a.shape[1]
    n_heads_m = g_out_m.shape[1]
    wa = n_heads_a * HEAD_DIM
    wm = n_heads_m * HEAD_DIM
    n_pool, page = cache_k.shape[1], cache_k.shape[2]

    yp = x_prompt.reshape(bp * seq, d)
    ys = x_sample.reshape(bs * t_new, d)
    outs = [[] for _ in range(14)]
    for l in range(depth):
        lw = {
            'w1_gate': w1_gate[l], 'w1_up': w1_up[l], 'w1_down': w1_down[l], 'w_in': w_in[l],
            'b_fox_f': b_fox_f[l], 'b_m_i': b_m_i[l], 'b_m_f': b_m_f[l], 'w_out': w_out[l],
            'w2_gate': w2_gate[l], 'w2_up': w2_up[l], 'w2_down': w2_down[l],
        }
        pw = _prep_weights(lw, n_heads_a, n_heads_m)
        gf1 = g_ffn1[l][None, :]
        gf2 = g_ffn2[l][None, :]
        gmix = g_mix[l][None, :]
        gq = g_q[l][None, :]
        gk = g_k[l][None, :]
        cw = conv_w[l]
        cb = conv_b[l][None, :]
        goa = g_out_a[l]
        gom = g_out_m[l]

        x1, xn = _ffn(yp, gf1, *pw['ffn1'], pw['tf'], g_next=gmix)
        q, k, kb, v, vb, qkm, vm, om, gt = _inproj(xn, pw['w_main'], pw['w_gates'], gq, gk, n_heads_a)
        act, cg, cl, crow = _gates(gt, pw['gate_bias'], seq, math.gcd(seq, MLSTM_CHUNK), n_heads_a)
        ya = _fox_prompt(q, kb, vb, cg, crow, goa[:, None, :], bp, seq, n_heads_a)
        ym, c_p, n_p, m_p = _mlstm_prompt(qkm, cw, cb, vm, om, act, cl, gom[:, None, :], bp, seq, n_heads_m)
        x2 = _outproj(ya, ym, pw['w_out'], x1)
        yp = _ffn(x2, gf2, *pw['ffn2'], pw['tf'])
        outs[0].append(k.reshape(bp, seq, n_heads_a, HEAD_DIM))
        outs[1].append(v.reshape(bp, seq, n_heads_a, HEAD_DIM))
        outs[2].append(act[:, :n_heads_a].reshape(bp, seq, n_heads_a))
        outs[3].append(qkm.reshape(bp, seq, 2 * wm)[:, seq - (CONV_W - 1):, :])
        outs[4].append(c_p)
        outs[5].append(n_p.reshape(bp, n_heads_m, HEAD_DIM))
        outs[6].append(m_p.reshape(bp, n_heads_m))

        x1, xn = _ffn(ys, gf1, *pw['ffn1'], pw['tf'], g_next=gmix)
        q, k, kb, v, vb, qkm, vm, om, gt = _inproj(xn, pw['w_main'], pw['w_gates'], gq, gk, n_heads_a)
        act, cg, cl, crow = _gates(gt, pw['gate_bias'], t_new, math.gcd(t_new, MLSTM_CHUNK), n_heads_a)
        r3 = lambda a: a.reshape(bs, t_new, a.shape[-1])
        cache_c = _pool_cumsum(cache_logf[l].reshape(n_pool, page * n_heads_a), n_heads_a)
        ya = _fox_sample(page_table, r3(q).astype(F32), r3(k), r3(v), r3(cg), goa,
                         cache_k[l].reshape(n_pool, page, wa), cache_v[l].reshape(n_pool, page, wa),
                         cache_c.reshape(n_pool, page, n_heads_a), n_heads_a)
        ym, c_s, n_s, m_s = _mlstm_sample(r3(qkm), state_conv[l], cw, cb, r3(vm), r3(om), r3(act), r3(cl), gom,
                                          state_C[l], state_n[l], state_m[l][:, None, :], n_heads_m)
        x2 = _outproj(ya.reshape(bs * t_new, wa), ym.reshape(bs * t_new, wm), pw['w_out'], x1)
        ys = _ffn(x2, gf2, *pw['ffn2'], pw['tf'])
        conv_ext = jnp.concatenate([state_conv[l], r3(qkm)], axis=1)
        outs[7].append(k.reshape(bs, t_new, n_heads_a, HEAD_DIM))
        outs[8].append(v.reshape(bs, t_new, n_heads_a, HEAD_DIM))
        outs[9].append(act[:, :n_heads_a].reshape(bs, t_new, n_heads_a))
        outs[10].append(conv_ext[:, t_new:, :])
        outs[11].append(c_s)
        outs[12].append(n_s)
        outs[13].append(m_s.reshape(bs, n_heads_m))

    return (yp.reshape(bp, seq, d), ys.reshape(bs, t_new, d)) + tuple(jnp.stack(o) for o in outs)
```

```python
import functools
import math

import jax
import jax.numpy as jnp
from jax import lax
from jax.experimental import pallas as pl
from jax.experimental.pallas import tpu as pltpu

F32 = jnp.float32
BF16 = jnp.bfloat16
HIGHEST = lax.Precision.HIGHEST

EPS = 1e-6
HEAD_DIM = 128
CONV_W = 4
MLSTM_CHUNK = 64
LANES = 128
SUBLANES = 8
VMEM_LIMIT_BYTES = 56 * 1024 * 1024
NEG_INF = float("-inf")


def _cparams(sem):
    return pltpu.CompilerParams(dimension_semantics=sem, vmem_limit_bytes=VMEM_LIMIT_BYTES)


def _rms(x, g):
    y = x * lax.rsqrt(jnp.mean(x * x, axis=-1, keepdims=True) + EPS)
    return y * g


def _log_sigmoid(x):
    return jnp.minimum(x, 0.0) - jnp.log1p(jnp.exp(-jnp.abs(x)))


def _dot(a, b, precision=None):
    return jnp.dot(a, b, preferred_element_type=F32, precision=precision)


def _dot_nt(a, b, precision=None):
    return lax.dot_general(a, b, (((1,), (1,)), ((), ())), preferred_element_type=F32, precision=precision)


def _pick_tile(n, pref):
    t = min(n, pref)
    while n % t:
        t //= 2
    return t


def _log2(n):
    k = int(math.log2(n))
    assert 1 << k == n
    return k


def _ffn_body(x_ref, g_ref, wg_ref, wu_ref, wd_ref, gn_ref, *rest, with_next):
    if with_next:
        o_ref, on_ref, xn_ref, acc_ref = rest
    else:
        o_ref, xn_ref, acc_ref = rest
    j = pl.program_id(1)

    @pl.when(j == 0)
    def _():
        xn_ref[...] = _rms(x_ref[...], g_ref[...]).astype(BF16)
        acc_ref[...] = jnp.zeros_like(acc_ref)

    xn = xn_ref[...]
    gate = _dot(xn, wg_ref[...])
    up = _dot(xn, wu_ref[...])
    h = (gate * jax.nn.sigmoid(gate)) * up
    acc_ref[...] += _dot(h.astype(BF16), wd_ref[...])

    @pl.when(j == pl.num_programs(1) - 1)
    def _():
        y = x_ref[...] + 0.5 * acc_ref[...]
        o_ref[...] = y
        if with_next:
            on_ref[...] = _rms(y, gn_ref[...]).astype(BF16)


def _ffn(x, g, wg, wu, wd, tf, g_next=None):
    n, d = x.shape
    fp = wg.shape[1]
    tm = _pick_tile(n, 512)
    with_next = g_next is not None
    row = pl.BlockSpec((tm, d), lambda i, j: (i, 0))
    vec = pl.BlockSpec((1, d), lambda i, j: (0, 0))
    out_shape = jax.ShapeDtypeStruct((n, d), F32)
    return pl.pallas_call(
        functools.partial(_ffn_body, with_next=with_next),
        out_shape=(out_shape, jax.ShapeDtypeStruct((n, d), BF16)) if with_next else out_shape,
        grid=(n // tm, fp // tf),
        in_specs=[
            row, vec,
            pl.BlockSpec((d, tf), lambda i, j: (0, j)),
            pl.BlockSpec((d, tf), lambda i, j: (0, j)),
            pl.BlockSpec((tf, d), lambda i, j: (j, 0)),
            vec,
        ],
        out_specs=(row, row) if with_next else row,
        scratch_shapes=[pltpu.VMEM((tm, d), BF16), pltpu.VMEM((tm, d), F32)],
        compiler_params=_cparams(("parallel", "arbitrary")),
        name="ffn",
    )(x, g, wg, wu, wd, g_next if with_next else g)


def _inproj_body(xn_ref, w_ref, wgt_ref, gq_ref, gk_ref,
                 q_ref, k_ref, kb_ref, v_ref, vb_ref, qkm_ref, vm_ref, om_ref, gt_ref, *, n_heads):
    j = pl.program_id(1)
    xn = xn_ref[...]
    acc = _dot(xn, w_ref[...])
    wa = n_heads * HEAD_DIM

    @pl.when(j == 0)
    def _():
        for h in range(n_heads):
            sl = slice(h * HEAD_DIM, (h + 1) * HEAD_DIM)
            q_ref[:, sl] = _rms(acc[:, sl], gq_ref[...]).astype(BF16)
        gt_ref[...] = _dot(xn, wgt_ref[...])

    @pl.when(j == 1)
    def _():
        for h in range(n_heads):
            sl = slice(h * HEAD_DIM, (h + 1) * HEAD_DIM)
            kn = _rms(acc[:, sl], gk_ref[...])
            k_ref[:, sl] = kn
            kb_ref[:, sl] = kn.astype(BF16)

    @pl.when(j == 2)
    def _():
        v_ref[...] = acc
        vb_ref[...] = acc.astype(BF16)

    @pl.when(j == 3)
    def _():
        qkm_ref[:, 0:wa] = acc

    @pl.when(j == 4)
    def _():
        qkm_ref[:, wa:2 * wa] = acc

    @pl.when(j == 5)
    def _():
        vm_ref[...] = acc

    @pl.when(j == 6)
    def _():
        om_ref[...] = acc


def _inproj(xn, w_main, w_gates, gq, gk, n_heads):
    n, d = xn.shape
    wa = n_heads * HEAD_DIM
    assert w_main.shape[1] == 7 * wa
    tm = _pick_tile(n, 512)
    row = lambda i, j: (i, 0)
    const = lambda i, j: (0, 0)
    out_shape = (
        jax.ShapeDtypeStruct((n, wa), BF16),
        jax.ShapeDtypeStruct((n, wa), F32),
        jax.ShapeDtypeStruct((n, wa), BF16),
        jax.ShapeDtypeStruct((n, wa), F32),
        jax.ShapeDtypeStruct((n, wa), BF16),
        jax.ShapeDtypeStruct((n, 2 * wa), F32),
        jax.ShapeDtypeStruct((n, wa), F32),
        jax.ShapeDtypeStruct((n, wa), F32),
        jax.ShapeDtypeStruct((n, LANES), F32),
    )
    out_specs = (
        pl.BlockSpec((tm, wa), row), pl.BlockSpec((tm, wa), row), pl.BlockSpec((tm, wa), row),
        pl.BlockSpec((tm, wa), row), pl.BlockSpec((tm, wa), row), pl.BlockSpec((tm, 2 * wa), row),
        pl.BlockSpec((tm, wa), row), pl.BlockSpec((tm, wa), row), pl.BlockSpec((tm, LANES), row),
    )
    return pl.pallas_call(
        functools.partial(_inproj_body, n_heads=n_heads),
        out_shape=out_shape,
        grid=(n // tm, 7),
        in_specs=[
            pl.BlockSpec((tm, d), row),
            pl.BlockSpec((d, wa), lambda i, j: (0, j)),
            pl.BlockSpec((d, LANES), const),
            pl.BlockSpec((1, HEAD_DIM), const),
            pl.BlockSpec((1, HEAD_DIM), const),
        ],
        out_specs=out_specs,
        compiler_params=_cparams(("parallel", "arbitrary")),
        name="inproj",
    )(xn, w_main, w_gates, gq, gk)


def _gates_body(g_ref, b_ref, act_ref, cg_ref, cl_ref, crow_ref, *, seg, lc, rows, n_heads):
    ch = min(rows, 256)
    sg = min(seg, ch)
    lane = lax.broadcasted_iota(jnp.int32, (1, LANES), 1)
    is_logsig = (lane < n_heads) | ((lane >= 2 * n_heads) & (lane < 3 * n_heads))
    r = lax.broadcasted_iota(jnp.int32, (ch, ch), 0)
    c = lax.broadcasted_iota(jnp.int32, (ch, ch), 1)
    sh_g = _log2(sg)
    sh_l = _log2(lc)
    tri_g = ((r >= c) & ((r >> sh_g) == (c >> sh_g))).astype(F32)
    tri_l = ((r >= c) & ((r >> sh_l) == (c >> sh_l))).astype(F32)
    sel = (lax.broadcasted_iota(jnp.int32, (SUBLANES, LANES), 0)
           == lax.broadcasted_iota(jnp.int32, (SUBLANES, LANES), 1)).astype(F32)
    carry = jnp.zeros((1, LANES), F32)
    for ci in range(rows // ch):
        sl = slice(ci * ch, (ci + 1) * ch)
        x = g_ref[sl, :] + b_ref[...]
        act = jnp.where(is_logsig, _log_sigmoid(x), x)
        act_ref[sl, :] = act
        cg = _dot(tri_g, act, precision=HIGHEST)
        if seg > ch:
            cg = cg + carry
            carry = cg[ch - 1:ch, :]
        cg_ref[sl, :] = cg
        cl_ref[sl, :] = _dot(tri_l, act, precision=HIGHEST)
        crow_ref[:, sl] = _dot_nt(sel, cg, precision=HIGHEST)


def _gates(gates, bias, seg, lc, n_heads):
    n = gates.shape[0]
    assert n_heads == SUBLANES
    rows = seg if seg >= 256 else _pick_tile(n, 256)
    nb = n // rows
    blk = pl.BlockSpec((rows, LANES), lambda b: (b, 0))
    return pl.pallas_call(
        functools.partial(_gates_body, seg=seg, lc=lc, rows=rows, n_heads=n_heads),
        out_shape=(jax.ShapeDtypeStruct((n, LANES), F32),) * 3 + (jax.ShapeDtypeStruct((nb, SUBLANES, rows), F32),),
        grid=(nb,),
        in_specs=[blk, pl.BlockSpec((1, LANES), lambda b: (0, 0))],
        out_specs=(blk, blk, blk, pl.BlockSpec((None, SUBLANES, rows), lambda b: (b, 0, 0))),
        compiler_params=_cparams(("parallel",)),
        name="gates",
    )(gates, bias)


def _fox_prompt_body(q_ref, k_ref, v_ref, ccol_ref, crow_ref, g_ref, o_ref,
                     cq_ref, m_ref, l_ref, acc_ref, *, tq, tk, scale):
    h = pl.program_id(1)
    qi = pl.program_id(2)
    ki = pl.program_id(3)

    @pl.when(ki == 0)
    def _():
        lane = lax.broadcasted_iota(jnp.int32, (1, LANES), 1)
        cq_ref[...] = jnp.sum(jnp.where(lane == h, ccol_ref[...], 0.0), axis=1, keepdims=True)
        m_ref[...] = jnp.full_like(m_ref, NEG_INF)
        l_ref[...] = jnp.zeros_like(l_ref)
        acc_ref[...] = jnp.zeros_like(acc_ref)

    @pl.when(ki * tk <= qi * tq + (tq - 1))
    def _():
        s = _dot_nt(q_ref[...], k_ref[...]) * scale
        ck = crow_ref[pl.ds(h, 1), :]
        s = s + (cq_ref[...] - ck)
        qpos = qi * tq + lax.broadcasted_iota(jnp.int32, (tq, tk), 0)
        kpos = ki * tk + lax.broadcasted_iota(jnp.int32, (tq, tk), 1)
        s = jnp.where(qpos >= kpos, s, NEG_INF)
        m_old = m_ref[...]
        m_new = jnp.maximum(m_old, jnp.max(s, axis=1, keepdims=True))
        alpha = jnp.exp(m_old - m_new)
        p = jnp.exp(s - m_new)
        l_ref[...] = alpha * l_ref[...] + jnp.sum(p, axis=1, keepdims=True)
        acc_ref[...] = alpha * acc_ref[...] + _dot(p.astype(BF16), v_ref[...])
        m_ref[...] = m_new

    @pl.when(ki == pl.num_programs(3) - 1)
    def _():
        o = acc_ref[...] / l_ref[...]
        o_ref[...] = _rms(o, g_ref[...]).astype(o_ref.dtype)


def _fox_prompt(q, kb, vb, cg, crow, g_out, batch, seq, n_heads):
    n, wa = q.shape
    tq = _pick_tile(seq, 512)
    tk = _pick_tile(seq, 512)
    nq, nk = seq // tq, seq // tk
    scale = HEAD_DIM ** -0.5

    def kv_blk(qi, ki):
        return jnp.minimum(ki, (qi * tq + tq - 1) // tk)

    return pl.pallas_call(
        functools.partial(_fox_prompt_body, tq=tq, tk=tk, scale=scale),
        out_shape=jax.ShapeDtypeStruct((n, wa), BF16),
        grid=(batch, n_heads, nq, nk),
        in_specs=[
            pl.BlockSpec((tq, HEAD_DIM), lambda b, h, qi, ki: (b * nq + qi, h)),
            pl.BlockSpec((tk, HEAD_DIM), lambda b, h, qi, ki: (b * nk + kv_blk(qi, ki), h)),
            pl.BlockSpec((tk, HEAD_DIM), lambda b, h, qi, ki: (b * nk + kv_blk(qi, ki), h)),
            pl.BlockSpec((tq, LANES), lambda b, h, qi, ki: (b * nq + qi, 0)),
            pl.BlockSpec((None, SUBLANES, tk), lambda b, h, qi, ki: (b, 0, kv_blk(qi, ki))),
            pl.BlockSpec((None, 1, HEAD_DIM), lambda b, h, qi, ki: (h, 0, 0)),
        ],
        out_specs=pl.BlockSpec((tq, HEAD_DIM), lambda b, h, qi, ki: (b * nq + qi, h)),
        scratch_shapes=[
            pltpu.VMEM((tq, 1), F32),
            pltpu.VMEM((tq, 1), F32),
            pltpu.VMEM((tq, 1), F32),
            pltpu.VMEM((tq, HEAD_DIM), F32),
        ],
        compiler_params=_cparams(("parallel", "parallel", "parallel", "arbitrary")),
        name="fox_prompt",
    )(q, kb, vb, cg, crow, g_out)


def _pool_cumsum_body(x_ref, o_ref, *, n_heads, width):
    x = x_ref[...]
    lane = lax.broadcasted_iota(jnp.int32, (1, width), 1)
    sh = n_heads
    while sh < width:
        x = x + jnp.where(lane >= sh, pltpu.roll(x, sh, axis=1), 0.0)
        sh *= 2
    o_ref[...] = x


def _pool_cumsum(lf, n_heads):
    n_pool, width = lf.shape
    rows = _pick_tile(n_pool, 256)
    blk = pl.BlockSpec((rows, width), lambda i: (i, 0))
    return pl.pallas_call(
        functools.partial(_pool_cumsum_body, n_heads=n_heads, width=width),
        out_shape=jax.ShapeDtypeStruct((n_pool, width), F32),
        grid=(n_pool // rows,),
        in_specs=[blk],
        out_specs=blk,
        compiler_params=_cparams(("parallel",)),
        name="pool_cumsum",
    )(lf)


def _fox_sample_body(pt_ref, q_ref, kn_ref, vn_ref, cl_ref, g_ref, *rest,
                     npg, n_pages, page, n_heads, t_new, scale):
    k_refs = rest[0:npg]
    c_refs = rest[npg:2 * npg]
    v_refs = rest[2 * npg:3 * npg]
    o_ref = rest[3 * npg]
    qx_ref, qxb_ref, pad_ref, cpad_ref, s_ref, carry_ref, acc_ref = rest[3 * npg + 1:]
    del pt_ref
    s_id = pl.program_id(1)
    n_ksteps = n_pages // npg
    past = n_pages * page
    wa = n_heads * HEAD_DIM
    row_h = lax.broadcasted_iota(jnp.int32, (n_heads, wa), 0)
    col_h = lax.broadcasted_iota(jnp.int32, (n_heads, wa), 1) >> _log2(HEAD_DIM)
    head_diag = row_h == col_h

    def expand(c):
        out = c
        for t in range(1, t_new):
            out = out + pltpu.roll(c, t * n_heads, axis=1)
        return out

    @pl.when(s_id == 0)
    def _():
        qx_ref[...] = jnp.zeros_like(qx_ref)
        q = q_ref[...]
        for t in range(t_new):
            qx_ref[t * n_heads:(t + 1) * n_heads, :] = jnp.where(head_diag, q[t:t + 1, :], 0.0)
        qxb_ref[...] = qx_ref[...].astype(BF16)
        cpad_ref[...] = jnp.zeros_like(cpad_ref)
        carry_ref[...] = jnp.zeros_like(carry_ref)
        acc_ref[...] = jnp.zeros_like(acc_ref)

    def head_major(ref):
        return jnp.concatenate([ref[pl.ds(h, page, stride=n_heads), :] for h in range(n_heads)], axis=1)

    @pl.when(s_id < n_ksteps)
    def _():
        for r in range(npg):
            st = _dot_nt(head_major(k_refs[r]).astype(BF16), qxb_ref[...]) * scale
            cpad_ref[:, 0:n_heads] = c_refs[r][...]
            cx = expand(cpad_ref[...]) + carry_ref[...]
            carry_ref[...] = cx[page - 1:page, :]
            off = pl.multiple_of((s_id * npg + r) * page, page)
            s_ref[pl.ds(off, page), :] = st - cx

    @pl.when(s_id == n_ksteps - 1)
    def _():
        pad_ref[...] = jnp.zeros_like(pad_ref)
        pad_ref[0:t_new, :] = kn_ref[...]
        st = _dot_nt(pad_ref[...].astype(BF16), qxb_ref[...]) * scale
        lane = lax.broadcasted_iota(jnp.int32, (1, LANES), 1)
        cpad_ref[...] = jnp.zeros_like(cpad_ref)
        cpad_ref[0:t_new, :] = jnp.where(lane < n_heads, cl_ref[...], 0.0)
        cl_x = expand(cpad_ref[...])
        u = lax.broadcasted_iota(jnp.int32, (page, LANES), 0)
        t = lax.broadcasted_iota(jnp.int32, (page, LANES), 1) >> _log2(n_heads)
        st = jnp.where((u < t_new) & (u <= t), st - cl_x, NEG_INF)
        s_past = s_ref[0:past, :] + carry_ref[...]
        m = jnp.maximum(jnp.max(s_past, axis=0, keepdims=True), jnp.max(st, axis=0, keepdims=True))
        p_past = jnp.exp(s_past - m)
        p_new = jnp.exp(st - m)
        inv = 1.0 / (jnp.sum(p_past, axis=0, keepdims=True) + jnp.sum(p_new, axis=0, keepdims=True))
        s_ref[0:past, :] = p_past * inv
        s_ref[past:past + page, :] = p_new * inv

    @pl.when(s_id >= n_ksteps)
    def _():
        for r in range(npg):
            off = pl.multiple_of(((s_id - n_ksteps) * npg + r) * page, page)
            p = s_ref[pl.ds(off, page), :].T.astype(BF16)
            acc_ref[...] += _dot(p, head_major(v_refs[r]).astype(BF16))

    @pl.when(s_id == 2 * n_ksteps - 1)
    def _():
        pad_ref[...] = jnp.zeros_like(pad_ref)
        pad_ref[0:t_new, :] = vn_ref[...]
        p = s_ref[past:past + page, :].T.astype(BF16)
        acc = acc_ref[...] + _dot(p, pad_ref[...].astype(BF16))
        for t in range(t_new):
            blk = acc[t * n_heads:(t + 1) * n_heads, :]
            o_t = jnp.sum(jnp.where(head_diag, blk, 0.0), axis=0, keepdims=True)
            for h in range(n_heads):
                sl = slice(h * HEAD_DIM, (h + 1) * HEAD_DIM)
                o_ref[t:t + 1, sl] = _rms(o_t[:, sl], g_ref[h:h + 1, :])


def _fox_sample(page_table, q, k_new, v_new, cl, g_out, cache_k, cache_v, cache_c, page, pool_base, n_heads):
    bsz, t_new, wa = q.shape
    n_pages = page_table.shape[1]
    assert page == LANES and n_heads == SUBLANES and n_heads * t_new <= LANES
    npg = _pick_tile(n_pages, 4)
    n_ksteps = n_pages // npg
    scale = HEAD_DIM ** -0.5

    def k_page(b, s, pt, r):
        return pt[b, jnp.minimum(s, n_ksteps - 1) * npg + r]

    def v_page(b, s, pt, r):
        return pt[b, jnp.maximum(s - n_ksteps, 0) * npg + r]

    def kv_map(page_fn, r):
        return lambda b, s, pt: (pool_base + page_fn(b, s, pt, r), 0)

    def c_map(r):
        return lambda b, s, pt: (k_page(b, s, pt, r), 0, 0)

    seq3 = lambda b, s, pt: (b, 0, 0)
    in_specs = [
        pl.BlockSpec((None, t_new, wa), seq3),
        pl.BlockSpec((None, t_new, wa), seq3),
        pl.BlockSpec((None, t_new, wa), seq3),
        pl.BlockSpec((None, t_new, LANES), seq3),
        pl.BlockSpec((n_heads, HEAD_DIM), lambda b, s, pt: (0, 0)),
    ]
    in_specs += [pl.BlockSpec((page * n_heads, HEAD_DIM), kv_map(k_page, r)) for r in range(npg)]
    in_specs += [pl.BlockSpec((None, page, n_heads), c_map(r)) for r in range(npg)]
    in_specs += [pl.BlockSpec((page * n_heads, HEAD_DIM), kv_map(v_page, r)) for r in range(npg)]
    grid_spec = pltpu.PrefetchScalarGridSpec(
        num_scalar_prefetch=1,
        grid=(bsz, 2 * n_ksteps),
        in_specs=in_specs,
        out_specs=pl.BlockSpec((None, t_new, wa), seq3),
        scratch_shapes=[
            pltpu.VMEM((LANES, wa), F32),
            pltpu.VMEM((LANES, wa), BF16),
            pltpu.VMEM((page, wa), F32),
            pltpu.VMEM((page, LANES), F32),
            pltpu.VMEM((n_pages * page + page, LANES), F32),
            pltpu.VMEM((1, LANES), F32),
            pltpu.VMEM((LANES, wa), F32),
        ],
    )
    return pl.pallas_call(
        functools.partial(_fox_sample_body, npg=npg, n_pages=n_pages, page=page, n_heads=n_heads,
                          t_new=t_new, scale=scale),
        out_shape=jax.ShapeDtypeStruct((bsz, t_new, wa), F32),
        grid_spec=grid_spec,
        compiler_params=_cparams(("parallel", "arbitrary")),
        name="fox_sample",
    )(page_table, q, k_new, v_new, cl, g_out, *([cache_k] * npg), *([cache_c] * npg), *([cache_v] * npg))


def _col_to_row(col, n):
    eye = lax.broadcasted_iota(jnp.int32, (n, n), 0) == lax.broadcasted_iota(jnp.int32, (n, n), 1)
    return jnp.sum(jnp.where(eye, col, 0.0), axis=0, keepdims=True)


def _mlstm_chunk(qc, kc, vc, i_col, b_col, c_st, n_st, m_st, n_valid):
    ln = qc.shape[0]
    i_row = _col_to_row(i_col, ln)
    b_row = _col_to_row(b_col, ln)
    r = lax.broadcasted_iota(jnp.int32, (ln, ln), 0)
    c = lax.broadcasted_iota(jnp.int32, (ln, ln), 1)
    a_col = b_col + m_st
    dmat = jnp.where(r >= c, b_col - b_row + i_row, NEG_INF)
    mt = jnp.maximum(a_col, jnp.max(dmat, axis=1, keepdims=True))
    qb = qc.astype(BF16)
    kb = kc.astype(BF16)
    vb = vc.astype(BF16)
    w_intra = jnp.exp(dmat - mt) * _dot_nt(qb, kb)
    w_inter = jnp.exp(a_col - mt)
    num = w_inter * _dot(qb, c_st.astype(BF16)) + _dot(w_intra.astype(BF16), vb)
    den = w_inter * jnp.sum(qc * n_st, axis=1, keepdims=True) + jnp.sum(w_intra, axis=1, keepdims=True)
    h = num / jnp.maximum(jnp.abs(den), jnp.exp(-mt))
    last = n_valid - 1
    m_new = mt[last:last + 1, :]
    b_last = b_col[last:last + 1, :]
    w_state = jnp.exp(b_last - b_col + i_col - m_new)
    if n_valid < ln:
        row = lax.broadcasted_iota(jnp.int32, (ln, 1), 0)
        w_state = jnp.where(row < n_valid, w_state, 0.0)
    decay = jnp.exp(b_last + m_st - m_new)
    ks = w_state * kc
    c_new = decay * c_st + _dot(ks.T.astype(BF16), vb)
    n_new = decay * n_st + jnp.sum(ks, axis=0, keepdims=True)
    return h, c_new, n_new, m_new


def _pick_lane(x, idx):
    lane = lax.broadcasted_iota(jnp.int32, (1, LANES), 1)
    return jnp.sum(jnp.where(lane == idx, x, 0.0), axis=1, keepdims=True)


def _mlstm_prompt_body(qr_ref, kr_ref, wq_ref, wk_ref, bq_ref, bk_ref, v_ref, o_ref, act_ref, cl_ref, g_ref,
                       y_ref, c_out_ref, n_out_ref, m_out_ref,
                       ext_ref, qc_ref, kc_ref, c_ref, n_ref, m_ref, *, seq, chunk, n_heads):
    h = pl.program_id(1)
    pad = SUBLANES
    kscale = HEAD_DIM ** -0.5

    def conv(raw_ref, w_ref, b_ref, dst_ref, post_scale):
        ext_ref[0:pad, :] = jnp.zeros((pad, HEAD_DIM), F32)
        ext_ref[pad:pad + seq, :] = raw_ref[...]
        acc = b_ref[...] + w_ref[CONV_W - 1:CONV_W, :] * raw_ref[...]
        for j in range(CONV_W - 1):
            off = pad - (CONV_W - 1) + j
            acc = acc + w_ref[j:j + 1, :] * ext_ref[off:off + seq, :]
        acc = acc * jax.nn.sigmoid(acc)
        dst_ref[...] = acc * post_scale if post_scale is not None else acc

    conv(qr_ref, wq_ref, bq_ref, qc_ref, None)
    conv(kr_ref, wk_ref, bk_ref, kc_ref, kscale)
    c_ref[...] = jnp.zeros_like(c_ref)
    n_ref[...] = jnp.zeros_like(n_ref)
    m_ref[...] = jnp.zeros_like(m_ref)

    def step(ci, carry):
        rows = pl.ds(pl.multiple_of(ci * chunk, chunk), chunk)
        i_col = _pick_lane(act_ref[rows, :], n_heads + h)
        b_col = _pick_lane(cl_ref[rows, :], 2 * n_heads + h)
        hh, c_new, n_new, m_new = _mlstm_chunk(qc_ref[rows, :], kc_ref[rows, :], v_ref[rows, :], i_col, b_col,
                                               c_ref[...], n_ref[...], m_ref[...], chunk)
        c_ref[...] = c_new
        n_ref[...] = n_new
        m_ref[...] = m_new
        y = jax.nn.sigmoid(o_ref[rows, :]) * _rms(hh, g_ref[...])
        y_ref[rows, :] = y.astype(y_ref.dtype)
        return carry

    lax.fori_loop(0, seq // chunk, step, 0)
    c_out_ref[...] = c_ref[...]
    n_out_ref[...] = n_ref[...]
    m_out_ref[...] = m_ref[...]


def _mlstm_prompt(qkm, conv_w, conv_b, vm, om, act, cl, g_out, batch, seq, n_heads):
    n = qkm.shape[0]
    wm = n_heads * HEAD_DIM
    chunk = math.gcd(seq, MLSTM_CHUNK)
    tok = lambda off: (lambda b, h: (b, off + h))
    hd = pl.BlockSpec((seq, HEAD_DIM), tok(0))
    out_shape = (
        jax.ShapeDtypeStruct((n, wm), BF16),
        jax.ShapeDtypeStruct((batch, n_heads, HEAD_DIM, HEAD_DIM), F32),
        jax.ShapeDtypeStruct((batch, n_heads, 1, HEAD_DIM), F32),
        jax.ShapeDtypeStruct((batch, n_heads, 1, 1), F32),
    )
    st = lambda b, h: (b, h, 0, 0)
    return pl.pallas_call(
        functools.partial(_mlstm_prompt_body, seq=seq, chunk=chunk, n_heads=n_heads),
        out_shape=out_shape,
        grid=(batch, n_heads),
        in_specs=[
            hd,
            pl.BlockSpec((seq, HEAD_DIM), tok(n_heads)),
            pl.BlockSpec((CONV_W, HEAD_DIM), lambda b, h: (0, h)),
            pl.BlockSpec((CONV_W, HEAD_DIM), lambda b, h: (0, n_heads + h)),
            pl.BlockSpec((1, HEAD_DIM), lambda b, h: (0, h)),
            pl.BlockSpec((1, HEAD_DIM), lambda b, h: (0, n_heads + h)),
            hd,
            hd,
            pl.BlockSpec((seq, LANES), lambda b, h: (b, 0)),
            pl.BlockSpec((seq, LANES), lambda b, h: (b, 0)),
            pl.BlockSpec((None, 1, HEAD_DIM), lambda b, h: (h, 0, 0)),
        ],
        out_specs=(
            hd,
            pl.BlockSpec((None, None, HEAD_DIM, HEAD_DIM), st),
            pl.BlockSpec((None, None, 1, HEAD_DIM), st),
            pl.BlockSpec((None, None, 1, 1), st),
        ),
        scratch_shapes=[
            pltpu.VMEM((seq + SUBLANES, HEAD_DIM), F32),
            pltpu.VMEM((seq, HEAD_DIM), F32),
            pltpu.VMEM((seq, HEAD_DIM), F32),
            pltpu.VMEM((HEAD_DIM, HEAD_DIM), F32),
            pltpu.VMEM((1, HEAD_DIM), F32),
            pltpu.VMEM((1, 1), F32),
        ],
        compiler_params=_cparams(("parallel", "parallel")),
        name="mlstm_prompt",
    )(qkm, qkm, conv_w, conv_w, conv_b, conv_b, vm, om, act, cl, g_out)


def _mlstm_sample_body(qk_ref, prev_ref, w_ref, b_ref, v_ref, o_ref, act_ref, cl_ref, g_ref, c0_ref, n0_ref, m0_ref,
                       y_ref, c_out_ref, n_out_ref, m_out_ref, ext_ref, pad_ref, *, t_new, n_heads):
    wm = n_heads * HEAD_DIM
    rows = SUBLANES
    kscale = HEAD_DIM ** -0.5
    ext_ref[...] = jnp.zeros_like(ext_ref)
    ext_ref[0:CONV_W - 1, :] = prev_ref[...]
    ext_ref[CONV_W - 1:CONV_W - 1 + t_new, :] = qk_ref[...]
    acc = b_ref[...] + w_ref[0:1, :] * ext_ref[0:rows, :]
    for j in range(1, CONV_W):
        acc = acc + w_ref[j:j + 1, :] * ext_ref[j:j + rows, :]
    qk = acc * jax.nn.sigmoid(acc)
    valid = lax.broadcasted_iota(jnp.int32, (rows, 1), 0) < t_new

    def padded(ref, width):
        pad_ref[:, 0:width] = jnp.zeros((rows, width), F32)
        pad_ref[0:t_new, 0:width] = ref[...]
        return pad_ref[:, 0:width]

    v = padded(v_ref, wm)
    o = padded(o_ref, wm)
    act = padded(act_ref, LANES)
    cl = padded(cl_ref, LANES)
    for h in range(n_heads):
        sl = slice(h * HEAD_DIM, (h + 1) * HEAD_DIM)
        qc = jnp.where(valid, qk[:, sl], 0.0)
        kc = jnp.where(valid, qk[:, wm + h * HEAD_DIM: wm + (h + 1) * HEAD_DIM] * kscale, 0.0)
        i_col = jnp.where(valid, act[:, n_heads + h:n_heads + h + 1], NEG_INF)
        b_col = cl[:, 2 * n_heads + h:2 * n_heads + h + 1]
        b_col = jnp.where(valid, b_col, b_col[t_new - 1:t_new, :])
        hh, c_new, n_new, m_new = _mlstm_chunk(qc, kc, v[:, sl], i_col, b_col,
                                               c0_ref[h], n0_ref[h:h + 1, :], m0_ref[:, h:h + 1], t_new)
        y = jax.nn.sigmoid(o[:, sl]) * _rms(hh, g_ref[h:h + 1, :])
        y_ref[:, sl] = y[0:t_new, :]
        c_out_ref[h] = c_new
        n_out_ref[h:h + 1, :] = n_new
        m_out_ref[:, h:h + 1] = m_new


def _mlstm_sample(qkm, conv_prev, conv_w, conv_b, vm, om, act, cl, g_out, c0, n0, m0, n_heads):
    bsz, t_new, wm2 = qkm.shape
    wm = wm2 // 2
    assert t_new + CONV_W - 1 <= SUBLANES
    seq3 = lambda b: (b, 0, 0)
    const = lambda b: (0, 0)
    out_shape = (
        jax.ShapeDtypeStruct((bsz, t_new, wm), F32),
        jax.ShapeDtypeStruct((bsz, n_heads, HEAD_DIM, HEAD_DIM), F32),
        jax.ShapeDtypeStruct((bsz, n_heads, HEAD_DIM), F32),
        jax.ShapeDtypeStruct((bsz, 1, n_heads), F32),
    )
    return pl.pallas_call(
        functools.partial(_mlstm_sample_body, t_new=t_new, n_heads=n_heads),
        out_shape=out_shape,
        grid=(bsz,),
        in_specs=[
            pl.BlockSpec((None, t_new, wm2), seq3),
            pl.BlockSpec((None, CONV_W - 1, wm2), seq3),
            pl.BlockSpec((CONV_W, wm2), const),
            pl.BlockSpec((1, wm2), const),
            pl.BlockSpec((None, t_new, wm), seq3),
            pl.BlockSpec((None, t_new, wm), seq3),
            pl.BlockSpec((None, t_new, LANES), seq3),
            pl.BlockSpec((None, t_new, LANES), seq3),
            pl.BlockSpec((n_heads, HEAD_DIM), const),
            pl.BlockSpec((None, n_heads, HEAD_DIM, HEAD_DIM), lambda b: (b, 0, 0, 0)),
            pl.BlockSpec((None, n_heads, HEAD_DIM), seq3),
            pl.BlockSpec((None, 1, n_heads), seq3),
        ],
        out_specs=(
            pl.BlockSpec((None, t_new, wm), seq3),
            pl.BlockSpec((None, n_heads, HEAD_DIM, HEAD_DIM), lambda b: (b, 0, 0, 0)),
            pl.BlockSpec((None, n_heads, HEAD_DIM), seq3),
            pl.BlockSpec((None, 1, n_heads), seq3),
        ),
        scratch_shapes=[pltpu.VMEM((2 * SUBLANES, wm2), F32), pltpu.VMEM((SUBLANES, wm), F32)],
        compiler_params=_cparams(("parallel",)),
        name="mlstm_sample",
    )(qkm, conv_prev, conv_w, conv_b, vm, om, act, cl, g_out, c0, n0, m0)


def _outproj_body(ya_ref, ym_ref, w_ref, x_ref, o_ref, *, wa):
    y = _dot(ya_ref[...].astype(BF16), w_ref[0:wa, :]) + _dot(ym_ref[...].astype(BF16), w_ref[wa:, :])
    o_ref[...] = x_ref[...] + y


def _outproj(ya, ym, w, x):
    n, d = x.shape
    wa = ya.shape[1]
    wm = ym.shape[1]
    tm = _pick_tile(n, 512)
    row = lambda i: (i, 0)
    return pl.pallas_call(
        functools.partial(_outproj_body, wa=wa),
        out_shape=jax.ShapeDtypeStruct((n, d), F32),
        grid=(n // tm,),
        in_specs=[
            pl.BlockSpec((tm, wa), row),
            pl.BlockSpec((tm, wm), row),
            pl.BlockSpec((wa + wm, d), lambda i: (0, 0)),
            pl.BlockSpec((tm, d), row),
        ],
        out_specs=pl.BlockSpec((tm, d), row),
        compiler_params=_cparams(("parallel",)),
        name="outproj",
    )(ya, ym, w, x)


def _prep_weights(lw, n_heads_a, n_heads_m):
    wa = n_heads_a * HEAD_DIM
    wm = n_heads_m * HEAD_DIM
    f = lw['w1_gate'].shape[1]
    tf = 512
    fp = tf * ((f + tf - 1) // tf)

    def ffn_w(wg, wu, wd):
        d = wg.shape[0]
        zc = jnp.zeros((d, fp - f), BF16)
        zr = jnp.zeros((fp - f, d), BF16)
        return (jnp.concatenate([wg.astype(BF16), zc], axis=1), jnp.concatenate([wu.astype(BF16), zc], axis=1),
                jnp.concatenate([wd.astype(BF16), zr], axis=0))

    w_in = lw['w_in']
    o_f = 3 * wa
    o_qk = o_f + n_heads_a
    o_i = o_qk + 2 * wm + 2 * wm
    o_fm = o_i + n_heads_m
    w_main = jnp.concatenate([w_in[:, :o_f], w_in[:, o_qk:o_i]], axis=1).astype(BF16)
    ng = 2 * n_heads_m + n_heads_a
    w_gates = jnp.concatenate(
        [w_in[:, o_f:o_qk], w_in[:, o_i:o_fm], w_in[:, o_fm:o_fm + n_heads_m],
         jnp.zeros((w_in.shape[0], LANES - ng), w_in.dtype)], axis=1).astype(BF16)
    bias = jnp.concatenate([lw['b_fox_f'], lw['b_m_i'], lw['b_m_f'], jnp.zeros((LANES - ng,), F32)])[None, :]
    return {
        'ffn1': ffn_w(lw['w1_gate'], lw['w1_up'], lw['w1_down']),
        'ffn2': ffn_w(lw['w2_gate'], lw['w2_up'], lw['w2_down']),
        'tf': tf,
        'w_main': w_main,
        'w_gates': w_gates,
        'gate_bias': bias,
        'w_out': lw['w_out'].astype(BF16),
    }


def kernel(x_prompt, x_sample, cache_k, cache_v, cache_logf, state_conv, state_C, state_n, state_m, page_table,
           g_ffn1, w1_gate, w1_up, w1_down, g_mix, w_in, b_fox_f, b_m_i, b_m_f, conv_w, conv_b, g_q, g_k,
           g_out_a, g_out_m, w_out, g_ffn2, w2_gate, w2_up, w2_down):
    depth = w_in.shape[0]
    bp, seq, d = x_prompt.shape
    bs, t_new, _ = x_sample.shape
    n_heads_a = g_out_a.shape[1]
    n_heads_m = g_out_m.shape[1]
    wa = n_heads_a * HEAD_DIM
    wm = n_heads_m * HEAD_DIM
    n_pool, page = cache_k.shape[1], cache_k.shape[2]

    yp = x_prompt.reshape(bp * seq, d)
    ys = x_sample.reshape(bs * t_new, d)
    outs = [[] for _ in range(14)]
    for l in range(depth):
        lw = {
            'w1_gate': w1_gate[l], 'w1_up': w1_up[l], 'w1_down': w1_down[l], 'w_in': w_in[l],
            'b_fox_f': b_fox_f[l], 'b_m_i': b_m_i[l], 'b_m_f': b_m_f[l], 'w_out': w_out[l],
            'w2_gate': w2_gate[l], 'w2_up': w2_up[l], 'w2_down': w2_down[l],
        }
        pw = _prep_weights(lw, n_heads_a, n_heads_m)
        gf1 = g_ffn1[l][None, :]
        gf2 = g_ffn2[l][None, :]
        gmix = g_mix[l][None, :]
        gq = g_q[l][None, :]
        gk = g_k[l][None, :]
        cw = conv_w[l]
        cb = conv_b[l][None, :]
        goa = g_out_a[l]
        gom = g_out_m[l]

        x1, xn = _ffn(yp, gf1, *pw['ffn1'], pw['tf'], g_next=gmix)
        q, k, kb, v, vb, qkm, vm, om, gt = _inproj(xn, pw['w_main'], pw['w_gates'], gq, gk, n_heads_a)
        act, cg, cl, crow = _gates(gt, pw['gate_bias'], seq, math.gcd(seq, MLSTM_CHUNK), n_heads_a)
        ya = _fox_prompt(q, kb, vb, cg, crow, goa[:, None, :], bp, seq, n_heads_a)
        ym, c_p, n_p, m_p = _mlstm_prompt(qkm, cw, cb, vm, om, act, cl, gom[:, None, :], bp, seq, n_heads_m)
        x2 = _outproj(ya, ym, pw['w_out'], x1)
        yp = _ffn(x2, gf2, *pw['ffn2'], pw['tf'])
        outs[0].append(k.reshape(bp, seq, n_heads_a, HEAD_DIM))
        outs[1].append(v.reshape(bp, seq, n_heads_a, HEAD_DIM))
        outs[2].append(act[:, :n_heads_a].reshape(bp, seq, n_heads_a))
        outs[3].append(qkm.reshape(bp, seq, 2 * wm)[:, seq - (CONV_W - 1):, :])
        outs[4].append(c_p)
        outs[5].append(n_p.reshape(bp, n_heads_m, HEAD_DIM))
        outs[6].append(m_p.reshape(bp, n_heads_m))

        x1, xn = _ffn(ys, gf1, *pw['ffn1'], pw['tf'], g_next=gmix)
        q, k, kb, v, vb, qkm, vm, om, gt = _inproj(xn, pw['w_main'], pw['w_gates'], gq, gk, n_heads_a)
        act, cg, cl, crow = _gates(gt, pw['gate_bias'], t_new, math.gcd(t_new, MLSTM_CHUNK), n_heads_a)
        r3 = lambda a: a.reshape(bs, t_new, a.shape[-1])
        cache_c = _pool_cumsum(cache_logf[l].reshape(n_pool, page * n_heads_a), n_heads_a)
        ya = _fox_sample(page_table, r3(q).astype(F32), r3(k), r3(v), r3(cg), goa,
                         cache_k.reshape(-1, HEAD_DIM), cache_v.reshape(-1, HEAD_DIM),
                         cache_c.reshape(n_pool, page, n_heads_a), page, l * n_pool, n_heads_a)
        ym, c_s, n_s, m_s = _mlstm_sample(r3(qkm), state_conv[l], cw, cb, r3(vm), r3(om), r3(act), r3(cl), gom,
                                          state_C[l], state_n[l], state_m[l][:, None, :], n_heads_m)
        x2 = _outproj(ya.reshape(bs * t_new, wa), ym.reshape(bs * t_new, wm), pw['w_out'], x1)
        ys = _ffn(x2, gf2, *pw['ffn2'], pw['tf'])
        conv_ext = jnp.concatenate([state_conv[l], r3(qkm)], axis=1)
        outs[7].append(k.reshape(bs, t_new, n_heads_a, HEAD_DIM))
        outs[8].append(v.reshape(bs, t_new, n_heads_a, HEAD_DIM))
        outs[9].append(act[:, :n_heads_a].reshape(bs, t_new, n_heads_a))
        outs[10].append(conv_ext[:, t_new:, :])
        outs[11].append(c_s)
        outs[12].append(n_s)
        outs[13].append(m_s.reshape(bs, n_heads_m))

    return (yp.reshape(bp, seq, d), ys.reshape(bs, t_new, d)) + tuple(jnp.stack(o) for o in outs)
```

```python
import functools
import math

import jax
import jax.numpy as jnp
from jax import lax
from jax.experimental import pallas as pl
from jax.experimental.pallas import tpu as pltpu

F32 = jnp.float32
BF16 = jnp.bfloat16
HIGHEST = lax.Precision.HIGHEST

EPS = 1e-6
HEAD_DIM = 128
CONV_W = 4
MLSTM_CHUNK = 64
LANES = 128
SUBLANES = 8
VMEM_LIMIT_BYTES = 56 * 1024 * 1024
NEG_INF = float("-inf")


def _cparams(sem):
    return pltpu.CompilerParams(dimension_semantics=sem, vmem_limit_bytes=VMEM_LIMIT_BYTES)


def _rms(x, g):
    y = x * lax.rsqrt(jnp.mean(x * x, axis=-1, keepdims=True) + EPS)
    return y * g


def _log_sigmoid(x):
    return jnp.minimum(x, 0.0) - jnp.log1p(jnp.exp(-jnp.abs(x)))


def _dot(a, b, precision=None):
    return jnp.dot(a, b, preferred_element_type=F32, precision=precision)


def _dot_nt(a, b, precision=None):
    return lax.dot_general(a, b, (((1,), (1,)), ((), ())), preferred_element_type=F32, precision=precision)


def _pick_tile(n, pref):
    t = min(n, pref)
    while n % t:
        t //= 2
    return t


def _log2(n):
    k = int(math.log2(n))
    assert 1 << k == n
    return k


def _ffn_body(x_ref, g_ref, wg_ref, wu_ref, wd_ref, gn_ref, *rest, with_next):
    if with_next:
        o_ref, on_ref, xn_ref, acc_ref = rest
    else:
        o_ref, xn_ref, acc_ref = rest
    j = pl.program_id(1)

    @pl.when(j == 0)
    def _():
        xn_ref[...] = _rms(x_ref[...], g_ref[...]).astype(BF16)
        acc_ref[...] = jnp.zeros_like(acc_ref)

    xn = xn_ref[...]
    gate = _dot(xn, wg_ref[...])
    up = _dot(xn, wu_ref[...])
    h = (gate * jax.nn.sigmoid(gate)) * up
    acc_ref[...] += _dot(h.astype(BF16), wd_ref[...])

    @pl.when(j == pl.num_programs(1) - 1)
    def _():
        y = x_ref[...] + 0.5 * acc_ref[...]
        o_ref[...] = y
        if with_next:
            on_ref[...] = _rms(y, gn_ref[...]).astype(BF16)


def _ffn(x, g, wg, wu, wd, tf, g_next=None):
    n, d = x.shape
    fp = wg.shape[1]
    tm = _pick_tile(n, 512)
    with_next = g_next is not None
    row = pl.BlockSpec((tm, d), lambda i, j: (i, 0))
    vec = pl.BlockSpec((1, d), lambda i, j: (0, 0))
    out_shape = jax.ShapeDtypeStruct((n, d), F32)
    return pl.pallas_call(
        functools.partial(_ffn_body, with_next=with_next),
        out_shape=(out_shape, jax.ShapeDtypeStruct((n, d), BF16)) if with_next else out_shape,
        grid=(n // tm, fp // tf),
        in_specs=[
            row, vec,
            pl.BlockSpec((d, tf), lambda i, j: (0, j)),
            pl.BlockSpec((d, tf), lambda i, j: (0, j)),
            pl.BlockSpec((tf, d), lambda i, j: (j, 0)),
            vec,
        ],
        out_specs=(row, row) if with_next else row,
        scratch_shapes=[pltpu.VMEM((tm, d), BF16), pltpu.VMEM((tm, d), F32)],
        compiler_params=_cparams(("parallel", "arbitrary")),
        name="ffn",
    )(x, g, wg, wu, wd, g_next if with_next else g)


def _inproj_body(xn_ref, w_ref, wgt_ref, gq_ref, gk_ref,
                 q_ref, k_ref, kb_ref, v_ref, vb_ref, qkm_ref, vm_ref, om_ref, gt_ref, *, n_heads):
    j = pl.program_id(1)
    xn = xn_ref[...]
    acc = _dot(xn, w_ref[...])
    wa = n_heads * HEAD_DIM

    @pl.when(j == 0)
    def _():
        for h in range(n_heads):
            sl = slice(h * HEAD_DIM, (h + 1) * HEAD_DIM)
            q_ref[:, sl] = _rms(acc[:, sl], gq_ref[...]).astype(BF16)
        gt_ref[...] = _dot(xn, wgt_ref[...])

    @pl.when(j == 1)
    def _():
        for h in range(n_heads):
            sl = slice(h * HEAD_DIM, (h + 1) * HEAD_DIM)
            kn = _rms(acc[:, sl], gk_ref[...])
            k_ref[:, sl] = kn
            kb_ref[:, sl] = kn.astype(BF16)

    @pl.when(j == 2)
    def _():
        v_ref[...] = acc
        vb_ref[...] = acc.astype(BF16)

    @pl.when(j == 3)
    def _():
        qkm_ref[:, 0:wa] = acc

    @pl.when(j == 4)
    def _():
        qkm_ref[:, wa:2 * wa] = acc

    @pl.when(j == 5)
    def _():
        vm_ref[...] = acc

    @pl.when(j == 6)
    def _():
        om_ref[...] = acc


def _inproj(xn, w_main, w_gates, gq, gk, n_heads):
    n, d = xn.shape
    wa = n_heads * HEAD_DIM
    assert w_main.shape[1] == 7 * wa
    tm = _pick_tile(n, 512)
    row = lambda i, j: (i, 0)
    const = lambda i, j: (0, 0)
    out_shape = (
        jax.ShapeDtypeStruct((n, wa), BF16),
        jax.ShapeDtypeStruct((n, wa), F32),
        jax.ShapeDtypeStruct((n, wa), BF16),
        jax.ShapeDtypeStruct((n, wa), F32),
        jax.ShapeDtypeStruct((n, wa), BF16),
        jax.ShapeDtypeStruct((n, 2 * wa), F32),
        jax.ShapeDtypeStruct((n, wa), F32),
        jax.ShapeDtypeStruct((n, wa), F32),
        jax.ShapeDtypeStruct((n, LANES), F32),
    )
    out_specs = (
        pl.BlockSpec((tm, wa), row), pl.BlockSpec((tm, wa), row), pl.BlockSpec((tm, wa), row),
        pl.BlockSpec((tm, wa), row), pl.BlockSpec((tm, wa), row), pl.BlockSpec((tm, 2 * wa), row),
        pl.BlockSpec((tm, wa), row), pl.BlockSpec((tm, wa), row), pl.BlockSpec((tm, LANES), row),
    )
    return pl.pallas_call(
        functools.partial(_inproj_body, n_heads=n_heads),
        out_shape=out_shape,
        grid=(n // tm, 7),
        in_specs=[
            pl.BlockSpec((tm, d), row),
            pl.BlockSpec((d, wa), lambda i, j: (0, j)),
            pl.BlockSpec((d, LANES), const),
            pl.BlockSpec((1, HEAD_DIM), const),
            pl.BlockSpec((1, HEAD_DIM), const),
        ],
        out_specs=out_specs,
        compiler_params=_cparams(("parallel", "arbitrary")),
        name="inproj",
    )(xn, w_main, w_gates, gq, gk)


def _gates_body(g_ref, b_ref, act_ref, cg_ref, cl_ref, crow_ref, *, seg, lc, rows, n_heads):
    ch = min(rows, 256)
    sg = min(seg, ch)
    lane = lax.broadcasted_iota(jnp.int32, (1, LANES), 1)
    is_logsig = (lane < n_heads) | ((lane >= 2 * n_heads) & (lane < 3 * n_heads))
    r = lax.broadcasted_iota(jnp.int32, (ch, ch), 0)
    c = lax.broadcasted_iota(jnp.int32, (ch, ch), 1)
    sh_g = _log2(sg)
    sh_l = _log2(lc)
    tri_g = ((r >= c) & ((r >> sh_g) == (c >> sh_g))).astype(F32)
    tri_l = ((r >= c) & ((r >> sh_l) == (c >> sh_l))).astype(F32)
    sel = (lax.broadcasted_iota(jnp.int32, (SUBLANES, LANES), 0)
           == lax.broadcasted_iota(jnp.int32, (SUBLANES, LANES), 1)).astype(F32)
    carry = jnp.zeros((1, LANES), F32)
    for ci in range(rows // ch):
        sl = slice(ci * ch, (ci + 1) * ch)
        x = g_ref[sl, :] + b_ref[...]
        act = jnp.where(is_logsig, _log_sigmoid(x), x)
        act_ref[sl, :] = act
        cg = _dot(tri_g, act, precision=HIGHEST)
        if seg > ch:
            cg = cg + carry
            carry = cg[ch - 1:ch, :]
        cg_ref[sl, :] = cg
        cl_ref[sl, :] = _dot(tri_l, act, precision=HIGHEST)
        crow_ref[:, sl] = _dot_nt(sel, cg, precision=HIGHEST)


def _gates(gates, bias, seg, lc, n_heads):
    n = gates.shape[0]
    assert n_heads == SUBLANES
    rows = seg if seg >= 256 else _pick_tile(n, 256)
    nb = n // rows
    blk = pl.BlockSpec((rows, LANES), lambda b: (b, 0))
    return pl.pallas_call(
        functools.partial(_gates_body, seg=seg, lc=lc, rows=rows, n_heads=n_heads),
        out_shape=(jax.ShapeDtypeStruct((n, LANES), F32),) * 3 + (jax.ShapeDtypeStruct((nb, SUBLANES, rows), F32),),
        grid=(nb,),
        in_specs=[blk, pl.BlockSpec((1, LANES), lambda b: (0, 0))],
        out_specs=(blk, blk, blk, pl.BlockSpec((None, SUBLANES, rows), lambda b: (b, 0, 0))),
        compiler_params=_cparams(("parallel",)),
        name="gates",
    )(gates, bias)


def _fox_prompt_body(q_ref, k_ref, v_ref, ccol_ref, crow_ref, g_ref, o_ref,
                     cq_ref, m_ref, l_ref, acc_ref, *, tq, tk, scale):
    h = pl.program_id(1)
    qi = pl.program_id(2)
    ki = pl.program_id(3)

    @pl.when(ki == 0)
    def _():
        lane = lax.broadcasted_iota(jnp.int32, (1, LANES), 1)
        cq_ref[...] = jnp.sum(jnp.where(lane == h, ccol_ref[...], 0.0), axis=1, keepdims=True)
        m_ref[...] = jnp.full_like(m_ref, NEG_INF)
        l_ref[...] = jnp.zeros_like(l_ref)
        acc_ref[...] = jnp.zeros_like(acc_ref)

    @pl.when(ki * tk <= qi * tq + (tq - 1))
    def _():
        s = _dot_nt(q_ref[...], k_ref[...]) * scale
        ck = crow_ref[pl.ds(h, 1), :]
        s = s + (cq_ref[...] - ck)
        qpos = qi * tq + lax.broadcasted_iota(jnp.int32, (tq, tk), 0)
        kpos = ki * tk + lax.broadcasted_iota(jnp.int32, (tq, tk), 1)
        s = jnp.where(qpos >= kpos, s, NEG_INF)
        m_old = m_ref[...]
        m_new = jnp.maximum(m_old, jnp.max(s, axis=1, keepdims=True))
        alpha = jnp.exp(m_old - m_new)
        p = jnp.exp(s - m_new)
        l_ref[...] = alpha * l_ref[...] + jnp.sum(p, axis=1, keepdims=True)
        acc_ref[...] = alpha * acc_ref[...] + _dot(p.astype(BF16), v_ref[...])
        m_ref[...] = m_new

    @pl.when(ki == pl.num_programs(3) - 1)
    def _():
        o = acc_ref[...] / l_ref[...]
        o_ref[...] = _rms(o, g_ref[...]).astype(o_ref.dtype)


def _fox_prompt(q, kb, vb, cg, crow, g_out, batch, seq, n_heads):
    n, wa = q.shape
    tq = _pick_tile(seq, 512)
    tk = _pick_tile(seq, 512)
    nq, nk = seq // tq, seq // tk
    scale = HEAD_DIM ** -0.5

    def kv_blk(qi, ki):
        return jnp.minimum(ki, (qi * tq + tq - 1) // tk)

    return pl.pallas_call(
        functools.partial(_fox_prompt_body, tq=tq, tk=tk, scale=scale),
        out_shape=jax.ShapeDtypeStruct((n, wa), BF16),
        grid=(batch, n_heads, nq, nk),
        in_specs=[
            pl.BlockSpec((tq, HEAD_DIM), lambda b, h, qi, ki: (b * nq + qi, h)),
            pl.BlockSpec((tk, HEAD_DIM), lambda b, h, qi, ki: (b * nk + kv_blk(qi, ki), h)),
            pl.BlockSpec((tk, HEAD_DIM), lambda b, h, qi, ki: (b * nk + kv_blk(qi, ki), h)),
            pl.BlockSpec((tq, LANES), lambda b, h, qi, ki: (b * nq + qi, 0)),
            pl.BlockSpec((None, SUBLANES, tk), lambda b, h, qi, ki: (b, 0, kv_blk(qi, ki))),
            pl.BlockSpec((None, 1, HEAD_DIM), lambda b, h, qi, ki: (h, 0, 0)),
        ],
        out_specs=pl.BlockSpec((tq, HEAD_DIM), lambda b, h, qi, ki: (b * nq + qi, h)),
        scratch_shapes=[
            pltpu.VMEM((tq, 1), F32),
            pltpu.VMEM((tq, 1), F32),
            pltpu.VMEM((tq, 1), F32),
            pltpu.VMEM((tq, HEAD_DIM), F32),
        ],
        compiler_params=_cparams(("parallel", "parallel", "parallel", "arbitrary")),
        name="fox_prompt",
    )(q, kb, vb, cg, crow, g_out)


def _pool_cumsum_body(x_ref, o_ref, *, n_heads, width):
    x = x_ref[...]
    lane = lax.broadcasted_iota(jnp.int32, (1, width), 1)
    sh = n_heads
    while sh < width:
        x = x + jnp.where(lane >= sh, pltpu.roll(x, sh, axis=1), 0.0)
        sh *= 2
    o_ref[...] = x


def _pool_cumsum(lf, n_heads):
    n_pool, width = lf.shape
    rows = _pick_tile(n_pool, 256)
    blk = pl.BlockSpec((rows, width), lambda i: (i, 0))
    return pl.pallas_call(
        functools.partial(_pool_cumsum_body, n_heads=n_heads, width=width),
        out_shape=jax.ShapeDtypeStruct((n_pool, width), F32),
        grid=(n_pool // rows,),
        in_specs=[blk],
        out_specs=blk,
        compiler_params=_cparams(("parallel",)),
        name="pool_cumsum",
    )(lf)


def _fox_sample_body(pt_ref, q_ref, kn_ref, vn_ref, cl_ref, g_ref, *rest,
                     npg, n_pages, page, n_heads, t_new, scale):
    k_refs = rest[0:npg]
    c_refs = rest[npg:2 * npg]
    v_refs = rest[2 * npg:3 * npg]
    o_ref = rest[3 * npg]
    qx_ref, qxb_ref, pad_ref, cpad_ref, s2_ref, carry_ref, acc_ref = rest[3 * npg + 1:]
    del pt_ref
    b_id = pl.program_id(0)
    s_id = pl.program_id(1)
    n_seq = pl.num_programs(0) - 1
    n_steps = n_pages // npg
    past = n_pages * page
    wa = n_heads * HEAD_DIM
    s_ref = s2_ref.at[b_id % 2]
    p_ref = s2_ref.at[(b_id + 1) % 2]
    k_phase = b_id < n_seq
    v_phase = b_id >= 1
    row_h = lax.broadcasted_iota(jnp.int32, (n_heads, wa), 0)
    col_h = lax.broadcasted_iota(jnp.int32, (n_heads, wa), 1) >> _log2(HEAD_DIM)
    head_diag = row_h == col_h

    def expand(c):
        out = c
        for t in range(1, t_new):
            out = out + pltpu.roll(c, t * n_heads, axis=1)
        return out

    @pl.when(k_phase & (s_id == 0))
    def _():
        qx_ref[...] = jnp.zeros_like(qx_ref)
        q = q_ref[...]
        for t in range(t_new):
            qx_ref[t * n_heads:(t + 1) * n_heads, :] = jnp.where(head_diag, q[t:t + 1, :], 0.0)
        qxb_ref[...] = qx_ref[...].astype(BF16)
        cpad_ref[...] = jnp.zeros_like(cpad_ref)
        carry_ref[...] = jnp.zeros_like(carry_ref)

    @pl.when(v_phase & (s_id == 0))
    def _():
        acc_ref[...] = jnp.zeros_like(acc_ref)

    def head_major(ref):
        return jnp.concatenate([ref[pl.ds(h, page, stride=n_heads), :] for h in range(n_heads)], axis=1)

    @pl.when(k_phase)
    def _():
        for r in range(npg):
            st = _dot_nt(head_major(k_refs[r]).astype(BF16), qxb_ref[...]) * scale
            cpad_ref[:, 0:n_heads] = c_refs[r][...]
            cx = expand(cpad_ref[...]) + carry_ref[...]
            carry_ref[...] = cx[page - 1:page, :]
            off = pl.multiple_of((s_id * npg + r) * page, page)
            s_ref[pl.ds(off, page), :] = st - cx

    @pl.when(v_phase)
    def _():
        for r in range(npg):
            off = pl.multiple_of((s_id * npg + r) * page, page)
            p = p_ref[pl.ds(off, page), :].T.astype(BF16)
            acc_ref[...] += _dot(p, head_major(v_refs[r]).astype(BF16))

    @pl.when(v_phase & (s_id == n_steps - 1))
    def _():
        pad_ref[...] = jnp.zeros_like(pad_ref)
        pad_ref[0:t_new, :] = vn_ref[...]
        p = p_ref[past:past + page, :].T.astype(BF16)
        acc = acc_ref[...] + _dot(p, pad_ref[...].astype(BF16))
        for t in range(t_new):
            blk = acc[t * n_heads:(t + 1) * n_heads, :]
            o_t = jnp.sum(jnp.where(head_diag, blk, 0.0), axis=0, keepdims=True)
            for h in range(n_heads):
                sl = slice(h * HEAD_DIM, (h + 1) * HEAD_DIM)
                o_ref[t:t + 1, sl] = _rms(o_t[:, sl], g_ref[h:h + 1, :])

    @pl.when(k_phase & (s_id == n_steps - 1))
    def _():
        pad_ref[...] = jnp.zeros_like(pad_ref)
        pad_ref[0:t_new, :] = kn_ref[...]
        st = _dot_nt(pad_ref[...].astype(BF16), qxb_ref[...]) * scale
        lane = lax.broadcasted_iota(jnp.int32, (1, LANES), 1)
        cpad_ref[...] = jnp.zeros_like(cpad_ref)
        cpad_ref[0:t_new, :] = jnp.where(lane < n_heads, cl_ref[...], 0.0)
        cl_x = expand(cpad_ref[...])
        u = lax.broadcasted_iota(jnp.int32, (page, LANES), 0)
        t = lax.broadcasted_iota(jnp.int32, (page, LANES), 1) >> _log2(n_heads)
        st = jnp.where((u < t_new) & (u <= t), st - cl_x, NEG_INF)
        s_past = s_ref[0:past, :] + carry_ref[...]
        m = jnp.maximum(jnp.max(s_past, axis=0, keepdims=True), jnp.max(st, axis=0, keepdims=True))
        p_past = jnp.exp(s_past - m)
        p_new = jnp.exp(st - m)
        inv = 1.0 / (jnp.sum(p_past, axis=0, keepdims=True) + jnp.sum(p_new, axis=0, keepdims=True))
        s_ref[0:past, :] = p_past * inv
        s_ref[past:past + page, :] = p_new * inv


def _fox_sample(page_table, q, k_new, v_new, cl, g_out, cache_k, cache_v, cache_c, page, pool_base, n_heads):
    bsz, t_new, wa = q.shape
    n_pages = page_table.shape[1]
    assert page == LANES and n_heads == SUBLANES and n_heads * t_new <= LANES
    npg = _pick_tile(n_pages, 4)
    n_steps = n_pages // npg
    scale = HEAD_DIM ** -0.5

    def k_seq(b):
        return jnp.minimum(b, bsz - 1)

    def v_seq(b):
        return jnp.maximum(b - 1, 0)

    def kv_map(seq_fn, r):
        return lambda b, s, pt: (pool_base + pt[seq_fn(b), s * npg + r], 0)

    def c_map(r):
        return lambda b, s, pt: (pt[k_seq(b), s * npg + r], 0, 0)

    k3 = lambda b, s, pt: (k_seq(b), 0, 0)
    v3 = lambda b, s, pt: (v_seq(b), 0, 0)
    in_specs = [
        pl.BlockSpec((None, t_new, wa), k3),
        pl.BlockSpec((None, t_new, wa), k3),
        pl.BlockSpec((None, t_new, wa), v3),
        pl.BlockSpec((None, t_new, LANES), k3),
        pl.BlockSpec((n_heads, HEAD_DIM), lambda b, s, pt: (0, 0)),
    ]
    in_specs += [pl.BlockSpec((page * n_heads, HEAD_DIM), kv_map(k_seq, r)) for r in range(npg)]
    in_specs += [pl.BlockSpec((None, page, n_heads), c_map(r)) for r in range(npg)]
    in_specs += [pl.BlockSpec((page * n_heads, HEAD_DIM), kv_map(v_seq, r)) for r in range(npg)]
    grid_spec = pltpu.PrefetchScalarGridSpec(
        num_scalar_prefetch=1,
        grid=(bsz + 1, n_steps),
        in_specs=in_specs,
        out_specs=pl.BlockSpec((None, t_new, wa), v3),
        scratch_shapes=[
            pltpu.VMEM((LANES, wa), F32),
            pltpu.VMEM((LANES, wa), BF16),
            pltpu.VMEM((page, wa), F32),
            pltpu.VMEM((page, LANES), F32),
            pltpu.VMEM((2, n_pages * page + page, LANES), F32),
            pltpu.VMEM((1, LANES), F32),
            pltpu.VMEM((LANES, wa), F32),
        ],
    )
    return pl.pallas_call(
        functools.partial(_fox_sample_body, npg=npg, n_pages=n_pages, page=page, n_heads=n_heads,
                          t_new=t_new, scale=scale),
        out_shape=jax.ShapeDtypeStruct((bsz, t_new, wa), F32),
        grid_spec=grid_spec,
        compiler_params=_cparams(("arbitrary", "arbitrary")),
        name="fox_sample",
    )(page_table, q, k_new, v_new, cl, g_out, *([cache_k] * npg), *([cache_c] * npg), *([cache_v] * npg))


def _bdot(a, b):
    return lax.dot_general(a, b, (((2,), (1,)), ((0,), (0,))), preferred_element_type=F32)


def _bdot_nt(a, b):
    return lax.dot_general(a, b, (((2,), (2,)), ((0,), (0,))), preferred_element_type=F32)


def _mlstm_chunk(qc, kc, vc, i_col, b_col, c_st, n_st, m_st, n_valid):
    nh, ln, _ = qc.shape
    r = lax.broadcasted_iota(jnp.int32, (ln, ln), 0)
    c = lax.broadcasted_iota(jnp.int32, (ln, ln), 1)
    eye = (r == c)[None]
    tril = (r >= c)[None]

    def col_to_row(col):
        return jnp.sum(jnp.where(eye, col, 0.0), axis=1, keepdims=True)

    i_row = col_to_row(i_col)
    b_row = col_to_row(b_col)
    a_col = b_col + m_st
    dmat = jnp.where(tril, b_col - b_row + i_row, NEG_INF)
    mt = jnp.maximum(a_col, jnp.max(dmat, axis=2, keepdims=True))
    qb = qc.astype(BF16)
    kb = kc.astype(BF16)
    vb = vc.astype(BF16)
    w_intra = jnp.exp(dmat - mt) * _bdot_nt(qb, kb)
    w_inter = jnp.exp(a_col - mt)
    num = w_inter * _bdot(qb, c_st.astype(BF16)) + _bdot(w_intra.astype(BF16), vb)
    den = w_inter * jnp.sum(qc * n_st, axis=2, keepdims=True) + jnp.sum(w_intra, axis=2, keepdims=True)
    h = num / jnp.maximum(jnp.abs(den), jnp.exp(-mt))
    last = n_valid - 1
    m_new = mt[:, last:last + 1, :]
    b_last = b_col[:, last:last + 1, :]
    w_state = jnp.exp(b_last - b_col + i_col - m_new)
    if n_valid < ln:
        row = lax.broadcasted_iota(jnp.int32, (1, ln, 1), 1)
        w_state = jnp.where(row < n_valid, w_state, 0.0)
    decay = jnp.exp(b_last + m_st - m_new)
    ks = w_state * kc
    ks_t = jnp.stack([ks[hd].T for hd in range(nh)]).astype(BF16)
    c_new = decay * c_st + _bdot(ks_t, vb)
    n_new = decay * n_st + jnp.sum(ks, axis=1, keepdims=True)
    return h, c_new, n_new, m_new


def _heads(x, n_heads, offset=0, width=HEAD_DIM):
    return jnp.stack([x[:, offset + h * width:offset + (h + 1) * width] for h in range(n_heads)])


def _mlstm_prompt_body(qk_ref, w_ref, b_ref, v_ref, o_ref, act_ref, cl_ref, g_ref,
                       y_ref, c_out_ref, n_out_ref, m_out_ref,
                       ext_ref, qk_scr, c_ref, n_ref, m_ref, *, rb, chunk, n_heads):
    blk = pl.program_id(1)
    pad = SUBLANES
    wm = n_heads * HEAD_DIM
    kscale = HEAD_DIM ** -0.5

    @pl.when(blk == 0)
    def _():
        ext_ref[0:pad, :] = jnp.zeros((pad, 2 * wm), F32)
        c_ref[...] = jnp.zeros_like(c_ref)
        n_ref[...] = jnp.zeros_like(n_ref)
        m_ref[...] = jnp.zeros_like(m_ref)

    ext_ref[pad:pad + rb, :] = qk_ref[...]
    acc = b_ref[...] + w_ref[CONV_W - 1:CONV_W, :] * qk_ref[...]
    for j in range(CONV_W - 1):
        off = pad - (CONV_W - 1) + j
        acc = acc + w_ref[j:j + 1, :] * ext_ref[off:off + rb, :]
    qk_scr[...] = acc * jax.nn.sigmoid(acc)
    ext_ref[0:pad, :] = qk_ref[rb - pad:rb, :]

    g = g_ref[...][:, None, :]
    for ci in range(rb // chunk):
        rows = slice(ci * chunk, (ci + 1) * chunk)
        qk = qk_scr[rows, :]
        hh, c_new, n_new, m_new = _mlstm_chunk(
            _heads(qk, n_heads), _heads(qk, n_heads, wm) * kscale, _heads(v_ref[rows, :], n_heads),
            _heads(act_ref[rows, :], n_heads, n_heads, 1), _heads(cl_ref[rows, :], n_heads, 2 * n_heads, 1),
            c_ref[...], n_ref[...], m_ref[:, :, 0:1], chunk)
        c_ref[...] = c_new
        n_ref[...] = n_new
        m_ref[...] = jnp.broadcast_to(m_new, m_ref.shape)
        y = jax.nn.sigmoid(_heads(o_ref[rows, :], n_heads)) * _rms(hh, g)
        for h in range(n_heads):
            y_ref[rows, h * HEAD_DIM:(h + 1) * HEAD_DIM] = y[h].astype(y_ref.dtype)

    @pl.when(blk == pl.num_programs(1) - 1)
    def _():
        c_out_ref[...] = c_ref[...]
        n_out_ref[...] = n_ref[:, 0, :]
        m_out_ref[...] = m_ref[:, 0, :]


def _mlstm_prompt(qkm, conv_w, conv_b, vm, om, act, cl, g_out, batch, seq, n_heads):
    n = qkm.shape[0]
    wm = n_heads * HEAD_DIM
    chunk = math.gcd(seq, MLSTM_CHUNK)
    rb = _pick_tile(seq, 2 * chunk)
    assert rb % chunk == 0 and rb >= SUBLANES
    nblk = seq // rb
    row = lambda b, i: (b * nblk + i, 0)
    const = lambda b, i: (0, 0)
    st = lambda b, i: (b, 0, 0)
    out_shape = (
        jax.ShapeDtypeStruct((n, wm), BF16),
        jax.ShapeDtypeStruct((batch, n_heads, HEAD_DIM, HEAD_DIM), F32),
        jax.ShapeDtypeStruct((batch, n_heads, HEAD_DIM), F32),
        jax.ShapeDtypeStruct((batch, n_heads, LANES), F32),
    )
    return pl.pallas_call(
        functools.partial(_mlstm_prompt_body, rb=rb, chunk=chunk, n_heads=n_heads),
        out_shape=out_shape,
        grid=(batch, nblk),
        in_specs=[
            pl.BlockSpec((rb, 2 * wm), row),
            pl.BlockSpec((CONV_W, 2 * wm), const),
            pl.BlockSpec((1, 2 * wm), const),
            pl.BlockSpec((rb, wm), row),
            pl.BlockSpec((rb, wm), row),
            pl.BlockSpec((rb, LANES), row),
            pl.BlockSpec((rb, LANES), row),
            pl.BlockSpec((n_heads, HEAD_DIM), const),
        ],
        out_specs=(
            pl.BlockSpec((rb, wm), row),
            pl.BlockSpec((None, n_heads, HEAD_DIM, HEAD_DIM), lambda b, i: (b, 0, 0, 0)),
            pl.BlockSpec((None, n_heads, HEAD_DIM), st),
            pl.BlockSpec((None, n_heads, LANES), st),
        ),
        scratch_shapes=[
            pltpu.VMEM((rb + SUBLANES, 2 * wm), F32),
            pltpu.VMEM((rb, 2 * wm), F32),
            pltpu.VMEM((n_heads, HEAD_DIM, HEAD_DIM), F32),
            pltpu.VMEM((n_heads, 1, HEAD_DIM), F32),
            pltpu.VMEM((n_heads, 1, LANES), F32),
        ],
        compiler_params=_cparams(("parallel", "arbitrary")),
        name="mlstm_prompt",
    )(qkm, conv_w, conv_b, vm, om, act, cl, g_out)


def _mlstm_sample_body(qk_ref, prev_ref, w_ref, b_ref, v_ref, o_ref, act_ref, cl_ref, g_ref, c0_ref, n0_ref, m0_ref,
                       y_ref, c_out_ref, n_out_ref, m_out_ref, ext_ref, pad_ref, *, t_new, n_heads):
    wm = n_heads * HEAD_DIM
    rows = SUBLANES
    kscale = HEAD_DIM ** -0.5
    ext_ref[...] = jnp.zeros_like(ext_ref)
    ext_ref[0:CONV_W - 1, :] = prev_ref[...]
    ext_ref[CONV_W - 1:CONV_W - 1 + t_new, :] = qk_ref[...]
    acc = b_ref[...] + w_ref[0:1, :] * ext_ref[0:rows, :]
    for j in range(1, CONV_W):
        acc = acc + w_ref[j:j + 1, :] * ext_ref[j:j + rows, :]
    qk = acc * jax.nn.sigmoid(acc)
    valid = lax.broadcasted_iota(jnp.int32, (rows, 1), 0) < t_new

    def padded(ref, width):
        pad_ref[:, 0:width] = jnp.zeros((rows, width), F32)
        pad_ref[0:t_new, 0:width] = ref[...]
        return pad_ref[:, 0:width]

    v = padded(v_ref, wm)
    o = padded(o_ref, wm)
    act = padded(act_ref, LANES)
    cl = padded(cl_ref, LANES)
    qk = jnp.where(valid, qk, 0.0)
    i_col = jnp.where(valid, _heads(act, n_heads, n_heads, 1), NEG_INF)
    b_col = _heads(cl, n_heads, 2 * n_heads, 1)
    b_col = jnp.where(valid, b_col, b_col[:, t_new - 1:t_new, :])
    n0 = jnp.stack([n0_ref[h:h + 1, :] for h in range(n_heads)])
    m0 = jnp.stack([m0_ref[h:h + 1, :] for h in range(n_heads)])
    hh, c_new, n_new, m_new = _mlstm_chunk(_heads(qk, n_heads), _heads(qk, n_heads, wm) * kscale, _heads(v, n_heads),
                                           i_col, b_col, c0_ref[...], n0, m0, t_new)
    y = jax.nn.sigmoid(_heads(o, n_heads)) * _rms(hh, g_ref[...][:, None, :])
    for h in range(n_heads):
        y_ref[:, h * HEAD_DIM:(h + 1) * HEAD_DIM] = y[h, 0:t_new, :]
    c_out_ref[...] = c_new
    n_out_ref[...] = n_new[:, 0, :]
    m_out_ref[...] = jnp.broadcast_to(m_new, (n_heads, 1, LANES))[:, 0, :]


def _mlstm_sample(qkm, conv_prev, conv_w, conv_b, vm, om, act, cl, g_out, c0, n0, m0, n_heads):
    bsz, t_new, wm2 = qkm.shape
    wm = wm2 // 2
    assert t_new + CONV_W - 1 <= SUBLANES
    seq3 = lambda b: (b, 0, 0)
    const = lambda b: (0, 0)
    out_shape = (
        jax.ShapeDtypeStruct((bsz, t_new, wm), F32),
        jax.ShapeDtypeStruct((bsz, n_heads, HEAD_DIM, HEAD_DIM), F32),
        jax.ShapeDtypeStruct((bsz, n_heads, HEAD_DIM), F32),
        jax.ShapeDtypeStruct((bsz, n_heads, LANES), F32),
    )
    return pl.pallas_call(
        functools.partial(_mlstm_sample_body, t_new=t_new, n_heads=n_heads),
        out_shape=out_shape,
        grid=(bsz,),
        in_specs=[
            pl.BlockSpec((None, t_new, wm2), seq3),
            pl.BlockSpec((None, CONV_W - 1, wm2), seq3),
            pl.BlockSpec((CONV_W, wm2), const),
            pl.BlockSpec((1, wm2), const),
            pl.BlockSpec((None, t_new, wm), seq3),
            pl.BlockSpec((None, t_new, wm), seq3),
            pl.BlockSpec((None, t_new, LANES), seq3),
            pl.BlockSpec((None, t_new, LANES), seq3),
            pl.BlockSpec((n_heads, HEAD_DIM), const),
            pl.BlockSpec((None, n_heads, HEAD_DIM, HEAD_DIM), lambda b: (b, 0, 0, 0)),
            pl.BlockSpec((None, n_heads, HEAD_DIM), seq3),
            pl.BlockSpec((None, n_heads, 1), seq3),
        ],
        out_specs=(
            pl.BlockSpec((None, t_new, wm), seq3),
            pl.BlockSpec((None, n_heads, HEAD_DIM, HEAD_DIM), lambda b: (b, 0, 0, 0)),
            pl.BlockSpec((None, n_heads, HEAD_DIM), seq3),
            pl.BlockSpec((None, n_heads, LANES), seq3),
        ),
        scratch_shapes=[pltpu.VMEM((2 * SUBLANES, wm2), F32), pltpu.VMEM((SUBLANES, wm), F32)],
        compiler_params=_cparams(("parallel",)),
        name="mlstm_sample",
    )(qkm, conv_prev, conv_w, conv_b, vm, om, act, cl, g_out, c0, n0, m0)


def _outproj_body(ya_ref, ym_ref, w_ref, x_ref, o_ref, *, wa):
    y = _dot(ya_ref[...].astype(BF16), w_ref[0:wa, :]) + _dot(ym_ref[...].astype(BF16), w_ref[wa:, :])
    o_ref[...] = x_ref[...] + y


def _outproj(ya, ym, w, x):
    n, d = x.shape
    wa = ya.shape[1]
    wm = ym.shape[1]
    tm = _pick_tile(n, 512)
    row = lambda i: (i, 0)
    return pl.pallas_call(
        functools.partial(_outproj_body, wa=wa),
        out_shape=jax.ShapeDtypeStruct((n, d), F32),
        grid=(n // tm,),
        in_specs=[
            pl.BlockSpec((tm, wa), row),
            pl.BlockSpec((tm, wm), row),
            pl.BlockSpec((wa + wm, d), lambda i: (0, 0)),
            pl.BlockSpec((tm, d), row),
        ],
        out_specs=pl.BlockSpec((tm, d), row),
        compiler_params=_cparams(("parallel",)),
        name="outproj",
    )(ya, ym, w, x)


def _prep_weights(lw, n_heads_a, n_heads_m):
    wa = n_heads_a * HEAD_DIM
    wm = n_heads_m * HEAD_DIM
    f = lw['w1_gate'].shape[1]
    tf = 512
    fp = tf * ((f + tf - 1) // tf)

    def ffn_w(wg, wu, wd):
        d = wg.shape[0]
        zc = jnp.zeros((d, fp - f), BF16)
        zr = jnp.zeros((fp - f, d), BF16)
        return (jnp.concatenate([wg.astype(BF16), zc], axis=1), jnp.concatenate([wu.astype(BF16), zc], axis=1),
                jnp.concatenate([wd.astype(BF16), zr], axis=0))

    w_in = lw['w_in']
    o_f = 3 * wa
    o_qk = o_f + n_heads_a
    o_i = o_qk + 2 * wm + 2 * wm
    o_fm = o_i + n_heads_m
    w_main = jnp.concatenate([w_in[:, :o_f], w_in[:, o_qk:o_i]], axis=1).astype(BF16)
    ng = 2 * n_heads_m + n_heads_a
    w_gates = jnp.concatenate(
        [w_in[:, o_f:o_qk], w_in[:, o_i:o_fm], w_in[:, o_fm:o_fm + n_heads_m],
         jnp.zeros((w_in.shape[0], LANES - ng), w_in.dtype)], axis=1).astype(BF16)
    bias = jnp.concatenate([lw['b_fox_f'], lw['b_m_i'], lw['b_m_f'], jnp.zeros((LANES - ng,), F32)])[None, :]
    return {
        'ffn1': ffn_w(lw['w1_gate'], lw['w1_up'], lw['w1_down']),
        'ffn2': ffn_w(lw['w2_gate'], lw['w2_up'], lw['w2_down']),
        'tf': tf,
        'w_main': w_main,
        'w_gates': w_gates,
        'gate_bias': bias,
        'w_out': lw['w_out'].astype(BF16),
    }


def kernel(x_prompt, x_sample, cache_k, cache_v, cache_logf, state_conv, state_C, state_n, state_m, page_table,
           g_ffn1, w1_gate, w1_up, w1_down, g_mix, w_in, b_fox_f, b_m_i, b_m_f, conv_w, conv_b, g_q, g_k,
           g_out_a, g_out_m, w_out, g_ffn2, w2_gate, w2_up, w2_down):
    depth = w_in.shape[0]
    bp, seq, d = x_prompt.shape
    bs, t_new, _ = x_sample.shape
    n_heads_a = g_out_a.shape[1]
    n_heads_m = g_out_m.shape[1]
    wa = n_heads_a * HEAD_DIM
    wm = n_heads_m * HEAD_DIM
    n_pool, page = cache_k.shape[1], cache_k.shape[2]

    yp = x_prompt.reshape(bp * seq, d)
    ys = x_sample.reshape(bs * t_new, d)
    outs = [[] for _ in range(14)]
    for l in range(depth):
        lw = {
            'w1_gate': w1_gate[l], 'w1_up': w1_up[l], 'w1_down': w1_down[l], 'w_in': w_in[l],
            'b_fox_f': b_fox_f[l], 'b_m_i': b_m_i[l], 'b_m_f': b_m_f[l], 'w_out': w_out[l],
            'w2_gate': w2_gate[l], 'w2_up': w2_up[l], 'w2_down': w2_down[l],
        }
        pw = _prep_weights(lw, n_heads_a, n_heads_m)
        gf1 = g_ffn1[l][None, :]
        gf2 = g_ffn2[l][None, :]
        gmix = g_mix[l][None, :]
        gq = g_q[l][None, :]
        gk = g_k[l][None, :]
        cw = conv_w[l]
        cb = conv_b[l][None, :]
        goa = g_out_a[l]
        gom = g_out_m[l]

        x1, xn = _ffn(yp, gf1, *pw['ffn1'], pw['tf'], g_next=gmix)
        q, k, kb, v, vb, qkm, vm, om, gt = _inproj(xn, pw['w_main'], pw['w_gates'], gq, gk, n_heads_a)
        act, cg, cl, crow = _gates(gt, pw['gate_bias'], seq, math.gcd(seq, MLSTM_CHUNK), n_heads_a)
        ya = _fox_prompt(q, kb, vb, cg, crow, goa[:, None, :], bp, seq, n_heads_a)
        ym, c_p, n_p, m_p = _mlstm_prompt(qkm, cw, cb, vm, om, act, cl, gom, bp, seq, n_heads_m)
        x2 = _outproj(ya, ym, pw['w_out'], x1)
        yp = _ffn(x2, gf2, *pw['ffn2'], pw['tf'])
        outs[0].append(k.reshape(bp, seq, n_heads_a, HEAD_DIM))
        outs[1].append(v.reshape(bp, seq, n_heads_a, HEAD_DIM))
        outs[2].append(act[:, :n_heads_a].reshape(bp, seq, n_heads_a))
        outs[3].append(qkm.reshape(bp, seq, 2 * wm)[:, seq - (CONV_W - 1):, :])
        outs[4].append(c_p)
        outs[5].append(n_p)
        outs[6].append(m_p[:, :, 0])

        x1, xn = _ffn(ys, gf1, *pw['ffn1'], pw['tf'], g_next=gmix)
        q, k, kb, v, vb, qkm, vm, om, gt = _inproj(xn, pw['w_main'], pw['w_gates'], gq, gk, n_heads_a)
        act, cg, cl, crow = _gates(gt, pw['gate_bias'], t_new, math.gcd(t_new, MLSTM_CHUNK), n_heads_a)
        r3 = lambda a: a.reshape(bs, t_new, a.shape[-1])
        cache_c = _pool_cumsum(cache_logf[l].reshape(n_pool, page * n_heads_a), n_heads_a)
        ya = _fox_sample(page_table, r3(q).astype(F32), r3(k), r3(v), r3(cg), goa,
                         cache_k.reshape(-1, HEAD_DIM), cache_v.reshape(-1, HEAD_DIM),
                         cache_c.reshape(n_pool, page, n_heads_a), page, l * n_pool, n_heads_a)
        ym, c_s, n_s, m_s = _mlstm_sample(r3(qkm), state_conv[l], cw, cb, r3(vm), r3(om), r3(act), r3(cl), gom,
                                          state_C[l], state_n[l], state_m[l][:, :, None], n_heads_m)
        x2 = _outproj(ya.reshape(bs * t_new, wa), ym.reshape(bs * t_new, wm), pw['w_out'], x1)
        ys = _ffn(x2, gf2, *pw['ffn2'], pw['tf'])
        conv_ext = jnp.concatenate([state_conv[l], r3(qkm)], axis=1)
        outs[7].append(k.reshape(bs, t_new, n_heads_a, HEAD_DIM))
        outs[8].append(v.reshape(bs, t_new, n_heads_a, HEAD_DIM))
        outs[9].append(act[:, :n_heads_a].reshape(bs, t_new, n_heads_a))
        outs[10].append(conv_ext[:, t_new:, :])
        outs[11].append(c_s)
        outs[12].append(n_s)
        outs[13].append(m_s[:, :, 0])

    return (yp.reshape(bp, seq, d), ys.reshape(bs, t_new, d)) + tuple(jnp.stack(o) for o in outs)
```

```python
import functools
import math

import jax
import jax.numpy as jnp
from jax import lax
from jax.experimental import pallas as pl
from jax.experimental.pallas import tpu as pltpu

F32 = jnp.float32
BF16 = jnp.bfloat16
HIGHEST = lax.Precision.HIGHEST

EPS = 1e-6
HEAD_DIM = 128
CONV_W = 4
MLSTM_CHUNK = 64
LANES = 128
SUBLANES = 8
VMEM_LIMIT_BYTES = 56 * 1024 * 1024
PAGE_BUFFER_BYTES = 32 * 1024 * 1024
NEG_INF = float("-inf")
LOG2E = math.log2(math.e)


def _cparams(sem):
    return pltpu.CompilerParams(dimension_semantics=sem, vmem_limit_bytes=VMEM_LIMIT_BYTES)


def _rms(x, g):
    y = x * lax.rsqrt(jnp.mean(x * x, axis=-1, keepdims=True) + EPS)
    return y * g


def _log_sigmoid(x):
    return jnp.minimum(x, 0.0) - jnp.log1p(jnp.exp(-jnp.abs(x)))


def _dot(a, b, precision=None):
    return jnp.dot(a, b, preferred_element_type=F32, precision=precision)


def _dot_nt(a, b, precision=None):
    return lax.dot_general(a, b, (((1,), (1,)), ((), ())), preferred_element_type=F32, precision=precision)


def _pick_tile(n, pref):
    t = min(n, pref)
    while n % t:
        t //= 2
    return t


def _log2(n):
    k = int(math.log2(n))
    assert 1 << k == n
    return k


def _ffn_body(x_ref, g_ref, wg_ref, wu_ref, wd_ref, gn_ref, *rest, with_next):
    if with_next:
        o_ref, on_ref, xn_ref, acc_ref = rest
    else:
        o_ref, xn_ref, acc_ref = rest
    j = pl.program_id(1)

    @pl.when(j == 0)
    def _():
        xn_ref[...] = _rms(x_ref[...], g_ref[...]).astype(BF16)
        acc_ref[...] = jnp.zeros_like(acc_ref)

    xn = xn_ref[...]
    gate = _dot(xn, wg_ref[...])
    up = _dot(xn, wu_ref[...])
    h = (gate * jax.nn.sigmoid(gate)) * up
    acc_ref[...] += _dot(h.astype(BF16), wd_ref[...])

    @pl.when(j == pl.num_programs(1) - 1)
    def _():
        y = x_ref[...] + 0.5 * acc_ref[...]
        o_ref[...] = y
        if with_next:
            on_ref[...] = _rms(y, gn_ref[...]).astype(BF16)


def _ffn(x, g, wg, wu, wd, tf, g_next=None):
    n, d = x.shape
    fp = wg.shape[1]
    tm = _pick_tile(n, 512)
    with_next = g_next is not None
    row = pl.BlockSpec((tm, d), lambda i, j: (i, 0))
    vec = pl.BlockSpec((1, d), lambda i, j: (0, 0))
    out_shape = jax.ShapeDtypeStruct((n, d), F32)
    return pl.pallas_call(
        functools.partial(_ffn_body, with_next=with_next),
        out_shape=(out_shape, jax.ShapeDtypeStruct((n, d), BF16)) if with_next else out_shape,
        grid=(n // tm, fp // tf),
        in_specs=[
            row, vec,
            pl.BlockSpec((d, tf), lambda i, j: (0, j)),
            pl.BlockSpec((d, tf), lambda i, j: (0, j)),
            pl.BlockSpec((tf, d), lambda i, j: (j, 0)),
            vec,
        ],
        out_specs=(row, row) if with_next else row,
        scratch_shapes=[pltpu.VMEM((tm, d), BF16), pltpu.VMEM((tm, d), F32)],
        compiler_params=_cparams(("parallel", "arbitrary")),
        name="ffn",
    )(x, g, wg, wu, wd, g_next if with_next else g)


def _inproj_body(xn_ref, w_ref, wgt_ref, gq_ref, gk_ref,
                 q_ref, k_ref, kb_ref, v_ref, vt_ref, qkm_ref, vm_ref, om_ref, gt_ref, *, n_heads):
    j = pl.program_id(1)
    xn = xn_ref[...]
    acc = _dot(xn, w_ref[...])
    wa = n_heads * HEAD_DIM

    @pl.when(j == 0)
    def _():
        for h in range(n_heads):
            sl = slice(h * HEAD_DIM, (h + 1) * HEAD_DIM)
            q_ref[:, sl] = _rms(acc[:, sl], gq_ref[...]).astype(BF16)
        gt_ref[...] = _dot(xn, wgt_ref[...])

    @pl.when(j == 1)
    def _():
        for h in range(n_heads):
            sl = slice(h * HEAD_DIM, (h + 1) * HEAD_DIM)
            kn = _rms(acc[:, sl], gk_ref[...])
            k_ref[:, sl] = kn
            kb_ref[:, sl] = kn.astype(BF16)

    @pl.when(j == 2)
    def _():
        v_ref[...] = acc
        vt_ref[...] = acc.T.astype(BF16)

    @pl.when(j == 3)
    def _():
        qkm_ref[:, 0:wa] = acc

    @pl.when(j == 4)
    def _():
        qkm_ref[:, wa:2 * wa] = acc

    @pl.when(j == 5)
    def _():
        vm_ref[...] = acc

    @pl.when(j == 6)
    def _():
        om_ref[...] = acc


def _inproj(xn, w_main, w_gates, gq, gk, n_heads):
    n, d = xn.shape
    wa = n_heads * HEAD_DIM
    assert w_main.shape[1] == 7 * wa
    tm = _pick_tile(n, 512)
    row = lambda i, j: (i, 0)
    const = lambda i, j: (0, 0)
    out_shape = (
        jax.ShapeDtypeStruct((n, wa), BF16),
        jax.ShapeDtypeStruct((n, wa), F32),
        jax.ShapeDtypeStruct((n, wa), BF16),
        jax.ShapeDtypeStruct((n, wa), F32),
        jax.ShapeDtypeStruct((wa, n), BF16),
        jax.ShapeDtypeStruct((n, 2 * wa), F32),
        jax.ShapeDtypeStruct((n, wa), F32),
        jax.ShapeDtypeStruct((n, wa), F32),
        jax.ShapeDtypeStruct((n, LANES), F32),
    )
    out_specs = (
        pl.BlockSpec((tm, wa), row), pl.BlockSpec((tm, wa), row), pl.BlockSpec((tm, wa), row),
        pl.BlockSpec((tm, wa), row), pl.BlockSpec((wa, tm), lambda i, j: (0, i)), pl.BlockSpec((tm, 2 * wa), row),
        pl.BlockSpec((tm, wa), row), pl.BlockSpec((tm, wa), row), pl.BlockSpec((tm, LANES), row),
    )
    return pl.pallas_call(
        functools.partial(_inproj_body, n_heads=n_heads),
        out_shape=out_shape,
        grid=(n // tm, 7),
        in_specs=[
            pl.BlockSpec((tm, d), row),
            pl.BlockSpec((d, wa), lambda i, j: (0, j)),
            pl.BlockSpec((d, LANES), const),
            pl.BlockSpec((1, HEAD_DIM), const),
            pl.BlockSpec((1, HEAD_DIM), const),
        ],
        out_specs=out_specs,
        compiler_params=_cparams(("parallel", "arbitrary")),
        name="inproj",
    )(xn, w_main, w_gates, gq, gk)


def _split3(x):
    hi = x.astype(BF16)
    r1 = x - hi.astype(F32)
    mid = r1.astype(BF16)
    lo = (r1 - mid.astype(F32)).astype(BF16)
    return hi, mid, lo


def _dot_01(m01, x):
    hi, mid, lo = _split3(x)
    return _dot(m01, hi) + _dot(m01, mid) + _dot(m01, lo)


def _gates_body(g_ref, b_ref, act_ref, cg_ref, cl_ref, *, seg, lc, rows, n_heads):
    ch = min(rows, 256)
    sg = min(seg, ch)
    lane = lax.broadcasted_iota(jnp.int32, (1, LANES), 1)
    is_logsig = (lane < n_heads) | ((lane >= 2 * n_heads) & (lane < 3 * n_heads))
    r = lax.broadcasted_iota(jnp.int32, (ch, ch), 0)
    c = lax.broadcasted_iota(jnp.int32, (ch, ch), 1)
    sh_g = _log2(sg)
    sh_l = _log2(lc)
    tri_g = jnp.where((r >= c) & ((r >> sh_g) == (c >> sh_g)), 1.0, 0.0).astype(BF16)
    tri_l = jnp.where((r >= c) & ((r >> sh_l) == (c >> sh_l)), 1.0, 0.0).astype(BF16)
    carry = jnp.zeros((1, LANES), F32)
    for ci in range(rows // ch):
        sl = slice(ci * ch, (ci + 1) * ch)
        x = g_ref[sl, :] + b_ref[...]
        act = jnp.where(is_logsig, _log_sigmoid(x), x)
        act_ref[sl, :] = act
        cg = _dot_01(tri_g, act)
        if seg > ch:
            cg = cg + carry
            carry = cg[ch - 1:ch, :]
        cg_ref[sl, :] = cg
        cl_ref[sl, :] = _dot_01(tri_l, act)


def _gates(gates, bias, seg, lc, n_heads):
    n = gates.shape[0]
    rows = seg if seg >= 256 else _pick_tile(n, 256)
    blk = pl.BlockSpec((rows, LANES), lambda b: (b, 0))
    return pl.pallas_call(
        functools.partial(_gates_body, seg=seg, lc=lc, rows=rows, n_heads=n_heads),
        out_shape=(jax.ShapeDtypeStruct((n, LANES), F32),) * 3,
        grid=(n // rows,),
        in_specs=[blk, pl.BlockSpec((1, LANES), lambda b: (0, 0))],
        out_specs=(blk, blk, blk),
        compiler_params=_cparams(("parallel",)),
        name="gates",
    )(gates, bias)


def _fox_prompt_body(q_ref, k_ref, vt_ref, c_ref, g_ref, o_ref,
                     qtail_ref, ktail_ref, m_ref, l_ref, acc_ref, *, tile, scale):
    h = pl.program_id(1)
    qi = pl.program_id(2)
    ki = pl.program_id(3)
    exp2_scale = scale * LOG2E

    @pl.when((qi == 0) & (ki == 0))
    def _():
        lane = lax.broadcasted_iota(jnp.int32, (1, LANES), 1)
        c = jnp.sum(jnp.where(lane == h, c_ref[...], 0.0), axis=1, keepdims=True) * (1.0 / scale)
        p1, p2, p3 = (p.astype(F32) for p in _split3(c))
        one = jnp.ones_like(p1)

        def tail(cols):
            t = jnp.zeros((c.shape[0], LANES), F32)
            for j, col in enumerate(cols):
                t = jnp.where(lane == j, col, t)
            return t.astype(BF16)

        qtail_ref[...] = tail((p1, p2, p3, one, one, one))
        ktail_ref[...] = tail((one, one, one, -p1, -p2, -p3))

    @pl.when(ki == 0)
    def _():
        m_ref[...] = jnp.full_like(m_ref, NEG_INF)
        l_ref[...] = jnp.zeros_like(l_ref)
        acc_ref[...] = jnp.zeros_like(acc_ref)

    def block(masked):
        q_aug = jnp.concatenate([q_ref[...], qtail_ref[pl.ds(pl.multiple_of(qi * tile, tile), tile), :]], axis=1)
        k_aug = jnp.concatenate([k_ref[...], ktail_ref[pl.ds(pl.multiple_of(ki * tile, tile), tile), :]], axis=1)
        st = _dot_nt(k_aug, q_aug)
        if masked:
            kpos = lax.broadcasted_iota(jnp.int32, (tile, tile), 0)
            qpos = lax.broadcasted_iota(jnp.int32, (tile, tile), 1)
            st = jnp.where(kpos <= qpos, st, NEG_INF)
        m_old = m_ref[...]
        m_new = jnp.maximum(m_old, jnp.max(st, axis=0, keepdims=True))
        alpha = jnp.exp2((m_old - m_new) * exp2_scale)
        p = jnp.exp2((st - m_new) * exp2_scale)
        l_ref[...] = alpha * l_ref[...] + jnp.sum(p, axis=0, keepdims=True)
        acc_ref[...] = alpha * acc_ref[...] + _dot(vt_ref[...], p.astype(BF16))
        m_ref[...] = m_new

    @pl.when(ki < qi)
    def _():
        block(False)

    @pl.when(ki == qi)
    def _():
        block(True)

    @pl.when(ki == pl.num_programs(3) - 1)
    def _():
        o = (acc_ref[...] / l_ref[...]).T
        o_ref[...] = _rms(o, g_ref[...]).astype(o_ref.dtype)


def _fox_prompt(q, kb, vt, cg, g_out, batch, seq, n_heads):
    n, wa = q.shape
    tile = _pick_tile(seq, 512)
    nt = seq // tile

    def kv_blk(qi, ki):
        return jnp.minimum(ki, qi)

    return pl.pallas_call(
        functools.partial(_fox_prompt_body, tile=tile, scale=HEAD_DIM ** -0.5),
        out_shape=jax.ShapeDtypeStruct((n, wa), BF16),
        grid=(batch, n_heads, nt, nt),
        in_specs=[
            pl.BlockSpec((tile, HEAD_DIM), lambda b, h, qi, ki: (b * nt + qi, h)),
            pl.BlockSpec((tile, HEAD_DIM), lambda b, h, qi, ki: (b * nt + kv_blk(qi, ki), h)),
            pl.BlockSpec((HEAD_DIM, tile), lambda b, h, qi, ki: (h, b * nt + kv_blk(qi, ki))),
            pl.BlockSpec((seq, LANES), lambda b, h, qi, ki: (b, 0)),
            pl.BlockSpec((None, 1, HEAD_DIM), lambda b, h, qi, ki: (h, 0, 0)),
        ],
        out_specs=pl.BlockSpec((tile, HEAD_DIM), lambda b, h, qi, ki: (b * nt + qi, h)),
        scratch_shapes=[
            pltpu.VMEM((seq, LANES), BF16),
            pltpu.VMEM((seq, LANES), BF16),
            pltpu.VMEM((1, tile), F32),
            pltpu.VMEM((1, tile), F32),
            pltpu.VMEM((HEAD_DIM, tile), F32),
        ],
        compiler_params=_cparams(("parallel", "parallel", "arbitrary", "arbitrary")),
        name="fox_prompt",
    )(q, kb, vt, cg, g_out)


def _pool_cumsum_body(x_ref, o_ref, m_ref, *, n_heads, page):
    width = page * n_heads

    @pl.when(pl.program_id(0) == 0)
    def _():
        r = lax.broadcasted_iota(jnp.int32, (width, width), 0)
        c = lax.broadcasted_iota(jnp.int32, (width, width), 1)
        same_head = (r & (n_heads - 1)) == (c >> _log2(page))
        earlier = (r >> _log2(n_heads)) <= (c & (page - 1))
        m_ref[...] = jnp.where(same_head & earlier, 1.0, 0.0).astype(BF16)

    hi, mid, lo = _split3(x_ref[...])
    m = m_ref[...]
    res = _dot(hi, m) + _dot(mid, m) + _dot(lo, m)
    for h in range(n_heads):
        o_ref[:, h, :] = res[:, h * page:(h + 1) * page]


def _pool_cumsum(lf, n_heads, page):
    n_pool, width = lf.shape
    rows = _pick_tile(n_pool, 256)
    return pl.pallas_call(
        functools.partial(_pool_cumsum_body, n_heads=n_heads, page=page),
        out_shape=jax.ShapeDtypeStruct((n_pool, n_heads, page), F32),
        grid=(n_pool // rows,),
        in_specs=[pl.BlockSpec((rows, width), lambda i: (i, 0))],
        out_specs=pl.BlockSpec((rows, n_heads, page), lambda i: (i, 0, 0)),
        scratch_shapes=[pltpu.VMEM((width, width), BF16)],
        compiler_params=_cparams(("arbitrary",)),
        name="pool_cumsum",
    )(lf)


def _fox_sample_body(pt_ref, q_ref, kn_ref, vn_ref, cl_ref, g_ref, *rest,
                     npg, n_pages, page, n_heads, t_new, scale):
    k_refs = rest[0:npg]
    c_refs = rest[npg:2 * npg]
    v_refs = rest[2 * npg:3 * npg]
    o_ref = rest[3 * npg]
    qx_ref, qxb_ref, pad_ref, cpad_ref, s2_ref, carry_ref, acc_ref = rest[3 * npg + 1:]
    del pt_ref
    b_id = pl.program_id(0)
    s_id = pl.program_id(1)
    n_seq = pl.num_programs(0) - 1
    n_steps = n_pages // npg
    past = n_pages * page
    wa = n_heads * HEAD_DIM
    s_ref = s2_ref.at[b_id % 2]
    p_ref = s2_ref.at[(b_id + 1) % 2]
    k_phase = b_id < n_seq
    v_phase = b_id >= 1
    row_h = lax.broadcasted_iota(jnp.int32, (n_heads, wa), 0)
    col_h = lax.broadcasted_iota(jnp.int32, (n_heads, wa), 1) >> _log2(HEAD_DIM)
    head_diag = row_h == col_h

    def expand(c):
        out = c
        for t in range(1, t_new):
            out = out + pltpu.roll(c, t * n_heads, axis=1)
        return out

    @pl.when(k_phase & (s_id == 0))
    def _():
        qx_ref[...] = jnp.zeros_like(qx_ref)
        q = q_ref[...]
        for t in range(t_new):
            qx_ref[t * n_heads:(t + 1) * n_heads, :] = jnp.where(head_diag, q[t:t + 1, :], 0.0)
        qxb_ref[...] = qx_ref[...].astype(BF16)
        cpad_ref[...] = jnp.zeros_like(cpad_ref)
        carry_ref[...] = jnp.zeros_like(carry_ref)

    @pl.when(v_phase & (s_id == 0))
    def _():
        acc_ref[...] = jnp.zeros_like(acc_ref)

    def head_major(ref):
        return jnp.concatenate([ref[pl.ds(h, page, stride=n_heads), :] for h in range(n_heads)], axis=1)

    @pl.when(k_phase)
    def _():
        for r in range(npg):
            st = _dot_nt(head_major(k_refs[r]).astype(BF16), qxb_ref[...]) * scale
            cpad_ref[:, 0:n_heads] = c_refs[r][...].T
            cx = expand(cpad_ref[...]) + carry_ref[...]
            carry_ref[...] = cx[page - 1:page, :]
            off = pl.multiple_of((s_id * npg + r) * page, page)
            s_ref[pl.ds(off, page), :] = st - cx

    @pl.when(v_phase)
    def _():
        for r in range(npg):
            off = pl.multiple_of((s_id * npg + r) * page, page)
            p = p_ref[pl.ds(off, page), :].T.astype(BF16)
            acc_ref[...] += _dot(p, head_major(v_refs[r]).astype(BF16))

    @pl.when(v_phase & (s_id == n_steps - 1))
    def _():
        pad_ref[...] = jnp.zeros_like(pad_ref)
        pad_ref[0:t_new, :] = vn_ref[...]
        p = p_ref[past:past + page, :].T.astype(BF16)
        acc = acc_ref[...] + _dot(p, pad_ref[...].astype(BF16))
        for t in range(t_new):
            blk = acc[t * n_heads:(t + 1) * n_heads, :]
            o_t = jnp.sum(jnp.where(head_diag, blk, 0.0), axis=0, keepdims=True)
            for h in range(n_heads):
                sl = slice(h * HEAD_DIM, (h + 1) * HEAD_DIM)
                o_ref[t:t + 1, sl] = _rms(o_t[:, sl], g_ref[h:h + 1, :])

    @pl.when(k_phase & (s_id == n_steps - 1))
    def _():
        pad_ref[...] = jnp.zeros_like(pad_ref)
        pad_ref[0:t_new, :] = kn_ref[...]
        st = _dot_nt(pad_ref[...].astype(BF16), qxb_ref[...]) * scale
        lane = lax.broadcasted_iota(jnp.int32, (1, LANES), 1)
        cpad_ref[...] = jnp.zeros_like(cpad_ref)
        cpad_ref[0:t_new, :] = jnp.where(lane < n_heads, cl_ref[...], 0.0)
        cl_x = expand(cpad_ref[...])
        u = lax.broadcasted_iota(jnp.int32, (page, LANES), 0)
        t = lax.broadcasted_iota(jnp.int32, (page, LANES), 1) >> _log2(n_heads)
        st = jnp.where((u < t_new) & (u <= t), st - cl_x, NEG_INF)
        s_past = s_ref[0:past, :] + carry_ref[...]
        m = jnp.maximum(jnp.max(s_past, axis=0, keepdims=True), jnp.max(st, axis=0, keepdims=True))
        p_past = jnp.exp(s_past - m)
        p_new = jnp.exp(st - m)
        inv = 1.0 / (jnp.sum(p_past, axis=0, keepdims=True) + jnp.sum(p_new, axis=0, keepdims=True))
        s_ref[0:past, :] = p_past * inv
        s_ref[past:past + page, :] = p_new * inv


def _fox_sample(page_table, q, k_new, v_new, cl, g_out, cache_k, cache_v, cache_c, page, pool_base, n_heads):
    bsz, t_new, wa = q.shape
    n_pages = page_table.shape[1]
    assert page == LANES and n_heads == SUBLANES and n_heads * t_new <= LANES
    page_bytes = page * wa * 4
    npg = _pick_tile(n_pages, max(1, PAGE_BUFFER_BYTES // (4 * page_bytes)))
    n_steps = n_pages // npg
    scale = HEAD_DIM ** -0.5

    def k_seq(b):
        return jnp.minimum(b, bsz - 1)

    def v_seq(b):
        return jnp.maximum(b - 1, 0)

    def kv_map(seq_fn, r):
        return lambda b, s, pt: (pool_base + pt[seq_fn(b), s * npg + r], 0)

    def c_map(r):
        return lambda b, s, pt: (pt[k_seq(b), s * npg + r], 0, 0)

    k3 = lambda b, s, pt: (k_seq(b), 0, 0)
    v3 = lambda b, s, pt: (v_seq(b), 0, 0)
    in_specs = [
        pl.BlockSpec((None, t_new, wa), k3),
        pl.BlockSpec((None, t_new, wa), k3),
        pl.BlockSpec((None, t_new, wa), v3),
        pl.BlockSpec((None, t_new, LANES), k3),
        pl.BlockSpec((n_heads, HEAD_DIM), lambda b, s, pt: (0, 0)),
    ]
    in_specs += [pl.BlockSpec((page * n_heads, HEAD_DIM), kv_map(k_seq, r)) for r in range(npg)]
    in_specs += [pl.BlockSpec((None, n_heads, page), c_map(r)) for r in range(npg)]
    in_specs += [pl.BlockSpec((page * n_heads, HEAD_DIM), kv_map(v_seq, r)) for r in range(npg)]
    grid_spec = pltpu.PrefetchScalarGridSpec(
        num_scalar_prefetch=1,
        grid=(bsz + 1, n_steps),
        in_specs=in_specs,
        out_specs=pl.BlockSpec((None, t_new, wa), v3),
        scratch_shapes=[
            pltpu.VMEM((LANES, wa), F32),
            pltpu.VMEM((LANES, wa), BF16),
            pltpu.VMEM((page, wa), F32),
            pltpu.VMEM((page, LANES), F32),
            pltpu.VMEM((2, n_pages * page + page, LANES), F32),
            pltpu.VMEM((1, LANES), F32),
            pltpu.VMEM((LANES, wa), F32),
        ],
    )
    return pl.pallas_call(
        functools.partial(_fox_sample_body, npg=npg, n_pages=n_pages, page=page, n_heads=n_heads,
                          t_new=t_new, scale=scale),
        out_shape=jax.ShapeDtypeStruct((bsz, t_new, wa), F32),
        grid_spec=grid_spec,
        compiler_params=_cparams(("arbitrary", "arbitrary")),
        name="fox_sample",
    )(page_table, q, k_new, v_new, cl, g_out, *([cache_k] * npg), *([cache_c] * npg), *([cache_v] * npg))


def _bdot(a, b):
    return lax.dot_general(a, b, (((2,), (1,)), ((0,), (0,))), preferred_element_type=F32)


def _bdot_nt(a, b):
    return lax.dot_general(a, b, (((2,), (2,)), ((0,), (0,))), preferred_element_type=F32)


def _mlstm_chunk(qc, kc, vc, i_col, b_col, c_st, n_st, m_st, n_valid):
    nh, ln, _ = qc.shape
    r = lax.broadcasted_iota(jnp.int32, (ln, ln), 0)
    c = lax.broadcasted_iota(jnp.int32, (ln, ln), 1)
    eye = (r == c)[None]
    tril = (r >= c)[None]

    def col_to_row(col):
        return jnp.sum(jnp.where(eye, col, 0.0), axis=1, keepdims=True)

    i_row = col_to_row(i_col)
    b_row = col_to_row(b_col)
    a_col = b_col + m_st
    dmat = jnp.where(tril, b_col - b_row + i_row, NEG_INF)
    mt = jnp.maximum(a_col, jnp.max(dmat, axis=2, keepdims=True))
    qb = qc.astype(BF16)
    kb = kc.astype(BF16)
    vb = vc.astype(BF16)
    w_intra = jnp.exp(dmat - mt) * _bdot_nt(qb, kb)
    w_inter = jnp.exp(a_col - mt)
    num = w_inter * _bdot(qb, c_st.astype(BF16)) + _bdot(w_intra.astype(BF16), vb)
    den = w_inter * jnp.sum(qc * n_st, axis=2, keepdims=True) + jnp.sum(w_intra, axis=2, keepdims=True)
    h = num / jnp.maximum(jnp.abs(den), jnp.exp(-mt))
    last = n_valid - 1
    m_new = mt[:, last:last + 1, :]
    b_last = b_col[:, last:last + 1, :]
    w_state = jnp.exp(b_last - b_col + i_col - m_new)
    if n_valid < ln:
        row = lax.broadcasted_iota(jnp.int32, (1, ln, 1), 1)
        w_state = jnp.where(row < n_valid, w_state, 0.0)
    decay = jnp.exp(b_last + m_st - m_new)
    ks = w_state * kc
    ks_t = jnp.stack([ks[hd].T for hd in range(nh)]).astype(BF16)
    c_new = decay * c_st + _bdot(ks_t, vb)
    n_new = decay * n_st + jnp.sum(ks, axis=1, keepdims=True)
    return h, c_new, n_new, m_new


def _heads(x, n_heads, offset=0, width=HEAD_DIM):
    return jnp.stack([x[:, offset + h * width:offset + (h + 1) * width] for h in range(n_heads)])


def _mlstm_prompt_body(qk_ref, w_ref, b_ref, v_ref, o_ref, act_ref, cl_ref, g_ref,
                       y_ref, c_out_ref, n_out_ref, m_out_ref,
                       ext_ref, qk_scr, c_ref, n_ref, m_ref, *, rb, chunk, n_heads):
    blk = pl.program_id(1)
    pad = SUBLANES
    wm = n_heads * HEAD_DIM
    kscale = HEAD_DIM ** -0.5

    @pl.when(blk == 0)
    def _():
        ext_ref[0:pad, :] = jnp.zeros((pad, 2 * wm), F32)
        c_ref[...] = jnp.zeros_like(c_ref)
        n_ref[...] = jnp.zeros_like(n_ref)
        m_ref[...] = jnp.zeros_like(m_ref)

    ext_ref[pad:pad + rb, :] = qk_ref[...]
    acc = b_ref[...] + w_ref[CONV_W - 1:CONV_W, :] * qk_ref[...]
    for j in range(CONV_W - 1):
        off = pad - (CONV_W - 1) + j
        acc = acc + w_ref[j:j + 1, :] * ext_ref[off:off + rb, :]
    qk_scr[...] = acc * jax.nn.sigmoid(acc)
    ext_ref[0:pad, :] = qk_ref[rb - pad:rb, :]

    g = g_ref[...][:, None, :]
    for ci in range(rb // chunk):
        rows = slice(ci * chunk, (ci + 1) * chunk)
        qk = qk_scr[rows, :]
        hh, c_new, n_new, m_new = _mlstm_chunk(
            _heads(qk, n_heads), _heads(qk, n_heads, wm) * kscale, _heads(v_ref[rows, :], n_heads),
            _heads(act_ref[rows, :], n_heads, n_heads, 1), _heads(cl_ref[rows, :], n_heads, 2 * n_heads, 1),
            c_ref[...], n_ref[...], m_ref[:, :, 0:1], chunk)
        c_ref[...] = c_new
        n_ref[...] = n_new
        m_ref[...] = jnp.broadcast_to(m_new, m_ref.shape)
        y = jax.nn.sigmoid(_heads(o_ref[rows, :], n_heads)) * _rms(hh, g)
        for h in range(n_heads):
            y_ref[rows, h * HEAD_DIM:(h + 1) * HEAD_DIM] = y[h].astype(y_ref.dtype)

    @pl.when(blk == pl.num_programs(1) - 1)
    def _():
        c_out_ref[...] = c_ref[...]
        n_out_ref[...] = n_ref[:, 0, :]
        m_out_ref[...] = m_ref[:, 0, :]


def _mlstm_prompt(qkm, conv_w, conv_b, vm, om, act, cl, g_out, batch, seq, n_heads):
    n = qkm.shape[0]
    wm = n_heads * HEAD_DIM
    chunk = math.gcd(seq, MLSTM_CHUNK)
    rb = _pick_tile(seq, 2 * chunk)
    assert rb % chunk == 0 and rb >= SUBLANES
    nblk = seq // rb
    row = lambda b, i: (b * nblk + i, 0)
    const = lambda b, i: (0, 0)
    st = lambda b, i: (b, 0, 0)
    out_shape = (
        jax.ShapeDtypeStruct((n, wm), BF16),
        jax.ShapeDtypeStruct((batch, n_heads, HEAD_DIM, HEAD_DIM), F32),
        jax.ShapeDtypeStruct((batch, n_heads, HEAD_DIM), F32),
        jax.ShapeDtypeStruct((batch, n_heads, LANES), F32),
    )
    return pl.pallas_call(
        functools.partial(_mlstm_prompt_body, rb=rb, chunk=chunk, n_heads=n_heads),
        out_shape=out_shape,
        grid=(batch, nblk),
        in_specs=[
            pl.BlockSpec((rb, 2 * wm), row),
            pl.BlockSpec((CONV_W, 2 * wm), const),
            pl.BlockSpec((1, 2 * wm), const),
            pl.BlockSpec((rb, wm), row),
            pl.BlockSpec((rb, wm), row),
            pl.BlockSpec((rb, LANES), row),
            pl.BlockSpec((rb, LANES), row),
            pl.BlockSpec((n_heads, HEAD_DIM), const),
        ],
        out_specs=(
            pl.BlockSpec((rb, wm), row),
            pl.BlockSpec((None, n_heads, HEAD_DIM, HEAD_DIM), lambda b, i: (b, 0, 0, 0)),
            pl.BlockSpec((None, n_heads, HEAD_DIM), st),
            pl.BlockSpec((None, n_heads, LANES), st),
        ),
        scratch_shapes=[
            pltpu.VMEM((rb + SUBLANES, 2 * wm), F32),
            pltpu.VMEM((rb, 2 * wm), F32),
            pltpu.VMEM((n_heads, HEAD_DIM, HEAD_DIM), F32),
            pltpu.VMEM((n_heads, 1, HEAD_DIM), F32),
            pltpu.VMEM((n_heads, 1, LANES), F32),
        ],
        compiler_params=_cparams(("parallel", "arbitrary")),
        name="mlstm_prompt",
    )(qkm, conv_w, conv_b, vm, om, act, cl, g_out)


def _mlstm_sample_body(qk_ref, prev_ref, w_ref, b_ref, v_ref, o_ref, act_ref, cl_ref, g_ref, c0_ref, n0_ref, m0_ref,
                       y_ref, c_out_ref, n_out_ref, m_out_ref, ext_ref, pad_ref, *, t_new, n_heads):
    wm = n_heads * HEAD_DIM
    rows = SUBLANES
    kscale = HEAD_DIM ** -0.5
    ext_ref[...] = jnp.zeros_like(ext_ref)
    ext_ref[0:CONV_W - 1, :] = prev_ref[...]
    ext_ref[CONV_W - 1:CONV_W - 1 + t_new, :] = qk_ref[...]
    acc = b_ref[...] + w_ref[0:1, :] * ext_ref[0:rows, :]
    for j in range(1, CONV_W):
        acc = acc + w_ref[j:j + 1, :] * ext_ref[j:j + rows, :]
    qk = acc * jax.nn.sigmoid(acc)
    valid = lax.broadcasted_iota(jnp.int32, (rows, 1), 0) < t_new

    def padded(ref, width):
        pad_ref[:, 0:width] = jnp.zeros((rows, width), F32)
        pad_ref[0:t_new, 0:width] = ref[...]
        return pad_ref[:, 0:width]

    v = padded(v_ref, wm)
    o = padded(o_ref, wm)
    act = padded(act_ref, LANES)
    cl = padded(cl_ref, LANES)
    qk = jnp.where(valid, qk, 0.0)
    i_col = jnp.where(valid, _heads(act, n_heads, n_heads, 1), NEG_INF)
    b_col = _heads(cl, n_heads, 2 * n_heads, 1)
    b_col = jnp.where(valid, b_col, b_col[:, t_new - 1:t_new, :])
    n0 = jnp.stack([n0_ref[h:h + 1, :] for h in range(n_heads)])
    m0 = jnp.stack([m0_ref[h:h + 1, :] for h in range(n_heads)])
    hh, c_new, n_new, m_new = _mlstm_chunk(_heads(qk, n_heads), _heads(qk, n_heads, wm) * kscale, _heads(v, n_heads),
                                           i_col, b_col, c0_ref[...], n0, m0, t_new)
    y = jax.nn.sigmoid(_heads(o, n_heads)) * _rms(hh, g_ref[...][:, None, :])
    for h in range(n_heads):
        y_ref[:, h * HEAD_DIM:(h + 1) * HEAD_DIM] = y[h, 0:t_new, :]
    c_out_ref[...] = c_new
    n_out_ref[...] = n_new[:, 0, :]
    m_out_ref[...] = jnp.broadcast_to(m_new, (n_heads, 1, LANES))[:, 0, :]


def _mlstm_sample(qkm, conv_prev, conv_w, conv_b, vm, om, act, cl, g_out, c0, n0, m0, n_heads):
    bsz, t_new, wm2 = qkm.shape
    wm = wm2 // 2
    assert t_new + CONV_W - 1 <= SUBLANES
    seq3 = lambda b: (b, 0, 0)
    const = lambda b: (0, 0)
    out_shape = (
        jax.ShapeDtypeStruct((bsz, t_new, wm), F32),
        jax.ShapeDtypeStruct((bsz, n_heads, HEAD_DIM, HEAD_DIM), F32),
        jax.ShapeDtypeStruct((bsz, n_heads, HEAD_DIM), F32),
        jax.ShapeDtypeStruct((bsz, n_heads, LANES), F32),
    )
    return pl.pallas_call(
        functools.partial(_mlstm_sample_body, t_new=t_new, n_heads=n_heads),
        out_shape=out_shape,
        grid=(bsz,),
        in_specs=[
            pl.BlockSpec((None, t_new, wm2), seq3),
            pl.BlockSpec((None, CONV_W - 1, wm2), seq3),
            pl.BlockSpec((CONV_W, wm2), const),
            pl.BlockSpec((1, wm2), const),
            pl.BlockSpec((None, t_new, wm), seq3),
            pl.BlockSpec((None, t_new, wm), seq3),
            pl.BlockSpec((None, t_new, LANES), seq3),
            pl.BlockSpec((None, t_new, LANES), seq3),
            pl.BlockSpec((n_heads, HEAD_DIM), const),
            pl.BlockSpec((None, n_heads, HEAD_DIM, HEAD_DIM), lambda b: (b, 0, 0, 0)),
            pl.BlockSpec((None, n_heads, HEAD_DIM), seq3),
            pl.BlockSpec((None, n_heads, 1), seq3),
        ],
        out_specs=(
            pl.BlockSpec((None, t_new, wm), seq3),
            pl.BlockSpec((None, n_heads, HEAD_DIM, HEAD_DIM), lambda b: (b, 0, 0, 0)),
            pl.BlockSpec((None, n_heads, HEAD_DIM), seq3),
            pl.BlockSpec((None, n_heads, LANES), seq3),
        ),
        scratch_shapes=[pltpu.VMEM((2 * SUBLANES, wm2), F32), pltpu.VMEM((SUBLANES, wm), F32)],
        compiler_params=_cparams(("parallel",)),
        name="mlstm_sample",
    )(qkm, conv_prev, conv_w, conv_b, vm, om, act, cl, g_out, c0, n0, m0)


def _outproj_body(ya_ref, ym_ref, w_ref, x_ref, o_ref, *, wa):
    y = _dot(ya_ref[...].astype(BF16), w_ref[0:wa, :]) + _dot(ym_ref[...].astype(BF16), w_ref[wa:, :])
    o_ref[...] = x_ref[...] + y


def _outproj(ya, ym, w, x):
    n, d = x.shape
    wa = ya.shape[1]
    wm = ym.shape[1]
    tm = _pick_tile(n, 512)
    row = lambda i: (i, 0)
    return pl.pallas_call(
        functools.partial(_outproj_body, wa=wa),
        out_shape=jax.ShapeDtypeStruct((n, d), F32),
        grid=(n // tm,),
        in_specs=[
            pl.BlockSpec((tm, wa), row),
            pl.BlockSpec((tm, wm), row),
            pl.BlockSpec((wa + wm, d), lambda i: (0, 0)),
            pl.BlockSpec((tm, d), row),
        ],
        out_specs=pl.BlockSpec((tm, d), row),
        compiler_params=_cparams(("parallel",)),
        name="outproj",
    )(ya, ym, w, x)


def _prep_weights(lw, n_heads_a, n_heads_m):
    wa = n_heads_a * HEAD_DIM
    wm = n_heads_m * HEAD_DIM
    f = lw['w1_gate'].shape[1]
    tf = 512
    fp = tf * ((f + tf - 1) // tf)

    def ffn_w(wg, wu, wd):
        d = wg.shape[0]
        zc = jnp.zeros((d, fp - f), BF16)
        zr = jnp.zeros((fp - f, d), BF16)
        return (jnp.concatenate([wg.astype(BF16), zc], axis=1), jnp.concatenate([wu.astype(BF16), zc], axis=1),
                jnp.concatenate([wd.astype(BF16), zr], axis=0))

    w_in = lw['w_in']
    o_f = 3 * wa
    o_qk = o_f + n_heads_a
    o_i = o_qk + 2 * wm + 2 * wm
    o_fm = o_i + n_heads_m
    w_main = jnp.concatenate([w_in[:, :o_f], w_in[:, o_qk:o_i]], axis=1).astype(BF16)
    ng = 2 * n_heads_m + n_heads_a
    w_gates = jnp.concatenate(
        [w_in[:, o_f:o_qk], w_in[:, o_i:o_fm], w_in[:, o_fm:o_fm + n_heads_m],
         jnp.zeros((w_in.shape[0], LANES - ng), w_in.dtype)], axis=1).astype(BF16)
    bias = jnp.concatenate([lw['b_fox_f'], lw['b_m_i'], lw['b_m_f'], jnp.zeros((LANES - ng,), F32)])[None, :]
    return {
        'ffn1': ffn_w(lw['w1_gate'], lw['w1_up'], lw['w1_down']),
        'ffn2': ffn_w(lw['w2_gate'], lw['w2_up'], lw['w2_down']),
        'tf': tf,
        'w_main': w_main,
        'w_gates': w_gates,
        'gate_bias': bias,
        'w_out': lw['w_out'].astype(BF16),
    }


def kernel(x_prompt, x_sample, cache_k, cache_v, cache_logf, state_conv, state_C, state_n, state_m, page_table,
           g_ffn1, w1_gate, w1_up, w1_down, g_mix, w_in, b_fox_f, b_m_i, b_m_f, conv_w, conv_b, g_q, g_k,
           g_out_a, g_out_m, w_out, g_ffn2, w2_gate, w2_up, w2_down):
    depth = w_in.shape[0]
    bp, seq, d = x_prompt.shape
    bs, t_new, _ = x_sample.shape
    n_heads_a = g_out_a.shape[1]
    n_heads_m = g_out_m.shape[1]
    wa = n_heads_a * HEAD_DIM
    wm = n_heads_m * HEAD_DIM
    n_pool, page = cache_k.shape[1], cache_k.shape[2]

    yp = x_prompt.reshape(bp * seq, d)
    ys = x_sample.reshape(bs * t_new, d)
    outs = [[] for _ in range(14)]
    for l in range(depth):
        lw = {
            'w1_gate': w1_gate[l], 'w1_up': w1_up[l], 'w1_down': w1_down[l], 'w_in': w_in[l],
            'b_fox_f': b_fox_f[l], 'b_m_i': b_m_i[l], 'b_m_f': b_m_f[l], 'w_out': w_out[l],
            'w2_gate': w2_gate[l], 'w2_up': w2_up[l], 'w2_down': w2_down[l],
        }
        pw = _prep_weights(lw, n_heads_a, n_heads_m)
        gf1 = g_ffn1[l][None, :]
        gf2 = g_ffn2[l][None, :]
        gmix = g_mix[l][None, :]
        gq = g_q[l][None, :]
        gk = g_k[l][None, :]
        cw = conv_w[l]
        cb = conv_b[l][None, :]
        goa = g_out_a[l]
        gom = g_out_m[l]

        x1, xn = _ffn(yp, gf1, *pw['ffn1'], pw['tf'], g_next=gmix)
        q, k, kb, v, vt, qkm, vm, om, gt = _inproj(xn, pw['w_main'], pw['w_gates'], gq, gk, n_heads_a)
        act, cg, cl = _gates(gt, pw['gate_bias'], seq, math.gcd(seq, MLSTM_CHUNK), n_heads_a)
        ya = _fox_prompt(q, kb, vt, cg, goa[:, None, :], bp, seq, n_heads_a)
        ym, c_p, n_p, m_p = _mlstm_prompt(qkm, cw, cb, vm, om, act, cl, gom, bp, seq, n_heads_m)
        x2 = _outproj(ya, ym, pw['w_out'], x1)
        yp = _ffn(x2, gf2, *pw['ffn2'], pw['tf'])
        outs[0].append(k.reshape(bp, seq, n_heads_a, HEAD_DIM))
        outs[1].append(v.reshape(bp, seq, n_heads_a, HEAD_DIM))
        outs[2].append(act[:, :n_heads_a].reshape(bp, seq, n_heads_a))
        outs[3].append(qkm.reshape(bp, seq, 2 * wm)[:, seq - (CONV_W - 1):, :])
        outs[4].append(c_p)
        outs[5].append(n_p)
        outs[6].append(m_p[:, :, 0])

        x1, xn = _ffn(ys, gf1, *pw['ffn1'], pw['tf'], g_next=gmix)
        q, k, kb, v, vt, qkm, vm, om, gt = _inproj(xn, pw['w_main'], pw['w_gates'], gq, gk, n_heads_a)
        act, cg, cl = _gates(gt, pw['gate_bias'], t_new, math.gcd(t_new, MLSTM_CHUNK), n_heads_a)
        r3 = lambda a: a.reshape(bs, t_new, a.shape[-1])
        cache_c = _pool_cumsum(cache_logf[l].reshape(n_pool, page * n_heads_a), n_heads_a, page)
        ya = _fox_sample(page_table, r3(q).astype(F32), r3(k), r3(v), r3(cg), goa,
                         cache_k.reshape(-1, HEAD_DIM), cache_v.reshape(-1, HEAD_DIM),
                         cache_c, page, l * n_pool, n_heads_a)
        ym, c_s, n_s, m_s = _mlstm_sample(r3(qkm), state_conv[l], cw, cb, r3(vm), r3(om), r3(act), r3(cl), gom,
                                          state_C[l], state_n[l], state_m[l][:, :, None], n_heads_m)
        x2 = _outproj(ya.reshape(bs * t_new, wa), ym.reshape(bs * t_new, wm), pw['w_out'], x1)
        ys = _ffn(x2, gf2, *pw['ffn2'], pw['tf'])
        conv_ext = jnp.concatenate([state_conv[l], r3(qkm)], axis=1)
        outs[7].append(k.reshape(bs, t_new, n_heads_a, HEAD_DIM))
        outs[8].append(v.reshape(bs, t_new, n_heads_a, HEAD_DIM))
        outs[9].append(act[:, :n_heads_a].reshape(bs, t_new, n_heads_a))
        outs[10].append(conv_ext[:, t_new:, :])
        outs[11].append(c_s)
        outs[12].append(n_s)
        outs[13].append(m_s[:, :, 0])

    return (yp.reshape(bp, seq, d), ys.reshape(bs, t_new, d)) + tuple(jnp.stack(o) for o in outs)
```

```python
import functools
import math

import jax
import jax.numpy as jnp
from jax import lax
from jax.experimental import pallas as pl
from jax.experimental.pallas import tpu as pltpu

F32 = jnp.float32
BF16 = jnp.bfloat16
HIGHEST = lax.Precision.HIGHEST

EPS = 1e-6
HEAD_DIM = 128
CONV_W = 4
MLSTM_CHUNK = 128
LANES = 128
SUBLANES = 8
VMEM_LIMIT_BYTES = 56 * 1024 * 1024
PAGE_BUFFER_BYTES = 32 * 1024 * 1024
NEG_INF = float("-inf")
LOG2E = math.log2(math.e)


def _cparams(sem):
    return pltpu.CompilerParams(dimension_semantics=sem, vmem_limit_bytes=VMEM_LIMIT_BYTES)


def _rms(x, g):
    y = x * lax.rsqrt(jnp.mean(x * x, axis=-1, keepdims=True) + EPS)
    return y * g


def _log_sigmoid(x):
    return jnp.minimum(x, 0.0) - jnp.log1p(jnp.exp(-jnp.abs(x)))


def _dot(a, b, precision=None):
    return jnp.dot(a, b, preferred_element_type=F32, precision=precision)


def _dot_nt(a, b, precision=None):
    return lax.dot_general(a, b, (((1,), (1,)), ((), ())), preferred_element_type=F32, precision=precision)


def _pick_tile(n, pref):
    t = min(n, pref)
    while n % t:
        t //= 2
    return t


def _log2(n):
    k = int(math.log2(n))
    assert 1 << k == n
    return k


def _ffn_body(x_ref, g_ref, wg_ref, wu_ref, wd_ref, gn_ref, *rest, with_next):
    if with_next:
        o_ref, on_ref, xn_ref, acc_ref = rest
    else:
        o_ref, xn_ref, acc_ref = rest
    j = pl.program_id(1)

    @pl.when(j == 0)
    def _():
        xn_ref[...] = _rms(x_ref[...], g_ref[...]).astype(BF16)
        acc_ref[...] = jnp.zeros_like(acc_ref)

    xn = xn_ref[...]
    gate = _dot(xn, wg_ref[...])
    up = _dot(xn, wu_ref[...])
    h = (gate * jax.nn.sigmoid(gate)) * up
    acc_ref[...] += _dot(h.astype(BF16), wd_ref[...])

    @pl.when(j == pl.num_programs(1) - 1)
    def _():
        y = x_ref[...] + 0.5 * acc_ref[...]
        o_ref[...] = y
        if with_next:
            on_ref[...] = _rms(y, gn_ref[...]).astype(BF16)


def _ffn(x, g, wg, wu, wd, tf, g_next=None):
    n, d = x.shape
    fp = wg.shape[1]
    tm = _pick_tile(n, 512)
    with_next = g_next is not None
    row = pl.BlockSpec((tm, d), lambda i, j: (i, 0))
    vec = pl.BlockSpec((1, d), lambda i, j: (0, 0))
    out_shape = jax.ShapeDtypeStruct((n, d), F32)
    return pl.pallas_call(
        functools.partial(_ffn_body, with_next=with_next),
        out_shape=(out_shape, jax.ShapeDtypeStruct((n, d), BF16)) if with_next else out_shape,
        grid=(n // tm, fp // tf),
        in_specs=[
            row, vec,
            pl.BlockSpec((d, tf), lambda i, j: (0, j)),
            pl.BlockSpec((d, tf), lambda i, j: (0, j)),
            pl.BlockSpec((tf, d), lambda i, j: (j, 0)),
            vec,
        ],
        out_specs=(row, row) if with_next else row,
        scratch_shapes=[pltpu.VMEM((tm, d), BF16), pltpu.VMEM((tm, d), F32)],
        compiler_params=_cparams(("parallel", "arbitrary")),
        name="ffn",
    )(x, g, wg, wu, wd, g_next if with_next else g)


def _inproj_body(xn_ref, w_ref, wgt_ref, gq_ref, gk_ref,
                 q_ref, k_ref, kb_ref, v_ref, vt_ref, qkm_ref, vm_ref, om_ref, gt_ref, *, n_heads):
    j = pl.program_id(0)
    xn = xn_ref[...]
    acc = _dot(xn, w_ref[...])

    @pl.when(j == 0)
    def _():
        for h in range(n_heads):
            sl = slice(h * HEAD_DIM, (h + 1) * HEAD_DIM)
            q_ref[:, sl] = _rms(acc[:, sl], gq_ref[...]).astype(BF16)
        gt_ref[...] = _dot(xn, wgt_ref[...])

    @pl.when(j == 1)
    def _():
        for h in range(n_heads):
            sl = slice(h * HEAD_DIM, (h + 1) * HEAD_DIM)
            kn = _rms(acc[:, sl], gk_ref[...])
            k_ref[:, sl] = kn
            kb_ref[:, sl] = kn.astype(BF16)

    @pl.when(j == 2)
    def _():
        v_ref[...] = acc
        vt_ref[...] = acc.T.astype(BF16)

    @pl.when((j == 3) | (j == 4))
    def _():
        qkm_ref[...] = acc

    @pl.when(j == 5)
    def _():
        vm_ref[...] = acc

    @pl.when(j == 6)
    def _():
        om_ref[...] = acc


def _inproj(xn, w_main, w_gates, gq, gk, n_heads):
    n, d = xn.shape
    wa = n_heads * HEAD_DIM
    assert w_main.shape[1] == 7 * wa
    tm = _pick_tile(n, 512)
    ni = n // tm
    const = lambda j, i: (0, 0)

    def rows_at(j, i, j0, j1):
        return jnp.where(j < j0, 0, jnp.where(j > j1, ni - 1, i))

    def row_map(j0, j1=None):
        j1 = j0 if j1 is None else j1
        return lambda j, i: (rows_at(j, i, j0, j1), 0)

    out_shape = (
        jax.ShapeDtypeStruct((n, wa), BF16),
        jax.ShapeDtypeStruct((n, wa), F32),
        jax.ShapeDtypeStruct((n, wa), BF16),
        jax.ShapeDtypeStruct((n, wa), F32),
        jax.ShapeDtypeStruct((wa, n), BF16),
        jax.ShapeDtypeStruct((n, 2 * wa), F32),
        jax.ShapeDtypeStruct((n, wa), F32),
        jax.ShapeDtypeStruct((n, wa), F32),
        jax.ShapeDtypeStruct((n, LANES), F32),
    )
    out_specs = (
        pl.BlockSpec((tm, wa), row_map(0)),
        pl.BlockSpec((tm, wa), row_map(1)),
        pl.BlockSpec((tm, wa), row_map(1)),
        pl.BlockSpec((tm, wa), row_map(2)),
        pl.BlockSpec((wa, tm), lambda j, i: (0, rows_at(j, i, 2, 2))),
        pl.BlockSpec((tm, wa), lambda j, i: (rows_at(j, i, 3, 4), jnp.clip(j - 3, 0, 1))),
        pl.BlockSpec((tm, wa), row_map(5)),
        pl.BlockSpec((tm, wa), row_map(6)),
        pl.BlockSpec((tm, LANES), row_map(0)),
    )
    return pl.pallas_call(
        functools.partial(_inproj_body, n_heads=n_heads),
        out_shape=out_shape,
        grid=(7, ni),
        in_specs=[
            pl.BlockSpec((tm, d), lambda j, i: (i, 0)),
            pl.BlockSpec((d, wa), lambda j, i: (0, j)),
            pl.BlockSpec((d, LANES), const),
            pl.BlockSpec((1, HEAD_DIM), const),
            pl.BlockSpec((1, HEAD_DIM), const),
        ],
        out_specs=out_specs,
        compiler_params=_cparams(("arbitrary", "arbitrary")),
        name="inproj",
    )(xn, w_main, w_gates, gq, gk)


def _split3(x):
    hi = x.astype(BF16)
    r1 = x - hi.astype(F32)
    mid = r1.astype(BF16)
    lo = (r1 - mid.astype(F32)).astype(BF16)
    return hi, mid, lo


def _dot_01(m01, x):
    hi, mid, lo = _split3(x)
    return _dot(m01, hi) + _dot(m01, mid) + _dot(m01, lo)


def _gates_body(g_ref, b_ref, act_ref, cg_ref, cl_ref, *, seg, lc, rows, n_heads):
    ch = min(rows, 256)
    sg = min(seg, ch)
    lane = lax.broadcasted_iota(jnp.int32, (1, LANES), 1)
    is_logsig = (lane < n_heads) | ((lane >= 2 * n_heads) & (lane < 3 * n_heads))
    r = lax.broadcasted_iota(jnp.int32, (ch, ch), 0)
    c = lax.broadcasted_iota(jnp.int32, (ch, ch), 1)
    sh_g = _log2(sg)
    sh_l = _log2(lc)
    tri_g = jnp.where((r >= c) & ((r >> sh_g) == (c >> sh_g)), 1.0, 0.0).astype(BF16)
    tri_l = jnp.where((r >= c) & ((r >> sh_l) == (c >> sh_l)), 1.0, 0.0).astype(BF16)
    carry = jnp.zeros((1, LANES), F32)
    for ci in range(rows // ch):
        sl = slice(ci * ch, (ci + 1) * ch)
        x = g_ref[sl, :] + b_ref[...]
        act = jnp.where(is_logsig, _log_sigmoid(x), x)
        act_ref[sl, :] = act
        cg = _dot_01(tri_g, act)
        if seg > ch:
            cg = cg + carry
            carry = cg[ch - 1:ch, :]
        cg_ref[sl, :] = cg
        cl_ref[sl, :] = _dot_01(tri_l, act)


def _gates(gates, bias, seg, lc, n_heads):
    n = gates.shape[0]
    rows = seg if seg >= 256 else _pick_tile(n, 256)
    blk = pl.BlockSpec((rows, LANES), lambda b: (b, 0))
    return pl.pallas_call(
        functools.partial(_gates_body, seg=seg, lc=lc, rows=rows, n_heads=n_heads),
        out_shape=(jax.ShapeDtypeStruct((n, LANES), F32),) * 3,
        grid=(n // rows,),
        in_specs=[blk, pl.BlockSpec((1, LANES), lambda b: (0, 0))],
        out_specs=(blk, blk, blk),
        compiler_params=_cparams(("parallel",)),
        name="gates",
    )(gates, bias)


def _fox_prompt_body(q_ref, k_ref, vt_ref, c_ref, g_ref, o_ref,
                     qtail_ref, ktail_ref, m_ref, l_ref, acc_ref, *, tile, scale):
    h = pl.program_id(1)
    qi = pl.program_id(2)
    ki = pl.program_id(3)
    exp2_scale = scale * LOG2E

    @pl.when((qi == 0) & (ki == 0))
    def _():
        lane = lax.broadcasted_iota(jnp.int32, (1, LANES), 1)
        c = jnp.sum(jnp.where(lane == h, c_ref[...], 0.0), axis=1, keepdims=True) * (1.0 / scale)
        p1, p2, p3 = (p.astype(F32) for p in _split3(c))
        one = jnp.ones_like(p1)

        def tail(cols):
            t = jnp.zeros((c.shape[0], LANES), F32)
            for j, col in enumerate(cols):
                t = jnp.where(lane == j, col, t)
            return t.astype(BF16)

        qtail_ref[...] = tail((p1, p2, p3, one, one, one))
        ktail_ref[...] = tail((one, one, one, -p1, -p2, -p3))

    @pl.when(ki == 0)
    def _():
        m_ref[...] = jnp.full_like(m_ref, NEG_INF)
        l_ref[...] = jnp.zeros_like(l_ref)
        acc_ref[...] = jnp.zeros_like(acc_ref)

    def block(masked):
        q_aug = jnp.concatenate([q_ref[...], qtail_ref[pl.ds(pl.multiple_of(qi * tile, tile), tile), :]], axis=1)
        k_aug = jnp.concatenate([k_ref[...], ktail_ref[pl.ds(pl.multiple_of(ki * tile, tile), tile), :]], axis=1)
        st = _dot_nt(k_aug, q_aug)
        if masked:
            kpos = lax.broadcasted_iota(jnp.int32, (tile, tile), 0)
            qpos = lax.broadcasted_iota(jnp.int32, (tile, tile), 1)
            st = jnp.where(kpos <= qpos, st, NEG_INF)
        m_old = m_ref[...]
        m_new = jnp.maximum(m_old, jnp.max(st, axis=0, keepdims=True))
        alpha = jnp.exp2((m_old - m_new) * exp2_scale)
        p = jnp.exp2((st - m_new) * exp2_scale)
        l_ref[...] = alpha * l_ref[...] + jnp.sum(p, axis=0, keepdims=True)
        acc_ref[...] = alpha * acc_ref[...] + _dot(vt_ref[...], p.astype(BF16))
        m_ref[...] = m_new

    @pl.when(ki < qi)
    def _():
        block(False)

    @pl.when(ki == qi)
    def _():
        block(True)

    @pl.when(ki == pl.num_programs(3) - 1)
    def _():
        o = (acc_ref[...] / l_ref[...]).T
        o_ref[...] = _rms(o, g_ref[...]).astype(o_ref.dtype)


def _fox_prompt(q, kb, vt, cg, g_out, batch, seq, n_heads):
    n, wa = q.shape
    tile = _pick_tile(seq, 512)
    nt = seq // tile

    def kv_blk(qi, ki):
        return jnp.minimum(ki, qi)

    return pl.pallas_call(
        functools.partial(_fox_prompt_body, tile=tile, scale=HEAD_DIM ** -0.5),
        out_shape=jax.ShapeDtypeStruct((n, wa), BF16),
        grid=(batch, n_heads, nt, nt),
        in_specs=[
            pl.BlockSpec((tile, HEAD_DIM), lambda b, h, qi, ki: (b * nt + qi, h)),
            pl.BlockSpec((tile, HEAD_DIM), lambda b, h, qi, ki: (b * nt + kv_blk(qi, ki), h)),
            pl.BlockSpec((HEAD_DIM, tile), lambda b, h, qi, ki: (h, b * nt + kv_blk(qi, ki))),
            pl.BlockSpec((seq, LANES), lambda b, h, qi, ki: (b, 0)),
            pl.BlockSpec((None, 1, HEAD_DIM), lambda b, h, qi, ki: (h, 0, 0)),
        ],
        out_specs=pl.BlockSpec((tile, HEAD_DIM), lambda b, h, qi, ki: (b * nt + qi, h)),
        scratch_shapes=[
            pltpu.VMEM((seq, LANES), BF16),
            pltpu.VMEM((seq, LANES), BF16),
            pltpu.VMEM((1, tile), F32),
            pltpu.VMEM((1, tile), F32),
            pltpu.VMEM((HEAD_DIM, tile), F32),
        ],
        compiler_params=_cparams(("parallel", "parallel", "arbitrary", "arbitrary")),
        name="fox_prompt",
    )(q, kb, vt, cg, g_out)


def _pool_cumsum_body(x_ref, o_ref, m_ref, *, n_heads, page):
    width = page * n_heads

    @pl.when(pl.program_id(0) == 0)
    def _():
        r = lax.broadcasted_iota(jnp.int32, (width, width), 0)
        c = lax.broadcasted_iota(jnp.int32, (width, width), 1)
        same_head = (r & (n_heads - 1)) == (c >> _log2(page))
        earlier = (r >> _log2(n_heads)) <= (c & (page - 1))
        m_ref[...] = jnp.where(same_head & earlier, 1.0, 0.0).astype(BF16)

    hi, mid, lo = _split3(x_ref[...])
    m = m_ref[...]
    res = _dot(hi, m) + _dot(mid, m) + _dot(lo, m)
    for h in range(n_heads):
        o_ref[:, h, :] = res[:, h * page:(h + 1) * page]


def _pool_cumsum(lf, n_heads, page):
    n_pool, width = lf.shape
    rows = _pick_tile(n_pool, 256)
    return pl.pallas_call(
        functools.partial(_pool_cumsum_body, n_heads=n_heads, page=page),
        out_shape=jax.ShapeDtypeStruct((n_pool, n_heads, page), F32),
        grid=(n_pool // rows,),
        in_specs=[pl.BlockSpec((rows, width), lambda i: (i, 0))],
        out_specs=pl.BlockSpec((rows, n_heads, page), lambda i: (i, 0, 0)),
        scratch_shapes=[pltpu.VMEM((width, width), BF16)],
        compiler_params=_cparams(("arbitrary",)),
        name="pool_cumsum",
    )(lf)


def _fox_sample_body(pt_ref, q_ref, kn_ref, vn_ref, cl_ref, g_ref, *rest,
                     npg, n_pages, page, n_heads, t_new, scale):
    k_refs = rest[0:npg]
    c_refs = rest[npg:2 * npg]
    v_refs = rest[2 * npg:3 * npg]
    o_ref = rest[3 * npg]
    qx_ref, qxb_ref, pad_ref, cpad_ref, s2_ref, carry_ref, acc_ref = rest[3 * npg + 1:]
    del pt_ref
    b_id = pl.program_id(0)
    s_id = pl.program_id(1)
    n_seq = pl.num_programs(0) - 1
    n_steps = n_pages // npg
    past = n_pages * page
    wa = n_heads * HEAD_DIM
    s_ref = s2_ref.at[b_id % 2]
    p_ref = s2_ref.at[(b_id + 1) % 2]
    k_phase = b_id < n_seq
    v_phase = b_id >= 1
    row_h = lax.broadcasted_iota(jnp.int32, (n_heads, wa), 0)
    col_h = lax.broadcasted_iota(jnp.int32, (n_heads, wa), 1) >> _log2(HEAD_DIM)
    head_diag = row_h == col_h

    def expand(c):
        out = c
        for t in range(1, t_new):
            out = out + pltpu.roll(c, t * n_heads, axis=1)
        return out

    @pl.when(k_phase & (s_id == 0))
    def _():
        qx_ref[...] = jnp.zeros_like(qx_ref)
        q = q_ref[...]
        for t in range(t_new):
            qx_ref[t * n_heads:(t + 1) * n_heads, :] = jnp.where(head_diag, q[t:t + 1, :], 0.0)
        qxb_ref[...] = qx_ref[...].astype(BF16)
        cpad_ref[...] = jnp.zeros_like(cpad_ref)
        carry_ref[...] = jnp.zeros_like(carry_ref)

    @pl.when(v_phase & (s_id == 0))
    def _():
        acc_ref[...] = jnp.zeros_like(acc_ref)

    def head_major(ref):
        return jnp.concatenate([ref[pl.ds(h, page, stride=n_heads), :] for h in range(n_heads)], axis=1)

    @pl.when(k_phase)
    def _():
        for r in range(npg):
            st = _dot_nt(head_major(k_refs[r]).astype(BF16), qxb_ref[...]) * scale
            cpad_ref[:, 0:n_heads] = c_refs[r][...].T
            cx = expand(cpad_ref[...]) + carry_ref[...]
            carry_ref[...] = cx[page - 1:page, :]
            off = pl.multiple_of((s_id * npg + r) * page, page)
            s_ref[pl.ds(off, page), :] = st - cx

    @pl.when(v_phase)
    def _():
        for r in range(npg):
            off = pl.multiple_of((s_id * npg + r) * page, page)
            p = p_ref[pl.ds(off, page), :].T.astype(BF16)
            acc_ref[...] += _dot(p, head_major(v_refs[r]).astype(BF16))

    @pl.when(v_phase & (s_id == n_steps - 1))
    def _():
        pad_ref[...] = jnp.zeros_like(pad_ref)
        pad_ref[0:t_new, :] = vn_ref[...]
        p = p_ref[past:past + page, :].T.astype(BF16)
        acc = acc_ref[...] + _dot(p, pad_ref[...].astype(BF16))
        for t in range(t_new):
            blk = acc[t * n_heads:(t + 1) * n_heads, :]
            o_t = jnp.sum(jnp.where(head_diag, blk, 0.0), axis=0, keepdims=True)
            for h in range(n_heads):
                sl = slice(h * HEAD_DIM, (h + 1) * HEAD_DIM)
                o_ref[t:t + 1, sl] = _rms(o_t[:, sl], g_ref[h:h + 1, :])

    @pl.when(k_phase & (s_id == n_steps - 1))
    def _():
        pad_ref[...] = jnp.zeros_like(pad_ref)
        pad_ref[0:t_new, :] = kn_ref[...]
        st = _dot_nt(pad_ref[...].astype(BF16), qxb_ref[...]) * scale
        lane = lax.broadcasted_iota(jnp.int32, (1, LANES), 1)
        cpad_ref[...] = jnp.zeros_like(cpad_ref)
        cpad_ref[0:t_new, :] = jnp.where(lane < n_heads, cl_ref[...], 0.0)
        cl_x = expand(cpad_ref[...])
        u = lax.broadcasted_iota(jnp.int32, (page, LANES), 0)
        t = lax.broadcasted_iota(jnp.int32, (page, LANES), 1) >> _log2(n_heads)
        st = jnp.where((u < t_new) & (u <= t), st - cl_x, NEG_INF)
        s_past = s_ref[0:past, :] + carry_ref[...]
        m = jnp.maximum(jnp.max(s_past, axis=0, keepdims=True), jnp.max(st, axis=0, keepdims=True))
        p_past = jnp.exp(s_past - m)
        p_new = jnp.exp(st - m)
        inv = 1.0 / (jnp.sum(p_past, axis=0, keepdims=True) + jnp.sum(p_new, axis=0, keepdims=True))
        s_ref[0:past, :] = p_past * inv
        s_ref[past:past + page, :] = p_new * inv


def _fox_sample(page_table, q, k_new, v_new, cl, g_out, cache_k, cache_v, cache_c, page, pool_base, n_heads):
    bsz, t_new, wa = q.shape
    n_pages = page_table.shape[1]
    assert page == LANES and n_heads == SUBLANES and n_heads * t_new <= LANES
    page_bytes = page * wa * 4
    npg = _pick_tile(n_pages, max(1, PAGE_BUFFER_BYTES // (4 * page_bytes)))
    n_steps = n_pages // npg
    scale = HEAD_DIM ** -0.5

    def k_seq(b):
        return jnp.minimum(b, bsz - 1)

    def v_seq(b):
        return jnp.maximum(b - 1, 0)

    def kv_map(seq_fn, r):
        return lambda b, s, pt: (pool_base + pt[seq_fn(b), s * npg + r], 0)

    def c_map(r):
        return lambda b, s, pt: (pt[k_seq(b), s * npg + r], 0, 0)

    k3 = lambda b, s, pt: (k_seq(b), 0, 0)
    v3 = lambda b, s, pt: (v_seq(b), 0, 0)
    in_specs = [
        pl.BlockSpec((None, t_new, wa), k3),
        pl.BlockSpec((None, t_new, wa), k3),
        pl.BlockSpec((None, t_new, wa), v3),
        pl.BlockSpec((None, t_new, LANES), k3),
        pl.BlockSpec((n_heads, HEAD_DIM), lambda b, s, pt: (0, 0)),
    ]
    in_specs += [pl.BlockSpec((page * n_heads, HEAD_DIM), kv_map(k_seq, r)) for r in range(npg)]
    in_specs += [pl.BlockSpec((None, n_heads, page), c_map(r)) for r in range(npg)]
    in_specs += [pl.BlockSpec((page * n_heads, HEAD_DIM), kv_map(v_seq, r)) for r in range(npg)]
    grid_spec = pltpu.PrefetchScalarGridSpec(
        num_scalar_prefetch=1,
        grid=(bsz + 1, n_steps),
        in_specs=in_specs,
        out_specs=pl.BlockSpec((None, t_new, wa), v3),
        scratch_shapes=[
            pltpu.VMEM((LANES, wa), F32),
            pltpu.VMEM((LANES, wa), BF16),
            pltpu.VMEM((page, wa), F32),
            pltpu.VMEM((page, LANES), F32),
            pltpu.VMEM((2, n_pages * page + page, LANES), F32),
            pltpu.VMEM((1, LANES), F32),
            pltpu.VMEM((LANES, wa), F32),
        ],
    )
    return pl.pallas_call(
        functools.partial(_fox_sample_body, npg=npg, n_pages=n_pages, page=page, n_heads=n_heads,
                          t_new=t_new, scale=scale),
        out_shape=jax.ShapeDtypeStruct((bsz, t_new, wa), F32),
        grid_spec=grid_spec,
        compiler_params=_cparams(("arbitrary", "arbitrary")),
        name="fox_sample",
    )(page_table, q, k_new, v_new, cl, g_out, *([cache_k] * npg), *([cache_c] * npg), *([cache_v] * npg))


def _bdot(a, b):
    return lax.dot_general(a, b, (((2,), (1,)), ((0,), (0,))), preferred_element_type=F32)


def _bdot_nt(a, b):
    return lax.dot_general(a, b, (((2,), (2,)), ((0,), (0,))), preferred_element_type=F32)


def _mlstm_chunk(qc, kc, vc, i_col, b_col, c_st, n_st, m_st, n_valid):
    nh, ln, _ = qc.shape
    r = lax.broadcasted_iota(jnp.int32, (ln, ln), 0)
    c = lax.broadcasted_iota(jnp.int32, (ln, ln), 1)
    eye = (r == c)[None]
    tril = (r >= c)[None]

    def col_to_row(col):
        return jnp.sum(jnp.where(eye, col, 0.0), axis=1, keepdims=True)

    i_row = col_to_row(i_col)
    b_row = col_to_row(b_col)
    a_col = b_col + m_st
    dmat = jnp.where(tril, b_col - b_row + i_row, NEG_INF)
    mt = jnp.maximum(a_col, jnp.max(dmat, axis=2, keepdims=True))
    qb = qc.astype(BF16)
    kb = kc.astype(BF16)
    vb = vc.astype(BF16)
    w_intra = jnp.exp(dmat - mt) * _bdot_nt(qb, kb)
    w_inter = jnp.exp(a_col - mt)
    num = w_inter * _bdot(qb, c_st.astype(BF16)) + _bdot(w_intra.astype(BF16), vb)
    den = w_inter * jnp.sum(qc * n_st, axis=2, keepdims=True) + jnp.sum(w_intra, axis=2, keepdims=True)
    h = num / jnp.maximum(jnp.abs(den), jnp.exp(-mt))
    last = n_valid - 1
    m_new = mt[:, last:last + 1, :]
    b_last = b_col[:, last:last + 1, :]
    w_state = jnp.exp(b_last - b_col + i_col - m_new)
    if n_valid < ln:
        row = lax.broadcasted_iota(jnp.int32, (1, ln, 1), 1)
        w_state = jnp.where(row < n_valid, w_state, 0.0)
    decay = jnp.exp(b_last + m_st - m_new)
    ks = w_state * kc
    ks_t = jnp.stack([ks[hd].T for hd in range(nh)]).astype(BF16)
    c_new = decay * c_st + _bdot(ks_t, vb)
    n_new = decay * n_st + jnp.sum(ks, axis=1, keepdims=True)
    return h, c_new, n_new, m_new


def _heads(x, n_heads, offset=0, width=HEAD_DIM):
    return jnp.stack([x[:, offset + h * width:offset + (h + 1) * width] for h in range(n_heads)])


def _mlstm_prompt_body(qk_ref, w_ref, b_ref, v_ref, o_ref, act_ref, cl_ref, g_ref,
                       y_ref, c_out_ref, n_out_ref, m_out_ref,
                       ext_ref, qk_scr, c_ref, n_ref, m_ref, *, rb, chunk, n_heads):
    blk = pl.program_id(1)
    pad = SUBLANES
    wm = n_heads * HEAD_DIM
    kscale = HEAD_DIM ** -0.5

    @pl.when(blk == 0)
    def _():
        ext_ref[0:pad, :] = jnp.zeros((pad, 2 * wm), F32)
        c_ref[...] = jnp.zeros_like(c_ref)
        n_ref[...] = jnp.zeros_like(n_ref)
        m_ref[...] = jnp.zeros_like(m_ref)

    ext_ref[pad:pad + rb, :] = qk_ref[...]
    acc = b_ref[...] + w_ref[CONV_W - 1:CONV_W, :] * qk_ref[...]
    for j in range(CONV_W - 1):
        off = pad - (CONV_W - 1) + j
        acc = acc + w_ref[j:j + 1, :] * ext_ref[off:off + rb, :]
    qk_scr[...] = acc * jax.nn.sigmoid(acc)
    ext_ref[0:pad, :] = qk_ref[rb - pad:rb, :]

    g = g_ref[...][:, None, :]
    for ci in range(rb // chunk):
        rows = slice(ci * chunk, (ci + 1) * chunk)
        qk = qk_scr[rows, :]
        hh, c_new, n_new, m_new = _mlstm_chunk(
            _heads(qk, n_heads), _heads(qk, n_heads, wm) * kscale, _heads(v_ref[rows, :], n_heads),
            _heads(act_ref[rows, :], n_heads, n_heads, 1), _heads(cl_ref[rows, :], n_heads, 2 * n_heads, 1),
            c_ref[...], n_ref[...], m_ref[:, :, 0:1], chunk)
        c_ref[...] = c_new
        n_ref[...] = n_new
        m_ref[...] = jnp.broadcast_to(m_new, m_ref.shape)
        y = jax.nn.sigmoid(_heads(o_ref[rows, :], n_heads)) * _rms(hh, g)
        for h in range(n_heads):
            y_ref[rows, h * HEAD_DIM:(h + 1) * HEAD_DIM] = y[h].astype(y_ref.dtype)

    @pl.when(blk == pl.num_programs(1) - 1)
    def _():
        c_out_ref[...] = c_ref[...]
        n_out_ref[...] = n_ref[:, 0, :]
        m_out_ref[...] = m_ref[:, 0, :]


def _mlstm_prompt(qkm, conv_w, conv_b, vm, om, act, cl, g_out, batch, seq, n_heads):
    n = qkm.shape[0]
    wm = n_heads * HEAD_DIM
    chunk = math.gcd(seq, MLSTM_CHUNK)
    rb = _pick_tile(seq, 2 * chunk)
    assert rb % chunk == 0 and rb >= SUBLANES
    nblk = seq // rb
    row = lambda b, i: (b * nblk + i, 0)
    const = lambda b, i: (0, 0)
    st = lambda b, i: (b, 0, 0)
    out_shape = (
        jax.ShapeDtypeStruct((n, wm), BF16),
        jax.ShapeDtypeStruct((batch, n_heads, HEAD_DIM, HEAD_DIM), F32),
        jax.ShapeDtypeStruct((batch, n_heads, HEAD_DIM), F32),
        jax.ShapeDtypeStruct((batch, n_heads, LANES), F32),
    )
    return pl.pallas_call(
        functools.partial(_mlstm_prompt_body, rb=rb, chunk=chunk, n_heads=n_heads),
        out_shape=out_shape,
        grid=(batch, nblk),
        in_specs=[
            pl.BlockSpec((rb, 2 * wm), row),
            pl.BlockSpec((CONV_W, 2 * wm), const),
            pl.BlockSpec((1, 2 * wm), const),
            pl.BlockSpec((rb, wm), row),
            pl.BlockSpec((rb, wm), row),
            pl.BlockSpec((rb, LANES), row),
            pl.BlockSpec((rb, LANES), row),
            pl.BlockSpec((n_heads, HEAD_DIM), const),
        ],
        out_specs=(
            pl.BlockSpec((rb, wm), row),
            pl.BlockSpec((None, n_heads, HEAD_DIM, HEAD_DIM), lambda b, i: (b, 0, 0, 0)),
            pl.BlockSpec((None, n_heads, HEAD_DIM), st),
            pl.BlockSpec((None, n_heads, LANES), st),
        ),
        scratch_shapes=[
            pltpu.VMEM((rb + SUBLANES, 2 * wm), F32),
            pltpu.VMEM((rb, 2 * wm), F32),
            pltpu.VMEM((n_heads, HEAD_DIM, HEAD_DIM), F32),
            pltpu.VMEM((n_heads, 1, HEAD_DIM), F32),
            pltpu.VMEM((n_heads, 1, LANES), F32),
        ],
        compiler_params=_cparams(("parallel", "arbitrary")),
        name="mlstm_prompt",
    )(qkm, conv_w, conv_b, vm, om, act, cl, g_out)


def _mlstm_sample_body(qk_ref, prev_ref, w_ref, b_ref, v_ref, o_ref, act_ref, cl_ref, g_ref, c0_ref, n0_ref, m0_ref,
                       y_ref, c_out_ref, n_out_ref, m_out_ref, ext_ref, pad_ref, *, t_new, n_heads):
    wm = n_heads * HEAD_DIM
    rows = SUBLANES
    kscale = HEAD_DIM ** -0.5
    ext_ref[...] = jnp.zeros_like(ext_ref)
    ext_ref[0:CONV_W - 1, :] = prev_ref[...]
    ext_ref[CONV_W - 1:CONV_W - 1 + t_new, :] = qk_ref[...]
    acc = b_ref[...] + w_ref[0:1, :] * ext_ref[0:rows, :]
    for j in range(1, CONV_W):
        acc = acc + w_ref[j:j + 1, :] * ext_ref[j:j + rows, :]
    qk = acc * jax.nn.sigmoid(acc)
    valid = lax.broadcasted_iota(jnp.int32, (rows, 1), 0) < t_new

    def padded(ref, width):
        pad_ref[:, 0:width] = jnp.zeros((rows, width), F32)
        pad_ref[0:t_new, 0:width] = ref[...]
        return pad_ref[:, 0:width]

    v = padded(v_ref, wm)
    o = padded(o_ref, wm)
    act = padded(act_ref, LANES)
    cl = padded(cl_ref, LANES)
    qk = jnp.where(valid, qk, 0.0)
    i_col = jnp.where(valid, _heads(act, n_heads, n_heads, 1), NEG_INF)
    b_col = _heads(cl, n_heads, 2 * n_heads, 1)
    b_col = jnp.where(valid, b_col, b_col[:, t_new - 1:t_new, :])
    n0 = jnp.stack([n0_ref[h:h + 1, :] for h in range(n_heads)])
    m0 = jnp.stack([m0_ref[h:h + 1, :] for h in range(n_heads)])
    hh, c_new, n_new, m_new = _mlstm_chunk(_heads(qk, n_heads), _heads(qk, n_heads, wm) * kscale, _heads(v, n_heads),
                                           i_col, b_col, c0_ref[...], n0, m0, t_new)
    y = jax.nn.sigmoid(_heads(o, n_heads)) * _rms(hh, g_ref[...][:, None, :])
    for h in range(n_heads):
        y_ref[:, h * HEAD_DIM:(h + 1) * HEAD_DIM] = y[h, 0:t_new, :]
    c_out_ref[...] = c_new
    n_out_ref[...] = n_new[:, 0, :]
    m_out_ref[...] = jnp.broadcast_to(m_new, (n_heads, 1, LANES))[:, 0, :]


def _mlstm_sample(qkm, conv_prev, conv_w, conv_b, vm, om, act, cl, g_out, c0, n0, m0, n_heads):
    bsz, t_new, wm2 = qkm.shape
    wm = wm2 // 2
    assert t_new + CONV_W - 1 <= SUBLANES
    seq3 = lambda b: (b, 0, 0)
    const = lambda b: (0, 0)
    out_shape = (
        jax.ShapeDtypeStruct((bsz, t_new, wm), F32),
        jax.ShapeDtypeStruct((bsz, n_heads, HEAD_DIM, HEAD_DIM), F32),
        jax.ShapeDtypeStruct((bsz, n_heads, HEAD_DIM), F32),
        jax.ShapeDtypeStruct((bsz, n_heads, LANES), F32),
    )
    return pl.pallas_call(
        functools.partial(_mlstm_sample_body, t_new=t_new, n_heads=n_heads),
        out_shape=out_shape,
        grid=(bsz,),
        in_specs=[
            pl.BlockSpec((None, t_new, wm2), seq3),
            pl.BlockSpec((None, CONV_W - 1, wm2), seq3),
            pl.BlockSpec((CONV_W, wm2), const),
            pl.BlockSpec((1, wm2), const),
            pl.BlockSpec((None, t_new, wm), seq3),
            pl.BlockSpec((None, t_new, wm), seq3),
            pl.BlockSpec((None, t_new, LANES), seq3),
            pl.BlockSpec((None, t_new, LANES), seq3),
            pl.BlockSpec((n_heads, HEAD_DIM), const),
            pl.BlockSpec((None, n_heads, HEAD_DIM, HEAD_DIM), lambda b: (b, 0, 0, 0)),
            pl.BlockSpec((None, n_heads, HEAD_DIM), seq3),
            pl.BlockSpec((None, n_heads, 1), seq3),
        ],
        out_specs=(
            pl.BlockSpec((None, t_new, wm), seq3),
            pl.BlockSpec((None, n_heads, HEAD_DIM, HEAD_DIM), lambda b: (b, 0, 0, 0)),
            pl.BlockSpec((None, n_heads, HEAD_DIM), seq3),
            pl.BlockSpec((None, n_heads, LANES), seq3),
        ),
        scratch_shapes=[pltpu.VMEM((2 * SUBLANES, wm2), F32), pltpu.VMEM((SUBLANES, wm), F32)],
        compiler_params=_cparams(("parallel",)),
        name="mlstm_sample",
    )(qkm, conv_prev, conv_w, conv_b, vm, om, act, cl, g_out, c0, n0, m0)


def _outproj_body(ya_ref, ym_ref, w_ref, x_ref, o_ref, *, wa):
    y = _dot(ya_ref[...].astype(BF16), w_ref[0:wa, :]) + _dot(ym_ref[...].astype(BF16), w_ref[wa:, :])
    o_ref[...] = x_ref[...] + y


def _outproj(ya, ym, w, x):
    n, d = x.shape
    wa = ya.shape[1]
    wm = ym.shape[1]
    tm = _pick_tile(n, 512)
    row = lambda i: (i, 0)
    return pl.pallas_call(
        functools.partial(_outproj_body, wa=wa),
        out_shape=jax.ShapeDtypeStruct((n, d), F32),
        grid=(n // tm,),
        in_specs=[
            pl.BlockSpec((tm, wa), row),
            pl.BlockSpec((tm, wm), row),
            pl.BlockSpec((wa + wm, d), lambda i: (0, 0)),
            pl.BlockSpec((tm, d), row),
        ],
        out_specs=pl.BlockSpec((tm, d), row),
        compiler_params=_cparams(("parallel",)),
        name="outproj",
    )(ya, ym, w, x)


def _cast_pad_body(x_ref, o_ref, *, axis, size):
    if axis == 1:
        o_ref[:, 0:size] = x_ref[...].astype(BF16)
        o_ref[:, size:] = jnp.zeros((o_ref.shape[0], o_ref.shape[1] - size), BF16)
    else:
        o_ref[0:size, :] = x_ref[...].astype(BF16)
        o_ref[size:, :] = jnp.zeros((o_ref.shape[0] - size, o_ref.shape[1]), BF16)


def _cast_pad(w, axis, padded):
    r, c = w.shape
    size = w.shape[axis]
    assert padded > size
    if axis == 1:
        t = _pick_tile(r, 256)
        grid, in_blk, out_blk, imap = (r // t,), (t, c), (t, padded), (lambda i: (i, 0))
        out_shape = (r, padded)
    else:
        t = _pick_tile(c, 256)
        grid, in_blk, out_blk, imap = (c // t,), (r, t), (padded, t), (lambda i: (0, i))
        out_shape = (padded, c)
    return pl.pallas_call(
        functools.partial(_cast_pad_body, axis=axis, size=size),
        out_shape=jax.ShapeDtypeStruct(out_shape, BF16),
        grid=grid,
        in_specs=[pl.BlockSpec(in_blk, imap)],
        out_specs=pl.BlockSpec(out_blk, imap),
        compiler_params=_cparams(("parallel",)),
        name="cast_pad",
    )(w)


def _prep_weights(lw, n_heads_a, n_heads_m):
    wa = n_heads_a * HEAD_DIM
    wm = n_heads_m * HEAD_DIM
    f = lw['w1_gate'].shape[1]
    tf = 512
    fp = tf * ((f + tf - 1) // tf)

    def ffn_w(wg, wu, wd):
        if fp == f:
            return wg.astype(BF16), wu.astype(BF16), wd.astype(BF16)
        return _cast_pad(wg, 1, fp), _cast_pad(wu, 1, fp), _cast_pad(wd, 0, fp)

    w_in = lw['w_in']
    o_f = 3 * wa
    o_qk = o_f + n_heads_a
    o_i = o_qk + 2 * wm + 2 * wm
    o_fm = o_i + n_heads_m
    w_main = jnp.concatenate([w_in[:, :o_f], w_in[:, o_qk:o_i]], axis=1).astype(BF16)
    ng = 2 * n_heads_m + n_heads_a
    w_gates = jnp.concatenate(
        [w_in[:, o_f:o_qk], w_in[:, o_i:o_fm], w_in[:, o_fm:o_fm + n_heads_m],
         jnp.zeros((w_in.shape[0], LANES - ng), w_in.dtype)], axis=1).astype(BF16)
    bias = jnp.concatenate([lw['b_fox_f'], lw['b_m_i'], lw['b_m_f'], jnp.zeros((LANES - ng,), F32)])[None, :]
    return {
        'ffn1': ffn_w(lw['w1_gate'], lw['w1_up'], lw['w1_down']),
        'ffn2': ffn_w(lw['w2_gate'], lw['w2_up'], lw['w2_down']),
        'tf': tf,
        'w_main': w_main,
        'w_gates': w_gates,
        'gate_bias': bias,
        'w_out': lw['w_out'].astype(BF16),
    }


def kernel(x_prompt, x_sample, cache_k, cache_v, cache_logf, state_conv, state_C, state_n, state_m, page_table,
           g_ffn1, w1_gate, w1_up, w1_down, g_mix, w_in, b_fox_f, b_m_i, b_m_f, conv_w, conv_b, g_q, g_k,
           g_out_a, g_out_m, w_out, g_ffn2, w2_gate, w2_up, w2_down):
    depth = w_in.shape[0]
    bp, seq, d = x_prompt.shape
    bs, t_new, _ = x_sample.shape
    n_heads_a = g_out_a.shape[1]
    n_heads_m = g_out_m.shape[1]
    wa = n_heads_a * HEAD_DIM
    wm = n_heads_m * HEAD_DIM
    n_pool, page = cache_k.shape[1], cache_k.shape[2]

    yp = x_prompt.reshape(bp * seq, d)
    ys = x_sample.reshape(bs * t_new, d)
    outs = [[] for _ in range(14)]
    for l in range(depth):
        lw = {
            'w1_gate': w1_gate[l], 'w1_up': w1_up[l], 'w1_down': w1_down[l], 'w_in': w_in[l],
            'b_fox_f': b_fox_f[l], 'b_m_i': b_m_i[l], 'b_m_f': b_m_f[l], 'w_out': w_out[l],
            'w2_gate': w2_gate[l], 'w2_up': w2_up[l], 'w2_down': w2_down[l],
        }
        pw = _prep_weights(lw, n_heads_a, n_heads_m)
        gf1 = g_ffn1[l][None, :]
        gf2 = g_ffn2[l][None, :]
        gmix = g_mix[l][None, :]
        gq = g_q[l][None, :]
        gk = g_k[l][None, :]
        cw = conv_w[l]
        cb = conv_b[l][None, :]
        goa = g_out_a[l]
        gom = g_out_m[l]

        x1, xn = _ffn(yp, gf1, *pw['ffn1'], pw['tf'], g_next=gmix)
        q, k, kb, v, vt, qkm, vm, om, gt = _inproj(xn, pw['w_main'], pw['w_gates'], gq, gk, n_heads_a)
        act, cg, cl = _gates(gt, pw['gate_bias'], seq, math.gcd(seq, MLSTM_CHUNK), n_heads_a)
        ya = _fox_prompt(q, kb, vt, cg, goa[:, None, :], bp, seq, n_heads_a)
        ym, c_p, n_p, m_p = _mlstm_prompt(qkm, cw, cb, vm, om, act, cl, gom, bp, seq, n_heads_m)
        x2 = _outproj(ya, ym, pw['w_out'], x1)
        yp = _ffn(x2, gf2, *pw['ffn2'], pw['tf'])
        outs[0].append(k.reshape(bp, seq, n_heads_a, HEAD_DIM))
        outs[1].append(v.reshape(bp, seq, n_heads_a, HEAD_DIM))
        outs[2].append(act[:, :n_heads_a].reshape(bp, seq, n_heads_a))
        outs[3].append(qkm.reshape(bp, seq, 2 * wm)[:, seq - (CONV_W - 1):, :])
        outs[4].append(c_p)
        outs[5].append(n_p)
        outs[6].append(m_p[:, :, 0])

        x1, xn = _ffn(ys, gf1, *pw['ffn1'], pw['tf'], g_next=gmix)
        q, k, kb, v, vt, qkm, vm, om, gt = _inproj(xn, pw['w_main'], pw['w_gates'], gq, gk, n_heads_a)
        act, cg, cl = _gates(gt, pw['gate_bias'], t_new, math.gcd(t_new, MLSTM_CHUNK), n_heads_a)
        r3 = lambda a: a.reshape(bs, t_new, a.shape[-1])
        cache_c = _pool_cumsum(cache_logf[l].reshape(n_pool, page * n_heads_a), n_heads_a, page)
        ya = _fox_sample(page_table, r3(q).astype(F32), r3(k), r3(v), r3(cg), goa,
                         cache_k.reshape(-1, HEAD_DIM), cache_v.reshape(-1, HEAD_DIM),
                         cache_c, page, l * n_pool, n_heads_a)
        ym, c_s, n_s, m_s = _mlstm_sample(r3(qkm), state_conv[l], cw, cb, r3(vm), r3(om), r3(act), r3(cl), gom,
                                          state_C[l], state_n[l], state_m[l][:, :, None], n_heads_m)
        x2 = _outproj(ya.reshape(bs * t_new, wa), ym.reshape(bs * t_new, wm), pw['w_out'], x1)
        ys = _ffn(x2, gf2, *pw['ffn2'], pw['tf'])
        conv_ext = jnp.concatenate([state_conv[l], r3(qkm)], axis=1)
        outs[7].append(k.reshape(bs, t_new, n_heads_a, HEAD_DIM))
        outs[8].append(v.reshape(bs, t_new, n_heads_a, HEAD_DIM))
        outs[9].append(act[:, :n_heads_a].reshape(bs, t_new, n_heads_a))
        outs[10].append(conv_ext[:, t_new:, :])
        outs[11].append(c_s)
        outs[12].append(n_s)
        outs[13].append(m_s[:, :, 0])

    return (yp.reshape(bp, seq, d), ys.reshape(bs, t_new, d)) + tuple(jnp.stack(o) for o in outs)
```

```python
import functools
import math

import jax
import jax.numpy as jnp
from jax import lax
from jax.experimental import pallas as pl
from jax.experimental.pallas import tpu as pltpu

F32 = jnp.float32
BF16 = jnp.bfloat16
HIGHEST = lax.Precision.HIGHEST

EPS = 1e-6
HEAD_DIM = 128
CONV_W = 4
MLSTM_CHUNK = 128
LANES = 128
SUBLANES = 8
VMEM_LIMIT_BYTES = 56 * 1024 * 1024
PAGE_BUFFER_BYTES = 32 * 1024 * 1024
NEG_INF = float("-inf")
LOG2E = math.log2(math.e)


def _cparams(sem):
    return pltpu.CompilerParams(dimension_semantics=sem, vmem_limit_bytes=VMEM_LIMIT_BYTES)


def _rms(x, g):
    y = x * lax.rsqrt(jnp.mean(x * x, axis=-1, keepdims=True) + EPS)
    return y * g


def _log_sigmoid(x):
    return jnp.minimum(x, 0.0) - jnp.log1p(jnp.exp(-jnp.abs(x)))


def _dot(a, b, precision=None):
    return jnp.dot(a, b, preferred_element_type=F32, precision=precision)


def _dot_nt(a, b, precision=None):
    return lax.dot_general(a, b, (((1,), (1,)), ((), ())), preferred_element_type=F32, precision=precision)


def _pick_tile(n, pref):
    t = min(n, pref)
    while n % t:
        t //= 2
    return t


def _log2(n):
    k = int(math.log2(n))
    assert 1 << k == n
    return k


def _ffn_body(x_ref, g_ref, wg_ref, wu_ref, wd_ref, gn_ref, *rest, with_next):
    if with_next:
        o_ref, on_ref, xn_ref, acc_ref = rest
    else:
        o_ref, xn_ref, acc_ref = rest
    j = pl.program_id(1)

    @pl.when(j == 0)
    def _():
        xn_ref[...] = _rms(x_ref[...], g_ref[...]).astype(BF16)
        acc_ref[...] = jnp.zeros_like(acc_ref)

    xn = xn_ref[...]
    gate = _dot(xn, wg_ref[...])
    up = _dot(xn, wu_ref[...])
    h = (gate * jax.nn.sigmoid(gate)) * up
    acc_ref[...] += _dot(h.astype(BF16), wd_ref[...])

    @pl.when(j == pl.num_programs(1) - 1)
    def _():
        y = x_ref[...] + 0.5 * acc_ref[...]
        o_ref[...] = y
        if with_next:
            on_ref[...] = _rms(y, gn_ref[...]).astype(BF16)


def _ffn(x, g, wg, wu, wd, tf, g_next=None):
    n, d = x.shape
    fp = wg.shape[1]
    tm = _pick_tile(n, 512)
    with_next = g_next is not None
    row = pl.BlockSpec((tm, d), lambda i, j: (i, 0))
    vec = pl.BlockSpec((1, d), lambda i, j: (0, 0))
    out_shape = jax.ShapeDtypeStruct((n, d), F32)
    return pl.pallas_call(
        functools.partial(_ffn_body, with_next=with_next),
        out_shape=(out_shape, jax.ShapeDtypeStruct((n, d), BF16)) if with_next else out_shape,
        grid=(n // tm, fp // tf),
        in_specs=[
            row, vec,
            pl.BlockSpec((d, tf), lambda i, j: (0, j)),
            pl.BlockSpec((d, tf), lambda i, j: (0, j)),
            pl.BlockSpec((tf, d), lambda i, j: (j, 0)),
            vec,
        ],
        out_specs=(row, row) if with_next else row,
        scratch_shapes=[pltpu.VMEM((tm, d), BF16), pltpu.VMEM((tm, d), F32)],
        compiler_params=_cparams(("parallel", "arbitrary")),
        name="ffn",
    )(x, g, wg, wu, wd, g_next if with_next else g)


def _inproj_body(xn_ref, w_ref, wgt_ref, gq_ref, gk_ref,
                 q_ref, k_ref, kb_ref, v_ref, vt_ref, qkm_ref, vm_ref, om_ref, gt_ref, *, n_heads):
    j = pl.program_id(0)
    xn = xn_ref[...]
    acc = _dot(xn, w_ref[...])

    @pl.when(j == 0)
    def _():
        for h in range(n_heads):
            sl = slice(h * HEAD_DIM, (h + 1) * HEAD_DIM)
            q_ref[:, sl] = _rms(acc[:, sl], gq_ref[...]).astype(BF16)
        gt_ref[...] = _dot(xn, wgt_ref[...])

    @pl.when(j == 1)
    def _():
        for h in range(n_heads):
            sl = slice(h * HEAD_DIM, (h + 1) * HEAD_DIM)
            kn = _rms(acc[:, sl], gk_ref[...])
            k_ref[:, sl] = kn
            kb_ref[:, sl] = kn.astype(BF16)

    @pl.when(j == 2)
    def _():
        v_ref[...] = acc
        vt_ref[...] = acc.T.astype(BF16)

    @pl.when((j == 3) | (j == 4))
    def _():
        qkm_ref[...] = acc

    @pl.when(j == 5)
    def _():
        vm_ref[...] = acc

    @pl.when(j == 6)
    def _():
        om_ref[...] = acc


def _inproj(xn, w_main, w_gates, gq, gk, n_heads):
    n, d = xn.shape
    wa = n_heads * HEAD_DIM
    assert w_main.shape[1] == 7 * wa
    tm = _pick_tile(n, 512)
    ni = n // tm
    const = lambda j, i: (0, 0)

    def rows_at(j, i, j0, j1):
        return jnp.where(j < j0, 0, jnp.where(j > j1, ni - 1, i))

    def row_map(j0, j1=None):
        j1 = j0 if j1 is None else j1
        return lambda j, i: (rows_at(j, i, j0, j1), 0)

    out_shape = (
        jax.ShapeDtypeStruct((n, wa), BF16),
        jax.ShapeDtypeStruct((n, wa), F32),
        jax.ShapeDtypeStruct((n, wa), BF16),
        jax.ShapeDtypeStruct((n, wa), F32),
        jax.ShapeDtypeStruct((wa, n), BF16),
        jax.ShapeDtypeStruct((n, 2 * wa), F32),
        jax.ShapeDtypeStruct((n, wa), F32),
        jax.ShapeDtypeStruct((n, wa), F32),
        jax.ShapeDtypeStruct((n, LANES), F32),
    )
    out_specs = (
        pl.BlockSpec((tm, wa), row_map(0)),
        pl.BlockSpec((tm, wa), row_map(1)),
        pl.BlockSpec((tm, wa), row_map(1)),
        pl.BlockSpec((tm, wa), row_map(2)),
        pl.BlockSpec((wa, tm), lambda j, i: (0, rows_at(j, i, 2, 2))),
        pl.BlockSpec((tm, wa), lambda j, i: (rows_at(j, i, 3, 4), jnp.clip(j - 3, 0, 1))),
        pl.BlockSpec((tm, wa), row_map(5)),
        pl.BlockSpec((tm, wa), row_map(6)),
        pl.BlockSpec((tm, LANES), row_map(0)),
    )
    return pl.pallas_call(
        functools.partial(_inproj_body, n_heads=n_heads),
        out_shape=out_shape,
        grid=(7, ni),
        in_specs=[
            pl.BlockSpec((tm, d), lambda j, i: (i, 0)),
            pl.BlockSpec((d, wa), lambda j, i: (0, j)),
            pl.BlockSpec((d, LANES), const),
            pl.BlockSpec((1, HEAD_DIM), const),
            pl.BlockSpec((1, HEAD_DIM), const),
        ],
        out_specs=out_specs,
        compiler_params=_cparams(("arbitrary", "arbitrary")),
        name="inproj",
    )(xn, w_main, w_gates, gq, gk)


def _split3(x):
    hi = x.astype(BF16)
    r1 = x - hi.astype(F32)
    mid = r1.astype(BF16)
    lo = (r1 - mid.astype(F32)).astype(BF16)
    return hi, mid, lo


def _dot_01(m01, x):
    hi, mid, lo = _split3(x)
    return _dot(m01, hi) + _dot(m01, mid) + _dot(m01, lo)


def _gates_body(g_ref, b_ref, act_ref, cg_ref, cl_ref, *, seg, lc, rows, n_heads):
    ch = min(rows, 256)
    sg = min(seg, ch)
    lane = lax.broadcasted_iota(jnp.int32, (1, LANES), 1)
    is_logsig = (lane < n_heads) | ((lane >= 2 * n_heads) & (lane < 3 * n_heads))
    r = lax.broadcasted_iota(jnp.int32, (ch, ch), 0)
    c = lax.broadcasted_iota(jnp.int32, (ch, ch), 1)
    sh_g = _log2(sg)
    sh_l = _log2(lc)
    tri_g = jnp.where((r >= c) & ((r >> sh_g) == (c >> sh_g)), 1.0, 0.0).astype(BF16)
    tri_l = jnp.where((r >= c) & ((r >> sh_l) == (c >> sh_l)), 1.0, 0.0).astype(BF16)
    carry = jnp.zeros((1, LANES), F32)
    for ci in range(rows // ch):
        sl = slice(ci * ch, (ci + 1) * ch)
        x = g_ref[sl, :] + b_ref[...]
        act = jnp.where(is_logsig, _log_sigmoid(x), x)
        act_ref[sl, :] = act
        cg = _dot_01(tri_g, act)
        if seg > ch:
            cg = cg + carry
            carry = cg[ch - 1:ch, :]
        cg_ref[sl, :] = cg
        cl_ref[sl, :] = _dot_01(tri_l, act)


def _gates(gates, bias, seg, lc, n_heads):
    n = gates.shape[0]
    rows = seg if seg >= 256 else _pick_tile(n, 256)
    blk = pl.BlockSpec((rows, LANES), lambda b: (b, 0))
    return pl.pallas_call(
        functools.partial(_gates_body, seg=seg, lc=lc, rows=rows, n_heads=n_heads),
        out_shape=(jax.ShapeDtypeStruct((n, LANES), F32),) * 3,
        grid=(n // rows,),
        in_specs=[blk, pl.BlockSpec((1, LANES), lambda b: (0, 0))],
        out_specs=(blk, blk, blk),
        compiler_params=_cparams(("parallel",)),
        name="gates",
    )(gates, bias)


def _fox_prompt_body(q_ref, k_ref, vt_ref, c_ref, g_ref, o_ref,
                     qtail_ref, ktail_ref, m_ref, l_ref, acc_ref, *, tile, scale, n_heads):
    qi = pl.program_id(1)
    ki = pl.program_id(2)
    exp2_scale = scale * LOG2E

    def head_cols(h):
        return pl.ds(pl.multiple_of(h * HEAD_DIM, HEAD_DIM), HEAD_DIM)

    @pl.when((qi == 0) & (ki == 0))
    def _():
        lane = lax.broadcasted_iota(jnp.int32, (1, LANES), 1)

        def tails(h, carry):
            c = jnp.sum(jnp.where(lane == h, c_ref[...], 0.0), axis=1, keepdims=True) * (1.0 / scale)
            p1, p2, p3 = (p.astype(F32) for p in _split3(c))
            one = jnp.ones_like(p1)

            def tail(cols):
                t = jnp.zeros((c.shape[0], LANES), F32)
                for j, col in enumerate(cols):
                    t = jnp.where(lane == j, col, t)
                return t.astype(BF16)

            qtail_ref[h] = tail((p1, p2, p3, one, one, one))
            ktail_ref[h] = tail((one, one, one, -p1, -p2, -p3))
            return carry

        lax.fori_loop(0, n_heads, tails, 0)

    @pl.when(ki == 0)
    def _():
        m_ref[...] = jnp.full_like(m_ref, NEG_INF)
        l_ref[...] = jnp.zeros_like(l_ref)
        acc_ref[...] = jnp.zeros_like(acc_ref)

    def block(masked):
        q_rows = pl.ds(pl.multiple_of(qi * tile, tile), tile)
        k_rows = pl.ds(pl.multiple_of(ki * tile, tile), tile)

        def one_head(h, carry):
            q_aug = jnp.concatenate([q_ref[:, head_cols(h)], qtail_ref[h, q_rows, :]], axis=1)
            k_aug = jnp.concatenate([k_ref[:, head_cols(h)], ktail_ref[h, k_rows, :]], axis=1)
            st = _dot_nt(k_aug, q_aug)
            if masked:
                kpos = lax.broadcasted_iota(jnp.int32, (tile, tile), 0)
                qpos = lax.broadcasted_iota(jnp.int32, (tile, tile), 1)
                st = jnp.where(kpos <= qpos, st, NEG_INF)
            m_old = m_ref[h]
            m_new = jnp.maximum(m_old, jnp.max(st, axis=0, keepdims=True))
            alpha = jnp.exp2((m_old - m_new) * exp2_scale)
            p = jnp.exp2((st - m_new) * exp2_scale)
            l_ref[h] = alpha * l_ref[h] + jnp.sum(p, axis=0, keepdims=True)
            acc_ref[h] = alpha * acc_ref[h] + _dot(vt_ref[head_cols(h), :], p.astype(BF16))
            m_ref[h] = m_new
            return carry

        lax.fori_loop(0, n_heads, one_head, 0)

    @pl.when(ki < qi)
    def _():
        block(False)

    @pl.when(ki == qi)
    def _():
        block(True)

    @pl.when(ki == pl.num_programs(2) - 1)
    def _():
        def finish(h, carry):
            o = (acc_ref[h] / l_ref[h]).T
            o_ref[:, head_cols(h)] = _rms(o, g_ref[pl.ds(h, 1), :]).astype(o_ref.dtype)
            return carry

        lax.fori_loop(0, n_heads, finish, 0)


def _fox_prompt(q, kb, vt, cg, g_out, batch, seq, n_heads):
    n, wa = q.shape
    tile = _pick_tile(seq, 512)
    nt = seq // tile

    def kv_blk(qi, ki):
        return jnp.minimum(ki, qi)

    return pl.pallas_call(
        functools.partial(_fox_prompt_body, tile=tile, scale=HEAD_DIM ** -0.5, n_heads=n_heads),
        out_shape=jax.ShapeDtypeStruct((n, wa), BF16),
        grid=(batch, nt, nt),
        in_specs=[
            pl.BlockSpec((tile, wa), lambda b, qi, ki: (b * nt + qi, 0)),
            pl.BlockSpec((tile, wa), lambda b, qi, ki: (b * nt + kv_blk(qi, ki), 0)),
            pl.BlockSpec((wa, tile), lambda b, qi, ki: (0, b * nt + kv_blk(qi, ki))),
            pl.BlockSpec((seq, LANES), lambda b, qi, ki: (b, 0)),
            pl.BlockSpec((n_heads, HEAD_DIM), lambda b, qi, ki: (0, 0)),
        ],
        out_specs=pl.BlockSpec((tile, wa), lambda b, qi, ki: (b * nt + qi, 0)),
        scratch_shapes=[
            pltpu.VMEM((n_heads, seq, LANES), BF16),
            pltpu.VMEM((n_heads, seq, LANES), BF16),
            pltpu.VMEM((n_heads, 1, tile), F32),
            pltpu.VMEM((n_heads, 1, tile), F32),
            pltpu.VMEM((n_heads, HEAD_DIM, tile), F32),
        ],
        compiler_params=_cparams(("parallel", "arbitrary", "arbitrary")),
        name="fox_prompt",
    )(q, kb, vt, cg, g_out)


def _pool_cumsum_body(x_ref, o_ref, m_ref, *, n_heads, page):
    width = page * n_heads

    @pl.when(pl.program_id(0) == 0)
    def _():
        r = lax.broadcasted_iota(jnp.int32, (width, width), 0)
        c = lax.broadcasted_iota(jnp.int32, (width, width), 1)
        same_head = (r & (n_heads - 1)) == (c >> _log2(page))
        earlier = (r >> _log2(n_heads)) <= (c & (page - 1))
        m_ref[...] = jnp.where(same_head & earlier, 1.0, 0.0).astype(BF16)

    hi, mid, lo = _split3(x_ref[...])
    m = m_ref[...]
    res = _dot(hi, m) + _dot(mid, m) + _dot(lo, m)
    for h in range(n_heads):
        o_ref[:, h, :] = res[:, h * page:(h + 1) * page]


def _pool_cumsum(lf, n_heads, page):
    n_pool, width = lf.shape
    rows = _pick_tile(n_pool, 256)
    return pl.pallas_call(
        functools.partial(_pool_cumsum_body, n_heads=n_heads, page=page),
        out_shape=jax.ShapeDtypeStruct((n_pool, n_heads, page), F32),
        grid=(n_pool // rows,),
        in_specs=[pl.BlockSpec((rows, width), lambda i: (i, 0))],
        out_specs=pl.BlockSpec((rows, n_heads, page), lambda i: (i, 0, 0)),
        scratch_shapes=[pltpu.VMEM((width, width), BF16)],
        compiler_params=_cparams(("arbitrary",)),
        name="pool_cumsum",
    )(lf)


def _fox_sample_body(pt_ref, q_ref, kn_ref, vn_ref, cl_ref, g_ref, *rest,
                     npg, n_pages, page, n_heads, t_new, scale):
    k_refs = rest[0:npg]
    c_refs = rest[npg:2 * npg]
    v_refs = rest[2 * npg:3 * npg]
    o_ref = rest[3 * npg]
    qx_ref, qxb_ref, pad_ref, cpad_ref, s2_ref, carry_ref, acc_ref = rest[3 * npg + 1:]
    del pt_ref
    b_id = pl.program_id(0)
    s_id = pl.program_id(1)
    n_seq = pl.num_programs(0) - 1
    n_steps = n_pages // npg
    past = n_pages * page
    wa = n_heads * HEAD_DIM
    s_ref = s2_ref.at[b_id % 2]
    p_ref = s2_ref.at[(b_id + 1) % 2]
    k_phase = b_id < n_seq
    v_phase = b_id >= 1
    row_h = lax.broadcasted_iota(jnp.int32, (n_heads, wa), 0)
    col_h = lax.broadcasted_iota(jnp.int32, (n_heads, wa), 1) >> _log2(HEAD_DIM)
    head_diag = row_h == col_h

    def expand(c):
        out = c
        for t in range(1, t_new):
            out = out + pltpu.roll(c, t * n_heads, axis=1)
        return out

    @pl.when(k_phase & (s_id == 0))
    def _():
        qx_ref[...] = jnp.zeros_like(qx_ref)
        q = q_ref[...]
        for t in range(t_new):
            qx_ref[t * n_heads:(t + 1) * n_heads, :] = jnp.where(head_diag, q[t:t + 1, :], 0.0)
        qxb_ref[...] = qx_ref[...].astype(BF16)
        cpad_ref[...] = jnp.zeros_like(cpad_ref)
        carry_ref[...] = jnp.zeros_like(carry_ref)

    @pl.when(v_phase & (s_id == 0))
    def _():
        acc_ref[...] = jnp.zeros_like(acc_ref)

    def head_major(ref):
        return jnp.concatenate([ref[pl.ds(h, page, stride=n_heads), :] for h in range(n_heads)], axis=1)

    @pl.when(k_phase)
    def _():
        for r in range(npg):
            st = _dot_nt(head_major(k_refs[r]).astype(BF16), qxb_ref[...]) * scale
            cpad_ref[:, 0:n_heads] = c_refs[r][...].T
            cx = expand(cpad_ref[...]) + carry_ref[...]
            carry_ref[...] = cx[page - 1:page, :]
            off = pl.multiple_of((s_id * npg + r) * page, page)
            s_ref[pl.ds(off, page), :] = st - cx

    @pl.when(v_phase)
    def _():
        for r in range(npg):
            off = pl.multiple_of((s_id * npg + r) * page, page)
            p = p_ref[pl.ds(off, page), :].T.astype(BF16)
            acc_ref[...] += _dot(p, head_major(v_refs[r]).astype(BF16))

    @pl.when(v_phase & (s_id == n_steps - 1))
    def _():
        pad_ref[...] = jnp.zeros_like(pad_ref)
        pad_ref[0:t_new, :] = vn_ref[...]
        p = p_ref[past:past + page, :].T.astype(BF16)
        acc = acc_ref[...] + _dot(p, pad_ref[...].astype(BF16))
        for t in range(t_new):
            blk = acc[t * n_heads:(t + 1) * n_heads, :]
            o_t = jnp.sum(jnp.where(head_diag, blk, 0.0), axis=0, keepdims=True)
            for h in range(n_heads):
                sl = slice(h * HEAD_DIM, (h + 1) * HEAD_DIM)
                o_ref[t:t + 1, sl] = _rms(o_t[:, sl], g_ref[h:h + 1, :])

    @pl.when(k_phase & (s_id == n_steps - 1))
    def _():
        pad_ref[...] = jnp.zeros_like(pad_ref)
        pad_ref[0:t_new, :] = kn_ref[...]
        st = _dot_nt(pad_ref[...].astype(BF16), qxb_ref[...]) * scale
        lane = lax.broadcasted_iota(jnp.int32, (1, LANES), 1)
        cpad_ref[...] = jnp.zeros_like(cpad_ref)
        cpad_ref[0:t_new, :] = jnp.where(lane < n_heads, cl_ref[...], 0.0)
        cl_x = expand(cpad_ref[...])
        u = lax.broadcasted_iota(jnp.int32, (page, LANES), 0)
        t = lax.broadcasted_iota(jnp.int32, (page, LANES), 1) >> _log2(n_heads)
        st = jnp.where((u < t_new) & (u <= t), st - cl_x, NEG_INF)
        s_past = s_ref[0:past, :] + carry_ref[...]
        m = jnp.maximum(jnp.max(s_past, axis=0, keepdims=True), jnp.max(st, axis=0, keepdims=True))
        p_past = jnp.exp(s_past - m)
        p_new = jnp.exp(st - m)
        inv = 1.0 / (jnp.sum(p_past, axis=0, keepdims=True) + jnp.sum(p_new, axis=0, keepdims=True))
        s_ref[0:past, :] = p_past * inv
        s_ref[past:past + page, :] = p_new * inv


def _fox_sample(page_table, q, k_new, v_new, cl, g_out, cache_k, cache_v, cache_c, page, pool_base, n_heads):
    bsz, t_new, wa = q.shape
    n_pages = page_table.shape[1]
    assert page == LANES and n_heads == SUBLANES and n_heads * t_new <= LANES
    page_bytes = page * wa * 4
    npg = _pick_tile(n_pages, max(1, PAGE_BUFFER_BYTES // (4 * page_bytes)))
    n_steps = n_pages // npg
    scale = HEAD_DIM ** -0.5

    def k_seq(b):
        return jnp.minimum(b, bsz - 1)

    def v_seq(b):
        return jnp.maximum(b - 1, 0)

    def kv_map(seq_fn, r):
        return lambda b, s, pt: (pool_base + pt[seq_fn(b), s * npg + r], 0)

    def c_map(r):
        return lambda b, s, pt: (pt[k_seq(b), s * npg + r], 0, 0)

    k3 = lambda b, s, pt: (k_seq(b), 0, 0)
    v3 = lambda b, s, pt: (v_seq(b), 0, 0)
    in_specs = [
        pl.BlockSpec((None, t_new, wa), k3),
        pl.BlockSpec((None, t_new, wa), k3),
        pl.BlockSpec((None, t_new, wa), v3),
        pl.BlockSpec((None, t_new, LANES), k3),
        pl.BlockSpec((n_heads, HEAD_DIM), lambda b, s, pt: (0, 0)),
    ]
    in_specs += [pl.BlockSpec((page * n_heads, HEAD_DIM), kv_map(k_seq, r)) for r in range(npg)]
    in_specs += [pl.BlockSpec((None, n_heads, page), c_map(r)) for r in range(npg)]
    in_specs += [pl.BlockSpec((page * n_heads, HEAD_DIM), kv_map(v_seq, r)) for r in range(npg)]
    grid_spec = pltpu.PrefetchScalarGridSpec(
        num_scalar_prefetch=1,
        grid=(bsz + 1, n_steps),
        in_specs=in_specs,
        out_specs=pl.BlockSpec((None, t_new, wa), v3),
        scratch_shapes=[
            pltpu.VMEM((LANES, wa), F32),
            pltpu.VMEM((LANES, wa), BF16),
            pltpu.VMEM((page, wa), F32),
            pltpu.VMEM((page, LANES), F32),
            pltpu.VMEM((2, n_pages * page + page, LANES), F32),
            pltpu.VMEM((1, LANES), F32),
            pltpu.VMEM((LANES, wa), F32),
        ],
    )
    return pl.pallas_call(
        functools.partial(_fox_sample_body, npg=npg, n_pages=n_pages, page=page, n_heads=n_heads,
                          t_new=t_new, scale=scale),
        out_shape=jax.ShapeDtypeStruct((bsz, t_new, wa), F32),
        grid_spec=grid_spec,
        compiler_params=_cparams(("arbitrary", "arbitrary")),
        name="fox_sample",
    )(page_table, q, k_new, v_new, cl, g_out, *([cache_k] * npg), *([cache_c] * npg), *([cache_v] * npg))


def _bdot(a, b):
    return lax.dot_general(a, b, (((2,), (1,)), ((0,), (0,))), preferred_element_type=F32)


def _bdot_nt(a, b):
    return lax.dot_general(a, b, (((2,), (2,)), ((0,), (0,))), preferred_element_type=F32)


def _mlstm_chunk(qc, kc, vc, i_col, b_col, c_st, n_st, m_st, n_valid):
    nh, ln, _ = qc.shape
    r = lax.broadcasted_iota(jnp.int32, (ln, ln), 0)
    c = lax.broadcasted_iota(jnp.int32, (ln, ln), 1)
    eye = (r == c)[None]
    tril = (r >= c)[None]

    def col_to_row(col):
        return jnp.sum(jnp.where(eye, col, 0.0), axis=1, keepdims=True)

    i_row = col_to_row(i_col)
    b_row = col_to_row(b_col)
    a_col = b_col + m_st
    dmat = jnp.where(tril, b_col - b_row + i_row, NEG_INF)
    mt = jnp.maximum(a_col, jnp.max(dmat, axis=2, keepdims=True))
    qb = qc.astype(BF16)
    kb = kc.astype(BF16)
    vb = vc.astype(BF16)
    w_intra = jnp.exp(dmat - mt) * _bdot_nt(qb, kb)
    w_inter = jnp.exp(a_col - mt)
    num = w_inter * _bdot(qb, c_st.astype(BF16)) + _bdot(w_intra.astype(BF16), vb)
    den = w_inter * jnp.sum(qc * n_st, axis=2, keepdims=True) + jnp.sum(w_intra, axis=2, keepdims=True)
    h = num / jnp.maximum(jnp.abs(den), jnp.exp(-mt))
    last = n_valid - 1
    m_new = mt[:, last:last + 1, :]
    b_last = b_col[:, last:last + 1, :]
    w_state = jnp.exp(b_last - b_col + i_col - m_new)
    if n_valid < ln:
        row = lax.broadcasted_iota(jnp.int32, (1, ln, 1), 1)
        w_state = jnp.where(row < n_valid, w_state, 0.0)
    decay = jnp.exp(b_last + m_st - m_new)
    ks = w_state * kc
    ks_t = jnp.stack([ks[hd].T for hd in range(nh)]).astype(BF16)
    c_new = decay * c_st + _bdot(ks_t, vb)
    n_new = decay * n_st + jnp.sum(ks, axis=1, keepdims=True)
    return h, c_new, n_new, m_new


def _heads(x, n_heads, offset=0, width=HEAD_DIM):
    return jnp.stack([x[:, offset + h * width:offset + (h + 1) * width] for h in range(n_heads)])


def _mlstm_prompt_body(qk_ref, w_ref, b_ref, v_ref, o_ref, act_ref, cl_ref, g_ref,
                       y_ref, c_out_ref, n_out_ref, m_out_ref,
                       ext_ref, qk_scr, c_ref, n_ref, m_ref, *, rb, chunk, n_heads):
    blk = pl.program_id(1)
    pad = SUBLANES
    wm = n_heads * HEAD_DIM
    kscale = HEAD_DIM ** -0.5

    @pl.when(blk == 0)
    def _():
        ext_ref[0:pad, :] = jnp.zeros((pad, 2 * wm), F32)
        c_ref[...] = jnp.zeros_like(c_ref)
        n_ref[...] = jnp.zeros_like(n_ref)
        m_ref[...] = jnp.zeros_like(m_ref)

    ext_ref[pad:pad + rb, :] = qk_ref[...]
    acc = b_ref[...] + w_ref[CONV_W - 1:CONV_W, :] * qk_ref[...]
    for j in range(CONV_W - 1):
        off = pad - (CONV_W - 1) + j
        acc = acc + w_ref[j:j + 1, :] * ext_ref[off:off + rb, :]
    qk_scr[...] = acc * jax.nn.sigmoid(acc)
    ext_ref[0:pad, :] = qk_ref[rb - pad:rb, :]

    g = g_ref[...][:, None, :]
    for ci in range(rb // chunk):
        rows = slice(ci * chunk, (ci + 1) * chunk)
        qk = qk_scr[rows, :]
        hh, c_new, n_new, m_new = _mlstm_chunk(
            _heads(qk, n_heads), _heads(qk, n_heads, wm) * kscale, _heads(v_ref[rows, :], n_heads),
            _heads(act_ref[rows, :], n_heads, n_heads, 1), _heads(cl_ref[rows, :], n_heads, 2 * n_heads, 1),
            c_ref[...], n_ref[...], m_ref[:, :, 0:1], chunk)
        c_ref[...] = c_new
        n_ref[...] = n_new
        m_ref[...] = jnp.broadcast_to(m_new, m_ref.shape)
        y = jax.nn.sigmoid(_heads(o_ref[rows, :], n_heads)) * _rms(hh, g)
        for h in range(n_heads):
            y_ref[rows, h * HEAD_DIM:(h + 1) * HEAD_DIM] = y[h].astype(y_ref.dtype)

    @pl.when(blk == pl.num_programs(1) - 1)
    def _():
        c_out_ref[...] = c_ref[...]
        n_out_ref[...] = n_ref[:, 0, :]
        m_out_ref[...] = m_ref[:, 0, :]


def _mlstm_prompt(qkm, conv_w, conv_b, vm, om, act, cl, g_out, batch, seq, n_heads):
    n = qkm.shape[0]
    wm = n_heads * HEAD_DIM
    chunk = math.gcd(seq, MLSTM_CHUNK)
    rb = _pick_tile(seq, 2 * chunk)
    assert rb % chunk == 0 and rb >= SUBLANES
    nblk = seq // rb
    row = lambda b, i: (b * nblk + i, 0)
    const = lambda b, i: (0, 0)
    st = lambda b, i: (b, 0, 0)
    out_shape = (
        jax.ShapeDtypeStruct((n, wm), BF16),
        jax.ShapeDtypeStruct((batch, n_heads, HEAD_DIM, HEAD_DIM), F32),
        jax.ShapeDtypeStruct((batch, n_heads, HEAD_DIM), F32),
        jax.ShapeDtypeStruct((batch, n_heads, LANES), F32),
    )
    return pl.pallas_call(
        functools.partial(_mlstm_prompt_body, rb=rb, chunk=chunk, n_heads=n_heads),
        out_shape=out_shape,
        grid=(batch, nblk),
        in_specs=[
            pl.BlockSpec((rb, 2 * wm), row),
            pl.BlockSpec((CONV_W, 2 * wm), const),
            pl.BlockSpec((1, 2 * wm), const),
            pl.BlockSpec((rb, wm), row),
            pl.BlockSpec((rb, wm), row),
            pl.BlockSpec((rb, LANES), row),
            pl.BlockSpec((rb, LANES), row),
            pl.BlockSpec((n_heads, HEAD_DIM), const),
        ],
        out_specs=(
            pl.BlockSpec((rb, wm), row),
            pl.BlockSpec((None, n_heads, HEAD_DIM, HEAD_DIM), lambda b, i: (b, 0, 0, 0)),
            pl.BlockSpec((None, n_heads, HEAD_DIM), st),
            pl.BlockSpec((None, n_heads, LANES), st),
        ),
        scratch_shapes=[
            pltpu.VMEM((rb + SUBLANES, 2 * wm), F32),
            pltpu.VMEM((rb, 2 * wm), F32),
            pltpu.VMEM((n_heads, HEAD_DIM, HEAD_DIM), F32),
            pltpu.VMEM((n_heads, 1, HEAD_DIM), F32),
            pltpu.VMEM((n_heads, 1, LANES), F32),
        ],
        compiler_params=_cparams(("parallel", "arbitrary")),
        name="mlstm_prompt",
    )(qkm, conv_w, conv_b, vm, om, act, cl, g_out)


def _mlstm_sample_body(qk_ref, prev_ref, w_ref, b_ref, v_ref, o_ref, act_ref, cl_ref, g_ref, c0_ref, n0_ref, m0_ref,
                       y_ref, c_out_ref, n_out_ref, m_out_ref, ext_ref, pad_ref, *, t_new, n_heads, sb):
    wm = n_heads * HEAD_DIM
    rows = SUBLANES
    kscale = HEAD_DIM ** -0.5
    valid = lax.broadcasted_iota(jnp.int32, (rows, 1), 0) < t_new

    def padded(val, width):
        pad_ref[:, 0:width] = jnp.zeros((rows, width), F32)
        pad_ref[0:t_new, 0:width] = val
        return pad_ref[:, 0:width]

    qs, ks, vs, os_, i_cols, b_cols, n0s, m0s = [], [], [], [], [], [], [], []
    for s in range(sb):
        tok = slice(s * t_new, (s + 1) * t_new)
        ext_ref[...] = jnp.zeros_like(ext_ref)
        ext_ref[0:CONV_W - 1, :] = prev_ref[s]
        ext_ref[CONV_W - 1:CONV_W - 1 + t_new, :] = qk_ref[tok, :]
        acc = b_ref[...] + w_ref[0:1, :] * ext_ref[0:rows, :]
        for j in range(1, CONV_W):
            acc = acc + w_ref[j:j + 1, :] * ext_ref[j:j + rows, :]
        qk = jnp.where(valid, acc * jax.nn.sigmoid(acc), 0.0)
        qs.append(_heads(qk, n_heads))
        ks.append(_heads(qk, n_heads, wm) * kscale)
        vs.append(_heads(padded(v_ref[tok, :], wm), n_heads))
        os_.append(_heads(padded(o_ref[tok, :], wm), n_heads))
        i_cols.append(jnp.where(valid, _heads(padded(act_ref[tok, :], LANES), n_heads, n_heads, 1), NEG_INF))
        b_col = _heads(padded(cl_ref[tok, :], LANES), n_heads, 2 * n_heads, 1)
        b_cols.append(jnp.where(valid, b_col, b_col[:, t_new - 1:t_new, :]))
        n0s.append(jnp.stack([n0_ref[s, h:h + 1, :] for h in range(n_heads)]))
        m0s.append(jnp.stack([m0_ref[s, h:h + 1, :] for h in range(n_heads)]))

    cat = lambda parts: jnp.concatenate(parts, axis=0)
    c0 = cat([c0_ref[s] for s in range(sb)])
    hh, c_new, n_new, m_new = _mlstm_chunk(cat(qs), cat(ks), cat(vs), cat(i_cols), cat(b_cols),
                                           c0, cat(n0s), cat(m0s), t_new)
    g = jnp.concatenate([g_ref[...]] * sb, axis=0)[:, None, :]
    y = jax.nn.sigmoid(cat(os_)) * _rms(hh, g)
    m_wide = jnp.broadcast_to(m_new, (sb * n_heads, 1, LANES))
    for s in range(sb):
        for h in range(n_heads):
            y_ref[s * t_new:(s + 1) * t_new, h * HEAD_DIM:(h + 1) * HEAD_DIM] = y[s * n_heads + h, 0:t_new, :]
        grp = slice(s * n_heads, (s + 1) * n_heads)
        c_out_ref[s] = c_new[grp]
        n_out_ref[s] = n_new[grp, 0, :]
        m_out_ref[s] = m_wide[grp, 0, :]


def _mlstm_sample(qkm, conv_prev, conv_w, conv_b, vm, om, act, cl, g_out, c0, n0, m0, t_new, n_heads):
    n, wm2 = qkm.shape
    bsz = n // t_new
    wm = wm2 // 2
    assert t_new + CONV_W - 1 <= SUBLANES
    sb = _pick_tile(bsz, 4)
    assert (sb * t_new) % SUBLANES == 0 or sb == bsz
    tok = lambda b: (b, 0)
    seq3 = lambda b: (b, 0, 0)
    const = lambda b: (0, 0)
    out_shape = (
        jax.ShapeDtypeStruct((n, wm), F32),
        jax.ShapeDtypeStruct((bsz, n_heads, HEAD_DIM, HEAD_DIM), F32),
        jax.ShapeDtypeStruct((bsz, n_heads, HEAD_DIM), F32),
        jax.ShapeDtypeStruct((bsz, n_heads, LANES), F32),
    )
    return pl.pallas_call(
        functools.partial(_mlstm_sample_body, t_new=t_new, n_heads=n_heads, sb=sb),
        out_shape=out_shape,
        grid=(bsz // sb,),
        in_specs=[
            pl.BlockSpec((sb * t_new, wm2), tok),
            pl.BlockSpec((sb, CONV_W - 1, wm2), seq3),
            pl.BlockSpec((CONV_W, wm2), const),
            pl.BlockSpec((1, wm2), const),
            pl.BlockSpec((sb * t_new, wm), tok),
            pl.BlockSpec((sb * t_new, wm), tok),
            pl.BlockSpec((sb * t_new, LANES), tok),
            pl.BlockSpec((sb * t_new, LANES), tok),
            pl.BlockSpec((n_heads, HEAD_DIM), const),
            pl.BlockSpec((sb, n_heads, HEAD_DIM, HEAD_DIM), lambda b: (b, 0, 0, 0)),
            pl.BlockSpec((sb, n_heads, HEAD_DIM), seq3),
            pl.BlockSpec((sb, n_heads, 1), seq3),
        ],
        out_specs=(
            pl.BlockSpec((sb * t_new, wm), tok),
            pl.BlockSpec((sb, n_heads, HEAD_DIM, HEAD_DIM), lambda b: (b, 0, 0, 0)),
            pl.BlockSpec((sb, n_heads, HEAD_DIM), seq3),
            pl.BlockSpec((sb, n_heads, LANES), seq3),
        ),
        scratch_shapes=[pltpu.VMEM((2 * SUBLANES, wm2), F32), pltpu.VMEM((SUBLANES, wm), F32)],
        compiler_params=_cparams(("parallel",)),
        name="mlstm_sample",
    )(qkm, conv_prev, conv_w, conv_b, vm, om, act, cl, g_out, c0, n0, m0)


def _outproj_body(ya_ref, ym_ref, w_ref, x_ref, o_ref, *, wa):
    y = _dot(ya_ref[...].astype(BF16), w_ref[0:wa, :]) + _dot(ym_ref[...].astype(BF16), w_ref[wa:, :])
    o_ref[...] = x_ref[...] + y


def _outproj(ya, ym, w, x):
    n, d = x.shape
    wa = ya.shape[1]
    wm = ym.shape[1]
    tm = _pick_tile(n, 512)
    row = lambda i: (i, 0)
    return pl.pallas_call(
        functools.partial(_outproj_body, wa=wa),
        out_shape=jax.ShapeDtypeStruct((n, d), F32),
        grid=(n // tm,),
        in_specs=[
            pl.BlockSpec((tm, wa), row),
            pl.BlockSpec((tm, wm), row),
            pl.BlockSpec((wa + wm, d), lambda i: (0, 0)),
            pl.BlockSpec((tm, d), row),
        ],
        out_specs=pl.BlockSpec((tm, d), row),
        compiler_params=_cparams(("parallel",)),
        name="outproj",
    )(ya, ym, w, x)


def _cast_pad_body(x_ref, o_ref, *, axis, size):
    if axis == 1:
        o_ref[:, 0:size] = x_ref[...].astype(BF16)
        o_ref[:, size:] = jnp.zeros((o_ref.shape[0], o_ref.shape[1] - size), BF16)
    else:
        o_ref[0:size, :] = x_ref[...].astype(BF16)
        o_ref[size:, :] = jnp.zeros((o_ref.shape[0] - size, o_ref.shape[1]), BF16)


def _cast_pad(w, axis, padded):
    r, c = w.shape
    size = w.shape[axis]
    assert padded > size
    if axis == 1:
        t = _pick_tile(r, 256)
        grid, in_blk, out_blk, imap = (r // t,), (t, c), (t, padded), (lambda i: (i, 0))
        out_shape = (r, padded)
    else:
        t = _pick_tile(c, 256)
        grid, in_blk, out_blk, imap = (c // t,), (r, t), (padded, t), (lambda i: (0, i))
        out_shape = (padded, c)
    return pl.pallas_call(
        functools.partial(_cast_pad_body, axis=axis, size=size),
        out_shape=jax.ShapeDtypeStruct(out_shape, BF16),
        grid=grid,
        in_specs=[pl.BlockSpec(in_blk, imap)],
        out_specs=pl.BlockSpec(out_blk, imap),
        compiler_params=_cparams(("parallel",)),
        name="cast_pad",
    )(w)


def _win_prep_body(x_ref, main_ref, gates_ref, *, cuts, n_gate_cols):
    o_f, o_qk, o_i = cuts
    main_ref[:, 0:o_f] = x_ref[:, 0:o_f].astype(BF16)
    main_ref[:, o_f:o_f + (o_i - o_qk)] = x_ref[:, o_qk:o_i].astype(BF16)
    n_fa = o_qk - o_f
    gates_ref[...] = jnp.zeros_like(gates_ref)
    gates_ref[:, 0:n_fa] = x_ref[:, o_f:o_qk].astype(BF16)
    gates_ref[:, n_fa:n_gate_cols] = x_ref[:, o_i:o_i + (n_gate_cols - n_fa)].astype(BF16)


def _win_prep(w_in, cuts, n_gate_cols):
    d, n_in = w_in.shape
    o_f, o_qk, o_i = cuts
    n_main = o_f + (o_i - o_qk)
    t = _pick_tile(d, 256)
    row = lambda i: (i, 0)
    return pl.pallas_call(
        functools.partial(_win_prep_body, cuts=cuts, n_gate_cols=n_gate_cols),
        out_shape=(jax.ShapeDtypeStruct((d, n_main), BF16), jax.ShapeDtypeStruct((d, LANES), BF16)),
        grid=(d // t,),
        in_specs=[pl.BlockSpec((t, n_in), row)],
        out_specs=(pl.BlockSpec((t, n_main), row), pl.BlockSpec((t, LANES), row)),
        compiler_params=_cparams(("parallel",)),
        name="win_prep",
    )(w_in)


def _prep_weights(lw, n_heads_a, n_heads_m):
    wa = n_heads_a * HEAD_DIM
    wm = n_heads_m * HEAD_DIM
    f = lw['w1_gate'].shape[1]
    tf = 512
    fp = tf * ((f + tf - 1) // tf)

    def ffn_w(wg, wu, wd):
        if fp == f:
            return wg.astype(BF16), wu.astype(BF16), wd.astype(BF16)
        return _cast_pad(wg, 1, fp), _cast_pad(wu, 1, fp), _cast_pad(wd, 0, fp)

    w_in = lw['w_in']
    o_f = 3 * wa
    o_qk = o_f + n_heads_a
    o_i = o_qk + 2 * wm + 2 * wm
    o_fm = o_i + n_heads_m
    ng = 2 * n_heads_m + n_heads_a
    assert o_fm + n_heads_m == w_in.shape[1]
    w_main, w_gates = _win_prep(w_in, (o_f, o_qk, o_i), ng)
    bias = jnp.concatenate([lw['b_fox_f'], lw['b_m_i'], lw['b_m_f'], jnp.zeros((LANES - ng,), F32)])[None, :]
    return {
        'ffn1': ffn_w(lw['w1_gate'], lw['w1_up'], lw['w1_down']),
        'ffn2': ffn_w(lw['w2_gate'], lw['w2_up'], lw['w2_down']),
        'tf': tf,
        'w_main': w_main,
        'w_gates': w_gates,
        'gate_bias': bias,
        'w_out': lw['w_out'].astype(BF16),
    }


def kernel(x_prompt, x_sample, cache_k, cache_v, cache_logf, state_conv, state_C, state_n, state_m, page_table,
           g_ffn1, w1_gate, w1_up, w1_down, g_mix, w_in, b_fox_f, b_m_i, b_m_f, conv_w, conv_b, g_q, g_k,
           g_out_a, g_out_m, w_out, g_ffn2, w2_gate, w2_up, w2_down):
    depth = w_in.shape[0]
    bp, seq, d = x_prompt.shape
    bs, t_new, _ = x_sample.shape
    n_heads_a = g_out_a.shape[1]
    n_heads_m = g_out_m.shape[1]
    wa = n_heads_a * HEAD_DIM
    wm = n_heads_m * HEAD_DIM
    n_pool, page = cache_k.shape[1], cache_k.shape[2]

    yp = x_prompt.reshape(bp * seq, d)
    ys = x_sample.reshape(bs * t_new, d)
    outs = [[] for _ in range(14)]
    for l in range(depth):
        lw = {
            'w1_gate': w1_gate[l], 'w1_up': w1_up[l], 'w1_down': w1_down[l], 'w_in': w_in[l],
            'b_fox_f': b_fox_f[l], 'b_m_i': b_m_i[l], 'b_m_f': b_m_f[l], 'w_out': w_out[l],
            'w2_gate': w2_gate[l], 'w2_up': w2_up[l], 'w2_down': w2_down[l],
        }
        pw = _prep_weights(lw, n_heads_a, n_heads_m)
        gf1 = g_ffn1[l][None, :]
        gf2 = g_ffn2[l][None, :]
        gmix = g_mix[l][None, :]
        gq = g_q[l][None, :]
        gk = g_k[l][None, :]
        cw = conv_w[l]
        cb = conv_b[l][None, :]
        goa = g_out_a[l]
        gom = g_out_m[l]

        x1, xn = _ffn(yp, gf1, *pw['ffn1'], pw['tf'], g_next=gmix)
        q, k, kb, v, vt, qkm, vm, om, gt = _inproj(xn, pw['w_main'], pw['w_gates'], gq, gk, n_heads_a)
        act, cg, cl = _gates(gt, pw['gate_bias'], seq, math.gcd(seq, MLSTM_CHUNK), n_heads_a)
        ya = _fox_prompt(q, kb, vt, cg, goa, bp, seq, n_heads_a)
        ym, c_p, n_p, m_p = _mlstm_prompt(qkm, cw, cb, vm, om, act, cl, gom, bp, seq, n_heads_m)
        x2 = _outproj(ya, ym, pw['w_out'], x1)
        yp = _ffn(x2, gf2, *pw['ffn2'], pw['tf'])
        outs[0].append(k.reshape(bp, seq, n_heads_a, HEAD_DIM))
        outs[1].append(v.reshape(bp, seq, n_heads_a, HEAD_DIM))
        outs[2].append(act[:, :n_heads_a].reshape(bp, seq, n_heads_a))
        outs[3].append(qkm.reshape(bp, seq, 2 * wm)[:, seq - (CONV_W - 1):, :])
        outs[4].append(c_p)
        outs[5].append(n_p)
        outs[6].append(m_p[:, :, 0])

        x1, xn = _ffn(ys, gf1, *pw['ffn1'], pw['tf'], g_next=gmix)
        q, k, kb, v, vt, qkm, vm, om, gt = _inproj(xn, pw['w_main'], pw['w_gates'], gq, gk, n_heads_a)
        act, cg, cl = _gates(gt, pw['gate_bias'], t_new, math.gcd(t_new, MLSTM_CHUNK), n_heads_a)
        r3 = lambda a: a.reshape(bs, t_new, a.shape[-1])
        cache_c = _pool_cumsum(cache_logf[l].reshape(n_pool, page * n_heads_a), n_heads_a, page)
        ya = _fox_sample(page_table, r3(q).astype(F32), r3(k), r3(v), r3(cg), goa,
                         cache_k.reshape(-1, HEAD_DIM), cache_v.reshape(-1, HEAD_DIM),
                         cache_c, page, l * n_pool, n_heads_a)
        ym, c_s, n_s, m_s = _mlstm_sample(qkm, state_conv[l], cw, cb, vm, om, act, cl, gom,
                                          state_C[l], state_n[l], state_m[l][:, :, None], t_new, n_heads_m)
        x2 = _outproj(ya.reshape(bs * t_new, wa), ym, pw['w_out'], x1)
        ys = _ffn(x2, gf2, *pw['ffn2'], pw['tf'])
        conv_ext = jnp.concatenate([state_conv[l], r3(qkm)], axis=1)
        outs[7].append(k.reshape(bs, t_new, n_heads_a, HEAD_DIM))
        outs[8].append(v.reshape(bs, t_new, n_heads_a, HEAD_DIM))
        outs[9].append(act[:, :n_heads_a].reshape(bs, t_new, n_heads_a))
        outs[10].append(conv_ext[:, t_new:, :])
        outs[11].append(c_s)
        outs[12].append(n_s)
        outs[13].append(m_s[:, :, 0])

    return (yp.reshape(bp, seq, d), ys.reshape(bs, t_new, d)) + tuple(jnp.stack(o) for o in outs)
```

```python
import functools
import math

import jax
import jax.numpy as jnp
from jax import lax
from jax.experimental import pallas as pl
from jax.experimental.pallas import tpu as pltpu

F32 = jnp.float32
BF16 = jnp.bfloat16
HIGHEST = lax.Precision.HIGHEST

EPS = 1e-6
HEAD_DIM = 128
CONV_W = 4
MLSTM_CHUNK = 128
LANES = 128
SUBLANES = 8
VMEM_LIMIT_BYTES = 56 * 1024 * 1024
PAGE_BUFFER_BYTES = 32 * 1024 * 1024
NEG_INF = float("-inf")
LOG2E = math.log2(math.e)


def _cparams(sem):
    return pltpu.CompilerParams(dimension_semantics=sem, vmem_limit_bytes=VMEM_LIMIT_BYTES)


def _rms(x, g):
    y = x * lax.rsqrt(jnp.mean(x * x, axis=-1, keepdims=True) + EPS)
    return y * g


def _log_sigmoid(x):
    return jnp.minimum(x, 0.0) - jnp.log1p(jnp.exp(-jnp.abs(x)))


def _dot(a, b, precision=None):
    return jnp.dot(a, b, preferred_element_type=F32, precision=precision)


def _dot_nt(a, b, precision=None):
    return lax.dot_general(a, b, (((1,), (1,)), ((), ())), preferred_element_type=F32, precision=precision)


def _pick_tile(n, pref):
    t = min(n, pref)
    while n % t:
        t //= 2
    return t


def _log2(n):
    k = int(math.log2(n))
    assert 1 << k == n
    return k


def _ffn_body(x_ref, g_ref, wg_ref, wu_ref, wd_ref, gn_ref, *rest, with_next):
    if with_next:
        o_ref, on_ref, xn_ref, acc_ref = rest
    else:
        o_ref, xn_ref, acc_ref = rest
    j = pl.program_id(1)

    @pl.when(j == 0)
    def _():
        xn_ref[...] = _rms(x_ref[...], g_ref[...]).astype(BF16)
        acc_ref[...] = jnp.zeros_like(acc_ref)

    xn = xn_ref[...]
    gate = _dot(xn, wg_ref[...])
    up = _dot(xn, wu_ref[...])
    h = (gate * jax.nn.sigmoid(gate)) * up
    acc_ref[...] += _dot(h.astype(BF16), wd_ref[...])

    @pl.when(j == pl.num_programs(1) - 1)
    def _():
        y = x_ref[...] + 0.5 * acc_ref[...]
        o_ref[...] = y
        if with_next:
            on_ref[...] = _rms(y, gn_ref[...]).astype(BF16)


def _ffn(x, g, wg, wu, wd, tf, g_next=None):
    n, d = x.shape
    fp = wg.shape[1]
    tm = _pick_tile(n, 512)
    with_next = g_next is not None
    row = pl.BlockSpec((tm, d), lambda i, j: (i, 0))
    vec = pl.BlockSpec((1, d), lambda i, j: (0, 0))
    out_shape = jax.ShapeDtypeStruct((n, d), F32)
    return pl.pallas_call(
        functools.partial(_ffn_body, with_next=with_next),
        out_shape=(out_shape, jax.ShapeDtypeStruct((n, d), BF16)) if with_next else out_shape,
        grid=(n // tm, fp // tf),
        in_specs=[
            row, vec,
            pl.BlockSpec((d, tf), lambda i, j: (0, j)),
            pl.BlockSpec((d, tf), lambda i, j: (0, j)),
            pl.BlockSpec((tf, d), lambda i, j: (j, 0)),
            vec,
        ],
        out_specs=(row, row) if with_next else row,
        scratch_shapes=[pltpu.VMEM((tm, d), BF16), pltpu.VMEM((tm, d), F32)],
        compiler_params=_cparams(("parallel", "arbitrary")),
        name="ffn",
    )(x, g, wg, wu, wd, g_next if with_next else g)


def _inproj_body(xn_ref, w_ref, wgt_ref, gq_ref, gk_ref,
                 q_ref, k_ref, kb_ref, v_ref, vt_ref, qkm_ref, vm_ref, om_ref, gt_ref, *, n_heads):
    j = pl.program_id(0)
    xn = xn_ref[...]
    acc = _dot(xn, w_ref[...])

    @pl.when(j == 0)
    def _():
        for h in range(n_heads):
            sl = slice(h * HEAD_DIM, (h + 1) * HEAD_DIM)
            q_ref[:, sl] = _rms(acc[:, sl], gq_ref[...]).astype(BF16)
        gt_ref[...] = _dot(xn, wgt_ref[...])

    @pl.when(j == 1)
    def _():
        for h in range(n_heads):
            sl = slice(h * HEAD_DIM, (h + 1) * HEAD_DIM)
            kn = _rms(acc[:, sl], gk_ref[...])
            k_ref[:, sl] = kn
            kb_ref[:, sl] = kn.astype(BF16)

    @pl.when(j == 2)
    def _():
        v_ref[...] = acc
        vt_ref[...] = acc.T.astype(BF16)

    @pl.when((j == 3) | (j == 4))
    def _():
        qkm_ref[...] = acc

    @pl.when(j == 5)
    def _():
        vm_ref[...] = acc

    @pl.when(j == 6)
    def _():
        om_ref[...] = acc


def _inproj(xn, w_main, w_gates, gq, gk, n_heads):
    n, d = xn.shape
    wa = n_heads * HEAD_DIM
    assert w_main.shape[1] == 7 * wa
    tm = _pick_tile(n, 512)
    ni = n // tm
    const = lambda j, i: (0, 0)

    def rows_at(j, i, j0, j1):
        return jnp.where(j < j0, 0, jnp.where(j > j1, ni - 1, i))

    def row_map(j0, j1=None):
        j1 = j0 if j1 is None else j1
        return lambda j, i: (rows_at(j, i, j0, j1), 0)

    out_shape = (
        jax.ShapeDtypeStruct((n, wa), BF16),
        jax.ShapeDtypeStruct((n, wa), F32),
        jax.ShapeDtypeStruct((n, wa), BF16),
        jax.ShapeDtypeStruct((n, wa), F32),
        jax.ShapeDtypeStruct((wa, n), BF16),
        jax.ShapeDtypeStruct((n, 2 * wa), F32),
        jax.ShapeDtypeStruct((n, wa), F32),
        jax.ShapeDtypeStruct((n, wa), F32),
        jax.ShapeDtypeStruct((n, LANES), F32),
    )
    out_specs = (
        pl.BlockSpec((tm, wa), row_map(0)),
        pl.BlockSpec((tm, wa), row_map(1)),
        pl.BlockSpec((tm, wa), row_map(1)),
        pl.BlockSpec((tm, wa), row_map(2)),
        pl.BlockSpec((wa, tm), lambda j, i: (0, rows_at(j, i, 2, 2))),
        pl.BlockSpec((tm, wa), lambda j, i: (rows_at(j, i, 3, 4), jnp.clip(j - 3, 0, 1))),
        pl.BlockSpec((tm, wa), row_map(5)),
        pl.BlockSpec((tm, wa), row_map(6)),
        pl.BlockSpec((tm, LANES), row_map(0)),
    )
    return pl.pallas_call(
        functools.partial(_inproj_body, n_heads=n_heads),
        out_shape=out_shape,
        grid=(7, ni),
        in_specs=[
            pl.BlockSpec((tm, d), lambda j, i: (i, 0)),
            pl.BlockSpec((d, wa), lambda j, i: (0, j)),
            pl.BlockSpec((d, LANES), const),
            pl.BlockSpec((1, HEAD_DIM), const),
            pl.BlockSpec((1, HEAD_DIM), const),
        ],
        out_specs=out_specs,
        compiler_params=_cparams(("arbitrary", "arbitrary")),
        name="inproj",
    )(xn, w_main, w_gates, gq, gk)


def _split3(x):
    hi = x.astype(BF16)
    r1 = x - hi.astype(F32)
    mid = r1.astype(BF16)
    lo = (r1 - mid.astype(F32)).astype(BF16)
    return hi, mid, lo


def _dot_01(m01, x):
    hi, mid, lo = _split3(x)
    return _dot(m01, hi) + _dot(m01, mid) + _dot(m01, lo)


def _gates_body(g_ref, b_ref, act_ref, cg_ref, cl_ref, *, seg, lc, rows, n_heads):
    ch = min(rows, 256)
    sg = min(seg, ch)
    lane = lax.broadcasted_iota(jnp.int32, (1, LANES), 1)
    is_logsig = (lane < n_heads) | ((lane >= 2 * n_heads) & (lane < 3 * n_heads))
    r = lax.broadcasted_iota(jnp.int32, (ch, ch), 0)
    c = lax.broadcasted_iota(jnp.int32, (ch, ch), 1)
    sh_g = _log2(sg)
    sh_l = _log2(lc)
    tri_g = jnp.where((r >= c) & ((r >> sh_g) == (c >> sh_g)), 1.0, 0.0).astype(BF16)
    tri_l = jnp.where((r >= c) & ((r >> sh_l) == (c >> sh_l)), 1.0, 0.0).astype(BF16)
    carry = jnp.zeros((1, LANES), F32)
    for ci in range(rows // ch):
        sl = slice(ci * ch, (ci + 1) * ch)
        x = g_ref[sl, :] + b_ref[...]
        act = jnp.where(is_logsig, _log_sigmoid(x), x)
        act_ref[sl, :] = act
        cg = _dot_01(tri_g, act)
        if seg > ch:
            cg = cg + carry
            carry = cg[ch - 1:ch, :]
        cg_ref[sl, :] = cg
        cl_ref[sl, :] = _dot_01(tri_l, act)


def _gates(gates, bias, seg, lc, n_heads):
    n = gates.shape[0]
    rows = seg if seg >= 256 else _pick_tile(n, 256)
    blk = pl.BlockSpec((rows, LANES), lambda b: (b, 0))
    return pl.pallas_call(
        functools.partial(_gates_body, seg=seg, lc=lc, rows=rows, n_heads=n_heads),
        out_shape=(jax.ShapeDtypeStruct((n, LANES), F32),) * 3,
        grid=(n // rows,),
        in_specs=[blk, pl.BlockSpec((1, LANES), lambda b: (0, 0))],
        out_specs=(blk, blk, blk),
        compiler_params=_cparams(("parallel",)),
        name="gates",
    )(gates, bias)


def _fox_prompt_body(q_ref, k_ref, vt_ref, c_ref, g_ref, o_ref,
                     qtail_ref, ktail_ref, m_ref, l_ref, acc_ref, *, tile, scale, n_heads):
    qi = pl.program_id(1)
    ki = pl.program_id(2)
    exp2_scale = scale * LOG2E

    def head_cols(h):
        return pl.ds(pl.multiple_of(h * HEAD_DIM, HEAD_DIM), HEAD_DIM)

    @pl.when((qi == 0) & (ki == 0))
    def _():
        lane = lax.broadcasted_iota(jnp.int32, (1, LANES), 1)

        def tails(h, carry):
            c = jnp.sum(jnp.where(lane == h, c_ref[...], 0.0), axis=1, keepdims=True) * (1.0 / scale)
            p1, p2, p3 = (p.astype(F32) for p in _split3(c))
            one = jnp.ones_like(p1)

            def tail(cols):
                t = jnp.zeros((c.shape[0], LANES), F32)
                for j, col in enumerate(cols):
                    t = jnp.where(lane == j, col, t)
                return t.astype(BF16)

            qtail_ref[h] = tail((p1, p2, p3, one, one, one))
            ktail_ref[h] = tail((one, one, one, -p1, -p2, -p3))
            return carry

        lax.fori_loop(0, n_heads, tails, 0)

    @pl.when(ki == 0)
    def _():
        m_ref[...] = jnp.full_like(m_ref, NEG_INF)
        l_ref[...] = jnp.zeros_like(l_ref)
        acc_ref[...] = jnp.zeros_like(acc_ref)

    def block(masked):
        q_rows = pl.ds(pl.multiple_of(qi * tile, tile), tile)
        k_rows = pl.ds(pl.multiple_of(ki * tile, tile), tile)

        def one_head(h, carry):
            q_aug = jnp.concatenate([q_ref[:, head_cols(h)], qtail_ref[h, q_rows, :]], axis=1)
            k_aug = jnp.concatenate([k_ref[:, head_cols(h)], ktail_ref[h, k_rows, :]], axis=1)
            st = _dot_nt(k_aug, q_aug)
            if masked:
                kpos = lax.broadcasted_iota(jnp.int32, (tile, tile), 0)
                qpos = lax.broadcasted_iota(jnp.int32, (tile, tile), 1)
                st = jnp.where(kpos <= qpos, st, NEG_INF)
            m_old = m_ref[h]
            m_new = jnp.maximum(m_old, jnp.max(st, axis=0, keepdims=True))
            alpha = jnp.exp2((m_old - m_new) * exp2_scale)
            p = jnp.exp2((st - m_new) * exp2_scale)
            l_ref[h] = alpha * l_ref[h] + jnp.sum(p, axis=0, keepdims=True)
            acc_ref[h] = alpha * acc_ref[h] + _dot(vt_ref[head_cols(h), :], p.astype(BF16))
            m_ref[h] = m_new
            return carry

        lax.fori_loop(0, n_heads, one_head, 0)

    @pl.when(ki < qi)
    def _():
        block(False)

    @pl.when(ki == qi)
    def _():
        block(True)

    @pl.when(ki == pl.num_programs(2) - 1)
    def _():
        def finish(h, carry):
            o = (acc_ref[h] / l_ref[h]).T
            o_ref[:, head_cols(h)] = _rms(o, g_ref[pl.ds(h, 1), :]).astype(o_ref.dtype)
            return carry

        lax.fori_loop(0, n_heads, finish, 0)


def _fox_prompt(q, kb, vt, cg, g_out, batch, seq, n_heads):
    n, wa = q.shape
    tile = _pick_tile(seq, 512)
    nt = seq // tile

    def kv_blk(qi, ki):
        return jnp.minimum(ki, qi)

    return pl.pallas_call(
        functools.partial(_fox_prompt_body, tile=tile, scale=HEAD_DIM ** -0.5, n_heads=n_heads),
        out_shape=jax.ShapeDtypeStruct((n, wa), BF16),
        grid=(batch, nt, nt),
        in_specs=[
            pl.BlockSpec((tile, wa), lambda b, qi, ki: (b * nt + qi, 0)),
            pl.BlockSpec((tile, wa), lambda b, qi, ki: (b * nt + kv_blk(qi, ki), 0)),
            pl.BlockSpec((wa, tile), lambda b, qi, ki: (0, b * nt + kv_blk(qi, ki))),
            pl.BlockSpec((seq, LANES), lambda b, qi, ki: (b, 0)),
            pl.BlockSpec((n_heads, HEAD_DIM), lambda b, qi, ki: (0, 0)),
        ],
        out_specs=pl.BlockSpec((tile, wa), lambda b, qi, ki: (b * nt + qi, 0)),
        scratch_shapes=[
            pltpu.VMEM((n_heads, seq, LANES), BF16),
            pltpu.VMEM((n_heads, seq, LANES), BF16),
            pltpu.VMEM((n_heads, 1, tile), F32),
            pltpu.VMEM((n_heads, 1, tile), F32),
            pltpu.VMEM((n_heads, HEAD_DIM, tile), F32),
        ],
        compiler_params=_cparams(("parallel", "arbitrary", "arbitrary")),
        name="fox_prompt",
    )(q, kb, vt, cg, g_out)


def _pool_cumsum_body(x_ref, o_ref):
    page = x_ref.shape[1]
    r = lax.broadcasted_iota(jnp.int32, (page, page), 0)
    c = lax.broadcasted_iota(jnp.int32, (page, page), 1)
    tri = jnp.where(r <= c, 1.0, 0.0).astype(BF16)
    hi, mid, lo = _split3(x_ref[...])
    o_ref[...] = _dot(hi, tri) + _dot(mid, tri) + _dot(lo, tri)


def _pool_cumsum(lf):
    n_rows, page = lf.shape
    rows = _pick_tile(n_rows, 2048)
    blk = pl.BlockSpec((rows, page), lambda i: (i, 0))
    return pl.pallas_call(
        _pool_cumsum_body,
        out_shape=jax.ShapeDtypeStruct((n_rows, page), F32),
        grid=(n_rows // rows,),
        in_specs=[blk],
        out_specs=blk,
        compiler_params=_cparams(("parallel",)),
        name="pool_cumsum",
    )(lf)


def _fox_sample_body(pt_ref, q_ref, kn_ref, vn_ref, cl_ref, g_ref, *rest,
                     npg, n_pages, page, n_heads, t_new, scale):
    k_refs = rest[0:npg]
    c_refs = rest[npg:2 * npg]
    v_refs = rest[2 * npg:3 * npg]
    o_ref = rest[3 * npg]
    qx_ref, qxb_ref, pad_ref, cpad_ref, s2_ref, carry_ref, acc_ref = rest[3 * npg + 1:]
    del pt_ref
    b_id = pl.program_id(0)
    s_id = pl.program_id(1)
    n_seq = pl.num_programs(0) - 1
    n_steps = n_pages // npg
    past = n_pages * page
    wa = n_heads * HEAD_DIM
    s_ref = s2_ref.at[b_id % 2]
    p_ref = s2_ref.at[(b_id + 1) % 2]
    k_phase = b_id < n_seq
    v_phase = b_id >= 1
    row_h = lax.broadcasted_iota(jnp.int32, (n_heads, wa), 0)
    col_h = lax.broadcasted_iota(jnp.int32, (n_heads, wa), 1) >> _log2(HEAD_DIM)
    head_diag = row_h == col_h

    def expand(c):
        out = c
        for t in range(1, t_new):
            out = out + pltpu.roll(c, t * n_heads, axis=1)
        return out

    @pl.when(k_phase & (s_id == 0))
    def _():
        qx_ref[...] = jnp.zeros_like(qx_ref)
        q = q_ref[...]
        for t in range(t_new):
            qx_ref[t * n_heads:(t + 1) * n_heads, :] = jnp.where(head_diag, q[t:t + 1, :], 0.0)
        qxb_ref[...] = qx_ref[...].astype(BF16)
        cpad_ref[...] = jnp.zeros_like(cpad_ref)
        carry_ref[...] = jnp.zeros_like(carry_ref)

    @pl.when(v_phase & (s_id == 0))
    def _():
        acc_ref[...] = jnp.zeros_like(acc_ref)

    def head_major(ref):
        return jnp.concatenate([ref[pl.ds(h, page, stride=n_heads), :] for h in range(n_heads)], axis=1)

    @pl.when(k_phase)
    def _():
        for r in range(npg):
            st = _dot_nt(head_major(k_refs[r]).astype(BF16), qxb_ref[...]) * scale
            cpad_ref[:, 0:n_heads] = c_refs[r][...].T
            cx = expand(cpad_ref[...]) + carry_ref[...]
            carry_ref[...] = cx[page - 1:page, :]
            off = pl.multiple_of((s_id * npg + r) * page, page)
            s_ref[pl.ds(off, page), :] = st - cx

    @pl.when(v_phase)
    def _():
        for r in range(npg):
            off = pl.multiple_of((s_id * npg + r) * page, page)
            p = p_ref[pl.ds(off, page), :].T.astype(BF16)
            acc_ref[...] += _dot(p, head_major(v_refs[r]).astype(BF16))

    @pl.when(v_phase & (s_id == n_steps - 1))
    def _():
        pad_ref[...] = jnp.zeros_like(pad_ref)
        pad_ref[0:t_new, :] = vn_ref[...]
        p = p_ref[past:past + page, :].T.astype(BF16)
        acc = acc_ref[...] + _dot(p, pad_ref[...].astype(BF16))
        for t in range(t_new):
            blk = acc[t * n_heads:(t + 1) * n_heads, :]
            o_t = jnp.sum(jnp.where(head_diag, blk, 0.0), axis=0, keepdims=True)
            for h in range(n_heads):
                sl = slice(h * HEAD_DIM, (h + 1) * HEAD_DIM)
                o_ref[t:t + 1, sl] = _rms(o_t[:, sl], g_ref[h:h + 1, :])

    @pl.when(k_phase & (s_id == n_steps - 1))
    def _():
        pad_ref[...] = jnp.zeros_like(pad_ref)
        pad_ref[0:t_new, :] = kn_ref[...]
        st = _dot_nt(pad_ref[...].astype(BF16), qxb_ref[...]) * scale
        lane = lax.broadcasted_iota(jnp.int32, (1, LANES), 1)
        cpad_ref[...] = jnp.zeros_like(cpad_ref)
        cpad_ref[0:t_new, :] = jnp.where(lane < n_heads, cl_ref[...], 0.0)
        cl_x = expand(cpad_ref[...])
        u = lax.broadcasted_iota(jnp.int32, (page, LANES), 0)
        t = lax.broadcasted_iota(jnp.int32, (page, LANES), 1) >> _log2(n_heads)
        st = jnp.where((u < t_new) & (u <= t), st - cl_x, NEG_INF)
        s_past = s_ref[0:past, :] + carry_ref[...]
        m = jnp.maximum(jnp.max(s_past, axis=0, keepdims=True), jnp.max(st, axis=0, keepdims=True))
        p_past = jnp.exp(s_past - m)
        p_new = jnp.exp(st - m)
        inv = 1.0 / (jnp.sum(p_past, axis=0, keepdims=True) + jnp.sum(p_new, axis=0, keepdims=True))
        s_ref[0:past, :] = p_past * inv
        s_ref[past:past + page, :] = p_new * inv


def _fox_sample(page_table, q, k_new, v_new, cl, g_out, cache_k, cache_v, cache_c, page, pool_base, n_heads):
    bsz, t_new, wa = q.shape
    n_pages = page_table.shape[1]
    assert page == LANES and n_heads == SUBLANES and n_heads * t_new <= LANES
    page_bytes = page * wa * 4
    npg = _pick_tile(n_pages, max(1, PAGE_BUFFER_BYTES // (4 * page_bytes)))
    n_steps = n_pages // npg
    scale = HEAD_DIM ** -0.5

    def k_seq(b):
        return jnp.minimum(b, bsz - 1)

    def v_seq(b):
        return jnp.maximum(b - 1, 0)

    def kv_map(seq_fn, r):
        return lambda b, s, pt: (pool_base + pt[seq_fn(b), s * npg + r], 0)

    def c_map(r):
        return lambda b, s, pt: (pt[k_seq(b), s * npg + r], 0, 0)

    k3 = lambda b, s, pt: (k_seq(b), 0, 0)
    v3 = lambda b, s, pt: (v_seq(b), 0, 0)
    in_specs = [
        pl.BlockSpec((None, t_new, wa), k3),
        pl.BlockSpec((None, t_new, wa), k3),
        pl.BlockSpec((None, t_new, wa), v3),
        pl.BlockSpec((None, t_new, LANES), k3),
        pl.BlockSpec((n_heads, HEAD_DIM), lambda b, s, pt: (0, 0)),
    ]
    in_specs += [pl.BlockSpec((page * n_heads, HEAD_DIM), kv_map(k_seq, r)) for r in range(npg)]
    in_specs += [pl.BlockSpec((None, n_heads, page), c_map(r)) for r in range(npg)]
    in_specs += [pl.BlockSpec((page * n_heads, HEAD_DIM), kv_map(v_seq, r)) for r in range(npg)]
    grid_spec = pltpu.PrefetchScalarGridSpec(
        num_scalar_prefetch=1,
        grid=(bsz + 1, n_steps),
        in_specs=in_specs,
        out_specs=pl.BlockSpec((None, t_new, wa), v3),
        scratch_shapes=[
            pltpu.VMEM((LANES, wa), F32),
            pltpu.VMEM((LANES, wa), BF16),
            pltpu.VMEM((page, wa), F32),
            pltpu.VMEM((page, LANES), F32),
            pltpu.VMEM((2, n_pages * page + page, LANES), F32),
            pltpu.VMEM((1, LANES), F32),
            pltpu.VMEM((LANES, wa), F32),
        ],
    )
    return pl.pallas_call(
        functools.partial(_fox_sample_body, npg=npg, n_pages=n_pages, page=page, n_heads=n_heads,
                          t_new=t_new, scale=scale),
        out_shape=jax.ShapeDtypeStruct((bsz, t_new, wa), F32),
        grid_spec=grid_spec,
        compiler_params=_cparams(("arbitrary", "arbitrary")),
        name="fox_sample",
    )(page_table, q, k_new, v_new, cl, g_out, *([cache_k] * npg), *([cache_c] * npg), *([cache_v] * npg))


def _bdot(a, b):
    return lax.dot_general(a, b, (((2,), (1,)), ((0,), (0,))), preferred_element_type=F32)


def _bdot_nt(a, b):
    return lax.dot_general(a, b, (((2,), (2,)), ((0,), (0,))), preferred_element_type=F32)


def _mlstm_chunk(qc, kc, vc, i_col, b_col, c_st, n_st, m_st, n_valid):
    nh, ln, _ = qc.shape
    r = lax.broadcasted_iota(jnp.int32, (ln, ln), 0)
    c = lax.broadcasted_iota(jnp.int32, (ln, ln), 1)
    eye = (r == c)[None]
    tril = (r >= c)[None]

    def col_to_row(col):
        return jnp.sum(jnp.where(eye, col, 0.0), axis=1, keepdims=True)

    i_row = col_to_row(i_col)
    b_row = col_to_row(b_col)
    a_col = b_col + m_st
    dmat = jnp.where(tril, b_col - b_row + i_row, NEG_INF)
    mt = jnp.maximum(a_col, jnp.max(dmat, axis=2, keepdims=True))
    qb = qc.astype(BF16)
    kb = kc.astype(BF16)
    vb = vc.astype(BF16)
    w_intra = jnp.exp(dmat - mt) * _bdot_nt(qb, kb)
    w_inter = jnp.exp(a_col - mt)
    num = w_inter * _bdot(qb, c_st.astype(BF16)) + _bdot(w_intra.astype(BF16), vb)
    den = w_inter * jnp.sum(qc * n_st, axis=2, keepdims=True) + jnp.sum(w_intra, axis=2, keepdims=True)
    h = num / jnp.maximum(jnp.abs(den), jnp.exp(-mt))
    last = n_valid - 1
    m_new = mt[:, last:last + 1, :]
    b_last = b_col[:, last:last + 1, :]
    w_state = jnp.exp(b_last - b_col + i_col - m_new)
    if n_valid < ln:
        row = lax.broadcasted_iota(jnp.int32, (1, ln, 1), 1)
        w_state = jnp.where(row < n_valid, w_state, 0.0)
    decay = jnp.exp(b_last + m_st - m_new)
    ks = w_state * kc
    ks_t = jnp.stack([ks[hd].T for hd in range(nh)]).astype(BF16)
    c_new = decay * c_st + _bdot(ks_t, vb)
    n_new = decay * n_st + jnp.sum(ks, axis=1, keepdims=True)
    return h, c_new, n_new, m_new


def _heads(x, n_heads, offset=0, width=HEAD_DIM):
    return jnp.stack([x[:, offset + h * width:offset + (h + 1) * width] for h in range(n_heads)])


def _mlstm_prompt_body(qk_ref, w_ref, b_ref, v_ref, o_ref, act_ref, cl_ref, g_ref,
                       y_ref, c_out_ref, n_out_ref, m_out_ref,
                       ext_ref, qk_scr, c_ref, n_ref, m_ref, *, rb, chunk, n_heads):
    blk = pl.program_id(1)
    pad = SUBLANES
    wm = n_heads * HEAD_DIM
    kscale = HEAD_DIM ** -0.5

    @pl.when(blk == 0)
    def _():
        ext_ref[0:pad, :] = jnp.zeros((pad, 2 * wm), F32)
        c_ref[...] = jnp.zeros_like(c_ref)
        n_ref[...] = jnp.zeros_like(n_ref)
        m_ref[...] = jnp.zeros_like(m_ref)

    ext_ref[pad:pad + rb, :] = qk_ref[...]
    acc = b_ref[...] + w_ref[CONV_W - 1:CONV_W, :] * qk_ref[...]
    for j in range(CONV_W - 1):
        off = pad - (CONV_W - 1) + j
        acc = acc + w_ref[j:j + 1, :] * ext_ref[off:off + rb, :]
    qk_scr[...] = acc * jax.nn.sigmoid(acc)
    ext_ref[0:pad, :] = qk_ref[rb - pad:rb, :]

    g = g_ref[...][:, None, :]
    for ci in range(rb // chunk):
        rows = slice(ci * chunk, (ci + 1) * chunk)
        qk = qk_scr[rows, :]
        hh, c_new, n_new, m_new = _mlstm_chunk(
            _heads(qk, n_heads), _heads(qk, n_heads, wm) * kscale, _heads(v_ref[rows, :], n_heads),
            _heads(act_ref[rows, :], n_heads, n_heads, 1), _heads(cl_ref[rows, :], n_heads, 2 * n_heads, 1),
            c_ref[...], n_ref[...], m_ref[:, :, 0:1], chunk)
        c_ref[...] = c_new
        n_ref[...] = n_new
        m_ref[...] = jnp.broadcast_to(m_new, m_ref.shape)
        y = jax.nn.sigmoid(_heads(o_ref[rows, :], n_heads)) * _rms(hh, g)
        for h in range(n_heads):
            y_ref[rows, h * HEAD_DIM:(h + 1) * HEAD_DIM] = y[h].astype(y_ref.dtype)

    @pl.when(blk == pl.num_programs(1) - 1)
    def _():
        c_out_ref[...] = c_ref[...]
        n_out_ref[...] = n_ref[:, 0, :]
        m_out_ref[...] = m_ref[:, 0, :]


def _mlstm_prompt(qkm, conv_w, conv_b, vm, om, act, cl, g_out, batch, seq, n_heads):
    n = qkm.shape[0]
    wm = n_heads * HEAD_DIM
    chunk = math.gcd(seq, MLSTM_CHUNK)
    rb = _pick_tile(seq, 2 * chunk)
    assert rb % chunk == 0 and rb >= SUBLANES
    nblk = seq // rb
    row = lambda b, i: (b * nblk + i, 0)
    const = lambda b, i: (0, 0)
    st = lambda b, i: (b, 0, 0)
    out_shape = (
        jax.ShapeDtypeStruct((n, wm), BF16),
        jax.ShapeDtypeStruct((batch, n_heads, HEAD_DIM, HEAD_DIM), F32),
        jax.ShapeDtypeStruct((batch, n_heads, HEAD_DIM), F32),
        jax.ShapeDtypeStruct((batch, n_heads, LANES), F32),
    )
    return pl.pallas_call(
        functools.partial(_mlstm_prompt_body, rb=rb, chunk=chunk, n_heads=n_heads),
        out_shape=out_shape,
        grid=(batch, nblk),
        in_specs=[
            pl.BlockSpec((rb, 2 * wm), row),
            pl.BlockSpec((CONV_W, 2 * wm), const),
            pl.BlockSpec((1, 2 * wm), const),
            pl.BlockSpec((rb, wm), row),
            pl.BlockSpec((rb, wm), row),
            pl.BlockSpec((rb, LANES), row),
            pl.BlockSpec((rb, LANES), row),
            pl.BlockSpec((n_heads, HEAD_DIM), const),
        ],
        out_specs=(
            pl.BlockSpec((rb, wm), row),
            pl.BlockSpec((None, n_heads, HEAD_DIM, HEAD_DIM), lambda b, i: (b, 0, 0, 0)),
            pl.BlockSpec((None, n_heads, HEAD_DIM), st),
            pl.BlockSpec((None, n_heads, LANES), st),
        ),
        scratch_shapes=[
            pltpu.VMEM((rb + SUBLANES, 2 * wm), F32),
            pltpu.VMEM((rb, 2 * wm), F32),
            pltpu.VMEM((n_heads, HEAD_DIM, HEAD_DIM), F32),
            pltpu.VMEM((n_heads, 1, HEAD_DIM), F32),
            pltpu.VMEM((n_heads, 1, LANES), F32),
        ],
        compiler_params=_cparams(("parallel", "arbitrary")),
        name="mlstm_prompt",
    )(qkm, conv_w, conv_b, vm, om, act, cl, g_out)


def _mlstm_sample_body(qk_ref, prev_ref, w_ref, b_ref, v_ref, o_ref, act_ref, cl_ref, g_ref, c0_ref, n0_ref, m0_ref,
                       y_ref, c_out_ref, n_out_ref, m_out_ref, ext_ref, pad_ref, *, t_new, n_heads, sb):
    wm = n_heads * HEAD_DIM
    rows = SUBLANES
    kscale = HEAD_DIM ** -0.5
    valid = lax.broadcasted_iota(jnp.int32, (rows, 1), 0) < t_new

    def padded(val, width):
        pad_ref[:, 0:width] = jnp.zeros((rows, width), F32)
        pad_ref[0:t_new, 0:width] = val
        return pad_ref[:, 0:width]

    qs, ks, vs, os_, i_cols, b_cols, n0s, m0s = [], [], [], [], [], [], [], []
    for s in range(sb):
        tok = slice(s * t_new, (s + 1) * t_new)
        ext_ref[...] = jnp.zeros_like(ext_ref)
        ext_ref[0:CONV_W - 1, :] = prev_ref[s]
        ext_ref[CONV_W - 1:CONV_W - 1 + t_new, :] = qk_ref[tok, :]
        acc = b_ref[...] + w_ref[0:1, :] * ext_ref[0:rows, :]
        for j in range(1, CONV_W):
            acc = acc + w_ref[j:j + 1, :] * ext_ref[j:j + rows, :]
        qk = jnp.where(valid, acc * jax.nn.sigmoid(acc), 0.0)
        qs.append(_heads(qk, n_heads))
        ks.append(_heads(qk, n_heads, wm) * kscale)
        vs.append(_heads(padded(v_ref[tok, :], wm), n_heads))
        os_.append(_heads(padded(o_ref[tok, :], wm), n_heads))
        i_cols.append(jnp.where(valid, _heads(padded(act_ref[tok, :], LANES), n_heads, n_heads, 1), NEG_INF))
        b_col = _heads(padded(cl_ref[tok, :], LANES), n_heads, 2 * n_heads, 1)
        b_cols.append(jnp.where(valid, b_col, b_col[:, t_new - 1:t_new, :]))
        n0s.append(jnp.stack([n0_ref[s, h:h + 1, :] for h in range(n_heads)]))
        m0s.append(jnp.stack([m0_ref[s, h:h + 1, :] for h in range(n_heads)]))

    cat = lambda parts: jnp.concatenate(parts, axis=0)
    c0 = cat([c0_ref[s] for s in range(sb)])
    hh, c_new, n_new, m_new = _mlstm_chunk(cat(qs), cat(ks), cat(vs), cat(i_cols), cat(b_cols),
                                           c0, cat(n0s), cat(m0s), t_new)
    g = jnp.concatenate([g_ref[...]] * sb, axis=0)[:, None, :]
    y = jax.nn.sigmoid(cat(os_)) * _rms(hh, g)
    m_wide = jnp.broadcast_to(m_new, (sb * n_heads, 1, LANES))
    for s in range(sb):
        for h in range(n_heads):
            y_ref[s * t_new:(s + 1) * t_new, h * HEAD_DIM:(h + 1) * HEAD_DIM] = y[s * n_heads + h, 0:t_new, :]
        grp = slice(s * n_heads, (s + 1) * n_heads)
        c_out_ref[s] = c_new[grp]
        n_out_ref[s] = n_new[grp, 0, :]
        m_out_ref[s] = m_wide[grp, 0, :]


def _mlstm_sample(qkm, conv_prev, conv_w, conv_b, vm, om, act, cl, g_out, c0, n0, m0, t_new, n_heads):
    n, wm2 = qkm.shape
    bsz = n // t_new
    wm = wm2 // 2
    assert t_new + CONV_W - 1 <= SUBLANES
    sb = _pick_tile(bsz, 4)
    assert (sb * t_new) % SUBLANES == 0 or sb == bsz
    tok = lambda b: (b, 0)
    seq3 = lambda b: (b, 0, 0)
    const = lambda b: (0, 0)
    out_shape = (
        jax.ShapeDtypeStruct((n, wm), F32),
        jax.ShapeDtypeStruct((bsz, n_heads, HEAD_DIM, HEAD_DIM), F32),
        jax.ShapeDtypeStruct((bsz, n_heads, HEAD_DIM), F32),
        jax.ShapeDtypeStruct((bsz, n_heads, LANES), F32),
    )
    return pl.pallas_call(
        functools.partial(_mlstm_sample_body, t_new=t_new, n_heads=n_heads, sb=sb),
        out_shape=out_shape,
        grid=(bsz // sb,),
        in_specs=[
            pl.BlockSpec((sb * t_new, wm2), tok),
            pl.BlockSpec((sb, CONV_W - 1, wm2), seq3),
            pl.BlockSpec((CONV_W, wm2), const),
            pl.BlockSpec((1, wm2), const),
            pl.BlockSpec((sb * t_new, wm), tok),
            pl.BlockSpec((sb * t_new, wm), tok),
            pl.BlockSpec((sb * t_new, LANES), tok),
            pl.BlockSpec((sb * t_new, LANES), tok),
            pl.BlockSpec((n_heads, HEAD_DIM), const),
            pl.BlockSpec((sb, n_heads, HEAD_DIM, HEAD_DIM), lambda b: (b, 0, 0, 0)),
            pl.BlockSpec((sb, n_heads, HEAD_DIM), seq3),
            pl.BlockSpec((sb, n_heads, 1), seq3),
        ],
        out_specs=(
            pl.BlockSpec((sb * t_new, wm), tok),
            pl.BlockSpec((sb, n_heads, HEAD_DIM, HEAD_DIM), lambda b: (b, 0, 0, 0)),
            pl.BlockSpec((sb, n_heads, HEAD_DIM), seq3),
            pl.BlockSpec((sb, n_heads, LANES), seq3),
        ),
        scratch_shapes=[pltpu.VMEM((2 * SUBLANES, wm2), F32), pltpu.VMEM((SUBLANES, wm), F32)],
        compiler_params=_cparams(("parallel",)),
        name="mlstm_sample",
    )(qkm, conv_prev, conv_w, conv_b, vm, om, act, cl, g_out, c0, n0, m0)


def _outproj_body(ya_ref, ym_ref, w_ref, x_ref, o_ref, *, wa):
    y = _dot(ya_ref[...].astype(BF16), w_ref[0:wa, :]) + _dot(ym_ref[...].astype(BF16), w_ref[wa:, :])
    o_ref[...] = x_ref[...] + y


def _outproj(ya, ym, w, x):
    n, d = x.shape
    wa = ya.shape[1]
    wm = ym.shape[1]
    tm = _pick_tile(n, 512)
    row = lambda i: (i, 0)
    return pl.pallas_call(
        functools.partial(_outproj_body, wa=wa),
        out_shape=jax.ShapeDtypeStruct((n, d), F32),
        grid=(n // tm,),
        in_specs=[
            pl.BlockSpec((tm, wa), row),
            pl.BlockSpec((tm, wm), row),
            pl.BlockSpec((wa + wm, d), lambda i: (0, 0)),
            pl.BlockSpec((tm, d), row),
        ],
        out_specs=pl.BlockSpec((tm, d), row),
        compiler_params=_cparams(("parallel",)),
        name="outproj",
    )(ya, ym, w, x)


def _cast_pad_body(x_ref, o_ref, *, axis, size):
    if axis == 1:
        o_ref[:, 0:size] = x_ref[...].astype(BF16)
        o_ref[:, size:] = jnp.zeros((o_ref.shape[0], o_ref.shape[1] - size), BF16)
    else:
        o_ref[0:size, :] = x_ref[...].astype(BF16)
        o_ref[size:, :] = jnp.zeros((o_ref.shape[0] - size, o_ref.shape[1]), BF16)


def _cast_pad(w, axis, padded):
    r, c = w.shape
    size = w.shape[axis]
    assert padded > size
    if axis == 1:
        t = _pick_tile(r, 256)
        grid, in_blk, out_blk, imap = (r // t,), (t, c), (t, padded), (lambda i: (i, 0))
        out_shape = (r, padded)
    else:
        t = _pick_tile(c, 256)
        grid, in_blk, out_blk, imap = (c // t,), (r, t), (padded, t), (lambda i: (0, i))
        out_shape = (padded, c)
    return pl.pallas_call(
        functools.partial(_cast_pad_body, axis=axis, size=size),
        out_shape=jax.ShapeDtypeStruct(out_shape, BF16),
        grid=grid,
        in_specs=[pl.BlockSpec(in_blk, imap)],
        out_specs=pl.BlockSpec(out_blk, imap),
        compiler_params=_cparams(("parallel",)),
        name="cast_pad",
    )(w)


def _win_prep_body(wt_ref, fa_ref, im_ref, main_ref, gates_ref, *, n_gate_cols):
    main_ref[...] = wt_ref[...].T.astype(BF16)

    @pl.when(pl.program_id(0) == 0)
    def _():
        d = gates_ref.shape[0]
        g = jnp.concatenate([fa_ref[...], im_ref[...], jnp.zeros((LANES - n_gate_cols, d), F32)], axis=0)
        gates_ref[...] = g.T.astype(BF16)


def _win_prep(w_in_t, cuts, n_gate_cols, tile):
    n_in, d = w_in_t.shape
    o_f, o_qk, o_i = cuts
    n_main = o_f + (o_i - o_qk)
    assert o_f % tile == 0 and (o_i - o_qk) % tile == 0 and o_qk % SUBLANES == 0 and o_i % SUBLANES == 0
    n_lead = o_f // tile

    def row_off(j):
        return pl.multiple_of(jnp.where(j < n_lead, j * tile, o_qk + (j - n_lead) * tile), SUBLANES)

    el = pl.Element
    return pl.pallas_call(
        functools.partial(_win_prep_body, n_gate_cols=n_gate_cols),
        out_shape=(jax.ShapeDtypeStruct((d, n_main), BF16), jax.ShapeDtypeStruct((d, LANES), BF16)),
        grid=(n_main // tile,),
        in_specs=[
            pl.BlockSpec((el(tile), el(d)), lambda j: (row_off(j), 0)),
            pl.BlockSpec((el(o_qk - o_f), el(d)), lambda j: (o_f, 0)),
            pl.BlockSpec((el(n_in - o_i), el(d)), lambda j: (o_i, 0)),
        ],
        out_specs=(pl.BlockSpec((d, tile), lambda j: (0, j)), pl.BlockSpec((d, LANES), lambda j: (0, 0))),
        compiler_params=_cparams(("arbitrary",)),
        name="win_prep",
    )(w_in_t, w_in_t, w_in_t)


def _prep_weights(lw, n_heads_a, n_heads_m):
    wa = n_heads_a * HEAD_DIM
    wm = n_heads_m * HEAD_DIM
    f = lw['w1_gate'].shape[1]
    tf = 512
    fp = tf * ((f + tf - 1) // tf)

    def ffn_w(wg, wu, wd):
        if fp == f:
            return wg.astype(BF16), wu.astype(BF16), wd.astype(BF16)
        return _cast_pad(wg, 1, fp), _cast_pad(wu, 1, fp), _cast_pad(wd, 0, fp)

    w_in = lw['w_in']
    o_f = 3 * wa
    o_qk = o_f + n_heads_a
    o_i = o_qk + 2 * wm + 2 * wm
    o_fm = o_i + n_heads_m
    ng = 2 * n_heads_m + n_heads_a
    assert o_fm + n_heads_m == w_in.shape[1]
    w_main, w_gates = _win_prep(jnp.swapaxes(w_in, 0, 1), (o_f, o_qk, o_i), ng, wa)
    bias = jnp.concatenate([lw['b_fox_f'], lw['b_m_i'], lw['b_m_f'], jnp.zeros((LANES - ng,), F32)])[None, :]
    return {
        'ffn1': ffn_w(lw['w1_gate'], lw['w1_up'], lw['w1_down']),
        'ffn2': ffn_w(lw['w2_gate'], lw['w2_up'], lw['w2_down']),
        'tf': tf,
        'w_main': w_main,
        'w_gates': w_gates,
        'gate_bias': bias,
        'w_out': lw['w_out'].astype(BF16),
    }


def kernel(x_prompt, x_sample, cache_k, cache_v, cache_logf, state_conv, state_C, state_n, state_m, page_table,
           g_ffn1, w1_gate, w1_up, w1_down, g_mix, w_in, b_fox_f, b_m_i, b_m_f, conv_w, conv_b, g_q, g_k,
           g_out_a, g_out_m, w_out, g_ffn2, w2_gate, w2_up, w2_down):
    depth = w_in.shape[0]
    bp, seq, d = x_prompt.shape
    bs, t_new, _ = x_sample.shape
    n_heads_a = g_out_a.shape[1]
    n_heads_m = g_out_m.shape[1]
    wa = n_heads_a * HEAD_DIM
    wm = n_heads_m * HEAD_DIM
    n_pool, page = cache_k.shape[1], cache_k.shape[2]

    yp = x_prompt.reshape(bp * seq, d)
    ys = x_sample.reshape(bs * t_new, d)
    outs = [[] for _ in range(14)]
    for l in range(depth):
        lw = {
            'w1_gate': w1_gate[l], 'w1_up': w1_up[l], 'w1_down': w1_down[l], 'w_in': w_in[l],
            'b_fox_f': b_fox_f[l], 'b_m_i': b_m_i[l], 'b_m_f': b_m_f[l], 'w_out': w_out[l],
            'w2_gate': w2_gate[l], 'w2_up': w2_up[l], 'w2_down': w2_down[l],
        }
        pw = _prep_weights(lw, n_heads_a, n_heads_m)
        gf1 = g_ffn1[l][None, :]
        gf2 = g_ffn2[l][None, :]
        gmix = g_mix[l][None, :]
        gq = g_q[l][None, :]
        gk = g_k[l][None, :]
        cw = conv_w[l]
        cb = conv_b[l][None, :]
        goa = g_out_a[l]
        gom = g_out_m[l]

        x1, xn = _ffn(yp, gf1, *pw['ffn1'], pw['tf'], g_next=gmix)
        q, k, kb, v, vt, qkm, vm, om, gt = _inproj(xn, pw['w_main'], pw['w_gates'], gq, gk, n_heads_a)
        act, cg, cl = _gates(gt, pw['gate_bias'], seq, math.gcd(seq, MLSTM_CHUNK), n_heads_a)
        ya = _fox_prompt(q, kb, vt, cg, goa, bp, seq, n_heads_a)
        ym, c_p, n_p, m_p = _mlstm_prompt(qkm, cw, cb, vm, om, act, cl, gom, bp, seq, n_heads_m)
        x2 = _outproj(ya, ym, pw['w_out'], x1)
        yp = _ffn(x2, gf2, *pw['ffn2'], pw['tf'])
        outs[0].append(k.reshape(bp, seq, n_heads_a, HEAD_DIM))
        outs[1].append(v.reshape(bp, seq, n_heads_a, HEAD_DIM))
        outs[2].append(act[:, :n_heads_a].reshape(bp, seq, n_heads_a))
        outs[3].append(qkm.reshape(bp, seq, 2 * wm)[:, seq - (CONV_W - 1):, :])
        outs[4].append(c_p)
        outs[5].append(n_p)
        outs[6].append(m_p[:, :, 0])

        x1, xn = _ffn(ys, gf1, *pw['ffn1'], pw['tf'], g_next=gmix)
        q, k, kb, v, vt, qkm, vm, om, gt = _inproj(xn, pw['w_main'], pw['w_gates'], gq, gk, n_heads_a)
        act, cg, cl = _gates(gt, pw['gate_bias'], t_new, math.gcd(t_new, MLSTM_CHUNK), n_heads_a)
        r3 = lambda a: a.reshape(bs, t_new, a.shape[-1])
        lf_rows = jnp.swapaxes(cache_logf[l], 1, 2).reshape(n_pool * n_heads_a, page)
        cache_c = _pool_cumsum(lf_rows).reshape(n_pool, n_heads_a, page)
        ya = _fox_sample(page_table, r3(q).astype(F32), r3(k), r3(v), r3(cg), goa,
                         cache_k.reshape(-1, HEAD_DIM), cache_v.reshape(-1, HEAD_DIM),
                         cache_c, page, l * n_pool, n_heads_a)
        ym, c_s, n_s, m_s = _mlstm_sample(qkm, state_conv[l], cw, cb, vm, om, act, cl, gom,
                                          state_C[l], state_n[l], state_m[l][:, :, None], t_new, n_heads_m)
        x2 = _outproj(ya.reshape(bs * t_new, wa), ym, pw['w_out'], x1)
        ys = _ffn(x2, gf2, *pw['ffn2'], pw['tf'])
        conv_ext = jnp.concatenate([state_conv[l], r3(qkm)], axis=1)
        outs[7].append(k.reshape(bs, t_new, n_heads_a, HEAD_DIM))
        outs[8].append(v.reshape(bs, t_new, n_heads_a, HEAD_DIM))
        outs[9].append(act[:, :n_heads_a].reshape(bs, t_new, n_heads_a))
        outs[10].append(conv_ext[:, t_new:, :])
        outs[11].append(c_s)
        outs[12].append(n_s)
        outs[13].append(m_s[:, :, 0])

    return (yp.reshape(bp, seq, d), ys.reshape(bs, t_new, d)) + tuple(jnp.stack(o) for o in outs)
```

```python
import functools
import math

import jax
import jax.numpy as jnp
from jax import lax
from jax.experimental import pallas as pl
from jax.experimental.pallas import tpu as pltpu

F32 = jnp.float32
BF16 = jnp.bfloat16
HIGHEST = lax.Precision.HIGHEST

EPS = 1e-6
HEAD_DIM = 128
CONV_W = 4
MLSTM_CHUNK = 128
LANES = 128
SUBLANES = 8
VMEM_LIMIT_BYTES = 56 * 1024 * 1024
PAGE_BUFFER_BYTES = 32 * 1024 * 1024
NEG_INF = float("-inf")
LOG2E = math.log2(math.e)


def _cparams(sem):
    return pltpu.CompilerParams(dimension_semantics=sem, vmem_limit_bytes=VMEM_LIMIT_BYTES)


def _rms(x, g):
    y = x * lax.rsqrt(jnp.mean(x * x, axis=-1, keepdims=True) + EPS)
    return y * g


def _log_sigmoid(x):
    return jnp.minimum(x, 0.0) - jnp.log1p(jnp.exp(-jnp.abs(x)))


def _dot(a, b, precision=None):
    return jnp.dot(a, b, preferred_element_type=F32, precision=precision)


def _dot_nt(a, b, precision=None):
    return lax.dot_general(a, b, (((1,), (1,)), ((), ())), preferred_element_type=F32, precision=precision)


def _pick_tile(n, pref):
    t = min(n, pref)
    while n % t:
        t //= 2
    return t


def _log2(n):
    k = int(math.log2(n))
    assert 1 << k == n
    return k


def _ffn_body(x_ref, g_ref, wg_ref, wu_ref, wd_ref, gn_ref, *rest, with_next, n_prep, prep_real, prep_slabs):
    prep_in = rest[:n_prep]
    rest = rest[n_prep:]
    if with_next:
        o_ref, on_ref = rest[:2]
        rest = rest[2:]
    else:
        o_ref = rest[0]
        rest = rest[1:]
    prep_out = rest[:n_prep]
    xn_ref, acc_ref = rest[n_prep:]
    j = pl.program_id(1)

    @pl.when(j == 0)
    def _():
        xn_ref[...] = _rms(x_ref[...], g_ref[...]).astype(BF16)
        acc_ref[...] = jnp.zeros_like(acc_ref)

    xn = xn_ref[...]
    gate = _dot(xn, wg_ref[...])
    up = _dot(xn, wu_ref[...])
    h = (gate * jax.nn.sigmoid(gate)) * up
    acc_ref[...] += _dot(h.astype(BF16), wd_ref[...])

    @pl.when(j == pl.num_programs(1) - 1)
    def _():
        y = x_ref[...] + 0.5 * acc_ref[...]
        o_ref[...] = y
        if with_next:
            on_ref[...] = _rms(y, gn_ref[...]).astype(BF16)

    t = pl.program_id(0) * pl.num_programs(1) + j
    for k in range(n_prep):
        t0 = k * prep_slabs

        @pl.when((t >= t0) & (t < t0 + prep_real))
        def _(k=k):
            prep_out[k][...] = prep_in[k][...].astype(BF16)

        @pl.when((t >= t0 + prep_real) & (t < t0 + prep_slabs))
        def _(k=k):
            prep_out[k][...] = jnp.zeros_like(prep_out[k])


def _ffn(x, g, wg, wu, wd, tf, g_next=None, prep=None):
    n, d = x.shape
    fp = wg.shape[1]
    tm = _pick_tile(n, 512)
    ni, nj = n // tm, fp // tf
    with_next = g_next is not None
    row = pl.BlockSpec((tm, d), lambda i, j: (i, 0))
    vec = pl.BlockSpec((1, d), lambda i, j: (0, 0))
    out_shape = [jax.ShapeDtypeStruct((n, d), F32)]
    out_specs = [row]
    if with_next:
        out_shape.append(jax.ShapeDtypeStruct((n, d), BF16))
        out_specs.append(row)
    in_specs = [
        row, vec,
        pl.BlockSpec((d, tf), lambda i, j: (0, j)),
        pl.BlockSpec((d, tf), lambda i, j: (0, j)),
        pl.BlockSpec((tf, d), lambda i, j: (j, 0)),
        vec,
    ]
    operands = [x, g, wg, wu, wd, g_next if with_next else g]
    n_prep = prep_real = prep_slabs = 0
    if prep is not None:
        f = prep[0].shape[1]
        assert f % LANES == 0 and fp % LANES == 0
        n_prep, prep_real, prep_slabs = 3, f // LANES, fp // LANES
        assert ni * nj >= n_prep * prep_slabs

        def slab(k, last):
            return lambda i, j: jnp.clip(i * nj + j - k * prep_slabs, 0, last)

        for k in range(n_prep):
            cols = k < 2
            blk = (d, LANES) if cols else (LANES, d)
            for last, specs in ((prep_real - 1, in_specs), (prep_slabs - 1, out_specs)):
                s = slab(k, last)
                specs.append(pl.BlockSpec(blk, (lambda i, j, s=s: (0, s(i, j))) if cols
                                          else (lambda i, j, s=s: (s(i, j), 0))))
            out_shape.append(jax.ShapeDtypeStruct((d, fp) if cols else (fp, d), BF16))
        operands += list(prep)
    res = pl.pallas_call(
        functools.partial(_ffn_body, with_next=with_next, n_prep=n_prep, prep_real=prep_real,
                          prep_slabs=prep_slabs),
        out_shape=tuple(out_shape),
        grid=(ni, nj),
        in_specs=in_specs,
        out_specs=tuple(out_specs),
        scratch_shapes=[pltpu.VMEM((tm, d), BF16), pltpu.VMEM((tm, d), F32)],
        compiler_params=_cparams(("arbitrary", "arbitrary") if n_prep else ("parallel", "arbitrary")),
        name="ffn",
    )(*operands)
    n_main = 2 if with_next else 1
    main = res[0] if n_main == 1 else tuple(res[:2])
    return (main, tuple(res[n_main:])) if prep is not None else main


def _inproj_body(xn_ref, w_ref, wgt_ref, gq_ref, gk_ref,
                 q_ref, k_ref, kb_ref, v_ref, vt_ref, qkm_ref, vm_ref, om_ref, gt_ref, *, n_heads):
    j = pl.program_id(0)
    xn = xn_ref[...]
    acc = _dot(xn, w_ref[...])

    @pl.when(j == 0)
    def _():
        for h in range(n_heads):
            sl = slice(h * HEAD_DIM, (h + 1) * HEAD_DIM)
            q_ref[:, sl] = _rms(acc[:, sl], gq_ref[...]).astype(BF16)
        gt_ref[...] = _dot(xn, wgt_ref[...])

    @pl.when(j == 1)
    def _():
        for h in range(n_heads):
            sl = slice(h * HEAD_DIM, (h + 1) * HEAD_DIM)
            kn = _rms(acc[:, sl], gk_ref[...])
            k_ref[:, sl] = kn
            kb_ref[:, sl] = kn.astype(BF16)

    @pl.when(j == 2)
    def _():
        v_ref[...] = acc
        vt_ref[...] = acc.T.astype(BF16)

    @pl.when((j == 3) | (j == 4))
    def _():
        qkm_ref[...] = acc

    @pl.when(j == 5)
    def _():
        vm_ref[...] = acc

    @pl.when(j == 6)
    def _():
        om_ref[...] = acc


def _inproj(xn, w_main, w_gates, gq, gk, n_heads):
    n, d = xn.shape
    wa = n_heads * HEAD_DIM
    assert w_main.shape[1] == 7 * wa
    tm = _pick_tile(n, 512)
    ni = n // tm
    const = lambda j, i: (0, 0)

    def rows_at(j, i, j0, j1):
        return jnp.where(j < j0, 0, jnp.where(j > j1, ni - 1, i))

    def row_map(j0, j1=None):
        j1 = j0 if j1 is None else j1
        return lambda j, i: (rows_at(j, i, j0, j1), 0)

    out_shape = (
        jax.ShapeDtypeStruct((n, wa), BF16),
        jax.ShapeDtypeStruct((n, wa), F32),
        jax.ShapeDtypeStruct((n, wa), BF16),
        jax.ShapeDtypeStruct((n, wa), F32),
        jax.ShapeDtypeStruct((wa, n), BF16),
        jax.ShapeDtypeStruct((n, 2 * wa), F32),
        jax.ShapeDtypeStruct((n, wa), F32),
        jax.ShapeDtypeStruct((n, wa), F32),
        jax.ShapeDtypeStruct((n, LANES), F32),
    )
    out_specs = (
        pl.BlockSpec((tm, wa), row_map(0)),
        pl.BlockSpec((tm, wa), row_map(1)),
        pl.BlockSpec((tm, wa), row_map(1)),
        pl.BlockSpec((tm, wa), row_map(2)),
        pl.BlockSpec((wa, tm), lambda j, i: (0, rows_at(j, i, 2, 2))),
        pl.BlockSpec((tm, wa), lambda j, i: (rows_at(j, i, 3, 4), jnp.clip(j - 3, 0, 1))),
        pl.BlockSpec((tm, wa), row_map(5)),
        pl.BlockSpec((tm, wa), row_map(6)),
        pl.BlockSpec((tm, LANES), row_map(0)),
    )
    return pl.pallas_call(
        functools.partial(_inproj_body, n_heads=n_heads),
        out_shape=out_shape,
        grid=(7, ni),
        in_specs=[
            pl.BlockSpec((tm, d), lambda j, i: (i, 0)),
            pl.BlockSpec((d, wa), lambda j, i: (0, j)),
            pl.BlockSpec((d, LANES), const),
            pl.BlockSpec((1, HEAD_DIM), const),
            pl.BlockSpec((1, HEAD_DIM), const),
        ],
        out_specs=out_specs,
        compiler_params=_cparams(("arbitrary", "arbitrary")),
        name="inproj",
    )(xn, w_main, w_gates, gq, gk)


def _split3(x):
    hi = x.astype(BF16)
    r1 = x - hi.astype(F32)
    mid = r1.astype(BF16)
    lo = (r1 - mid.astype(F32)).astype(BF16)
    return hi, mid, lo


def _dot_01(m01, x):
    hi, mid, lo = _split3(x)
    return _dot(m01, hi) + _dot(m01, mid) + _dot(m01, lo)


def _gates_body(g_ref, b_ref, act_ref, cg_ref, cl_ref, *, seg, lc, rows, n_heads):
    ch = min(rows, 256)
    sg = min(seg, ch)
    lane = lax.broadcasted_iota(jnp.int32, (1, LANES), 1)
    is_logsig = (lane < n_heads) | ((lane >= 2 * n_heads) & (lane < 3 * n_heads))
    r = lax.broadcasted_iota(jnp.int32, (ch, ch), 0)
    c = lax.broadcasted_iota(jnp.int32, (ch, ch), 1)
    sh_g = _log2(sg)
    sh_l = _log2(lc)
    tri_g = jnp.where((r >= c) & ((r >> sh_g) == (c >> sh_g)), 1.0, 0.0).astype(BF16)
    tri_l = jnp.where((r >= c) & ((r >> sh_l) == (c >> sh_l)), 1.0, 0.0).astype(BF16)
    carry = jnp.zeros((1, LANES), F32)
    for ci in range(rows // ch):
        sl = slice(ci * ch, (ci + 1) * ch)
        x = g_ref[sl, :] + b_ref[...]
        act = jnp.where(is_logsig, _log_sigmoid(x), x)
        act_ref[sl, :] = act
        cg = _dot_01(tri_g, act)
        if seg > ch:
            cg = cg + carry
            carry = cg[ch - 1:ch, :]
        cg_ref[sl, :] = cg
        cl_ref[sl, :] = _dot_01(tri_l, act)


def _gates(gates, bias, seg, lc, n_heads):
    n = gates.shape[0]
    rows = seg if seg >= 256 else _pick_tile(n, 256)
    blk = pl.BlockSpec((rows, LANES), lambda b: (b, 0))
    return pl.pallas_call(
        functools.partial(_gates_body, seg=seg, lc=lc, rows=rows, n_heads=n_heads),
        out_shape=(jax.ShapeDtypeStruct((n, LANES), F32),) * 3,
        grid=(n // rows,),
        in_specs=[blk, pl.BlockSpec((1, LANES), lambda b: (0, 0))],
        out_specs=(blk, blk, blk),
        compiler_params=_cparams(("parallel",)),
        name="gates",
    )(gates, bias)


def _fox_prompt_body(q_ref, k_ref, vt_ref, c_ref, g_ref, o_ref,
                     qtail_ref, ktail_ref, m_ref, l_ref, acc_ref, *, tile, scale, n_heads):
    qi = pl.program_id(1)
    ki = pl.program_id(2)
    exp2_scale = scale * LOG2E

    def head_cols(h):
        return pl.ds(pl.multiple_of(h * HEAD_DIM, HEAD_DIM), HEAD_DIM)

    @pl.when((qi == 0) & (ki == 0))
    def _():
        lane = lax.broadcasted_iota(jnp.int32, (1, LANES), 1)

        def tails(h, carry):
            c = jnp.sum(jnp.where(lane == h, c_ref[...], 0.0), axis=1, keepdims=True) * (1.0 / scale)
            p1, p2, p3 = (p.astype(F32) for p in _split3(c))
            one = jnp.ones_like(p1)

            def tail(cols):
                t = jnp.zeros((c.shape[0], LANES), F32)
                for j, col in enumerate(cols):
                    t = jnp.where(lane == j, col, t)
                return t.astype(BF16)

            qtail_ref[h] = tail((p1, p2, p3, one, one, one))
            ktail_ref[h] = tail((one, one, one, -p1, -p2, -p3))
            return carry

        lax.fori_loop(0, n_heads, tails, 0)

    @pl.when(ki == 0)
    def _():
        m_ref[...] = jnp.full_like(m_ref, NEG_INF)
        l_ref[...] = jnp.zeros_like(l_ref)
        acc_ref[...] = jnp.zeros_like(acc_ref)

    def block(masked):
        q_rows = pl.ds(pl.multiple_of(qi * tile, tile), tile)
        k_rows = pl.ds(pl.multiple_of(ki * tile, tile), tile)

        def one_head(h, carry):
            q_aug = jnp.concatenate([q_ref[:, head_cols(h)], qtail_ref[h, q_rows, :]], axis=1)
            k_aug = jnp.concatenate([k_ref[:, head_cols(h)], ktail_ref[h, k_rows, :]], axis=1)
            st = _dot_nt(k_aug, q_aug)
            if masked:
                kpos = lax.broadcasted_iota(jnp.int32, (tile, tile), 0)
                qpos = lax.broadcasted_iota(jnp.int32, (tile, tile), 1)
                st = jnp.where(kpos <= qpos, st, NEG_INF)
            m_old = m_ref[h]
            m_new = jnp.maximum(m_old, jnp.max(st, axis=0, keepdims=True))
            alpha = jnp.exp2((m_old - m_new) * exp2_scale)
            p = jnp.exp2((st - m_new) * exp2_scale)
            l_ref[h] = alpha * l_ref[h] + jnp.sum(p, axis=0, keepdims=True)
            acc_ref[h] = alpha * acc_ref[h] + _dot(vt_ref[head_cols(h), :], p.astype(BF16))
            m_ref[h] = m_new
            return carry

        lax.fori_loop(0, n_heads, one_head, 0)

    @pl.when(ki < qi)
    def _():
        block(False)

    @pl.when(ki == qi)
    def _():
        block(True)

    @pl.when(ki == pl.num_programs(2) - 1)
    def _():
        def finish(h, carry):
            o = (acc_ref[h] / l_ref[h]).T
            o_ref[:, head_cols(h)] = _rms(o, g_ref[pl.ds(h, 1), :]).astype(o_ref.dtype)
            return carry

        lax.fori_loop(0, n_heads, finish, 0)


def _fox_prompt(q, kb, vt, cg, g_out, batch, seq, n_heads):
    n, wa = q.shape
    tile = _pick_tile(seq, 512)
    nt = seq // tile

    def kv_blk(qi, ki):
        return jnp.minimum(ki, qi)

    return pl.pallas_call(
        functools.partial(_fox_prompt_body, tile=tile, scale=HEAD_DIM ** -0.5, n_heads=n_heads),
        out_shape=jax.ShapeDtypeStruct((n, wa), BF16),
        grid=(batch, nt, nt),
        in_specs=[
            pl.BlockSpec((tile, wa), lambda b, qi, ki: (b * nt + qi, 0)),
            pl.BlockSpec((tile, wa), lambda b, qi, ki: (b * nt + kv_blk(qi, ki), 0)),
            pl.BlockSpec((wa, tile), lambda b, qi, ki: (0, b * nt + kv_blk(qi, ki))),
            pl.BlockSpec((seq, LANES), lambda b, qi, ki: (b, 0)),
            pl.BlockSpec((n_heads, HEAD_DIM), lambda b, qi, ki: (0, 0)),
        ],
        out_specs=pl.BlockSpec((tile, wa), lambda b, qi, ki: (b * nt + qi, 0)),
        scratch_shapes=[
            pltpu.VMEM((n_heads, seq, LANES), BF16),
            pltpu.VMEM((n_heads, seq, LANES), BF16),
            pltpu.VMEM((n_heads, 1, tile), F32),
            pltpu.VMEM((n_heads, 1, tile), F32),
            pltpu.VMEM((n_heads, HEAD_DIM, tile), F32),
        ],
        compiler_params=_cparams(("parallel", "arbitrary", "arbitrary")),
        name="fox_prompt",
    )(q, kb, vt, cg, g_out)


def _pool_cumsum_body(x_ref, o_ref):
    page = x_ref.shape[1]
    r = lax.broadcasted_iota(jnp.int32, (page, page), 0)
    c = lax.broadcasted_iota(jnp.int32, (page, page), 1)
    tri = jnp.where(r <= c, 1.0, 0.0).astype(BF16)
    hi, mid, lo = _split3(x_ref[...])
    o_ref[...] = _dot(hi, tri) + _dot(mid, tri) + _dot(lo, tri)


def _pool_cumsum(lf):
    n_rows, page = lf.shape
    rows = _pick_tile(n_rows, 2048)
    blk = pl.BlockSpec((rows, page), lambda i: (i, 0))
    return pl.pallas_call(
        _pool_cumsum_body,
        out_shape=jax.ShapeDtypeStruct((n_rows, page), F32),
        grid=(n_rows // rows,),
        in_specs=[blk],
        out_specs=blk,
        compiler_params=_cparams(("parallel",)),
        name="pool_cumsum",
    )(lf)


def _fox_sample_body(pt_ref, q_ref, kn_ref, vn_ref, cl_ref, g_ref, *rest,
                     npg, n_pages, page, n_heads, t_new, scale):
    k_refs = rest[0:npg]
    c_refs = rest[npg:2 * npg]
    v_refs = rest[2 * npg:3 * npg]
    o_ref = rest[3 * npg]
    qx_ref, qxb_ref, pad_ref, cpad_ref, s2_ref, carry_ref, acc_ref = rest[3 * npg + 1:]
    del pt_ref
    b_id = pl.program_id(0)
    s_id = pl.program_id(1)
    n_seq = pl.num_programs(0) - 1
    n_steps = n_pages // npg
    past = n_pages * page
    wa = n_heads * HEAD_DIM
    s_ref = s2_ref.at[b_id % 2]
    p_ref = s2_ref.at[(b_id + 1) % 2]
    k_phase = b_id < n_seq
    v_phase = b_id >= 1
    row_h = lax.broadcasted_iota(jnp.int32, (n_heads, wa), 0)
    col_h = lax.broadcasted_iota(jnp.int32, (n_heads, wa), 1) >> _log2(HEAD_DIM)
    head_diag = row_h == col_h

    def expand(c):
        out = c
        for t in range(1, t_new):
            out = out + pltpu.roll(c, t * n_heads, axis=1)
        return out

    @pl.when(k_phase & (s_id == 0))
    def _():
        qx_ref[...] = jnp.zeros_like(qx_ref)
        q = q_ref[...]
        for t in range(t_new):
            qx_ref[t * n_heads:(t + 1) * n_heads, :] = jnp.where(head_diag, q[t:t + 1, :], 0.0)
        qxb_ref[...] = qx_ref[...].astype(BF16)
        cpad_ref[...] = jnp.zeros_like(cpad_ref)
        carry_ref[...] = jnp.zeros_like(carry_ref)

    @pl.when(v_phase & (s_id == 0))
    def _():
        acc_ref[...] = jnp.zeros_like(acc_ref)

    def head_major(ref):
        return jnp.concatenate([ref[pl.ds(h, page, stride=n_heads), :] for h in range(n_heads)], axis=1)

    @pl.when(k_phase)
    def _():
        for r in range(npg):
            st = _dot_nt(head_major(k_refs[r]).astype(BF16), qxb_ref[...]) * scale
            cpad_ref[:, 0:n_heads] = c_refs[r][...].T
            cx = expand(cpad_ref[...]) + carry_ref[...]
            carry_ref[...] = cx[page - 1:page, :]
            off = pl.multiple_of((s_id * npg + r) * page, page)
            s_ref[pl.ds(off, page), :] = st - cx

    @pl.when(v_phase)
    def _():
        for r in range(npg):
            off = pl.multiple_of((s_id * npg + r) * page, page)
            p = p_ref[pl.ds(off, page), :].T.astype(BF16)
            acc_ref[...] += _dot(p, head_major(v_refs[r]).astype(BF16))

    @pl.when(v_phase & (s_id == n_steps - 1))
    def _():
        pad_ref[...] = jnp.zeros_like(pad_ref)
        pad_ref[0:t_new, :] = vn_ref[...]
        p = p_ref[past:past + page, :].T.astype(BF16)
        acc = acc_ref[...] + _dot(p, pad_ref[...].astype(BF16))
        for t in range(t_new):
            blk = acc[t * n_heads:(t + 1) * n_heads, :]
            o_t = jnp.sum(jnp.where(head_diag, blk, 0.0), axis=0, keepdims=True)
            for h in range(n_heads):
                sl = slice(h * HEAD_DIM, (h + 1) * HEAD_DIM)
                o_ref[t:t + 1, sl] = _rms(o_t[:, sl], g_ref[h:h + 1, :])

    @pl.when(k_phase & (s_id == n_steps - 1))
    def _():
        pad_ref[...] = jnp.zeros_like(pad_ref)
        pad_ref[0:t_new, :] = kn_ref[...]
        st = _dot_nt(pad_ref[...].astype(BF16), qxb_ref[...]) * scale
        lane = lax.broadcasted_iota(jnp.int32, (1, LANES), 1)
        cpad_ref[...] = jnp.zeros_like(cpad_ref)
        cpad_ref[0:t_new, :] = jnp.where(lane < n_heads, cl_ref[...], 0.0)
        cl_x = expand(cpad_ref[...])
        u = lax.broadcasted_iota(jnp.int32, (page, LANES), 0)
        t = lax.broadcasted_iota(jnp.int32, (page, LANES), 1) >> _log2(n_heads)
        st = jnp.where((u < t_new) & (u <= t), st - cl_x, NEG_INF)
        s_past = s_ref[0:past, :] + carry_ref[...]
        m = jnp.maximum(jnp.max(s_past, axis=0, keepdims=True), jnp.max(st, axis=0, keepdims=True))
        p_past = jnp.exp(s_past - m)
        p_new = jnp.exp(st - m)
        inv = 1.0 / (jnp.sum(p_past, axis=0, keepdims=True) + jnp.sum(p_new, axis=0, keepdims=True))
        s_ref[0:past, :] = p_past * inv
        s_ref[past:past + page, :] = p_new * inv


def _fox_sample(page_table, q, k_new, v_new, cl, g_out, cache_k, cache_v, cache_c, page, pool_base, n_heads):
    bsz, t_new, wa = q.shape
    n_pages = page_table.shape[1]
    assert page == LANES and n_heads == SUBLANES and n_heads * t_new <= LANES
    page_bytes = page * wa * 4
    npg = _pick_tile(n_pages, max(1, PAGE_BUFFER_BYTES // (4 * page_bytes)))
    n_steps = n_pages // npg
    scale = HEAD_DIM ** -0.5

    def k_seq(b):
        return jnp.minimum(b, bsz - 1)

    def v_seq(b):
        return jnp.maximum(b - 1, 0)

    def kv_map(seq_fn, r):
        return lambda b, s, pt: (pool_base + pt[seq_fn(b), s * npg + r], 0)

    def c_map(r):
        return lambda b, s, pt: (pt[k_seq(b), s * npg + r], 0, 0)

    k3 = lambda b, s, pt: (k_seq(b), 0, 0)
    v3 = lambda b, s, pt: (v_seq(b), 0, 0)
    in_specs = [
        pl.BlockSpec((None, t_new, wa), k3),
        pl.BlockSpec((None, t_new, wa), k3),
        pl.BlockSpec((None, t_new, wa), v3),
        pl.BlockSpec((None, t_new, LANES), k3),
        pl.BlockSpec((n_heads, HEAD_DIM), lambda b, s, pt: (0, 0)),
    ]
    in_specs += [pl.BlockSpec((page * n_heads, HEAD_DIM), kv_map(k_seq, r)) for r in range(npg)]
    in_specs += [pl.BlockSpec((None, n_heads, page), c_map(r)) for r in range(npg)]
    in_specs += [pl.BlockSpec((page * n_heads, HEAD_DIM), kv_map(v_seq, r)) for r in range(npg)]
    grid_spec = pltpu.PrefetchScalarGridSpec(
        num_scalar_prefetch=1,
        grid=(bsz + 1, n_steps),
        in_specs=in_specs,
        out_specs=pl.BlockSpec((None, t_new, wa), v3),
        scratch_shapes=[
            pltpu.VMEM((LANES, wa), F32),
            pltpu.VMEM((LANES, wa), BF16),
            pltpu.VMEM((page, wa), F32),
            pltpu.VMEM((page, LANES), F32),
            pltpu.VMEM((2, n_pages * page + page, LANES), F32),
            pltpu.VMEM((1, LANES), F32),
            pltpu.VMEM((LANES, wa), F32),
        ],
    )
    return pl.pallas_call(
        functools.partial(_fox_sample_body, npg=npg, n_pages=n_pages, page=page, n_heads=n_heads,
                          t_new=t_new, scale=scale),
        out_shape=jax.ShapeDtypeStruct((bsz, t_new, wa), F32),
        grid_spec=grid_spec,
        compiler_params=_cparams(("arbitrary", "arbitrary")),
        name="fox_sample",
    )(page_table, q, k_new, v_new, cl, g_out, *([cache_k] * npg), *([cache_c] * npg), *([cache_v] * npg))


def _bdot(a, b):
    return lax.dot_general(a, b, (((2,), (1,)), ((0,), (0,))), preferred_element_type=F32)


def _bdot_nt(a, b):
    return lax.dot_general(a, b, (((2,), (2,)), ((0,), (0,))), preferred_element_type=F32)


def _mlstm_chunk(qc, kc, vc, i_col, b_col, c_st, n_st, m_st, n_valid):
    nh, ln, _ = qc.shape
    r = lax.broadcasted_iota(jnp.int32, (ln, ln), 0)
    c = lax.broadcasted_iota(jnp.int32, (ln, ln), 1)
    eye = (r == c)[None]
    tril = (r >= c)[None]

    def col_to_row(col):
        return jnp.sum(jnp.where(eye, col, 0.0), axis=1, keepdims=True)

    i_row = col_to_row(i_col)
    b_row = col_to_row(b_col)
    a_col = b_col + m_st
    dmat = jnp.where(tril, b_col - b_row + i_row, NEG_INF)
    mt = jnp.maximum(a_col, jnp.max(dmat, axis=2, keepdims=True))
    qb = qc.astype(BF16)
    kb = kc.astype(BF16)
    vb = vc.astype(BF16)
    w_intra = jnp.exp(dmat - mt) * _bdot_nt(qb, kb)
    w_inter = jnp.exp(a_col - mt)
    num = w_inter * _bdot(qb, c_st.astype(BF16)) + _bdot(w_intra.astype(BF16), vb)
    den = w_inter * jnp.sum(qc * n_st, axis=2, keepdims=True) + jnp.sum(w_intra, axis=2, keepdims=True)
    h = num / jnp.maximum(jnp.abs(den), jnp.exp(-mt))
    last = n_valid - 1
    m_new = mt[:, last:last + 1, :]
    b_last = b_col[:, last:last + 1, :]
    w_state = jnp.exp(b_last - b_col + i_col - m_new)
    if n_valid < ln:
        row = lax.broadcasted_iota(jnp.int32, (1, ln, 1), 1)
        w_state = jnp.where(row < n_valid, w_state, 0.0)
    decay = jnp.exp(b_last + m_st - m_new)
    ks = w_state * kc
    ks_t = jnp.stack([ks[hd].T for hd in range(nh)]).astype(BF16)
    c_new = decay * c_st + _bdot(ks_t, vb)
    n_new = decay * n_st + jnp.sum(ks, axis=1, keepdims=True)
    return h, c_new, n_new, m_new


def _heads(x, n_heads, offset=0, width=HEAD_DIM):
    return jnp.stack([x[:, offset + h * width:offset + (h + 1) * width] for h in range(n_heads)])


def _mlstm_prompt_body(qk_ref, w_ref, b_ref, v_ref, o_ref, act_ref, cl_ref, g_ref,
                       y_ref, c_out_ref, n_out_ref, m_out_ref,
                       ext_ref, qk_scr, c_ref, n_ref, m_ref, *, rb, chunk, n_heads):
    blk = pl.program_id(1)
    pad = SUBLANES
    wm = n_heads * HEAD_DIM
    kscale = HEAD_DIM ** -0.5

    @pl.when(blk == 0)
    def _():
        ext_ref[0:pad, :] = jnp.zeros((pad, 2 * wm), F32)
        c_ref[...] = jnp.zeros_like(c_ref)
        n_ref[...] = jnp.zeros_like(n_ref)
        m_ref[...] = jnp.zeros_like(m_ref)

    ext_ref[pad:pad + rb, :] = qk_ref[...]
    acc = b_ref[...] + w_ref[CONV_W - 1:CONV_W, :] * qk_ref[...]
    for j in range(CONV_W - 1):
        off = pad - (CONV_W - 1) + j
        acc = acc + w_ref[j:j + 1, :] * ext_ref[off:off + rb, :]
    qk_scr[...] = acc * jax.nn.sigmoid(acc)
    ext_ref[0:pad, :] = qk_ref[rb - pad:rb, :]

    g = g_ref[...][:, None, :]
    for ci in range(rb // chunk):
        rows = slice(ci * chunk, (ci + 1) * chunk)
        qk = qk_scr[rows, :]
        hh, c_new, n_new, m_new = _mlstm_chunk(
            _heads(qk, n_heads), _heads(qk, n_heads, wm) * kscale, _heads(v_ref[rows, :], n_heads),
            _heads(act_ref[rows, :], n_heads, n_heads, 1), _heads(cl_ref[rows, :], n_heads, 2 * n_heads, 1),
            c_ref[...], n_ref[...], m_ref[:, :, 0:1], chunk)
        c_ref[...] = c_new
        n_ref[...] = n_new
        m_ref[...] = jnp.broadcast_to(m_new, m_ref.shape)
        y = jax.nn.sigmoid(_heads(o_ref[rows, :], n_heads)) * _rms(hh, g)
        for h in range(n_heads):
            y_ref[rows, h * HEAD_DIM:(h + 1) * HEAD_DIM] = y[h].astype(y_ref.dtype)

    @pl.when(blk == pl.num_programs(1) - 1)
    def _():
        c_out_ref[...] = c_ref[...]
        n_out_ref[...] = n_ref[:, 0, :]
        m_out_ref[...] = m_ref[:, 0, :]


def _mlstm_prompt(qkm, conv_w, conv_b, vm, om, act, cl, g_out, batch, seq, n_heads):
    n = qkm.shape[0]
    wm = n_heads * HEAD_DIM
    chunk = math.gcd(seq, MLSTM_CHUNK)
    rb = _pick_tile(seq, 2 * chunk)
    assert rb % chunk == 0 and rb >= SUBLANES
    nblk = seq // rb
    row = lambda b, i: (b * nblk + i, 0)
    const = lambda b, i: (0, 0)
    st = lambda b, i: (b, 0, 0)
    out_shape = (
        jax.ShapeDtypeStruct((n, wm), BF16),
        jax.ShapeDtypeStruct((batch, n_heads, HEAD_DIM, HEAD_DIM), F32),
        jax.ShapeDtypeStruct((batch, n_heads, HEAD_DIM), F32),
        jax.ShapeDtypeStruct((batch, n_heads, LANES), F32),
    )
    return pl.pallas_call(
        functools.partial(_mlstm_prompt_body, rb=rb, chunk=chunk, n_heads=n_heads),
        out_shape=out_shape,
        grid=(batch, nblk),
        in_specs=[
            pl.BlockSpec((rb, 2 * wm), row),
            pl.BlockSpec((CONV_W, 2 * wm), const),
            pl.BlockSpec((1, 2 * wm), const),
            pl.BlockSpec((rb, wm), row),
            pl.BlockSpec((rb, wm), row),
            pl.BlockSpec((rb, LANES), row),
            pl.BlockSpec((rb, LANES), row),
            pl.BlockSpec((n_heads, HEAD_DIM), const),
        ],
        out_specs=(
            pl.BlockSpec((rb, wm), row),
            pl.BlockSpec((None, n_heads, HEAD_DIM, HEAD_DIM), lambda b, i: (b, 0, 0, 0)),
            pl.BlockSpec((None, n_heads, HEAD_DIM), st),
            pl.BlockSpec((None, n_heads, LANES), st),
        ),
        scratch_shapes=[
            pltpu.VMEM((rb + SUBLANES, 2 * wm), F32),
            pltpu.VMEM((rb, 2 * wm), F32),
            pltpu.VMEM((n_heads, HEAD_DIM, HEAD_DIM), F32),
            pltpu.VMEM((n_heads, 1, HEAD_DIM), F32),
            pltpu.VMEM((n_heads, 1, LANES), F32),
        ],
        compiler_params=_cparams(("parallel", "arbitrary")),
        name="mlstm_prompt",
    )(qkm, conv_w, conv_b, vm, om, act, cl, g_out)


def _mlstm_sample_body(qk_ref, prev_ref, w_ref, b_ref, v_ref, o_ref, act_ref, cl_ref, g_ref, c0_ref, n0_ref, m0_ref,
                       y_ref, c_out_ref, n_out_ref, m_out_ref, ext_ref, pad_ref, *, t_new, n_heads, sb):
    wm = n_heads * HEAD_DIM
    rows = SUBLANES
    kscale = HEAD_DIM ** -0.5
    valid = lax.broadcasted_iota(jnp.int32, (rows, 1), 0) < t_new

    def padded(val, width):
        pad_ref[:, 0:width] = jnp.zeros((rows, width), F32)
        pad_ref[0:t_new, 0:width] = val
        return pad_ref[:, 0:width]

    qs, ks, vs, os_, i_cols, b_cols, n0s, m0s = [], [], [], [], [], [], [], []
    for s in range(sb):
        tok = slice(s * t_new, (s + 1) * t_new)
        ext_ref[...] = jnp.zeros_like(ext_ref)
        ext_ref[0:CONV_W - 1, :] = prev_ref[s]
        ext_ref[CONV_W - 1:CONV_W - 1 + t_new, :] = qk_ref[tok, :]
        acc = b_ref[...] + w_ref[0:1, :] * ext_ref[0:rows, :]
        for j in range(1, CONV_W):
            acc = acc + w_ref[j:j + 1, :] * ext_ref[j:j + rows, :]
        qk = jnp.where(valid, acc * jax.nn.sigmoid(acc), 0.0)
        qs.append(_heads(qk, n_heads))
        ks.append(_heads(qk, n_heads, wm) * kscale)
        vs.append(_heads(padded(v_ref[tok, :], wm), n_heads))
        os_.append(_heads(padded(o_ref[tok, :], wm), n_heads))
        i_cols.append(jnp.where(valid, _heads(padded(act_ref[tok, :], LANES), n_heads, n_heads, 1), NEG_INF))
        b_col = _heads(padded(cl_ref[tok, :], LANES), n_heads, 2 * n_heads, 1)
        b_cols.append(jnp.where(valid, b_col, b_col[:, t_new - 1:t_new, :]))
        n0s.append(jnp.stack([n0_ref[s, h:h + 1, :] for h in range(n_heads)]))
        m0s.append(jnp.stack([m0_ref[s, h:h + 1, :] for h in range(n_heads)]))

    cat = lambda parts: jnp.concatenate(parts, axis=0)
    c0 = cat([c0_ref[s] for s in range(sb)])
    hh, c_new, n_new, m_new = _mlstm_chunk(cat(qs), cat(ks), cat(vs), cat(i_cols), cat(b_cols),
                                           c0, cat(n0s), cat(m0s), t_new)
    g = jnp.concatenate([g_ref[...]] * sb, axis=0)[:, None, :]
    y = jax.nn.sigmoid(cat(os_)) * _rms(hh, g)
    m_wide = jnp.broadcast_to(m_new, (sb * n_heads, 1, LANES))
    for s in range(sb):
        for h in range(n_heads):
            y_ref[s * t_new:(s + 1) * t_new, h * HEAD_DIM:(h + 1) * HEAD_DIM] = y[s * n_heads + h, 0:t_new, :]
        grp = slice(s * n_heads, (s + 1) * n_heads)
        c_out_ref[s] = c_new[grp]
        n_out_ref[s] = n_new[grp, 0, :]
        m_out_ref[s] = m_wide[grp, 0, :]


def _mlstm_sample(qkm, conv_prev, conv_w, conv_b, vm, om, act, cl, g_out, c0, n0, m0, t_new, n_heads):
    n, wm2 = qkm.shape
    bsz = n // t_new
    wm = wm2 // 2
    assert t_new + CONV_W - 1 <= SUBLANES
    sb = _pick_tile(bsz, 4)
    assert (sb * t_new) % SUBLANES == 0 or sb == bsz
    tok = lambda b: (b, 0)
    seq3 = lambda b: (b, 0, 0)
    const = lambda b: (0, 0)
    out_shape = (
        jax.ShapeDtypeStruct((n, wm), F32),
        jax.ShapeDtypeStruct((bsz, n_heads, HEAD_DIM, HEAD_DIM), F32),
        jax.ShapeDtypeStruct((bsz, n_heads, HEAD_DIM), F32),
        jax.ShapeDtypeStruct((bsz, n_heads, LANES), F32),
    )
    return pl.pallas_call(
        functools.partial(_mlstm_sample_body, t_new=t_new, n_heads=n_heads, sb=sb),
        out_shape=out_shape,
        grid=(bsz // sb,),
        in_specs=[
            pl.BlockSpec((sb * t_new, wm2), tok),
            pl.BlockSpec((sb, CONV_W - 1, wm2), seq3),
            pl.BlockSpec((CONV_W, wm2), const),
            pl.BlockSpec((1, wm2), const),
            pl.BlockSpec((sb * t_new, wm), tok),
            pl.BlockSpec((sb * t_new, wm), tok),
            pl.BlockSpec((sb * t_new, LANES), tok),
            pl.BlockSpec((sb * t_new, LANES), tok),
            pl.BlockSpec((n_heads, HEAD_DIM), const),
            pl.BlockSpec((sb, n_heads, HEAD_DIM, HEAD_DIM), lambda b: (b, 0, 0, 0)),
            pl.BlockSpec((sb, n_heads, HEAD_DIM), seq3),
            pl.BlockSpec((sb, n_heads, 1), seq3),
        ],
        out_specs=(
            pl.BlockSpec((sb * t_new, wm), tok),
            pl.BlockSpec((sb, n_heads, HEAD_DIM, HEAD_DIM), lambda b: (b, 0, 0, 0)),
            pl.BlockSpec((sb, n_heads, HEAD_DIM), seq3),
            pl.BlockSpec((sb, n_heads, LANES), seq3),
        ),
        scratch_shapes=[pltpu.VMEM((2 * SUBLANES, wm2), F32), pltpu.VMEM((SUBLANES, wm), F32)],
        compiler_params=_cparams(("parallel",)),
        name="mlstm_sample",
    )(qkm, conv_prev, conv_w, conv_b, vm, om, act, cl, g_out, c0, n0, m0)


def _outproj_body(ya_ref, ym_ref, w_ref, x_ref, o_ref, *, wa):
    y = _dot(ya_ref[...].astype(BF16), w_ref[0:wa, :]) + _dot(ym_ref[...].astype(BF16), w_ref[wa:, :])
    o_ref[...] = x_ref[...] + y


def _outproj(ya, ym, w, x):
    n, d = x.shape
    wa = ya.shape[1]
    wm = ym.shape[1]
    tm = _pick_tile(n, 512)
    row = lambda i: (i, 0)
    return pl.pallas_call(
        functools.partial(_outproj_body, wa=wa),
        out_shape=jax.ShapeDtypeStruct((n, d), F32),
        grid=(n // tm,),
        in_specs=[
            pl.BlockSpec((tm, wa), row),
            pl.BlockSpec((tm, wm), row),
            pl.BlockSpec((wa + wm, d), lambda i: (0, 0)),
            pl.BlockSpec((tm, d), row),
        ],
        out_specs=pl.BlockSpec((tm, d), row),
        compiler_params=_cparams(("parallel",)),
        name="outproj",
    )(ya, ym, w, x)


def _cast_pad_body(x_ref, o_ref, *, axis, size):
    if axis == 1:
        o_ref[:, 0:size] = x_ref[...].astype(BF16)
        o_ref[:, size:] = jnp.zeros((o_ref.shape[0], o_ref.shape[1] - size), BF16)
    else:
        o_ref[0:size, :] = x_ref[...].astype(BF16)
        o_ref[size:, :] = jnp.zeros((o_ref.shape[0] - size, o_ref.shape[1]), BF16)


def _cast_pad(w, axis, padded):
    r, c = w.shape
    size = w.shape[axis]
    assert padded > size
    if axis == 1:
        t = _pick_tile(r, 256)
        grid, in_blk, out_blk, imap = (r // t,), (t, c), (t, padded), (lambda i: (i, 0))
        out_shape = (r, padded)
    else:
        t = _pick_tile(c, 256)
        grid, in_blk, out_blk, imap = (c // t,), (r, t), (padded, t), (lambda i: (0, i))
        out_shape = (padded, c)
    return pl.pallas_call(
        functools.partial(_cast_pad_body, axis=axis, size=size),
        out_shape=jax.ShapeDtypeStruct(out_shape, BF16),
        grid=grid,
        in_specs=[pl.BlockSpec(in_blk, imap)],
        out_specs=pl.BlockSpec(out_blk, imap),
        compiler_params=_cparams(("parallel",)),
        name="cast_pad",
    )(w)


def _win_prep_body(wt_ref, fa_ref, im_ref, main_ref, gates_ref, *, n_gate_cols):
    main_ref[...] = wt_ref[...].T.astype(BF16)

    @pl.when(pl.program_id(0) == 0)
    def _():
        d = gates_ref.shape[0]
        g = jnp.concatenate([fa_ref[...], im_ref[...], jnp.zeros((LANES - n_gate_cols, d), F32)], axis=0)
        gates_ref[...] = g.T.astype(BF16)


def _win_prep(w_in_t, cuts, n_gate_cols, tile):
    n_in, d = w_in_t.shape
    o_f, o_qk, o_i = cuts
    n_main = o_f + (o_i - o_qk)
    assert o_f % tile == 0 and (o_i - o_qk) % tile == 0 and o_qk % SUBLANES == 0 and o_i % SUBLANES == 0
    n_lead = o_f // tile

    def row_off(j):
        return pl.multiple_of(jnp.where(j < n_lead, j * tile, o_qk + (j - n_lead) * tile), SUBLANES)

    el = pl.Element
    return pl.pallas_call(
        functools.partial(_win_prep_body, n_gate_cols=n_gate_cols),
        out_shape=(jax.ShapeDtypeStruct((d, n_main), BF16), jax.ShapeDtypeStruct((d, LANES), BF16)),
        grid=(n_main // tile,),
        in_specs=[
            pl.BlockSpec((el(tile), el(d)), lambda j: (row_off(j), 0)),
            pl.BlockSpec((el(o_qk - o_f), el(d)), lambda j: (o_f, 0)),
            pl.BlockSpec((el(n_in - o_i), el(d)), lambda j: (o_i, 0)),
        ],
        out_specs=(pl.BlockSpec((d, tile), lambda j: (0, j)), pl.BlockSpec((d, LANES), lambda j: (0, 0))),
        compiler_params=_cparams(("arbitrary",)),
        name="win_prep",
    )(w_in_t, w_in_t, w_in_t)


def _prep_weights(lw, n_heads_a, n_heads_m):
    wa = n_heads_a * HEAD_DIM
    wm = n_heads_m * HEAD_DIM
    f = lw['w1_gate'].shape[1]
    tf = 512
    fp = tf * ((f + tf - 1) // tf)

    def ffn_w(wg, wu, wd):
        if fp == f:
            return wg.astype(BF16), wu.astype(BF16), wd.astype(BF16)
        return _cast_pad(wg, 1, fp), _cast_pad(wu, 1, fp), _cast_pad(wd, 0, fp)

    w_in = lw['w_in']
    o_f = 3 * wa
    o_qk = o_f + n_heads_a
    o_i = o_qk + 2 * wm + 2 * wm
    o_fm = o_i + n_heads_m
    ng = 2 * n_heads_m + n_heads_a
    assert o_fm + n_heads_m == w_in.shape[1]
    w_main, w_gates = _win_prep(jnp.swapaxes(w_in, 0, 1), (o_f, o_qk, o_i), ng, wa)
    bias = jnp.concatenate([lw['b_fox_f'], lw['b_m_i'], lw['b_m_f'], jnp.zeros((LANES - ng,), F32)])[None, :]
    return {
        'ffn1': ffn_w(lw['w1_gate'], lw['w1_up'], lw['w1_down']),
        'ffn_w': ffn_w,
        'tf': tf,
        'fp': fp,
        'w_main': w_main,
        'w_gates': w_gates,
        'gate_bias': bias,
        'w_out': lw['w_out'].astype(BF16),
    }


def kernel(x_prompt, x_sample, cache_k, cache_v, cache_logf, state_conv, state_C, state_n, state_m, page_table,
           g_ffn1, w1_gate, w1_up, w1_down, g_mix, w_in, b_fox_f, b_m_i, b_m_f, conv_w, conv_b, g_q, g_k,
           g_out_a, g_out_m, w_out, g_ffn2, w2_gate, w2_up, w2_down):
    depth = w_in.shape[0]
    bp, seq, d = x_prompt.shape
    bs, t_new, _ = x_sample.shape
    n_heads_a = g_out_a.shape[1]
    n_heads_m = g_out_m.shape[1]
    wa = n_heads_a * HEAD_DIM
    wm = n_heads_m * HEAD_DIM
    n_pool, page = cache_k.shape[1], cache_k.shape[2]

    yp = x_prompt.reshape(bp * seq, d)
    ys = x_sample.reshape(bs * t_new, d)
    outs = [[] for _ in range(14)]
    for l in range(depth):
        lw = {
            'w1_gate': w1_gate[l], 'w1_up': w1_up[l], 'w1_down': w1_down[l], 'w_in': w_in[l],
            'b_fox_f': b_fox_f[l], 'b_m_i': b_m_i[l], 'b_m_f': b_m_f[l], 'w_out': w_out[l],
            'w2_gate': w2_gate[l], 'w2_up': w2_up[l], 'w2_down': w2_down[l],
        }
        pw = _prep_weights(lw, n_heads_a, n_heads_m)
        gf1 = g_ffn1[l][None, :]
        gf2 = g_ffn2[l][None, :]
        gmix = g_mix[l][None, :]
        gq = g_q[l][None, :]
        gk = g_k[l][None, :]
        cw = conv_w[l]
        cb = conv_b[l][None, :]
        goa = g_out_a[l]
        gom = g_out_m[l]

        w2 = (lw['w2_gate'], lw['w2_up'], lw['w2_down'])
        ffn1_steps = (yp.shape[0] // _pick_tile(yp.shape[0], 512)) * (pw['fp'] // pw['tf'])
        if w2[0].shape[1] % LANES == 0 and ffn1_steps >= 3 * (pw['fp'] // LANES):
            (x1, xn), ffn2_w = _ffn(yp, gf1, *pw['ffn1'], pw['tf'], g_next=gmix, prep=w2)
        else:
            ffn2_w = pw['ffn_w'](*w2)
            x1, xn = _ffn(yp, gf1, *pw['ffn1'], pw['tf'], g_next=gmix)
        q, k, kb, v, vt, qkm, vm, om, gt = _inproj(xn, pw['w_main'], pw['w_gates'], gq, gk, n_heads_a)
        act, cg, cl = _gates(gt, pw['gate_bias'], seq, math.gcd(seq, MLSTM_CHUNK), n_heads_a)
        ya = _fox_prompt(q, kb, vt, cg, goa, bp, seq, n_heads_a)
        ym, c_p, n_p, m_p = _mlstm_prompt(qkm, cw, cb, vm, om, act, cl, gom, bp, seq, n_heads_m)
        x2 = _outproj(ya, ym, pw['w_out'], x1)
        yp = _ffn(x2, gf2, *ffn2_w, pw['tf'])
        outs[0].append(k.reshape(bp, seq, n_heads_a, HEAD_DIM))
        outs[1].append(v.reshape(bp, seq, n_heads_a, HEAD_DIM))
        outs[2].append(act[:, :n_heads_a].reshape(bp, seq, n_heads_a))
        outs[3].append(qkm.reshape(bp, seq, 2 * wm)[:, seq - (CONV_W - 1):, :])
        outs[4].append(c_p)
        outs[5].append(n_p)
        outs[6].append(m_p[:, :, 0])

        x1, xn = _ffn(ys, gf1, *pw['ffn1'], pw['tf'], g_next=gmix)
        q, k, kb, v, vt, qkm, vm, om, gt = _inproj(xn, pw['w_main'], pw['w_gates'], gq, gk, n_heads_a)
        act, cg, cl = _gates(gt, pw['gate_bias'], t_new, math.gcd(t_new, MLSTM_CHUNK), n_heads_a)
        r3 = lambda a: a.reshape(bs, t_new, a.shape[-1])
        lf_rows = jnp.swapaxes(cache_logf[l], 1, 2).reshape(n_pool * n_heads_a, page)
        cache_c = _pool_cumsum(lf_rows).reshape(n_pool, n_heads_a, page)
        ya = _fox_sample(page_table, r3(q).astype(F32), r3(k), r3(v), r3(cg), goa,
                         cache_k.reshape(-1, HEAD_DIM), cache_v.reshape(-1, HEAD_DIM),
                         cache_c, page, l * n_pool, n_heads_a)
        ym, c_s, n_s, m_s = _mlstm_sample(qkm, state_conv[l], cw, cb, vm, om, act, cl, gom,
                                          state_C[l], state_n[l], state_m[l][:, :, None], t_new, n_heads_m)
        x2 = _outproj(ya.reshape(bs * t_new, wa), ym, pw['w_out'], x1)
        ys = _ffn(x2, gf2, *ffn2_w, pw['tf'])
        conv_ext = jnp.concatenate([state_conv[l], r3(qkm)], axis=1)
        outs[7].append(k.reshape(bs, t_new, n_heads_a, HEAD_DIM))
        outs[8].append(v.reshape(bs, t_new, n_heads_a, HEAD_DIM))
        outs[9].append(act[:, :n_heads_a].reshape(bs, t_new, n_heads_a))
        outs[10].append(conv_ext[:, t_new:, :])
        outs[11].append(c_s)
        outs[12].append(n_s)
        outs[13].append(m_s[:, :, 0])

    return (yp.reshape(bp, seq, d), ys.reshape(bs, t_new, d)) + tuple(jnp.stack(o) for o in outs)
```

```python
import functools
import math

import jax
import jax.numpy as jnp
from jax import lax
from jax.experimental import pallas as pl
from jax.experimental.pallas import tpu as pltpu

F32 = jnp.float32
BF16 = jnp.bfloat16

EPS = 1e-6
HEAD_DIM = 128
CONV_W = 4
MLSTM_CHUNK = 128
LANES = 128
SUBLANES = 8
VMEM_LIMIT_BYTES = 56 * 1024 * 1024
PAGE_BUFFER_BYTES = 32 * 1024 * 1024
NEG_INF = float("-inf")
LOG2E = math.log2(math.e)


def _cparams(sem):
    return pltpu.CompilerParams(dimension_semantics=sem, vmem_limit_bytes=VMEM_LIMIT_BYTES)


def _rms(x, g):
    y = x * lax.rsqrt(jnp.mean(x * x, axis=-1, keepdims=True) + EPS)
    return y * g


def _log_sigmoid(x):
    return jnp.minimum(x, 0.0) - jnp.log1p(jnp.exp(-jnp.abs(x)))


def _dot(a, b, precision=None):
    return jnp.dot(a, b, preferred_element_type=F32, precision=precision)


def _dot_nt(a, b, precision=None):
    return lax.dot_general(a, b, (((1,), (1,)), ((), ())), preferred_element_type=F32, precision=precision)


def _pick_tile(n, pref):
    t = min(n, pref)
    while n % t:
        t //= 2
    return t


def _log2(n):
    k = int(math.log2(n))
    assert 1 << k == n
    return k


def _ffn_body(x_ref, g_ref, wg_ref, wu_ref, wd_ref, gn_ref, *rest, with_next, n_prep, prep_real, prep_slabs):
    prep_in = rest[:n_prep]
    rest = rest[n_prep:]
    if with_next:
        o_ref, on_ref = rest[:2]
        rest = rest[2:]
    else:
        o_ref = rest[0]
        rest = rest[1:]
    prep_out = rest[:n_prep]
    xn_ref, acc_ref = rest[n_prep:]
    j = pl.program_id(1)

    @pl.when(j == 0)
    def _():
        xn_ref[...] = _rms(x_ref[...], g_ref[...]).astype(BF16)
        acc_ref[...] = jnp.zeros_like(acc_ref)

    xn = xn_ref[...]
    gate = _dot(xn, wg_ref[...])
    up = _dot(xn, wu_ref[...])
    h = (gate * jax.nn.sigmoid(gate)) * up
    acc_ref[...] += _dot(h.astype(BF16), wd_ref[...])

    @pl.when(j == pl.num_programs(1) - 1)
    def _():
        y = x_ref[...] + 0.5 * acc_ref[...]
        o_ref[...] = y
        if with_next:
            on_ref[...] = _rms(y, gn_ref[...]).astype(BF16)

    t = pl.program_id(0) * pl.num_programs(1) + j
    for k in range(n_prep):
        t0 = k * prep_slabs

        @pl.when((t >= t0) & (t < t0 + prep_real))
        def _(k=k):
            prep_out[k][...] = prep_in[k][...].astype(BF16)

        @pl.when((t >= t0 + prep_real) & (t < t0 + prep_slabs))
        def _(k=k):
            prep_out[k][...] = jnp.zeros_like(prep_out[k])


def _ffn(x, g, wg, wu, wd, tf, g_next=None, prep=None):
    n, d = x.shape
    fp = wg.shape[1]
    tm = _pick_tile(n, 512)
    ni, nj = n // tm, fp // tf
    with_next = g_next is not None
    row = pl.BlockSpec((tm, d), lambda i, j: (i, 0))
    vec = pl.BlockSpec((1, d), lambda i, j: (0, 0))
    out_shape = [jax.ShapeDtypeStruct((n, d), F32)]
    out_specs = [row]
    if with_next:
        out_shape.append(jax.ShapeDtypeStruct((n, d), BF16))
        out_specs.append(row)
    in_specs = [
        row, vec,
        pl.BlockSpec((d, tf), lambda i, j: (0, j)),
        pl.BlockSpec((d, tf), lambda i, j: (0, j)),
        pl.BlockSpec((tf, d), lambda i, j: (j, 0)),
        vec,
    ]
    operands = [x, g, wg, wu, wd, g_next if with_next else g]
    n_prep = prep_real = prep_slabs = 0
    if prep is not None:
        f = prep[0].shape[1]
        assert f % LANES == 0 and fp % LANES == 0
        n_prep, prep_real, prep_slabs = 3, f // LANES, fp // LANES
        assert ni * nj >= n_prep * prep_slabs

        def slab(k, last):
            return lambda i, j: jnp.clip(i * nj + j - k * prep_slabs, 0, last)

        for k in range(n_prep):
            cols = k < 2
            blk = (d, LANES) if cols else (LANES, d)
            for last, specs in ((prep_real - 1, in_specs), (prep_slabs - 1, out_specs)):
                s = slab(k, last)
                specs.append(pl.BlockSpec(blk, (lambda i, j, s=s: (0, s(i, j))) if cols
                                          else (lambda i, j, s=s: (s(i, j), 0))))
            out_shape.append(jax.ShapeDtypeStruct((d, fp) if cols else (fp, d), BF16))
        operands += list(prep)
    res = pl.pallas_call(
        functools.partial(_ffn_body, with_next=with_next, n_prep=n_prep, prep_real=prep_real,
                          prep_slabs=prep_slabs),
        out_shape=tuple(out_shape),
        grid=(ni, nj),
        in_specs=in_specs,
        out_specs=tuple(out_specs),
        scratch_shapes=[pltpu.VMEM((tm, d), BF16), pltpu.VMEM((tm, d), F32)],
        compiler_params=_cparams(("arbitrary", "arbitrary") if n_prep else ("parallel", "arbitrary")),
        name="ffn",
    )(*operands)
    n_main = 2 if with_next else 1
    main = res[0] if n_main == 1 else tuple(res[:2])
    return (main, tuple(res[n_main:])) if prep is not None else main


def _inproj_body(xn_ref, w_ref, wgt_ref, gq_ref, gk_ref,
                 q_ref, k_ref, kb_ref, v_ref, vt_ref, qkm_ref, vm_ref, om_ref, gt_ref, *, n_heads):
    j = pl.program_id(0)
    xn = xn_ref[...]
    acc = _dot(xn, w_ref[...])

    @pl.when(j == 0)
    def _():
        for h in range(n_heads):
            sl = slice(h * HEAD_DIM, (h + 1) * HEAD_DIM)
            q_ref[:, sl] = _rms(acc[:, sl], gq_ref[...]).astype(BF16)
        gt_ref[...] = _dot(xn, wgt_ref[...])

    @pl.when(j == 1)
    def _():
        for h in range(n_heads):
            sl = slice(h * HEAD_DIM, (h + 1) * HEAD_DIM)
            kn = _rms(acc[:, sl], gk_ref[...])
            k_ref[:, sl] = kn
            kb_ref[:, sl] = kn.astype(BF16)

    @pl.when(j == 2)
    def _():
        v_ref[...] = acc
        vt_ref[...] = acc.T.astype(BF16)

    @pl.when((j == 3) | (j == 4))
    def _():
        qkm_ref[...] = acc

    @pl.when(j == 5)
    def _():
        vm_ref[...] = acc

    @pl.when(j == 6)
    def _():
        om_ref[...] = acc


def _inproj(xn, w_main, w_gates, gq, gk, n_heads):
    n, d = xn.shape
    wa = n_heads * HEAD_DIM
    assert w_main.shape[1] == 7 * wa
    tm = _pick_tile(n, 512)
    ni = n // tm
    const = lambda j, i: (0, 0)

    def rows_at(j, i, j0, j1):
        return jnp.where(j < j0, 0, jnp.where(j > j1, ni - 1, i))

    def row_map(j0, j1=None):
        j1 = j0 if j1 is None else j1
        return lambda j, i: (rows_at(j, i, j0, j1), 0)

    out_shape = (
        jax.ShapeDtypeStruct((n, wa), BF16),
        jax.ShapeDtypeStruct((n, wa), F32),
        jax.ShapeDtypeStruct((n, wa), BF16),
        jax.ShapeDtypeStruct((n, wa), F32),
        jax.ShapeDtypeStruct((wa, n), BF16),
        jax.ShapeDtypeStruct((n, 2 * wa), F32),
        jax.ShapeDtypeStruct((n, wa), F32),
        jax.ShapeDtypeStruct((n, wa), F32),
        jax.ShapeDtypeStruct((n, LANES), F32),
    )
    out_specs = (
        pl.BlockSpec((tm, wa), row_map(0)),
        pl.BlockSpec((tm, wa), row_map(1)),
        pl.BlockSpec((tm, wa), row_map(1)),
        pl.BlockSpec((tm, wa), row_map(2)),
        pl.BlockSpec((wa, tm), lambda j, i: (0, rows_at(j, i, 2, 2))),
        pl.BlockSpec((tm, wa), lambda j, i: (rows_at(j, i, 3, 4), jnp.clip(j - 3, 0, 1))),
        pl.BlockSpec((tm, wa), row_map(5)),
        pl.BlockSpec((tm, wa), row_map(6)),
        pl.BlockSpec((tm, LANES), row_map(0)),
    )
    return pl.pallas_call(
        functools.partial(_inproj_body, n_heads=n_heads),
        out_shape=out_shape,
        grid=(7, ni),
        in_specs=[
            pl.BlockSpec((tm, d), lambda j, i: (i, 0)),
            pl.BlockSpec((d, wa), lambda j, i: (0, j)),
            pl.BlockSpec((d, LANES), const),
            pl.BlockSpec((1, HEAD_DIM), const),
            pl.BlockSpec((1, HEAD_DIM), const),
        ],
        out_specs=out_specs,
        compiler_params=_cparams(("arbitrary", "arbitrary")),
        name="inproj",
    )(xn, w_main, w_gates, gq, gk)


def _split3(x):
    hi = x.astype(BF16)
    r1 = x - hi.astype(F32)
    mid = r1.astype(BF16)
    lo = (r1 - mid.astype(F32)).astype(BF16)
    return hi, mid, lo


def _dot_01(m01, x):
    hi, mid, lo = _split3(x)
    return _dot(m01, hi) + _dot(m01, mid) + _dot(m01, lo)


def _gates_body(g_ref, b_ref, act_ref, cg_ref, cl_ref, *, seg, lc, rows, n_heads):
    ch = min(rows, 256)
    sg = min(seg, ch)
    lane = lax.broadcasted_iota(jnp.int32, (1, LANES), 1)
    is_logsig = (lane < n_heads) | ((lane >= 2 * n_heads) & (lane < 3 * n_heads))
    r = lax.broadcasted_iota(jnp.int32, (ch, ch), 0)
    c = lax.broadcasted_iota(jnp.int32, (ch, ch), 1)
    sh_g = _log2(sg)
    sh_l = _log2(lc)
    tri_g = jnp.where((r >= c) & ((r >> sh_g) == (c >> sh_g)), 1.0, 0.0).astype(BF16)
    tri_l = jnp.where((r >= c) & ((r >> sh_l) == (c >> sh_l)), 1.0, 0.0).astype(BF16)
    carry = jnp.zeros((1, LANES), F32)
    for ci in range(rows // ch):
        sl = slice(ci * ch, (ci + 1) * ch)
        x = g_ref[sl, :] + b_ref[...]
        act = jnp.where(is_logsig, _log_sigmoid(x), x)
        act_ref[sl, :] = act
        cg = _dot_01(tri_g, act)
        if seg > ch:
            cg = cg + carry
            carry = cg[ch - 1:ch, :]
        cg_ref[sl, :] = cg
        cl_ref[sl, :] = _dot_01(tri_l, act)


def _gates(gates, bias, seg, lc, n_heads):
    n = gates.shape[0]
    rows = seg if seg >= 256 else _pick_tile(n, 256)
    blk = pl.BlockSpec((rows, LANES), lambda b: (b, 0))
    return pl.pallas_call(
        functools.partial(_gates_body, seg=seg, lc=lc, rows=rows, n_heads=n_heads),
        out_shape=(jax.ShapeDtypeStruct((n, LANES), F32),) * 3,
        grid=(n // rows,),
        in_specs=[blk, pl.BlockSpec((1, LANES), lambda b: (0, 0))],
        out_specs=(blk, blk, blk),
        compiler_params=_cparams(("parallel",)),
        name="gates",
    )(gates, bias)


def _fox_prompt_body(q_ref, k_ref, vt_ref, c_ref, g_ref, o_ref,
                     qtail_ref, ktail_ref, m_ref, l_ref, acc_ref, *, tile, scale, n_heads):
    qi = pl.program_id(1)
    ki = pl.program_id(2)
    exp2_scale = scale * LOG2E

    def head_cols(h):
        return pl.ds(pl.multiple_of(h * HEAD_DIM, HEAD_DIM), HEAD_DIM)

    @pl.when((qi == 0) & (ki == 0))
    def _():
        lane = lax.broadcasted_iota(jnp.int32, (1, LANES), 1)

        def tails(h, carry):
            c = jnp.sum(jnp.where(lane == h, c_ref[...], 0.0), axis=1, keepdims=True) * (1.0 / scale)
            p1, p2, p3 = (p.astype(F32) for p in _split3(c))
            one = jnp.ones_like(p1)

            def tail(cols):
                t = jnp.zeros((c.shape[0], LANES), F32)
                for j, col in enumerate(cols):
                    t = jnp.where(lane == j, col, t)
                return t.astype(BF16)

            qtail_ref[h] = tail((p1, p2, p3, one, one, one))
            ktail_ref[h] = tail((one, one, one, -p1, -p2, -p3))
            return carry

        lax.fori_loop(0, n_heads, tails, 0)

    @pl.when(ki == 0)
    def _():
        m_ref[...] = jnp.full_like(m_ref, NEG_INF)
        l_ref[...] = jnp.zeros_like(l_ref)
        acc_ref[...] = jnp.zeros_like(acc_ref)

    def block(masked):
        q_rows = pl.ds(pl.multiple_of(qi * tile, tile), tile)
        k_rows = pl.ds(pl.multiple_of(ki * tile, tile), tile)

        def one_head(h, carry):
            q_aug = jnp.concatenate([q_ref[:, head_cols(h)], qtail_ref[h, q_rows, :]], axis=1)
            k_aug = jnp.concatenate([k_ref[:, head_cols(h)], ktail_ref[h, k_rows, :]], axis=1)
            st = _dot_nt(k_aug, q_aug)
            if masked:
                kpos = lax.broadcasted_iota(jnp.int32, (tile, tile), 0)
                qpos = lax.broadcasted_iota(jnp.int32, (tile, tile), 1)
                st = jnp.where(kpos <= qpos, st, NEG_INF)
            m_old = m_ref[h]
            m_new = jnp.maximum(m_old, jnp.max(st, axis=0, keepdims=True))
            alpha = jnp.exp2((m_old - m_new) * exp2_scale)
            p = jnp.exp2((st - m_new) * exp2_scale)
            l_ref[h] = alpha * l_ref[h] + jnp.sum(p, axis=0, keepdims=True)
            acc_ref[h] = alpha * acc_ref[h] + _dot(vt_ref[head_cols(h), :], p.astype(BF16))
            m_ref[h] = m_new
            return carry

        lax.fori_loop(0, n_heads, one_head, 0)

    @pl.when(ki < qi)
    def _():
        block(False)

    @pl.when(ki == qi)
    def _():
        block(True)

    @pl.when(ki == pl.num_programs(2) - 1)
    def _():
        def finish(h, carry):
            o = (acc_ref[h] / l_ref[h]).T
            o_ref[:, head_cols(h)] = _rms(o, g_ref[pl.ds(h, 1), :]).astype(o_ref.dtype)
            return carry

        lax.fori_loop(0, n_heads, finish, 0)


def _fox_prompt(q, kb, vt, cg, g_out, batch, seq, n_heads):
    n, wa = q.shape
    tile = _pick_tile(seq, 512)
    nt = seq // tile

    def kv_blk(qi, ki):
        return jnp.minimum(ki, qi)

    return pl.pallas_call(
        functools.partial(_fox_prompt_body, tile=tile, scale=HEAD_DIM ** -0.5, n_heads=n_heads),
        out_shape=jax.ShapeDtypeStruct((n, wa), BF16),
        grid=(batch, nt, nt),
        in_specs=[
            pl.BlockSpec((tile, wa), lambda b, qi, ki: (b * nt + qi, 0)),
            pl.BlockSpec((tile, wa), lambda b, qi, ki: (b * nt + kv_blk(qi, ki), 0)),
            pl.BlockSpec((wa, tile), lambda b, qi, ki: (0, b * nt + kv_blk(qi, ki))),
            pl.BlockSpec((seq, LANES), lambda b, qi, ki: (b, 0)),
            pl.BlockSpec((n_heads, HEAD_DIM), lambda b, qi, ki: (0, 0)),
        ],
        out_specs=pl.BlockSpec((tile, wa), lambda b, qi, ki: (b * nt + qi, 0)),
        scratch_shapes=[
            pltpu.VMEM((n_heads, seq, LANES), BF16),
            pltpu.VMEM((n_heads, seq, LANES), BF16),
            pltpu.VMEM((n_heads, 1, tile), F32),
            pltpu.VMEM((n_heads, 1, tile), F32),
            pltpu.VMEM((n_heads, HEAD_DIM, tile), F32),
        ],
        compiler_params=_cparams(("parallel", "arbitrary", "arbitrary")),
        name="fox_prompt",
    )(q, kb, vt, cg, g_out)


def _pool_cumsum_body(x_ref, o_ref):
    page = x_ref.shape[1]
    r = lax.broadcasted_iota(jnp.int32, (page, page), 0)
    c = lax.broadcasted_iota(jnp.int32, (page, page), 1)
    tri = jnp.where(r <= c, 1.0, 0.0).astype(BF16)
    hi, mid, lo = _split3(x_ref[...])
    o_ref[...] = _dot(hi, tri) + _dot(mid, tri) + _dot(lo, tri)


def _pool_cumsum(lf):
    n_rows, page = lf.shape
    rows = _pick_tile(n_rows, 2048)
    blk = pl.BlockSpec((rows, page), lambda i: (i, 0))
    return pl.pallas_call(
        _pool_cumsum_body,
        out_shape=jax.ShapeDtypeStruct((n_rows, page), F32),
        grid=(n_rows // rows,),
        in_specs=[blk],
        out_specs=blk,
        compiler_params=_cparams(("parallel",)),
        name="pool_cumsum",
    )(lf)


def _fox_sample_body(pt_ref, q_ref, kn_ref, vn_ref, cl_ref, g_ref, *rest,
                     npg, n_pages, page, n_heads, t_new, scale):
    k_refs = rest[0:npg]
    c_refs = rest[npg:2 * npg]
    v_refs = rest[2 * npg:3 * npg]
    o_ref = rest[3 * npg]
    qx_ref, qxb_ref, pad_ref, cpad_ref, s2_ref, carry_ref, acc_ref = rest[3 * npg + 1:]
    del pt_ref
    b_id = pl.program_id(0)
    s_id = pl.program_id(1)
    n_seq = pl.num_programs(0) - 1
    n_steps = n_pages // npg
    past = n_pages * page
    wa = n_heads * HEAD_DIM
    s_ref = s2_ref.at[b_id % 2]
    p_ref = s2_ref.at[(b_id + 1) % 2]
    k_phase = b_id < n_seq
    v_phase = b_id >= 1
    row_h = lax.broadcasted_iota(jnp.int32, (n_heads, wa), 0)
    col_h = lax.broadcasted_iota(jnp.int32, (n_heads, wa), 1) >> _log2(HEAD_DIM)
    head_diag = row_h == col_h

    def expand(c):
        out = c
        for t in range(1, t_new):
            out = out + pltpu.roll(c, t * n_heads, axis=1)
        return out

    @pl.when(k_phase & (s_id == 0))
    def _():
        qx_ref[...] = jnp.zeros_like(qx_ref)
        q = q_ref[...]
        for t in range(t_new):
            qx_ref[t * n_heads:(t + 1) * n_heads, :] = jnp.where(head_diag, q[t:t + 1, :], 0.0)
        qxb_ref[...] = qx_ref[...].astype(BF16)
        cpad_ref[...] = jnp.zeros_like(cpad_ref)
        carry_ref[...] = jnp.zeros_like(carry_ref)

    @pl.when(v_phase & (s_id == 0))
    def _():
        acc_ref[...] = jnp.zeros_like(acc_ref)

    def head_major(ref):
        return jnp.concatenate([ref[pl.ds(h, page, stride=n_heads), :] for h in range(n_heads)], axis=1)

    @pl.when(k_phase)
    def _():
        for r in range(npg):
            st = _dot_nt(head_major(k_refs[r]).astype(BF16), qxb_ref[...]) * scale
            cpad_ref[:, 0:n_heads] = c_refs[r][...].T
            cx = expand(cpad_ref[...]) + carry_ref[...]
            carry_ref[...] = cx[page - 1:page, :]
            off = pl.multiple_of((s_id * npg + r) * page, page)
            s_ref[pl.ds(off, page), :] = st - cx

    nl = n_heads * t_new

    @pl.when(v_phase)
    def _():
        acc = acc_ref[...]
        for r in range(npg):
            off = pl.multiple_of((s_id * npg + r) * page, page)
            p = p_ref[pl.ds(off, page), :].T[0:nl, :].astype(BF16)
            acc = acc + _dot(p, head_major(v_refs[r]).astype(BF16))
        acc_ref[...] = acc

    @pl.when(v_phase & (s_id == n_steps - 1))
    def _():
        pad_ref[...] = jnp.zeros_like(pad_ref)
        pad_ref[0:t_new, :] = vn_ref[...]
        p = p_ref[past:past + page, :].T[0:nl, :].astype(BF16)
        acc = acc_ref[...] + _dot(p, pad_ref[...].astype(BF16))
        for t in range(t_new):
            blk = acc[t * n_heads:(t + 1) * n_heads, :]
            o_t = jnp.sum(jnp.where(head_diag, blk, 0.0), axis=0, keepdims=True)
            for h in range(n_heads):
                sl = slice(h * HEAD_DIM, (h + 1) * HEAD_DIM)
                o_ref[t:t + 1, sl] = _rms(o_t[:, sl], g_ref[h:h + 1, :])

    @pl.when(k_phase & (s_id == n_steps - 1))
    def _():
        pad_ref[...] = jnp.zeros_like(pad_ref)
        pad_ref[0:t_new, :] = kn_ref[...]
        st = _dot_nt(pad_ref[...].astype(BF16), qxb_ref[...]) * scale
        lane = lax.broadcasted_iota(jnp.int32, (1, LANES), 1)
        cpad_ref[...] = jnp.zeros_like(cpad_ref)
        cpad_ref[0:t_new, :] = jnp.where(lane < n_heads, cl_ref[...], 0.0)
        cl_x = expand(cpad_ref[...])
        u = lax.broadcasted_iota(jnp.int32, (page, LANES), 0)
        t = lax.broadcasted_iota(jnp.int32, (page, LANES), 1) >> _log2(n_heads)
        st = jnp.where((u < t_new) & (u <= t), st - cl_x, NEG_INF)
        s_past = s_ref[0:past, :] + carry_ref[...]
        m = jnp.maximum(jnp.max(s_past, axis=0, keepdims=True), jnp.max(st, axis=0, keepdims=True))
        p_past = jnp.exp(s_past - m)
        p_new = jnp.exp(st - m)
        inv = 1.0 / (jnp.sum(p_past, axis=0, keepdims=True) + jnp.sum(p_new, axis=0, keepdims=True))
        s_ref[0:past, :] = p_past * inv
        s_ref[past:past + page, :] = p_new * inv


def _fox_sample(page_table, q, k_new, v_new, cl, g_out, cache_k, cache_v, cache_c, page, pool_base, n_heads):
    bsz, t_new, wa = q.shape
    n_pages = page_table.shape[1]
    assert page == LANES and n_heads == SUBLANES and n_heads * t_new <= LANES
    page_bytes = page * wa * 4
    npg = _pick_tile(n_pages, max(1, PAGE_BUFFER_BYTES // (4 * page_bytes)))
    n_steps = n_pages // npg
    scale = HEAD_DIM ** -0.5

    def k_seq(b):
        return jnp.minimum(b, bsz - 1)

    def v_seq(b):
        return jnp.maximum(b - 1, 0)

    def kv_map(seq_fn, r):
        return lambda b, s, pt: (pool_base + pt[seq_fn(b), s * npg + r], 0)

    def c_map(r):
        return lambda b, s, pt: (pt[k_seq(b), s * npg + r], 0, 0)

    k3 = lambda b, s, pt: (k_seq(b), 0, 0)
    v3 = lambda b, s, pt: (v_seq(b), 0, 0)
    in_specs = [
        pl.BlockSpec((None, t_new, wa), k3),
        pl.BlockSpec((None, t_new, wa), k3),
        pl.BlockSpec((None, t_new, wa), v3),
        pl.BlockSpec((None, t_new, LANES), k3),
        pl.BlockSpec((n_heads, HEAD_DIM), lambda b, s, pt: (0, 0)),
    ]
    in_specs += [pl.BlockSpec((page * n_heads, HEAD_DIM), kv_map(k_seq, r)) for r in range(npg)]
    in_specs += [pl.BlockSpec((None, n_heads, page), c_map(r)) for r in range(npg)]
    in_specs += [pl.BlockSpec((page * n_heads, HEAD_DIM), kv_map(v_seq, r)) for r in range(npg)]
    grid_spec = pltpu.PrefetchScalarGridSpec(
        num_scalar_prefetch=1,
        grid=(bsz + 1, n_steps),
        in_specs=in_specs,
        out_specs=pl.BlockSpec((None, t_new, wa), v3),
        scratch_shapes=[
            pltpu.VMEM((LANES, wa), F32),
            pltpu.VMEM((LANES, wa), BF16),
            pltpu.VMEM((page, wa), F32),
            pltpu.VMEM((page, LANES), F32),
            pltpu.VMEM((2, n_pages * page + page, LANES), F32),
            pltpu.VMEM((1, LANES), F32),
            pltpu.VMEM((n_heads * t_new, wa), F32),
        ],
    )
    return pl.pallas_call(
        functools.partial(_fox_sample_body, npg=npg, n_pages=n_pages, page=page, n_heads=n_heads,
                          t_new=t_new, scale=scale),
        out_shape=jax.ShapeDtypeStruct((bsz, t_new, wa), F32),
        grid_spec=grid_spec,
        compiler_params=_cparams(("arbitrary", "arbitrary")),
        name="fox_sample",
    )(page_table, q, k_new, v_new, cl, g_out, *([cache_k] * npg), *([cache_c] * npg), *([cache_v] * npg))


def _bdot(a, b):
    return lax.dot_general(a, b, (((2,), (1,)), ((0,), (0,))), preferred_element_type=F32)


def _bdot_nt(a, b):
    return lax.dot_general(a, b, (((2,), (2,)), ((0,), (0,))), preferred_element_type=F32)


def _mlstm_chunk(qc, kc, vc, i_col, b_col, c_st, n_st, m_st, n_valid):
    nh, ln, _ = qc.shape
    r = lax.broadcasted_iota(jnp.int32, (ln, ln), 0)
    c = lax.broadcasted_iota(jnp.int32, (ln, ln), 1)
    eye = (r == c)[None]
    tril = (r >= c)[None]

    def col_to_row(col):
        return jnp.sum(jnp.where(eye, col, 0.0), axis=1, keepdims=True)

    i_row = col_to_row(i_col)
    b_row = col_to_row(b_col)
    a_col = b_col + m_st
    dmat = jnp.where(tril, b_col - b_row + i_row, NEG_INF)
    mt = jnp.maximum(a_col, jnp.max(dmat, axis=2, keepdims=True))
    qb = qc.astype(BF16)
    kb = kc.astype(BF16)
    vb = vc.astype(BF16)
    w_intra = jnp.exp(dmat - mt) * _bdot_nt(qb, kb)
    w_inter = jnp.exp(a_col - mt)
    num = w_inter * _bdot(qb, c_st.astype(BF16)) + _bdot(w_intra.astype(BF16), vb)
    den = w_inter * jnp.sum(qc * n_st, axis=2, keepdims=True) + jnp.sum(w_intra, axis=2, keepdims=True)
    h = num / jnp.maximum(jnp.abs(den), jnp.exp(-mt))
    last = n_valid - 1
    m_new = mt[:, last:last + 1, :]
    b_last = b_col[:, last:last + 1, :]
    w_state = jnp.exp(b_last - b_col + i_col - m_new)
    if n_valid < ln:
        row = lax.broadcasted_iota(jnp.int32, (1, ln, 1), 1)
        w_state = jnp.where(row < n_valid, w_state, 0.0)
    decay = jnp.exp(b_last + m_st - m_new)
    ks = w_state * kc
    ks_t = jnp.stack([ks[hd].T for hd in range(nh)]).astype(BF16)
    c_new = decay * c_st + _bdot(ks_t, vb)
    n_new = decay * n_st + jnp.sum(ks, axis=1, keepdims=True)
    return h, c_new, n_new, m_new


def _heads(x, n_heads, offset=0, width=HEAD_DIM):
    return jnp.stack([x[:, offset + h * width:offset + (h + 1) * width] for h in range(n_heads)])


def _mlstm_prompt_body(qk_ref, w_ref, b_ref, v_ref, o_ref, act_ref, cl_ref, g_ref,
                       y_ref, c_out_ref, n_out_ref, m_out_ref,
                       ext_ref, qk_scr, c_ref, n_ref, m_ref, *, rb, chunk, n_heads):
    blk = pl.program_id(1)
    pad = SUBLANES
    wm = n_heads * HEAD_DIM
    kscale = HEAD_DIM ** -0.5

    @pl.when(blk == 0)
    def _():
        ext_ref[0:pad, :] = jnp.zeros((pad, 2 * wm), F32)
        c_ref[...] = jnp.zeros_like(c_ref)
        n_ref[...] = jnp.zeros_like(n_ref)
        m_ref[...] = jnp.zeros_like(m_ref)

    ext_ref[pad:pad + rb, :] = qk_ref[...]
    acc = b_ref[...] + w_ref[CONV_W - 1:CONV_W, :] * qk_ref[...]
    for j in range(CONV_W - 1):
        off = pad - (CONV_W - 1) + j
        acc = acc + w_ref[j:j + 1, :] * ext_ref[off:off + rb, :]
    qk_scr[...] = acc * jax.nn.sigmoid(acc)
    ext_ref[0:pad, :] = qk_ref[rb - pad:rb, :]

    g = g_ref[...][:, None, :]
    for ci in range(rb // chunk):
        rows = slice(ci * chunk, (ci + 1) * chunk)
        qk = qk_scr[rows, :]
        hh, c_new, n_new, m_new = _mlstm_chunk(
            _heads(qk, n_heads), _heads(qk, n_heads, wm) * kscale, _heads(v_ref[rows, :], n_heads),
            _heads(act_ref[rows, :], n_heads, n_heads, 1), _heads(cl_ref[rows, :], n_heads, 2 * n_heads, 1),
            c_ref[...], n_ref[...], m_ref[:, :, 0:1], chunk)
        c_ref[...] = c_new
        n_ref[...] = n_new
        m_ref[...] = jnp.broadcast_to(m_new, m_ref.shape)
        y = jax.nn.sigmoid(_heads(o_ref[rows, :], n_heads)) * _rms(hh, g)
        for h in range(n_heads):
            y_ref[rows, h * HEAD_DIM:(h + 1) * HEAD_DIM] = y[h].astype(y_ref.dtype)

    @pl.when(blk == pl.num_programs(1) - 1)
    def _():
        c_out_ref[...] = c_ref[...]
        n_out_ref[...] = n_ref[:, 0, :]
        m_out_ref[...] = m_ref[:, 0, :]


def _mlstm_prompt(qkm, conv_w, conv_b, vm, om, act, cl, g_out, batch, seq, n_heads):
    n = qkm.shape[0]
    wm = n_heads * HEAD_DIM
    chunk = math.gcd(seq, MLSTM_CHUNK)
    rb = _pick_tile(seq, 4 * chunk)
    assert rb % chunk == 0 and rb >= SUBLANES
    nblk = seq // rb
    row = lambda b, i: (b * nblk + i, 0)
    const = lambda b, i: (0, 0)
    st = lambda b, i: (b, 0, 0)
    out_shape = (
        jax.ShapeDtypeStruct((n, wm), BF16),
        jax.ShapeDtypeStruct((batch, n_heads, HEAD_DIM, HEAD_DIM), F32),
        jax.ShapeDtypeStruct((batch, n_heads, HEAD_DIM), F32),
        jax.ShapeDtypeStruct((batch, n_heads, LANES), F32),
    )
    return pl.pallas_call(
        functools.partial(_mlstm_prompt_body, rb=rb, chunk=chunk, n_heads=n_heads),
        out_shape=out_shape,
        grid=(batch, nblk),
        in_specs=[
            pl.BlockSpec((rb, 2 * wm), row),
            pl.BlockSpec((CONV_W, 2 * wm), const),
            pl.BlockSpec((1, 2 * wm), const),
            pl.BlockSpec((rb, wm), row),
            pl.BlockSpec((rb, wm), row),
            pl.BlockSpec((rb, LANES), row),
            pl.BlockSpec((rb, LANES), row),
            pl.BlockSpec((n_heads, HEAD_DIM), const),
        ],
        out_specs=(
            pl.BlockSpec((rb, wm), row),
            pl.BlockSpec((None, n_heads, HEAD_DIM, HEAD_DIM), lambda b, i: (b, 0, 0, 0)),
            pl.BlockSpec((None, n_heads, HEAD_DIM), st),
            pl.BlockSpec((None, n_heads, LANES), st),
        ),
        scratch_shapes=[
            pltpu.VMEM((rb + SUBLANES, 2 * wm), F32),
            pltpu.VMEM((rb, 2 * wm), F32),
            pltpu.VMEM((n_heads, HEAD_DIM, HEAD_DIM), F32),
            pltpu.VMEM((n_heads, 1, HEAD_DIM), F32),
            pltpu.VMEM((n_heads, 1, LANES), F32),
        ],
        compiler_params=_cparams(("parallel", "arbitrary")),
        name="mlstm_prompt",
    )(qkm, conv_w, conv_b, vm, om, act, cl, g_out)


def _mlstm_sample_body(qk_ref, prev_ref, w_ref, b_ref, v_ref, o_ref, act_ref, cl_ref, g_ref, c0_ref, n0_ref, m0_ref,
                       y_ref, c_out_ref, n_out_ref, m_out_ref, ext_ref, pad_ref, *, t_new, n_heads, sb):
    wm = n_heads * HEAD_DIM
    rows = SUBLANES
    kscale = HEAD_DIM ** -0.5
    valid = lax.broadcasted_iota(jnp.int32, (rows, 1), 0) < t_new

    def padded(val, width):
        pad_ref[:, 0:width] = jnp.zeros((rows, width), F32)
        pad_ref[0:t_new, 0:width] = val
        return pad_ref[:, 0:width]

    qs, ks, vs, os_, i_cols, b_cols, n0s, m0s = [], [], [], [], [], [], [], []
    for s in range(sb):
        tok = slice(s * t_new, (s + 1) * t_new)
        ext_ref[...] = jnp.zeros_like(ext_ref)
        ext_ref[0:CONV_W - 1, :] = prev_ref[s]
        ext_ref[CONV_W - 1:CONV_W - 1 + t_new, :] = qk_ref[tok, :]
        acc = b_ref[...] + w_ref[0:1, :] * ext_ref[0:rows, :]
        for j in range(1, CONV_W):
            acc = acc + w_ref[j:j + 1, :] * ext_ref[j:j + rows, :]
        qk = jnp.where(valid, acc * jax.nn.sigmoid(acc), 0.0)
        qs.append(_heads(qk, n_heads))
        ks.append(_heads(qk, n_heads, wm) * kscale)
        vs.append(_heads(padded(v_ref[tok, :], wm), n_heads))
        os_.append(_heads(padded(o_ref[tok, :], wm), n_heads))
        i_cols.append(jnp.where(valid, _heads(padded(act_ref[tok, :], LANES), n_heads, n_heads, 1), NEG_INF))
        b_col = _heads(padded(cl_ref[tok, :], LANES), n_heads, 2 * n_heads, 1)
        b_cols.append(jnp.where(valid, b_col, b_col[:, t_new - 1:t_new, :]))
        n0s.append(jnp.stack([n0_ref[s, h:h + 1, :] for h in range(n_heads)]))
        m0s.append(jnp.stack([m0_ref[s, h:h + 1, :] for h in range(n_heads)]))

    cat = lambda parts: jnp.concatenate(parts, axis=0)
    c0 = cat([c0_ref[s] for s in range(sb)])
    hh, c_new, n_new, m_new = _mlstm_chunk(cat(qs), cat(ks), cat(vs), cat(i_cols), cat(b_cols),
                                           c0, cat(n0s), cat(m0s), t_new)
    g = jnp.concatenate([g_ref[...]] * sb, axis=0)[:, None, :]
    y = jax.nn.sigmoid(cat(os_)) * _rms(hh, g)
    m_wide = jnp.broadcast_to(m_new, (sb * n_heads, 1, LANES))
    for s in range(sb):
        for h in range(n_heads):
            y_ref[s * t_new:(s + 1) * t_new, h * HEAD_DIM:(h + 1) * HEAD_DIM] = y[s * n_heads + h, 0:t_new, :]
        grp = slice(s * n_heads, (s + 1) * n_heads)
        c_out_ref[s] = c_new[grp]
        n_out_ref[s] = n_new[grp, 0, :]
        m_out_ref[s] = m_wide[grp, 0, :]


def _mlstm_sample(qkm, conv_prev, conv_w, conv_b, vm, om, act, cl, g_out, c0, n0, m0, t_new, n_heads):
    n, wm2 = qkm.shape
    bsz = n // t_new
    wm = wm2 // 2
    assert t_new + CONV_W - 1 <= SUBLANES
    sb = _pick_tile(bsz, 4)
    assert (sb * t_new) % SUBLANES == 0 or sb == bsz
    tok = lambda b: (b, 0)
    seq3 = lambda b: (b, 0, 0)
    const = lambda b: (0, 0)
    out_shape = (
        jax.ShapeDtypeStruct((n, wm), F32),
        jax.ShapeDtypeStruct((bsz, n_heads, HEAD_DIM, HEAD_DIM), F32),
        jax.ShapeDtypeStruct((bsz, n_heads, HEAD_DIM), F32),
        jax.ShapeDtypeStruct((bsz, n_heads, LANES), F32),
    )
    return pl.pallas_call(
        functools.partial(_mlstm_sample_body, t_new=t_new, n_heads=n_heads, sb=sb),
        out_shape=out_shape,
        grid=(bsz // sb,),
        in_specs=[
            pl.BlockSpec((sb * t_new, wm2), tok),
            pl.BlockSpec((sb, CONV_W - 1, wm2), seq3),
            pl.BlockSpec((CONV_W, wm2), const),
            pl.BlockSpec((1, wm2), const),
            pl.BlockSpec((sb * t_new, wm), tok),
            pl.BlockSpec((sb * t_new, wm), tok),
            pl.BlockSpec((sb * t_new, LANES), tok),
            pl.BlockSpec((sb * t_new, LANES), tok),
            pl.BlockSpec((n_heads, HEAD_DIM), const),
            pl.BlockSpec((sb, n_heads, HEAD_DIM, HEAD_DIM), lambda b: (b, 0, 0, 0)),
            pl.BlockSpec((sb, n_heads, HEAD_DIM), seq3),
            pl.BlockSpec((sb, n_heads, 1), seq3),
        ],
        out_specs=(
            pl.BlockSpec((sb * t_new, wm), tok),
            pl.BlockSpec((sb, n_heads, HEAD_DIM, HEAD_DIM), lambda b: (b, 0, 0, 0)),
            pl.BlockSpec((sb, n_heads, HEAD_DIM), seq3),
            pl.BlockSpec((sb, n_heads, LANES), seq3),
        ),
        scratch_shapes=[pltpu.VMEM((2 * SUBLANES, wm2), F32), pltpu.VMEM((SUBLANES, wm), F32)],
        compiler_params=_cparams(("parallel",)),
        name="mlstm_sample",
    )(qkm, conv_prev, conv_w, conv_b, vm, om, act, cl, g_out, c0, n0, m0)


def _outproj_body(ya_ref, ym_ref, w_ref, x_ref, o_ref, *, wa):
    y = _dot(ya_ref[...].astype(BF16), w_ref[0:wa, :]) + _dot(ym_ref[...].astype(BF16), w_ref[wa:, :])
    o_ref[...] = x_ref[...] + y


def _outproj(ya, ym, w, x):
    n, d = x.shape
    wa = ya.shape[1]
    wm = ym.shape[1]
    tm = _pick_tile(n, 512)
    row = lambda i: (i, 0)
    return pl.pallas_call(
        functools.partial(_outproj_body, wa=wa),
        out_shape=jax.ShapeDtypeStruct((n, d), F32),
        grid=(n // tm,),
        in_specs=[
            pl.BlockSpec((tm, wa), row),
            pl.BlockSpec((tm, wm), row),
            pl.BlockSpec((wa + wm, d), lambda i: (0, 0)),
            pl.BlockSpec((tm, d), row),
        ],
        out_specs=pl.BlockSpec((tm, d), row),
        compiler_params=_cparams(("parallel",)),
        name="outproj",
    )(ya, ym, w, x)


def _cast_pad_body(x_ref, o_ref, *, axis, size):
    if axis == 1:
        o_ref[:, 0:size] = x_ref[...].astype(BF16)
        o_ref[:, size:] = jnp.zeros((o_ref.shape[0], o_ref.shape[1] - size), BF16)
    else:
        o_ref[0:size, :] = x_ref[...].astype(BF16)
        o_ref[size:, :] = jnp.zeros((o_ref.shape[0] - size, o_ref.shape[1]), BF16)


def _cast_pad(w, axis, padded):
    r, c = w.shape
    size = w.shape[axis]
    assert padded > size
    if axis == 1:
        t = _pick_tile(r, 256)
        grid, in_blk, out_blk, imap = (r // t,), (t, c), (t, padded), (lambda i: (i, 0))
        out_shape = (r, padded)
    else:
        t = _pick_tile(c, 256)
        grid, in_blk, out_blk, imap = (c // t,), (r, t), (padded, t), (lambda i: (0, i))
        out_shape = (padded, c)
    return pl.pallas_call(
        functools.partial(_cast_pad_body, axis=axis, size=size),
        out_shape=jax.ShapeDtypeStruct(out_shape, BF16),
        grid=grid,
        in_specs=[pl.BlockSpec(in_blk, imap)],
        out_specs=pl.BlockSpec(out_blk, imap),
        compiler_params=_cparams(("parallel",)),
        name="cast_pad",
    )(w)


def _win_prep_body(wt_ref, fa_ref, im_ref, main_ref, gates_ref, *, n_gate_cols):
    main_ref[...] = wt_ref[...].T.astype(BF16)

    @pl.when(pl.program_id(0) == 0)
    def _():
        d = gates_ref.shape[0]
        g = jnp.concatenate([fa_ref[...], im_ref[...], jnp.zeros((LANES - n_gate_cols, d), F32)], axis=0)
        gates_ref[...] = g.T.astype(BF16)


def _win_prep(w_in_t, cuts, n_gate_cols, tile):
    n_in, d = w_in_t.shape
    o_f, o_qk, o_i = cuts
    n_main = o_f + (o_i - o_qk)
    assert o_f % tile == 0 and (o_i - o_qk) % tile == 0 and o_qk % SUBLANES == 0 and o_i % SUBLANES == 0
    n_lead = o_f // tile

    def row_off(j):
        return pl.multiple_of(jnp.where(j < n_lead, j * tile, o_qk + (j - n_lead) * tile), SUBLANES)

    el = pl.Element
    return pl.pallas_call(
        functools.partial(_win_prep_body, n_gate_cols=n_gate_cols),
        out_shape=(jax.ShapeDtypeStruct((d, n_main), BF16), jax.ShapeDtypeStruct((d, LANES), BF16)),
        grid=(n_main // tile,),
        in_specs=[
            pl.BlockSpec((el(tile), el(d)), lambda j: (row_off(j), 0)),
            pl.BlockSpec((el(o_qk - o_f), el(d)), lambda j: (o_f, 0)),
            pl.BlockSpec((el(n_in - o_i), el(d)), lambda j: (o_i, 0)),
        ],
        out_specs=(pl.BlockSpec((d, tile), lambda j: (0, j)), pl.BlockSpec((d, LANES), lambda j: (0, 0))),
        compiler_params=_cparams(("arbitrary",)),
        name="win_prep",
    )(w_in_t, w_in_t, w_in_t)


def _prep_weights(lw, n_heads_a, n_heads_m):
    wa = n_heads_a * HEAD_DIM
    wm = n_heads_m * HEAD_DIM
    f = lw['w1_gate'].shape[1]
    tf = 512
    fp = tf * ((f + tf - 1) // tf)

    def ffn_w(wg, wu, wd):
        if fp == f:
            return wg.astype(BF16), wu.astype(BF16), wd.astype(BF16)
        return _cast_pad(wg, 1, fp), _cast_pad(wu, 1, fp), _cast_pad(wd, 0, fp)

    w_in = lw['w_in']
    o_f = 3 * wa
    o_qk = o_f + n_heads_a
    o_i = o_qk + 2 * wm + 2 * wm
    o_fm = o_i + n_heads_m
    ng = 2 * n_heads_m + n_heads_a
    assert o_fm + n_heads_m == w_in.shape[1]
    w_main, w_gates = _win_prep(jnp.swapaxes(w_in, 0, 1), (o_f, o_qk, o_i), ng, wa)
    bias = jnp.concatenate([lw['b_fox_f'], lw['b_m_i'], lw['b_m_f'], jnp.zeros((LANES - ng,), F32)])[None, :]
    return {
        'ffn1': ffn_w(lw['w1_gate'], lw['w1_up'], lw['w1_down']),
        'ffn_w': ffn_w,
        'tf': tf,
        'fp': fp,
        'w_main': w_main,
        'w_gates': w_gates,
        'gate_bias': bias,
        'w_out': lw['w_out'].astype(BF16),
    }


def kernel(x_prompt, x_sample, cache_k, cache_v, cache_logf, state_conv, state_C, state_n, state_m, page_table,
           g_ffn1, w1_gate, w1_up, w1_down, g_mix, w_in, b_fox_f, b_m_i, b_m_f, conv_w, conv_b, g_q, g_k,
           g_out_a, g_out_m, w_out, g_ffn2, w2_gate, w2_up, w2_down):
    depth = w_in.shape[0]
    bp, seq, d = x_prompt.shape
    bs, t_new, _ = x_sample.shape
    n_heads_a = g_out_a.shape[1]
    n_heads_m = g_out_m.shape[1]
    wa = n_heads_a * HEAD_DIM
    wm = n_heads_m * HEAD_DIM
    n_pool, page = cache_k.shape[1], cache_k.shape[2]

    yp = x_prompt.reshape(bp * seq, d)
    ys = x_sample.reshape(bs * t_new, d)
    outs = [[] for _ in range(14)]
    for l in range(depth):
        lw = {
            'w1_gate': w1_gate[l], 'w1_up': w1_up[l], 'w1_down': w1_down[l], 'w_in': w_in[l],
            'b_fox_f': b_fox_f[l], 'b_m_i': b_m_i[l], 'b_m_f': b_m_f[l], 'w_out': w_out[l],
            'w2_gate': w2_gate[l], 'w2_up': w2_up[l], 'w2_down': w2_down[l],
        }
        pw = _prep_weights(lw, n_heads_a, n_heads_m)
        gf1 = g_ffn1[l][None, :]
        gf2 = g_ffn2[l][None, :]
        gmix = g_mix[l][None, :]
        gq = g_q[l][None, :]
        gk = g_k[l][None, :]
        cw = conv_w[l]
        cb = conv_b[l][None, :]
        goa = g_out_a[l]
        gom = g_out_m[l]

        w2 = (lw['w2_gate'], lw['w2_up'], lw['w2_down'])
        ffn1_steps = (yp.shape[0] // _pick_tile(yp.shape[0], 512)) * (pw['fp'] // pw['tf'])
        if w2[0].shape[1] % LANES == 0 and ffn1_steps >= 3 * (pw['fp'] // LANES):
            (x1, xn), ffn2_w = _ffn(yp, gf1, *pw['ffn1'], pw['tf'], g_next=gmix, prep=w2)
        else:
            ffn2_w = pw['ffn_w'](*w2)
            x1, xn = _ffn(yp, gf1, *pw['ffn1'], pw['tf'], g_next=gmix)
        q, k, kb, v, vt, qkm, vm, om, gt = _inproj(xn, pw['w_main'], pw['w_gates'], gq, gk, n_heads_a)
        act, cg, cl = _gates(gt, pw['gate_bias'], seq, math.gcd(seq, MLSTM_CHUNK), n_heads_a)
        ya = _fox_prompt(q, kb, vt, cg, goa, bp, seq, n_heads_a)
        ym, c_p, n_p, m_p = _mlstm_prompt(qkm, cw, cb, vm, om, act, cl, gom, bp, seq, n_heads_m)
        x2 = _outproj(ya, ym, pw['w_out'], x1)
        yp = _ffn(x2, gf2, *ffn2_w, pw['tf'])
        outs[0].append(k.reshape(bp, seq, n_heads_a, HEAD_DIM))
        outs[1].append(v.reshape(bp, seq, n_heads_a, HEAD_DIM))
        outs[2].append(act[:, :n_heads_a].reshape(bp, seq, n_heads_a))
        outs[3].append(qkm.reshape(bp, seq, 2 * wm)[:, seq - (CONV_W - 1):, :])
        outs[4].append(c_p)
        outs[5].append(n_p)
        outs[6].append(m_p[:, :, 0])

        x1, xn = _ffn(ys, gf1, *pw['ffn1'], pw['tf'], g_next=gmix)
        q, k, kb, v, vt, qkm, vm, om, gt = _inproj(xn, pw['w_main'], pw['w_gates'], gq, gk, n_heads_a)
        act, cg, cl = _gates(gt, pw['gate_bias'], t_new, math.gcd(t_new, MLSTM_CHUNK), n_heads_a)
        r3 = lambda a: a.reshape(bs, t_new, a.shape[-1])
        lf_rows = jnp.swapaxes(cache_logf[l], 1, 2).reshape(n_pool * n_heads_a, page)
        cache_c = _pool_cumsum(lf_rows).reshape(n_pool, n_heads_a, page)
        ya = _fox_sample(page_table, r3(q).astype(F32), r3(k), r3(v), r3(cg), goa,
                         cache_k.reshape(-1, HEAD_DIM), cache_v.reshape(-1, HEAD_DIM),
                         cache_c, page, l * n_pool, n_heads_a)
        ym, c_s, n_s, m_s = _mlstm_sample(qkm, state_conv[l], cw, cb, vm, om, act, cl, gom,
                                          state_C[l], state_n[l], state_m[l][:, :, None], t_new, n_heads_m)
        x2 = _outproj(ya.reshape(bs * t_new, wa), ym, pw['w_out'], x1)
        ys = _ffn(x2, gf2, *ffn2_w, pw['tf'])
        conv_ext = jnp.concatenate([state_conv[l], r3(qkm)], axis=1)
        outs[7].append(k.reshape(bs, t_new, n_heads_a, HEAD_DIM))
        outs[8].append(v.reshape(bs, t_new, n_heads_a, HEAD_DIM))
        outs[9].append(act[:, :n_heads_a].reshape(bs, t_new, n_heads_a))
        outs[10].append(conv_ext[:, t_new:, :])
        outs[11].append(c_s)
        outs[12].append(n_s)
        outs[13].append(m_s[:, :, 0])

    return (yp.reshape(bp, seq, d), ys.reshape(bs, t_new, d)) + tuple(jnp.stack(o) for o in outs)
```

```python
import functools
import math

import jax
import jax.numpy as jnp
from jax import lax
from jax.experimental import pallas as pl
from jax.experimental.pallas import tpu as pltpu

F32 = jnp.float32
BF16 = jnp.bfloat16

EPS = 1e-6
HEAD_DIM = 128
CONV_W = 4
MLSTM_CHUNK = 128
LANES = 128
SUBLANES = 8
VMEM_LIMIT_BYTES = 56 * 1024 * 1024
PAGE_BUFFER_BYTES = 32 * 1024 * 1024
NEG_INF = float("-inf")
LOG2E = math.log2(math.e)


def _cparams(sem):
    return pltpu.CompilerParams(dimension_semantics=sem, vmem_limit_bytes=VMEM_LIMIT_BYTES)


def _rms(x, g):
    y = x * lax.rsqrt(jnp.mean(x * x, axis=-1, keepdims=True) + EPS)
    return y * g


def _log_sigmoid(x):
    return jnp.minimum(x, 0.0) - jnp.log1p(jnp.exp(-jnp.abs(x)))


def _dot(a, b, precision=None):
    return jnp.dot(a, b, preferred_element_type=F32, precision=precision)


def _dot_nt(a, b, precision=None):
    return lax.dot_general(a, b, (((1,), (1,)), ((), ())), preferred_element_type=F32, precision=precision)


def _pick_tile(n, pref):
    t = min(n, pref)
    while n % t:
        t //= 2
    return t


def _log2(n):
    k = int(math.log2(n))
    assert 1 << k == n
    return k


def _ffn_body(x_ref, g_ref, wg_ref, wu_ref, wd_ref, gn_ref, *rest, with_next, n_prep, prep_real, prep_slabs):
    prep_in = rest[:n_prep]
    rest = rest[n_prep:]
    if with_next:
        o_ref, on_ref = rest[:2]
        rest = rest[2:]
    else:
        o_ref = rest[0]
        rest = rest[1:]
    prep_out = rest[:n_prep]
    xn_ref, acc_ref = rest[n_prep:]
    j = pl.program_id(1)

    @pl.when(j == 0)
    def _():
        xn_ref[...] = _rms(x_ref[...], g_ref[...]).astype(BF16)
        acc_ref[...] = jnp.zeros_like(acc_ref)

    xn = xn_ref[...]
    half = wg_ref.shape[1] // 2
    part = None
    for c in range(2):
        cols = slice(c * half, (c + 1) * half)
        gate = _dot(xn, wg_ref[:, cols])
        up = _dot(xn, wu_ref[:, cols])
        h = (gate * jax.nn.sigmoid(gate)) * up
        d = _dot(h.astype(BF16), wd_ref[cols, :])
        part = d if part is None else part + d
    acc_ref[...] += part

    @pl.when(j == pl.num_programs(1) - 1)
    def _():
        y = x_ref[...] + 0.5 * acc_ref[...]
        o_ref[...] = y
        if with_next:
            on_ref[...] = _rms(y, gn_ref[...]).astype(BF16)

    t = pl.program_id(0) * pl.num_programs(1) + j
    for k in range(n_prep):
        t0 = k * prep_slabs

        @pl.when((t >= t0) & (t < t0 + prep_real))
        def _(k=k):
            prep_out[k][...] = prep_in[k][...].astype(BF16)

        @pl.when((t >= t0 + prep_real) & (t < t0 + prep_slabs))
        def _(k=k):
            prep_out[k][...] = jnp.zeros_like(prep_out[k])


def _ffn(x, g, wg, wu, wd, tf, g_next=None, prep=None):
    n, d = x.shape
    fp = wg.shape[1]
    tm = _pick_tile(n, 512)
    ni, nj = n // tm, fp // tf
    with_next = g_next is not None
    row = pl.BlockSpec((tm, d), lambda i, j: (i, 0))
    vec = pl.BlockSpec((1, d), lambda i, j: (0, 0))
    out_shape = [jax.ShapeDtypeStruct((n, d), F32)]
    out_specs = [row]
    if with_next:
        out_shape.append(jax.ShapeDtypeStruct((n, d), BF16))
        out_specs.append(row)
    in_specs = [
        row, vec,
        pl.BlockSpec((d, tf), lambda i, j: (0, j)),
        pl.BlockSpec((d, tf), lambda i, j: (0, j)),
        pl.BlockSpec((tf, d), lambda i, j: (j, 0)),
        vec,
    ]
    operands = [x, g, wg, wu, wd, g_next if with_next else g]
    n_prep = prep_real = prep_slabs = 0
    if prep is not None:
        f = prep[0].shape[1]
        assert f % LANES == 0 and fp % LANES == 0
        n_prep, prep_real, prep_slabs = 3, f // LANES, fp // LANES
        assert ni * nj >= n_prep * prep_slabs

        def slab(k, last):
            return lambda i, j: jnp.clip(i * nj + j - k * prep_slabs, 0, last)

        for k in range(n_prep):
            cols = k < 2
            blk = (d, LANES) if cols else (LANES, d)
            for last, specs in ((prep_real - 1, in_specs), (prep_slabs - 1, out_specs)):
                s = slab(k, last)
                specs.append(pl.BlockSpec(blk, (lambda i, j, s=s: (0, s(i, j))) if cols
                                          else (lambda i, j, s=s: (s(i, j), 0))))
            out_shape.append(jax.ShapeDtypeStruct((d, fp) if cols else (fp, d), BF16))
        operands += list(prep)
    res = pl.pallas_call(
        functools.partial(_ffn_body, with_next=with_next, n_prep=n_prep, prep_real=prep_real,
                          prep_slabs=prep_slabs),
        out_shape=tuple(out_shape),
        grid=(ni, nj),
        in_specs=in_specs,
        out_specs=tuple(out_specs),
        scratch_shapes=[pltpu.VMEM((tm, d), BF16), pltpu.VMEM((tm, d), F32)],
        compiler_params=_cparams(("arbitrary", "arbitrary") if n_prep else ("parallel", "arbitrary")),
        name="ffn",
    )(*operands)
    n_main = 2 if with_next else 1
    main = res[0] if n_main == 1 else tuple(res[:2])
    return (main, tuple(res[n_main:])) if prep is not None else main


def _inproj_body(xn_ref, w_ref, wgt_ref, gq_ref, gk_ref,
                 q_ref, k_ref, kb_ref, v_ref, vt_ref, qkm_ref, vm_ref, om_ref, gt_ref, *, n_heads):
    j = pl.program_id(0)
    xn = xn_ref[...]
    acc = _dot(xn, w_ref[...])

    @pl.when(j == 0)
    def _():
        for h in range(n_heads):
            sl = slice(h * HEAD_DIM, (h + 1) * HEAD_DIM)
            q_ref[:, sl] = _rms(acc[:, sl], gq_ref[...]).astype(BF16)
        gt_ref[...] = _dot(xn, wgt_ref[...])

    @pl.when(j == 1)
    def _():
        for h in range(n_heads):
            sl = slice(h * HEAD_DIM, (h + 1) * HEAD_DIM)
            kn = _rms(acc[:, sl], gk_ref[...])
            k_ref[:, sl] = kn
            kb_ref[:, sl] = kn.astype(BF16)

    @pl.when(j == 2)
    def _():
        v_ref[...] = acc
        vt_ref[...] = acc.T.astype(BF16)

    @pl.when((j == 3) | (j == 4))
    def _():
        qkm_ref[...] = acc

    @pl.when(j == 5)
    def _():
        vm_ref[...] = acc

    @pl.when(j == 6)
    def _():
        om_ref[...] = acc


def _inproj(xn, w_main, w_gates, gq, gk, n_heads):
    n, d = xn.shape
    wa = n_heads * HEAD_DIM
    assert w_main.shape[1] == 7 * wa
    tm = _pick_tile(n, 512)
    ni = n // tm
    const = lambda j, i: (0, 0)

    def rows_at(j, i, j0, j1):
        return jnp.where(j < j0, 0, jnp.where(j > j1, ni - 1, i))

    def row_map(j0, j1=None):
        j1 = j0 if j1 is None else j1
        return lambda j, i: (rows_at(j, i, j0, j1), 0)

    out_shape = (
        jax.ShapeDtypeStruct((n, wa), BF16),
        jax.ShapeDtypeStruct((n, wa), F32),
        jax.ShapeDtypeStruct((n, wa), BF16),
        jax.ShapeDtypeStruct((n, wa), F32),
        jax.ShapeDtypeStruct((wa, n), BF16),
        jax.ShapeDtypeStruct((n, 2 * wa), F32),
        jax.ShapeDtypeStruct((n, wa), F32),
        jax.ShapeDtypeStruct((n, wa), F32),
        jax.ShapeDtypeStruct((n, LANES), F32),
    )
    out_specs = (
        pl.BlockSpec((tm, wa), row_map(0)),
        pl.BlockSpec((tm, wa), row_map(1)),
        pl.BlockSpec((tm, wa), row_map(1)),
        pl.BlockSpec((tm, wa), row_map(2)),
        pl.BlockSpec((wa, tm), lambda j, i: (0, rows_at(j, i, 2, 2))),
        pl.BlockSpec((tm, wa), lambda j, i: (rows_at(j, i, 3, 4), jnp.clip(j - 3, 0, 1))),
        pl.BlockSpec((tm, wa), row_map(5)),
        pl.BlockSpec((tm, wa), row_map(6)),
        pl.BlockSpec((tm, LANES), row_map(0)),
    )
    return pl.pallas_call(
        functools.partial(_inproj_body, n_heads=n_heads),
        out_shape=out_shape,
        grid=(7, ni),
        in_specs=[
            pl.BlockSpec((tm, d), lambda j, i: (i, 0)),
            pl.BlockSpec((d, wa), lambda j, i: (0, j)),
            pl.BlockSpec((d, LANES), const),
            pl.BlockSpec((1, HEAD_DIM), const),
            pl.BlockSpec((1, HEAD_DIM), const),
        ],
        out_specs=out_specs,
        compiler_params=_cparams(("arbitrary", "arbitrary")),
        name="inproj",
    )(xn, w_main, w_gates, gq, gk)


def _split3(x):
    hi = x.astype(BF16)
    r1 = x - hi.astype(F32)
    mid = r1.astype(BF16)
    lo = (r1 - mid.astype(F32)).astype(BF16)
    return hi, mid, lo


def _dot_01(m01, x):
    hi, mid, lo = _split3(x)
    return _dot(m01, hi) + _dot(m01, mid) + _dot(m01, lo)


def _gates_body(g_ref, b_ref, act_ref, cg_ref, cl_ref, *, seg, lc, rows, n_heads):
    ch = min(rows, 256)
    sg = min(seg, ch)
    lane = lax.broadcasted_iota(jnp.int32, (1, LANES), 1)
    is_logsig = (lane < n_heads) | ((lane >= 2 * n_heads) & (lane < 3 * n_heads))
    r = lax.broadcasted_iota(jnp.int32, (ch, ch), 0)
    c = lax.broadcasted_iota(jnp.int32, (ch, ch), 1)
    sh_g = _log2(sg)
    sh_l = _log2(lc)
    tri_g = jnp.where((r >= c) & ((r >> sh_g) == (c >> sh_g)), 1.0, 0.0).astype(BF16)
    tri_l = jnp.where((r >= c) & ((r >> sh_l) == (c >> sh_l)), 1.0, 0.0).astype(BF16)
    carry = jnp.zeros((1, LANES), F32)
    for ci in range(rows // ch):
        sl = slice(ci * ch, (ci + 1) * ch)
        x = g_ref[sl, :] + b_ref[...]
        act = jnp.where(is_logsig, _log_sigmoid(x), x)
        act_ref[sl, :] = act
        cg = _dot_01(tri_g, act)
        if seg > ch:
            cg = cg + carry
            carry = cg[ch - 1:ch, :]
        cg_ref[sl, :] = cg
        cl_ref[sl, :] = _dot_01(tri_l, act)


def _gates(gates, bias, seg, lc, n_heads):
    n = gates.shape[0]
    rows = seg if seg >= 256 else _pick_tile(n, 256)
    blk = pl.BlockSpec((rows, LANES), lambda b: (b, 0))
    return pl.pallas_call(
        functools.partial(_gates_body, seg=seg, lc=lc, rows=rows, n_heads=n_heads),
        out_shape=(jax.ShapeDtypeStruct((n, LANES), F32),) * 3,
        grid=(n // rows,),
        in_specs=[blk, pl.BlockSpec((1, LANES), lambda b: (0, 0))],
        out_specs=(blk, blk, blk),
        compiler_params=_cparams(("parallel",)),
        name="gates",
    )(gates, bias)


def _fox_prompt_body(q_ref, k_ref, vt_ref, c_ref, g_ref, o_ref,
                     qtail_ref, ktail_ref, m_ref, l_ref, acc_ref, *, tile, scale, n_heads):
    qi = pl.program_id(1)
    ki = pl.program_id(2)
    exp2_scale = scale * LOG2E

    def head_cols(h):
        return pl.ds(pl.multiple_of(h * HEAD_DIM, HEAD_DIM), HEAD_DIM)

    @pl.when((qi == 0) & (ki == 0))
    def _():
        lane = lax.broadcasted_iota(jnp.int32, (1, LANES), 1)

        def tails(h, carry):
            c = jnp.sum(jnp.where(lane == h, c_ref[...], 0.0), axis=1, keepdims=True) * (1.0 / scale)
            p1, p2, p3 = (p.astype(F32) for p in _split3(c))
            one = jnp.ones_like(p1)

            def tail(cols):
                t = jnp.zeros((c.shape[0], LANES), F32)
                for j, col in enumerate(cols):
                    t = jnp.where(lane == j, col, t)
                return t.astype(BF16)

            qtail_ref[h] = tail((p1, p2, p3, one, one, one))
            ktail_ref[h] = tail((one, one, one, -p1, -p2, -p3))
            return carry

        lax.fori_loop(0, n_heads, tails, 0)

    @pl.when(ki == 0)
    def _():
        m_ref[...] = jnp.full_like(m_ref, NEG_INF)
        l_ref[...] = jnp.zeros_like(l_ref)
        acc_ref[...] = jnp.zeros_like(acc_ref)

    def block(masked):
        q_rows = pl.ds(pl.multiple_of(qi * tile, tile), tile)
        k_rows = pl.ds(pl.multiple_of(ki * tile, tile), tile)

        def one_head(h, carry):
            q_aug = jnp.concatenate([q_ref[:, head_cols(h)], qtail_ref[h, q_rows, :]], axis=1)
            k_aug = jnp.concatenate([k_ref[:, head_cols(h)], ktail_ref[h, k_rows, :]], axis=1)
            st = _dot_nt(k_aug, q_aug)
            if masked:
                kpos = lax.broadcasted_iota(jnp.int32, (tile, tile), 0)
                qpos = lax.broadcasted_iota(jnp.int32, (tile, tile), 1)
                st = jnp.where(kpos <= qpos, st, NEG_INF)
            m_old = m_ref[h]
            m_new = jnp.maximum(m_old, jnp.max(st, axis=0, keepdims=True))
            alpha = jnp.exp2((m_old - m_new) * exp2_scale)
            p = jnp.exp2((st - m_new) * exp2_scale)
            l_ref[h] = alpha * l_ref[h] + jnp.sum(p, axis=0, keepdims=True)
            acc_ref[h] = alpha * acc_ref[h] + _dot(vt_ref[head_cols(h), :], p.astype(BF16))
            m_ref[h] = m_new
            return carry

        lax.fori_loop(0, n_heads, one_head, 0)

    @pl.when(ki < qi)
    def _():
        block(False)

    @pl.when(ki == qi)
    def _():
        block(True)

    @pl.when(ki == pl.num_programs(2) - 1)
    def _():
        def finish(h, carry):
            o = (acc_ref[h] / l_ref[h]).T
            o_ref[:, head_cols(h)] = _rms(o, g_ref[pl.ds(h, 1), :]).astype(o_ref.dtype)
            return carry

        lax.fori_loop(0, n_heads, finish, 0)


def _fox_prompt(q, kb, vt, cg, g_out, batch, seq, n_heads):
    n, wa = q.shape
    tile = _pick_tile(seq, 512)
    nt = seq // tile

    def kv_blk(qi, ki):
        return jnp.minimum(ki, qi)

    return pl.pallas_call(
        functools.partial(_fox_prompt_body, tile=tile, scale=HEAD_DIM ** -0.5, n_heads=n_heads),
        out_shape=jax.ShapeDtypeStruct((n, wa), BF16),
        grid=(batch, nt, nt),
        in_specs=[
            pl.BlockSpec((tile, wa), lambda b, qi, ki: (b * nt + qi, 0)),
            pl.BlockSpec((tile, wa), lambda b, qi, ki: (b * nt + kv_blk(qi, ki), 0)),
            pl.BlockSpec((wa, tile), lambda b, qi, ki: (0, b * nt + kv_blk(qi, ki))),
            pl.BlockSpec((seq, LANES), lambda b, qi, ki: (b, 0)),
            pl.BlockSpec((n_heads, HEAD_DIM), lambda b, qi, ki: (0, 0)),
        ],
        out_specs=pl.BlockSpec((tile, wa), lambda b, qi, ki: (b * nt + qi, 0)),
        scratch_shapes=[
            pltpu.VMEM((n_heads, seq, LANES), BF16),
            pltpu.VMEM((n_heads, seq, LANES), BF16),
            pltpu.VMEM((n_heads, 1, tile), F32),
            pltpu.VMEM((n_heads, 1, tile), F32),
            pltpu.VMEM((n_heads, HEAD_DIM, tile), F32),
        ],
        compiler_params=_cparams(("parallel", "arbitrary", "arbitrary")),
        name="fox_prompt",
    )(q, kb, vt, cg, g_out)


def _pool_cumsum_body(x_ref, o_ref):
    page = x_ref.shape[1]
    r = lax.broadcasted_iota(jnp.int32, (page, page), 0)
    c = lax.broadcasted_iota(jnp.int32, (page, page), 1)
    tri = jnp.where(r <= c, 1.0, 0.0).astype(BF16)
    hi, mid, lo = _split3(x_ref[...])
    o_ref[...] = _dot(hi, tri) + _dot(mid, tri) + _dot(lo, tri)


def _pool_cumsum(lf):
    n_rows, page = lf.shape
    rows = _pick_tile(n_rows, 2048)
    blk = pl.BlockSpec((rows, page), lambda i: (i, 0))
    return pl.pallas_call(
        _pool_cumsum_body,
        out_shape=jax.ShapeDtypeStruct((n_rows, page), F32),
        grid=(n_rows // rows,),
        in_specs=[blk],
        out_specs=blk,
        compiler_params=_cparams(("parallel",)),
        name="pool_cumsum",
    )(lf)


def _fox_sample_body(pt_ref, q_ref, kn_ref, vn_ref, cl_ref, g_ref, *rest,
                     npg, n_pages, page, n_heads, t_new, scale):
    k_refs = rest[0:npg]
    c_refs = rest[npg:2 * npg]
    v_refs = rest[2 * npg:3 * npg]
    o_ref = rest[3 * npg]
    qx_ref, qxb_ref, pad_ref, cpad_ref, s2_ref, carry_ref, acc_ref = rest[3 * npg + 1:]
    del pt_ref
    b_id = pl.program_id(0)
    s_id = pl.program_id(1)
    n_seq = pl.num_programs(0) - 1
    n_steps = n_pages // npg
    past = n_pages * page
    wa = n_heads * HEAD_DIM
    s_ref = s2_ref.at[b_id % 2]
    p_ref = s2_ref.at[(b_id + 1) % 2]
    k_phase = b_id < n_seq
    v_phase = b_id >= 1
    row_h = lax.broadcasted_iota(jnp.int32, (n_heads, wa), 0)
    col_h = lax.broadcasted_iota(jnp.int32, (n_heads, wa), 1) >> _log2(HEAD_DIM)
    head_diag = row_h == col_h

    def expand(c):
        out = c
        for t in range(1, t_new):
            out = out + pltpu.roll(c, t * n_heads, axis=1)
        return out

    @pl.when(k_phase & (s_id == 0))
    def _():
        qx_ref[...] = jnp.zeros_like(qx_ref)
        q = q_ref[...]
        for t in range(t_new):
            qx_ref[t * n_heads:(t + 1) * n_heads, :] = jnp.where(head_diag, q[t:t + 1, :], 0.0)
        qxb_ref[...] = qx_ref[...].astype(BF16)
        cpad_ref[...] = jnp.zeros_like(cpad_ref)
        carry_ref[...] = jnp.zeros_like(carry_ref)

    @pl.when(v_phase & (s_id == 0))
    def _():
        acc_ref[...] = jnp.zeros_like(acc_ref)

    def head_major(ref):
        return jnp.concatenate([ref[pl.ds(h, page, stride=n_heads), :] for h in range(n_heads)], axis=1)

    @pl.when(k_phase)
    def _():
        for r in range(npg):
            st = _dot_nt(head_major(k_refs[r]).astype(BF16), qxb_ref[...]) * scale
            cpad_ref[:, 0:n_heads] = c_refs[r][...].T
            cx = expand(cpad_ref[...]) + carry_ref[...]
            carry_ref[...] = cx[page - 1:page, :]
            off = pl.multiple_of((s_id * npg + r) * page, page)
            s_ref[pl.ds(off, page), :] = st - cx

    @pl.when(v_phase)
    def _():
        for r in range(npg):
            off = pl.multiple_of((s_id * npg + r) * page, page)
            p = p_ref[pl.ds(off, page), :].T.astype(BF16)
            acc_ref[...] += _dot(p, head_major(v_refs[r]).astype(BF16))

    @pl.when(v_phase & (s_id == n_steps - 1))
    def _():
        pad_ref[...] = jnp.zeros_like(pad_ref)
        pad_ref[0:t_new, :] = vn_ref[...]
        p = p_ref[past:past + page, :].T.astype(BF16)
        acc = acc_ref[...] + _dot(p, pad_ref[...].astype(BF16))
        for t in range(t_new):
            blk = acc[t * n_heads:(t + 1) * n_heads, :]
            o_t = jnp.sum(jnp.where(head_diag, blk, 0.0), axis=0, keepdims=True)
            for h in range(n_heads):
                sl = slice(h * HEAD_DIM, (h + 1) * HEAD_DIM)
                o_ref[t:t + 1, sl] = _rms(o_t[:, sl], g_ref[h:h + 1, :])

    @pl.when(k_phase & (s_id == n_steps - 1))
    def _():
        pad_ref[...] = jnp.zeros_like(pad_ref)
        pad_ref[0:t_new, :] = kn_ref[...]
        st = _dot_nt(pad_ref[...].astype(BF16), qxb_ref[...]) * scale
        lane = lax.broadcasted_iota(jnp.int32, (1, LANES), 1)
        cpad_ref[...] = jnp.zeros_like(cpad_ref)
        cpad_ref[0:t_new, :] = jnp.where(lane < n_heads, cl_ref[...], 0.0)
        cl_x = expand(cpad_ref[...])
        u = lax.broadcasted_iota(jnp.int32, (page, LANES), 0)
        t = lax.broadcasted_iota(jnp.int32, (page, LANES), 1) >> _log2(n_heads)
        st = jnp.where((u < t_new) & (u <= t), st - cl_x, NEG_INF)
        s_past = s_ref[0:past, :] + carry_ref[...]
        m = jnp.maximum(jnp.max(s_past, axis=0, keepdims=True), jnp.max(st, axis=0, keepdims=True))
        p_past = jnp.exp(s_past - m)
        p_new = jnp.exp(st - m)
        inv = 1.0 / (jnp.sum(p_past, axis=0, keepdims=True) + jnp.sum(p_new, axis=0, keepdims=True))
        s_ref[0:past, :] = p_past * inv
        s_ref[past:past + page, :] = p_new * inv


def _fox_sample(page_table, q, k_new, v_new, cl, g_out, cache_k, cache_v, cache_c, page, pool_base, n_heads):
    bsz, t_new, wa = q.shape
    n_pages = page_table.shape[1]
    assert page == LANES and n_heads == SUBLANES and n_heads * t_new <= LANES
    page_bytes = page * wa * 4
    npg = _pick_tile(n_pages, max(1, PAGE_BUFFER_BYTES // (4 * page_bytes)))
    n_steps = n_pages // npg
    scale = HEAD_DIM ** -0.5

    def k_seq(b):
        return jnp.minimum(b, bsz - 1)

    def v_seq(b):
        return jnp.maximum(b - 1, 0)

    def kv_map(seq_fn, r):
        return lambda b, s, pt: (pool_base + pt[seq_fn(b), s * npg + r], 0)

    def c_map(r):
        return lambda b, s, pt: (pt[k_seq(b), s * npg + r], 0, 0)

    k3 = lambda b, s, pt: (k_seq(b), 0, 0)
    v3 = lambda b, s, pt: (v_seq(b), 0, 0)
    in_specs = [
        pl.BlockSpec((None, t_new, wa), k3),
        pl.BlockSpec((None, t_new, wa), k3),
        pl.BlockSpec((None, t_new, wa), v3),
        pl.BlockSpec((None, t_new, LANES), k3),
        pl.BlockSpec((n_heads, HEAD_DIM), lambda b, s, pt: (0, 0)),
    ]
    in_specs += [pl.BlockSpec((page * n_heads, HEAD_DIM), kv_map(k_seq, r)) for r in range(npg)]
    in_specs += [pl.BlockSpec((None, n_heads, page), c_map(r)) for r in range(npg)]
    in_specs += [pl.BlockSpec((page * n_heads, HEAD_DIM), kv_map(v_seq, r)) for r in range(npg)]
    grid_spec = pltpu.PrefetchScalarGridSpec(
        num_scalar_prefetch=1,
        grid=(bsz + 1, n_steps),
        in_specs=in_specs,
        out_specs=pl.BlockSpec((None, t_new, wa), v3),
        scratch_shapes=[
            pltpu.VMEM((LANES, wa), F32),
            pltpu.VMEM((LANES, wa), BF16),
            pltpu.VMEM((page, wa), F32),
            pltpu.VMEM((page, LANES), F32),
            pltpu.VMEM((2, n_pages * page + page, LANES), F32),
            pltpu.VMEM((1, LANES), F32),
            pltpu.VMEM((LANES, wa), F32),
        ],
    )
    return pl.pallas_call(
        functools.partial(_fox_sample_body, npg=npg, n_pages=n_pages, page=page, n_heads=n_heads,
                          t_new=t_new, scale=scale),
        out_shape=jax.ShapeDtypeStruct((bsz, t_new, wa), F32),
        grid_spec=grid_spec,
        compiler_params=_cparams(("arbitrary", "arbitrary")),
        name="fox_sample",
    )(page_table, q, k_new, v_new, cl, g_out, *([cache_k] * npg), *([cache_c] * npg), *([cache_v] * npg))


def _bdot(a, b):
    return lax.dot_general(a, b, (((2,), (1,)), ((0,), (0,))), preferred_element_type=F32)


def _bdot_nt(a, b):
    return lax.dot_general(a, b, (((2,), (2,)), ((0,), (0,))), preferred_element_type=F32)


def _mlstm_chunk(qc, kc, vc, i_col, b_col, c_st, n_st, m_st, n_valid):
    nh, ln, _ = qc.shape
    r = lax.broadcasted_iota(jnp.int32, (ln, ln), 0)
    c = lax.broadcasted_iota(jnp.int32, (ln, ln), 1)
    eye = (r == c)[None]
    tril = (r >= c)[None]

    def col_to_row(col):
        return jnp.sum(jnp.where(eye, col, 0.0), axis=1, keepdims=True)

    i_row = col_to_row(i_col)
    b_row = col_to_row(b_col)
    a_col = b_col + m_st
    dmat = jnp.where(tril, b_col - b_row + i_row, NEG_INF)
    mt = jnp.maximum(a_col, jnp.max(dmat, axis=2, keepdims=True))
    qb = qc.astype(BF16)
    kb = kc.astype(BF16)
    vb = vc.astype(BF16)
    w_intra = jnp.exp(dmat - mt) * _bdot_nt(qb, kb)
    w_inter = jnp.exp(a_col - mt)
    num = w_inter * _bdot(qb, c_st.astype(BF16)) + _bdot(w_intra.astype(BF16), vb)
    den = w_inter * jnp.sum(qc * n_st, axis=2, keepdims=True) + jnp.sum(w_intra, axis=2, keepdims=True)
    h = num / jnp.maximum(jnp.abs(den), jnp.exp(-mt))
    last = n_valid - 1
    m_new = mt[:, last:last + 1, :]
    b_last = b_col[:, last:last + 1, :]
    w_state = jnp.exp(b_last - b_col + i_col - m_new)
    if n_valid < ln:
        row = lax.broadcasted_iota(jnp.int32, (1, ln, 1), 1)
        w_state = jnp.where(row < n_valid, w_state, 0.0)
    decay = jnp.exp(b_last + m_st - m_new)
    ks = w_state * kc
    ks_t = jnp.stack([ks[hd].T for hd in range(nh)]).astype(BF16)
    c_new = decay * c_st + _bdot(ks_t, vb)
    n_new = decay * n_st + jnp.sum(ks, axis=1, keepdims=True)
    return h, c_new, n_new, m_new


def _heads(x, n_heads, offset=0, width=HEAD_DIM):
    return jnp.stack([x[:, offset + h * width:offset + (h + 1) * width] for h in range(n_heads)])


def _mlstm_prompt_body(qk_ref, w_ref, b_ref, v_ref, o_ref, act_ref, cl_ref, g_ref,
                       y_ref, c_out_ref, n_out_ref, m_out_ref,
                       ext_ref, qk_scr, c_ref, n_ref, m_ref, *, rb, chunk, n_heads):
    blk = pl.program_id(1)
    pad = SUBLANES
    wm = n_heads * HEAD_DIM
    kscale = HEAD_DIM ** -0.5

    @pl.when(blk == 0)
    def _():
        ext_ref[0:pad, :] = jnp.zeros((pad, 2 * wm), F32)
        c_ref[...] = jnp.zeros_like(c_ref)
        n_ref[...] = jnp.zeros_like(n_ref)
        m_ref[...] = jnp.zeros_like(m_ref)

    ext_ref[pad:pad + rb, :] = qk_ref[...]
    acc = b_ref[...] + w_ref[CONV_W - 1:CONV_W, :] * qk_ref[...]
    for j in range(CONV_W - 1):
        off = pad - (CONV_W - 1) + j
        acc = acc + w_ref[j:j + 1, :] * ext_ref[off:off + rb, :]
    qk_scr[...] = acc * jax.nn.sigmoid(acc)
    ext_ref[0:pad, :] = qk_ref[rb - pad:rb, :]

    g = g_ref[...][:, None, :]
    for ci in range(rb // chunk):
        rows = slice(ci * chunk, (ci + 1) * chunk)
        qk = qk_scr[rows, :]
        hh, c_new, n_new, m_new = _mlstm_chunk(
            _heads(qk, n_heads), _heads(qk, n_heads, wm) * kscale, _heads(v_ref[rows, :], n_heads),
            _heads(act_ref[rows, :], n_heads, n_heads, 1), _heads(cl_ref[rows, :], n_heads, 2 * n_heads, 1),
            c_ref[...], n_ref[...], m_ref[:, :, 0:1], chunk)
        c_ref[...] = c_new
        n_ref[...] = n_new
        m_ref[...] = jnp.broadcast_to(m_new, m_ref.shape)
        y = jax.nn.sigmoid(_heads(o_ref[rows, :], n_heads)) * _rms(hh, g)
        for h in range(n_heads):
            y_ref[rows, h * HEAD_DIM:(h + 1) * HEAD_DIM] = y[h].astype(y_ref.dtype)

    @pl.when(blk == pl.num_programs(1) - 1)
    def _():
        c_out_ref[...] = c_ref[...]
        n_out_ref[...] = n_ref[:, 0, :]
        m_out_ref[...] = m_ref[:, 0, :]


def _mlstm_prompt(qkm, conv_w, conv_b, vm, om, act, cl, g_out, batch, seq, n_heads):
    n = qkm.shape[0]
    wm = n_heads * HEAD_DIM
    chunk = math.gcd(seq, MLSTM_CHUNK)
    rb = _pick_tile(seq, 4 * chunk)
    assert rb % chunk == 0 and rb >= SUBLANES
    nblk = seq // rb
    row = lambda b, i: (b * nblk + i, 0)
    const = lambda b, i: (0, 0)
    st = lambda b, i: (b, 0, 0)
    out_shape = (
        jax.ShapeDtypeStruct((n, wm), BF16),
        jax.ShapeDtypeStruct((batch, n_heads, HEAD_DIM, HEAD_DIM), F32),
        jax.ShapeDtypeStruct((batch, n_heads, HEAD_DIM), F32),
        jax.ShapeDtypeStruct((batch, n_heads, LANES), F32),
    )
    return pl.pallas_call(
        functools.partial(_mlstm_prompt_body, rb=rb, chunk=chunk, n_heads=n_heads),
        out_shape=out_shape,
        grid=(batch, nblk),
        in_specs=[
            pl.BlockSpec((rb, 2 * wm), row),
            pl.BlockSpec((CONV_W, 2 * wm), const),
            pl.BlockSpec((1, 2 * wm), const),
            pl.BlockSpec((rb, wm), row),
            pl.BlockSpec((rb, wm), row),
            pl.BlockSpec((rb, LANES), row),
            pl.BlockSpec((rb, LANES), row),
            pl.BlockSpec((n_heads, HEAD_DIM), const),
        ],
        out_specs=(
            pl.BlockSpec((rb, wm), row),
            pl.BlockSpec((None, n_heads, HEAD_DIM, HEAD_DIM), lambda b, i: (b, 0, 0, 0)),
            pl.BlockSpec((None, n_heads, HEAD_DIM), st),
            pl.BlockSpec((None, n_heads, LANES), st),
        ),
        scratch_shapes=[
            pltpu.VMEM((rb + SUBLANES, 2 * wm), F32),
            pltpu.VMEM((rb, 2 * wm), F32),
            pltpu.VMEM((n_heads, HEAD_DIM, HEAD_DIM), F32),
            pltpu.VMEM((n_heads, 1, HEAD_DIM), F32),
            pltpu.VMEM((n_heads, 1, LANES), F32),
        ],
        compiler_params=_cparams(("parallel", "arbitrary")),
        name="mlstm_prompt",
    )(qkm, conv_w, conv_b, vm, om, act, cl, g_out)


def _mlstm_sample_body(qk_ref, prev_ref, w_ref, b_ref, v_ref, o_ref, act_ref, cl_ref, g_ref, c0_ref, n0_ref, m0_ref,
                       y_ref, c_out_ref, n_out_ref, m_out_ref, ext_ref, pad_ref, *, t_new, n_heads, sb):
    wm = n_heads * HEAD_DIM
    rows = SUBLANES
    kscale = HEAD_DIM ** -0.5
    valid = lax.broadcasted_iota(jnp.int32, (rows, 1), 0) < t_new

    def padded(val, width):
        pad_ref[:, 0:width] = jnp.zeros((rows, width), F32)
        pad_ref[0:t_new, 0:width] = val
        return pad_ref[:, 0:width]

    qs, ks, vs, os_, i_cols, b_cols, n0s, m0s = [], [], [], [], [], [], [], []
    for s in range(sb):
        tok = slice(s * t_new, (s + 1) * t_new)
        ext_ref[...] = jnp.zeros_like(ext_ref)
        ext_ref[0:CONV_W - 1, :] = prev_ref[s]
        ext_ref[CONV_W - 1:CONV_W - 1 + t_new, :] = qk_ref[tok, :]
        acc = b_ref[...] + w_ref[0:1, :] * ext_ref[0:rows, :]
        for j in range(1, CONV_W):
            acc = acc + w_ref[j:j + 1, :] * ext_ref[j:j + rows, :]
        qk = jnp.where(valid, acc * jax.nn.sigmoid(acc), 0.0)
        qs.append(_heads(qk, n_heads))
        ks.append(_heads(qk, n_heads, wm) * kscale)
        vs.append(_heads(padded(v_ref[tok, :], wm), n_heads))
        os_.append(_heads(padded(o_ref[tok, :], wm), n_heads))
        i_cols.append(jnp.where(valid, _heads(padded(act_ref[tok, :], LANES), n_heads, n_heads, 1), NEG_INF))
        b_col = _heads(padded(cl_ref[tok, :], LANES), n_heads, 2 * n_heads, 1)
        b_cols.append(jnp.where(valid, b_col, b_col[:, t_new - 1:t_new, :]))
        n0s.append(jnp.stack([n0_ref[s, h:h + 1, :] for h in range(n_heads)]))
        m0s.append(jnp.stack([m0_ref[s, h:h + 1, :] for h in range(n_heads)]))

    cat = lambda parts: jnp.concatenate(parts, axis=0)
    c0 = cat([c0_ref[s] for s in range(sb)])
    hh, c_new, n_new, m_new = _mlstm_chunk(cat(qs), cat(ks), cat(vs), cat(i_cols), cat(b_cols),
                                           c0, cat(n0s), cat(m0s), t_new)
    g = jnp.concatenate([g_ref[...]] * sb, axis=0)[:, None, :]
    y = jax.nn.sigmoid(cat(os_)) * _rms(hh, g)
    m_wide = jnp.broadcast_to(m_new, (sb * n_heads, 1, LANES))
    for s in range(sb):
        for h in range(n_heads):
            y_ref[s * t_new:(s + 1) * t_new, h * HEAD_DIM:(h + 1) * HEAD_DIM] = y[s * n_heads + h, 0:t_new, :]
        grp = slice(s * n_heads, (s + 1) * n_heads)
        c_out_ref[s] = c_new[grp]
        n_out_ref[s] = n_new[grp, 0, :]
        m_out_ref[s] = m_wide[grp, 0, :]


def _mlstm_sample(qkm, conv_prev, conv_w, conv_b, vm, om, act, cl, g_out, c0, n0, m0, t_new, n_heads):
    n, wm2 = qkm.shape
    bsz = n // t_new
    wm = wm2 // 2
    assert t_new + CONV_W - 1 <= SUBLANES
    sb = _pick_tile(bsz, 4)
    assert (sb * t_new) % SUBLANES == 0 or sb == bsz
    tok = lambda b: (b, 0)
    seq3 = lambda b: (b, 0, 0)
    const = lambda b: (0, 0)
    out_shape = (
        jax.ShapeDtypeStruct((n, wm), F32),
        jax.ShapeDtypeStruct((bsz, n_heads, HEAD_DIM, HEAD_DIM), F32),
        jax.ShapeDtypeStruct((bsz, n_heads, HEAD_DIM), F32),
        jax.ShapeDtypeStruct((bsz, n_heads, LANES), F32),
    )
    return pl.pallas_call(
        functools.partial(_mlstm_sample_body, t_new=t_new, n_heads=n_heads, sb=sb),
        out_shape=out_shape,
        grid=(bsz // sb,),
        in_specs=[
            pl.BlockSpec((sb * t_new, wm2), tok),
            pl.BlockSpec((sb, CONV_W - 1, wm2), seq3),
            pl.BlockSpec((CONV_W, wm2), const),
            pl.BlockSpec((1, wm2), const),
            pl.BlockSpec((sb * t_new, wm), tok),
            pl.BlockSpec((sb * t_new, wm), tok),
            pl.BlockSpec((sb * t_new, LANES), tok),
            pl.BlockSpec((sb * t_new, LANES), tok),
            pl.BlockSpec((n_heads, HEAD_DIM), const),
            pl.BlockSpec((sb, n_heads, HEAD_DIM, HEAD_DIM), lambda b: (b, 0, 0, 0)),
            pl.BlockSpec((sb, n_heads, HEAD_DIM), seq3),
            pl.BlockSpec((sb, n_heads, 1), seq3),
        ],
        out_specs=(
            pl.BlockSpec((sb * t_new, wm), tok),
            pl.BlockSpec((sb, n_heads, HEAD_DIM, HEAD_DIM), lambda b: (b, 0, 0, 0)),
            pl.BlockSpec((sb, n_heads, HEAD_DIM), seq3),
            pl.BlockSpec((sb, n_heads, LANES), seq3),
        ),
        scratch_shapes=[pltpu.VMEM((2 * SUBLANES, wm2), F32), pltpu.VMEM((SUBLANES, wm), F32)],
        compiler_params=_cparams(("parallel",)),
        name="mlstm_sample",
    )(qkm, conv_prev, conv_w, conv_b, vm, om, act, cl, g_out, c0, n0, m0)


def _outproj_body(ya_ref, ym_ref, w_ref, x_ref, o_ref, *, wa):
    y = _dot(ya_ref[...].astype(BF16), w_ref[0:wa, :]) + _dot(ym_ref[...].astype(BF16), w_ref[wa:, :])
    o_ref[...] = x_ref[...] + y


def _outproj(ya, ym, w, x):
    n, d = x.shape
    wa = ya.shape[1]
    wm = ym.shape[1]
    tm = _pick_tile(n, 512)
    row = lambda i: (i, 0)
    return pl.pallas_call(
        functools.partial(_outproj_body, wa=wa),
        out_shape=jax.ShapeDtypeStruct((n, d), F32),
        grid=(n // tm,),
        in_specs=[
            pl.BlockSpec((tm, wa), row),
            pl.BlockSpec((tm, wm), row),
            pl.BlockSpec((wa + wm, d), lambda i: (0, 0)),
            pl.BlockSpec((tm, d), row),
        ],
        out_specs=pl.BlockSpec((tm, d), row),
        compiler_params=_cparams(("parallel",)),
        name="outproj",
    )(ya, ym, w, x)


def _cast_pad_body(x_ref, o_ref, *, axis, size):
    if axis == 1:
        o_ref[:, 0:size] = x_ref[...].astype(BF16)
        o_ref[:, size:] = jnp.zeros((o_ref.shape[0], o_ref.shape[1] - size), BF16)
    else:
        o_ref[0:size, :] = x_ref[...].astype(BF16)
        o_ref[size:, :] = jnp.zeros((o_ref.shape[0] - size, o_ref.shape[1]), BF16)


def _cast_pad(w, axis, padded):
    r, c = w.shape
    size = w.shape[axis]
    assert padded > size
    if axis == 1:
        t = _pick_tile(r, 256)
        grid, in_blk, out_blk, imap = (r // t,), (t, c), (t, padded), (lambda i: (i, 0))
        out_shape = (r, padded)
    else:
        t = _pick_tile(c, 256)
        grid, in_blk, out_blk, imap = (c // t,), (r, t), (padded, t), (lambda i: (0, i))
        out_shape = (padded, c)
    return pl.pallas_call(
        functools.partial(_cast_pad_body, axis=axis, size=size),
        out_shape=jax.ShapeDtypeStruct(out_shape, BF16),
        grid=grid,
        in_specs=[pl.BlockSpec(in_blk, imap)],
        out_specs=pl.BlockSpec(out_blk, imap),
        compiler_params=_cparams(("parallel",)),
        name="cast_pad",
    )(w)


def _win_prep_body(wt_ref, fa_ref, im_ref, main_ref, gates_ref, *, n_gate_cols):
    main_ref[...] = wt_ref[...].T.astype(BF16)

    @pl.when(pl.program_id(0) == 0)
    def _():
        d = gates_ref.shape[0]
        g = jnp.concatenate([fa_ref[...], im_ref[...], jnp.zeros((LANES - n_gate_cols, d), F32)], axis=0)
        gates_ref[...] = g.T.astype(BF16)


def _win_prep(w_in_t, cuts, n_gate_cols, tile):
    n_in, d = w_in_t.shape
    o_f, o_qk, o_i = cuts
    n_main = o_f + (o_i - o_qk)
    assert o_f % tile == 0 and (o_i - o_qk) % tile == 0 and o_qk % SUBLANES == 0 and o_i % SUBLANES == 0
    n_lead = o_f // tile

    def row_off(j):
        return pl.multiple_of(jnp.where(j < n_lead, j * tile, o_qk + (j - n_lead) * tile), SUBLANES)

    el = pl.Element
    return pl.pallas_call(
        functools.partial(_win_prep_body, n_gate_cols=n_gate_cols),
        out_shape=(jax.ShapeDtypeStruct((d, n_main), BF16), jax.ShapeDtypeStruct((d, LANES), BF16)),
        grid=(n_main // tile,),
        in_specs=[
            pl.BlockSpec((el(tile), el(d)), lambda j: (row_off(j), 0)),
            pl.BlockSpec((el(o_qk - o_f), el(d)), lambda j: (o_f, 0)),
            pl.BlockSpec((el(n_in - o_i), el(d)), lambda j: (o_i, 0)),
        ],
        out_specs=(pl.BlockSpec((d, tile), lambda j: (0, j)), pl.BlockSpec((d, LANES), lambda j: (0, 0))),
        compiler_params=_cparams(("arbitrary",)),
        name="win_prep",
    )(w_in_t, w_in_t, w_in_t)


def _prep_weights(lw, n_heads_a, n_heads_m):
    wa = n_heads_a * HEAD_DIM
    wm = n_heads_m * HEAD_DIM
    f = lw['w1_gate'].shape[1]
    tf = 512
    fp = tf * ((f + tf - 1) // tf)

    def ffn_w(wg, wu, wd):
        if fp == f:
            return wg.astype(BF16), wu.astype(BF16), wd.astype(BF16)
        return _cast_pad(wg, 1, fp), _cast_pad(wu, 1, fp), _cast_pad(wd, 0, fp)

    w_in = lw['w_in']
    o_f = 3 * wa
    o_qk = o_f + n_heads_a
    o_i = o_qk + 2 * wm + 2 * wm
    o_fm = o_i + n_heads_m
    ng = 2 * n_heads_m + n_heads_a
    assert o_fm + n_heads_m == w_in.shape[1]
    w_main, w_gates = _win_prep(jnp.swapaxes(w_in, 0, 1), (o_f, o_qk, o_i), ng, wa)
    bias = jnp.concatenate([lw['b_fox_f'], lw['b_m_i'], lw['b_m_f'], jnp.zeros((LANES - ng,), F32)])[None, :]
    return {
        'ffn1': ffn_w(lw['w1_gate'], lw['w1_up'], lw['w1_down']),
        'ffn_w': ffn_w,
        'tf': tf,
        'fp': fp,
        'w_main': w_main,
        'w_gates': w_gates,
        'gate_bias': bias,
        'w_out': lw['w_out'].astype(BF16),
    }


def kernel(x_prompt, x_sample, cache_k, cache_v, cache_logf, state_conv, state_C, state_n, state_m, page_table,
           g_ffn1, w1_gate, w1_up, w1_down, g_mix, w_in, b_fox_f, b_m_i, b_m_f, conv_w, conv_b, g_q, g_k,
           g_out_a, g_out_m, w_out, g_ffn2, w2_gate, w2_up, w2_down):
    depth = w_in.shape[0]
    bp, seq, d = x_prompt.shape
    bs, t_new, _ = x_sample.shape
    n_heads_a = g_out_a.shape[1]
    n_heads_m = g_out_m.shape[1]
    wa = n_heads_a * HEAD_DIM
    wm = n_heads_m * HEAD_DIM
    n_pool, page = cache_k.shape[1], cache_k.shape[2]

    yp = x_prompt.reshape(bp * seq, d)
    ys = x_sample.reshape(bs * t_new, d)
    outs = [[] for _ in range(14)]
    for l in range(depth):
        lw = {
            'w1_gate': w1_gate[l], 'w1_up': w1_up[l], 'w1_down': w1_down[l], 'w_in': w_in[l],
            'b_fox_f': b_fox_f[l], 'b_m_i': b_m_i[l], 'b_m_f': b_m_f[l], 'w_out': w_out[l],
            'w2_gate': w2_gate[l], 'w2_up': w2_up[l], 'w2_down': w2_down[l],
        }
        pw = _prep_weights(lw, n_heads_a, n_heads_m)
        gf1 = g_ffn1[l][None, :]
        gf2 = g_ffn2[l][None, :]
        gmix = g_mix[l][None, :]
        gq = g_q[l][None, :]
        gk = g_k[l][None, :]
        cw = conv_w[l]
        cb = conv_b[l][None, :]
        goa = g_out_a[l]
        gom = g_out_m[l]

        w2 = (lw['w2_gate'], lw['w2_up'], lw['w2_down'])
        ffn1_steps = (yp.shape[0] // _pick_tile(yp.shape[0], 512)) * (pw['fp'] // pw['tf'])
        if w2[0].shape[1] % LANES == 0 and ffn1_steps >= 3 * (pw['fp'] // LANES):
            (x1, xn), ffn2_w = _ffn(yp, gf1, *pw['ffn1'], pw['tf'], g_next=gmix, prep=w2)
        else:
            ffn2_w = pw['ffn_w'](*w2)
            x1, xn = _ffn(yp, gf1, *pw['ffn1'], pw['tf'], g_next=gmix)
        q, k, kb, v, vt, qkm, vm, om, gt = _inproj(xn, pw['w_main'], pw['w_gates'], gq, gk, n_heads_a)
        act, cg, cl = _gates(gt, pw['gate_bias'], seq, math.gcd(seq, MLSTM_CHUNK), n_heads_a)
        ya = _fox_prompt(q, kb, vt, cg, goa, bp, seq, n_heads_a)
        ym, c_p, n_p, m_p = _mlstm_prompt(qkm, cw, cb, vm, om, act, cl, gom, bp, seq, n_heads_m)
        x2 = _outproj(ya, ym, pw['w_out'], x1)
        yp = _ffn(x2, gf2, *ffn2_w, pw['tf'])
        outs[0].append(k.reshape(bp, seq, n_heads_a, HEAD_DIM))
        outs[1].append(v.reshape(bp, seq, n_heads_a, HEAD_DIM))
        outs[2].append(act[:, :n_heads_a].reshape(bp, seq, n_heads_a))
        outs[3].append(qkm.reshape(bp, seq, 2 * wm)[:, seq - (CONV_W - 1):, :])
        outs[4].append(c_p)
        outs[5].append(n_p)
        outs[6].append(m_p[:, :, 0])

        x1, xn = _ffn(ys, gf1, *pw['ffn1'], pw['tf'], g_next=gmix)
        q, k, kb, v, vt, qkm, vm, om, gt = _inproj(xn, pw['w_main'], pw['w_gates'], gq, gk, n_heads_a)
        act, cg, cl = _gates(gt, pw['gate_bias'], t_new, math.gcd(t_new, MLSTM_CHUNK), n_heads_a)
        r3 = lambda a: a.reshape(bs, t_new, a.shape[-1])
        lf_rows = jnp.swapaxes(cache_logf[l], 1, 2).reshape(n_pool * n_heads_a, page)
        cache_c = _pool_cumsum(lf_rows).reshape(n_pool, n_heads_a, page)
        ya = _fox_sample(page_table, r3(q).astype(F32), r3(k), r3(v), r3(cg), goa,
                         cache_k.reshape(-1, HEAD_DIM), cache_v.reshape(-1, HEAD_DIM),
                         cache_c, page, l * n_pool, n_heads_a)
        ym, c_s, n_s, m_s = _mlstm_sample(qkm, state_conv[l], cw, cb, vm, om, act, cl, gom,
                                          state_C[l], state_n[l], state_m[l][:, :, None], t_new, n_heads_m)
        x2 = _outproj(ya.reshape(bs * t_new, wa), ym, pw['w_out'], x1)
        ys = _ffn(x2, gf2, *ffn2_w, pw['tf'])
        conv_ext = jnp.concatenate([state_conv[l], r3(qkm)], axis=1)
        outs[7].append(k.reshape(bs, t_new, n_heads_a, HEAD_DIM))
        outs[8].append(v.reshape(bs, t_new, n_heads_a, HEAD_DIM))
        outs[9].append(act[:, :n_heads_a].reshape(bs, t_new, n_heads_a))
        outs[10].append(conv_ext[:, t_new:, :])
        outs[11].append(c_s)
        outs[12].append(n_s)
        outs[13].append(m_s[:, :, 0])

    return (yp.reshape(bp, seq, d), ys.reshape(bs, t_new, d)) + tuple(jnp.stack(o) for o in outs)
```

```python
import functools
import math

import jax
import jax.numpy as jnp
from jax import lax
from jax.experimental import pallas as pl
from jax.experimental.pallas import tpu as pltpu

F32 = jnp.float32
BF16 = jnp.bfloat16

EPS = 1e-6
HEAD_DIM = 128
CONV_W = 4
MLSTM_CHUNK = 128
LANES = 128
SUBLANES = 8
VMEM_LIMIT_BYTES = 56 * 1024 * 1024
PAGE_BUFFER_BYTES = 32 * 1024 * 1024
NEG_INF = float("-inf")
LOG2E = math.log2(math.e)


def _cparams(sem):
    return pltpu.CompilerParams(dimension_semantics=sem, vmem_limit_bytes=VMEM_LIMIT_BYTES)


def _rms(x, g):
    y = x * lax.rsqrt(jnp.mean(x * x, axis=-1, keepdims=True) + EPS)
    return y * g


def _log_sigmoid(x):
    return jnp.minimum(x, 0.0) - jnp.log1p(jnp.exp(-jnp.abs(x)))


def _dot(a, b, precision=None):
    return jnp.dot(a, b, preferred_element_type=F32, precision=precision)


def _dot_nt(a, b, precision=None):
    return lax.dot_general(a, b, (((1,), (1,)), ((), ())), preferred_element_type=F32, precision=precision)


def _pick_tile(n, pref):
    t = min(n, pref)
    while n % t:
        t //= 2
    return t


def _log2(n):
    k = int(math.log2(n))
    assert 1 << k == n
    return k


def _ffn_body(x_ref, g_ref, wg_ref, wu_ref, wd_ref, gn_ref, *rest, with_next, n_prep, prep_real, prep_slabs):
    prep_in = rest[:n_prep]
    rest = rest[n_prep:]
    if with_next:
        o_ref, on_ref = rest[:2]
        rest = rest[2:]
    else:
        o_ref = rest[0]
        rest = rest[1:]
    prep_out = rest[:n_prep]
    xn_ref, acc_ref = rest[n_prep:]
    j = pl.program_id(1)

    @pl.when(j == 0)
    def _():
        xn_ref[...] = _rms(x_ref[...], g_ref[...]).astype(BF16)
        acc_ref[...] = jnp.zeros_like(acc_ref)

    xn = xn_ref[...]
    half = wg_ref.shape[1] // 2
    part = None
    for c in range(2):
        cols = slice(c * half, (c + 1) * half)
        gate = _dot(xn, wg_ref[:, cols])
        up = _dot(xn, wu_ref[:, cols])
        h = (gate * jax.nn.sigmoid(gate)) * up
        d = _dot(h.astype(BF16), wd_ref[cols, :])
        part = d if part is None else part + d
    acc_ref[...] += part

    @pl.when(j == pl.num_programs(1) - 1)
    def _():
        y = x_ref[...] + 0.5 * acc_ref[...]
        o_ref[...] = y
        if with_next:
            on_ref[...] = _rms(y, gn_ref[...]).astype(BF16)

    t = pl.program_id(0) * pl.num_programs(1) + j
    for k in range(n_prep):
        t0 = k * prep_slabs

        @pl.when((t >= t0) & (t < t0 + prep_real))
        def _(k=k):
            prep_out[k][...] = prep_in[k][...].astype(BF16)

        @pl.when((t >= t0 + prep_real) & (t < t0 + prep_slabs))
        def _(k=k):
            prep_out[k][...] = jnp.zeros_like(prep_out[k])


def _ffn(x, g, wg, wu, wd, tf, g_next=None, prep=None):
    n, d = x.shape
    fp = wg.shape[1]
    tm = _pick_tile(n, 512)
    ni, nj = n // tm, fp // tf
    with_next = g_next is not None
    row = pl.BlockSpec((tm, d), lambda i, j: (i, 0))
    vec = pl.BlockSpec((1, d), lambda i, j: (0, 0))
    out_shape = [jax.ShapeDtypeStruct((n, d), F32)]
    out_specs = [row]
    if with_next:
        out_shape.append(jax.ShapeDtypeStruct((n, d), BF16))
        out_specs.append(row)
    in_specs = [
        row, vec,
        pl.BlockSpec((d, tf), lambda i, j: (0, j)),
        pl.BlockSpec((d, tf), lambda i, j: (0, j)),
        pl.BlockSpec((tf, d), lambda i, j: (j, 0)),
        vec,
    ]
    operands = [x, g, wg, wu, wd, g_next if with_next else g]
    n_prep = prep_real = prep_slabs = 0
    if prep is not None:
        f = prep[0].shape[1]
        assert f % LANES == 0 and fp % LANES == 0
        n_prep, prep_real, prep_slabs = 3, f // LANES, fp // LANES
        assert ni * nj >= n_prep * prep_slabs

        def slab(k, last):
            return lambda i, j: jnp.clip(i * nj + j - k * prep_slabs, 0, last)

        for k in range(n_prep):
            cols = k < 2
            blk = (d, LANES) if cols else (LANES, d)
            for last, specs in ((prep_real - 1, in_specs), (prep_slabs - 1, out_specs)):
                s = slab(k, last)
                specs.append(pl.BlockSpec(blk, (lambda i, j, s=s: (0, s(i, j))) if cols
                                          else (lambda i, j, s=s: (s(i, j), 0))))
            out_shape.append(jax.ShapeDtypeStruct((d, fp) if cols else (fp, d), BF16))
        operands += list(prep)
    res = pl.pallas_call(
        functools.partial(_ffn_body, with_next=with_next, n_prep=n_prep, prep_real=prep_real,
                          prep_slabs=prep_slabs),
        out_shape=tuple(out_shape),
        grid=(ni, nj),
        in_specs=in_specs,
        out_specs=tuple(out_specs),
        scratch_shapes=[pltpu.VMEM((tm, d), BF16), pltpu.VMEM((tm, d), F32)],
        compiler_params=_cparams(("arbitrary", "arbitrary") if n_prep else ("parallel", "arbitrary")),
        name="ffn",
    )(*operands)
    n_main = 2 if with_next else 1
    main = res[0] if n_main == 1 else tuple(res[:2])
    return (main, tuple(res[n_main:])) if prep is not None else main


def _inproj_body(xn_ref, w_ref, wgt_ref, gq_ref, gk_ref,
                 q_ref, k_ref, kb_ref, v_ref, vt_ref, qkm_ref, vm_ref, om_ref, gt_ref, *, n_heads):
    j = pl.program_id(0)
    xn = xn_ref[...]
    acc = _dot(xn, w_ref[...])

    @pl.when(j == 0)
    def _():
        for h in range(n_heads):
            sl = slice(h * HEAD_DIM, (h + 1) * HEAD_DIM)
            q_ref[:, sl] = _rms(acc[:, sl], gq_ref[...]).astype(BF16)
        gt_ref[...] = _dot(xn, wgt_ref[...])

    @pl.when(j == 1)
    def _():
        for h in range(n_heads):
            sl = slice(h * HEAD_DIM, (h + 1) * HEAD_DIM)
            kn = _rms(acc[:, sl], gk_ref[...])
            k_ref[:, sl] = kn
            kb_ref[:, sl] = kn.astype(BF16)

    @pl.when(j == 2)
    def _():
        v_ref[...] = acc
        vt_ref[...] = acc.T.astype(BF16)

    @pl.when((j == 3) | (j == 4))
    def _():
        qkm_ref[...] = acc

    @pl.when(j == 5)
    def _():
        vm_ref[...] = acc

    @pl.when(j == 6)
    def _():
        om_ref[...] = acc


def _inproj(xn, w_main, w_gates, gq, gk, n_heads):
    n, d = xn.shape
    wa = n_heads * HEAD_DIM
    assert w_main.shape[1] == 7 * wa
    tm = _pick_tile(n, 512)
    ni = n // tm
    const = lambda j, i: (0, 0)

    def rows_at(j, i, j0, j1):
        return jnp.where(j < j0, 0, jnp.where(j > j1, ni - 1, i))

    def row_map(j0, j1=None):
        j1 = j0 if j1 is None else j1
        return lambda j, i: (rows_at(j, i, j0, j1), 0)

    out_shape = (
        jax.ShapeDtypeStruct((n, wa), BF16),
        jax.ShapeDtypeStruct((n, wa), F32),
        jax.ShapeDtypeStruct((n, wa), BF16),
        jax.ShapeDtypeStruct((n, wa), F32),
        jax.ShapeDtypeStruct((wa, n), BF16),
        jax.ShapeDtypeStruct((n, 2 * wa), F32),
        jax.ShapeDtypeStruct((n, wa), F32),
        jax.ShapeDtypeStruct((n, wa), F32),
        jax.ShapeDtypeStruct((n, LANES), F32),
    )
    out_specs = (
        pl.BlockSpec((tm, wa), row_map(0)),
        pl.BlockSpec((tm, wa), row_map(1)),
        pl.BlockSpec((tm, wa), row_map(1)),
        pl.BlockSpec((tm, wa), row_map(2)),
        pl.BlockSpec((wa, tm), lambda j, i: (0, rows_at(j, i, 2, 2))),
        pl.BlockSpec((tm, wa), lambda j, i: (rows_at(j, i, 3, 4), jnp.clip(j - 3, 0, 1))),
        pl.BlockSpec((tm, wa), row_map(5)),
        pl.BlockSpec((tm, wa), row_map(6)),
        pl.BlockSpec((tm, LANES), row_map(0)),
    )
    return pl.pallas_call(
        functools.partial(_inproj_body, n_heads=n_heads),
        out_shape=out_shape,
        grid=(7, ni),
        in_specs=[
            pl.BlockSpec((tm, d), lambda j, i: (i, 0)),
            pl.BlockSpec((d, wa), lambda j, i: (0, j)),
            pl.BlockSpec((d, LANES), const),
            pl.BlockSpec((1, HEAD_DIM), const),
            pl.BlockSpec((1, HEAD_DIM), const),
        ],
        out_specs=out_specs,
        compiler_params=_cparams(("arbitrary", "arbitrary")),
        name="inproj",
    )(xn, w_main, w_gates, gq, gk)


def _split3(x):
    hi = x.astype(BF16)
    r1 = x - hi.astype(F32)
    mid = r1.astype(BF16)
    lo = (r1 - mid.astype(F32)).astype(BF16)
    return hi, mid, lo


def _dot_01(m01, x):
    hi, mid, lo = _split3(x)
    return _dot(m01, hi) + _dot(m01, mid) + _dot(m01, lo)


def _gates_body(g_ref, b_ref, act_ref, cg_ref, cl_ref, *, seg, lc, rows, n_heads):
    ch = min(rows, 256)
    sg = min(seg, ch)
    lane = lax.broadcasted_iota(jnp.int32, (1, LANES), 1)
    is_logsig = (lane < n_heads) | ((lane >= 2 * n_heads) & (lane < 3 * n_heads))
    r = lax.broadcasted_iota(jnp.int32, (ch, ch), 0)
    c = lax.broadcasted_iota(jnp.int32, (ch, ch), 1)
    sh_g = _log2(sg)
    sh_l = _log2(lc)
    tri_g = jnp.where((r >= c) & ((r >> sh_g) == (c >> sh_g)), 1.0, 0.0).astype(BF16)
    tri_l = jnp.where((r >= c) & ((r >> sh_l) == (c >> sh_l)), 1.0, 0.0).astype(BF16)
    carry = jnp.zeros((1, LANES), F32)
    for ci in range(rows // ch):
        sl = slice(ci * ch, (ci + 1) * ch)
        x = g_ref[sl, :] + b_ref[...]
        act = jnp.where(is_logsig, _log_sigmoid(x), x)
        act_ref[sl, :] = act
        cg = _dot_01(tri_g, act)
        if seg > ch:
            cg = cg + carry
            carry = cg[ch - 1:ch, :]
        cg_ref[sl, :] = cg
        cl_ref[sl, :] = _dot_01(tri_l, act)


def _gates(gates, bias, seg, lc, n_heads):
    n = gates.shape[0]
    rows = seg if seg >= 256 else _pick_tile(n, 256)
    blk = pl.BlockSpec((rows, LANES), lambda b: (b, 0))
    return pl.pallas_call(
        functools.partial(_gates_body, seg=seg, lc=lc, rows=rows, n_heads=n_heads),
        out_shape=(jax.ShapeDtypeStruct((n, LANES), F32),) * 3,
        grid=(n // rows,),
        in_specs=[blk, pl.BlockSpec((1, LANES), lambda b: (0, 0))],
        out_specs=(blk, blk, blk),
        compiler_params=_cparams(("parallel",)),
        name="gates",
    )(gates, bias)


def _fox_prompt_body(q_ref, k_ref, vt_ref, c_ref, g_ref, o_ref,
                     qtail_ref, ktail_ref, m_ref, l_ref, acc_ref, *, tile, scale, n_heads):
    qi = pl.program_id(1)
    ki = pl.program_id(2)
    exp2_scale = scale * LOG2E

    def head_cols(h):
        if isinstance(h, int):
            return slice(h * HEAD_DIM, (h + 1) * HEAD_DIM)
        return pl.ds(pl.multiple_of(h * HEAD_DIM, HEAD_DIM), HEAD_DIM)

    @pl.when((qi == 0) & (ki == 0))
    def _():
        lane = lax.broadcasted_iota(jnp.int32, (1, LANES), 1)

        def tails(h, carry):
            c = jnp.sum(jnp.where(lane == h, c_ref[...], 0.0), axis=1, keepdims=True) * (1.0 / scale)
            p1, p2, p3 = (p.astype(F32) for p in _split3(c))
            one = jnp.ones_like(p1)

            def tail(cols):
                t = jnp.zeros((c.shape[0], LANES), F32)
                for j, col in enumerate(cols):
                    t = jnp.where(lane == j, col, t)
                return t.astype(BF16)

            qtail_ref[h] = tail((p1, p2, p3, one, one, one))
            ktail_ref[h] = tail((one, one, one, -p1, -p2, -p3))
            return carry

        lax.fori_loop(0, n_heads, tails, 0)

    @pl.when(ki == 0)
    def _():
        m_ref[...] = jnp.full_like(m_ref, NEG_INF)
        l_ref[...] = jnp.zeros_like(l_ref)
        acc_ref[...] = jnp.zeros_like(acc_ref)

    def block(masked):
        q_rows = pl.ds(pl.multiple_of(qi * tile, tile), tile)
        k_rows = pl.ds(pl.multiple_of(ki * tile, tile), tile)

        def one_head(h, carry):
            q_aug = jnp.concatenate([q_ref[:, head_cols(h)], qtail_ref[h, q_rows, :]], axis=1)
            k_aug = jnp.concatenate([k_ref[:, head_cols(h)], ktail_ref[h, k_rows, :]], axis=1)
            st = _dot_nt(k_aug, q_aug)
            if masked:
                kpos = lax.broadcasted_iota(jnp.int32, (tile, tile), 0)
                qpos = lax.broadcasted_iota(jnp.int32, (tile, tile), 1)
                st = jnp.where(kpos <= qpos, st, NEG_INF)
            m_old = m_ref[h]
            m_new = jnp.maximum(m_old, jnp.max(st, axis=0, keepdims=True))
            alpha = jnp.exp2((m_old - m_new) * exp2_scale)
            p = jnp.exp2((st - m_new) * exp2_scale)
            l_ref[h] = alpha * l_ref[h] + jnp.sum(p, axis=0, keepdims=True)
            acc_ref[h] = alpha * acc_ref[h] + _dot(vt_ref[head_cols(h), :], p.astype(BF16))
            m_ref[h] = m_new
            return carry

        for h in range(n_heads):
            one_head(h, 0)

    @pl.when(ki < qi)
    def _():
        block(False)

    @pl.when(ki == qi)
    def _():
        block(True)

    @pl.when(ki == pl.num_programs(2) - 1)
    def _():
        def finish(h, carry):
            o = (acc_ref[h] / l_ref[h]).T
            o_ref[:, head_cols(h)] = _rms(o, g_ref[pl.ds(h, 1), :]).astype(o_ref.dtype)
            return carry

        lax.fori_loop(0, n_heads, finish, 0)


def _fox_prompt(q, kb, vt, cg, g_out, batch, seq, n_heads):
    n, wa = q.shape
    tile = _pick_tile(seq, 512)
    nt = seq // tile

    def kv_blk(qi, ki):
        return jnp.minimum(ki, qi)

    return pl.pallas_call(
        functools.partial(_fox_prompt_body, tile=tile, scale=HEAD_DIM ** -0.5, n_heads=n_heads),
        out_shape=jax.ShapeDtypeStruct((n, wa), BF16),
        grid=(batch, nt, nt),
        in_specs=[
            pl.BlockSpec((tile, wa), lambda b, qi, ki: (b * nt + qi, 0)),
            pl.BlockSpec((tile, wa), lambda b, qi, ki: (b * nt + kv_blk(qi, ki), 0)),
            pl.BlockSpec((wa, tile), lambda b, qi, ki: (0, b * nt + kv_blk(qi, ki))),
            pl.BlockSpec((seq, LANES), lambda b, qi, ki: (b, 0)),
            pl.BlockSpec((n_heads, HEAD_DIM), lambda b, qi, ki: (0, 0)),
        ],
        out_specs=pl.BlockSpec((tile, wa), lambda b, qi, ki: (b * nt + qi, 0)),
        scratch_shapes=[
            pltpu.VMEM((n_heads, seq, LANES), BF16),
            pltpu.VMEM((n_heads, seq, LANES), BF16),
            pltpu.VMEM((n_heads, 1, tile), F32),
            pltpu.VMEM((n_heads, 1, tile), F32),
            pltpu.VMEM((n_heads, HEAD_DIM, tile), F32),
        ],
        compiler_params=_cparams(("parallel", "arbitrary", "arbitrary")),
        name="fox_prompt",
    )(q, kb, vt, cg, g_out)


def _pool_cumsum_body(x_ref, o_ref):
    page = x_ref.shape[1]
    r = lax.broadcasted_iota(jnp.int32, (page, page), 0)
    c = lax.broadcasted_iota(jnp.int32, (page, page), 1)
    tri = jnp.where(r <= c, 1.0, 0.0).astype(BF16)
    hi, mid, lo = _split3(x_ref[...])
    o_ref[...] = _dot(hi, tri) + _dot(mid, tri) + _dot(lo, tri)


def _pool_cumsum(lf):
    n_rows, page = lf.shape
    rows = _pick_tile(n_rows, 2048)
    blk = pl.BlockSpec((rows, page), lambda i: (i, 0))
    return pl.pallas_call(
        _pool_cumsum_body,
        out_shape=jax.ShapeDtypeStruct((n_rows, page), F32),
        grid=(n_rows // rows,),
        in_specs=[blk],
        out_specs=blk,
        compiler_params=_cparams(("parallel",)),
        name="pool_cumsum",
    )(lf)


def _fox_sample_body(pt_ref, q_ref, kn_ref, vn_ref, cl_ref, g_ref, *rest,
                     npg, n_pages, page, n_heads, t_new, scale):
    k_refs = rest[0:npg]
    c_refs = rest[npg:2 * npg]
    v_refs = rest[2 * npg:3 * npg]
    o_ref = rest[3 * npg]
    qx_ref, qxb_ref, pad_ref, cpad_ref, s2_ref, carry_ref, acc_ref = rest[3 * npg + 1:]
    del pt_ref
    b_id = pl.program_id(0)
    s_id = pl.program_id(1)
    n_seq = pl.num_programs(0) - 1
    n_steps = n_pages // npg
    past = n_pages * page
    wa = n_heads * HEAD_DIM
    s_ref = s2_ref.at[b_id % 2]
    p_ref = s2_ref.at[(b_id + 1) % 2]
    k_phase = b_id < n_seq
    v_phase = b_id >= 1
    row_h = lax.broadcasted_iota(jnp.int32, (n_heads, wa), 0)
    col_h = lax.broadcasted_iota(jnp.int32, (n_heads, wa), 1) >> _log2(HEAD_DIM)
    head_diag = row_h == col_h

    def expand(c):
        out = c
        for t in range(1, t_new):
            out = out + pltpu.roll(c, t * n_heads, axis=1)
        return out

    @pl.when(k_phase & (s_id == 0))
    def _():
        qx_ref[...] = jnp.zeros_like(qx_ref)
        q = q_ref[...]
        for t in range(t_new):
            qx_ref[t * n_heads:(t + 1) * n_heads, :] = jnp.where(head_diag, q[t:t + 1, :], 0.0)
        qxb_ref[...] = qx_ref[...].astype(BF16)
        cpad_ref[...] = jnp.zeros_like(cpad_ref)
        carry_ref[...] = jnp.zeros_like(carry_ref)

    @pl.when(v_phase & (s_id == 0))
    def _():
        acc_ref[...] = jnp.zeros_like(acc_ref)

    def head_major(ref):
        return jnp.concatenate([ref[pl.ds(h, page, stride=n_heads), :] for h in range(n_heads)], axis=1)

    @pl.when(k_phase)
    def _():
        for r in range(npg):
            st = _dot_nt(head_major(k_refs[r]).astype(BF16), qxb_ref[...]) * scale
            cpad_ref[:, 0:n_heads] = c_refs[r][...].T
            cx = expand(cpad_ref[...]) + carry_ref[...]
            carry_ref[...] = cx[page - 1:page, :]
            off = pl.multiple_of((s_id * npg + r) * page, page)
            s_ref[pl.ds(off, page), :] = st - cx

    @pl.when(v_phase)
    def _():
        for r in range(npg):
            off = pl.multiple_of((s_id * npg + r) * page, page)
            p = p_ref[pl.ds(off, page), :].T.astype(BF16)
            acc_ref[...] += _dot(p, head_major(v_refs[r]).astype(BF16))

    @pl.when(v_phase & (s_id == n_steps - 1))
    def _():
        pad_ref[...] = jnp.zeros_like(pad_ref)
        pad_ref[0:t_new, :] = vn_ref[...]
        p = p_ref[past:past + page, :].T.astype(BF16)
        acc = acc_ref[...] + _dot(p, pad_ref[...].astype(BF16))
        for t in range(t_new):
            blk = acc[t * n_heads:(t + 1) * n_heads, :]
            o_t = jnp.sum(jnp.where(head_diag, blk, 0.0), axis=0, keepdims=True)
            for h in range(n_heads):
                sl = slice(h * HEAD_DIM, (h + 1) * HEAD_DIM)
                o_ref[t:t + 1, sl] = _rms(o_t[:, sl], g_ref[h:h + 1, :])

    @pl.when(k_phase & (s_id == n_steps - 1))
    def _():
        pad_ref[...] = jnp.zeros_like(pad_ref)
        pad_ref[0:t_new, :] = kn_ref[...]
        st = _dot_nt(pad_ref[...].astype(BF16), qxb_ref[...]) * scale
        lane = lax.broadcasted_iota(jnp.int32, (1, LANES), 1)
        cpad_ref[...] = jnp.zeros_like(cpad_ref)
        cpad_ref[0:t_new, :] = jnp.where(lane < n_heads, cl_ref[...], 0.0)
        cl_x = expand(cpad_ref[...])
        u = lax.broadcasted_iota(jnp.int32, (page, LANES), 0)
        t = lax.broadcasted_iota(jnp.int32, (page, LANES), 1) >> _log2(n_heads)
        st = jnp.where((u < t_new) & (u <= t), st - cl_x, NEG_INF)
        s_past = s_ref[0:past, :] + carry_ref[...]
        m = jnp.maximum(jnp.max(s_past, axis=0, keepdims=True), jnp.max(st, axis=0, keepdims=True))
        p_past = jnp.exp(s_past - m)
        p_new = jnp.exp(st - m)
        inv = 1.0 / (jnp.sum(p_past, axis=0, keepdims=True) + jnp.sum(p_new, axis=0, keepdims=True))
        s_ref[0:past, :] = p_past * inv
        s_ref[past:past + page, :] = p_new * inv


def _fox_sample(page_table, q, k_new, v_new, cl, g_out, cache_k, cache_v, cache_c, page, pool_base, n_heads):
    bsz, t_new, wa = q.shape
    n_pages = page_table.shape[1]
    assert page == LANES and n_heads == SUBLANES and n_heads * t_new <= LANES
    page_bytes = page * wa * 4
    npg = _pick_tile(n_pages, max(1, PAGE_BUFFER_BYTES // (4 * page_bytes)))
    n_steps = n_pages // npg
    scale = HEAD_DIM ** -0.5

    def k_seq(b):
        return jnp.minimum(b, bsz - 1)

    def v_seq(b):
        return jnp.maximum(b - 1, 0)

    def kv_map(seq_fn, r):
        return lambda b, s, pt: (pool_base + pt[seq_fn(b), s * npg + r], 0)

    def c_map(r):
        return lambda b, s, pt: (pt[k_seq(b), s * npg + r], 0, 0)

    k3 = lambda b, s, pt: (k_seq(b), 0, 0)
    v3 = lambda b, s, pt: (v_seq(b), 0, 0)
    in_specs = [
        pl.BlockSpec((None, t_new, wa), k3),
        pl.BlockSpec((None, t_new, wa), k3),
        pl.BlockSpec((None, t_new, wa), v3),
        pl.BlockSpec((None, t_new, LANES), k3),
        pl.BlockSpec((n_heads, HEAD_DIM), lambda b, s, pt: (0, 0)),
    ]
    in_specs += [pl.BlockSpec((page * n_heads, HEAD_DIM), kv_map(k_seq, r)) for r in range(npg)]
    in_specs += [pl.BlockSpec((None, n_heads, page), c_map(r)) for r in range(npg)]
    in_specs += [pl.BlockSpec((page * n_heads, HEAD_DIM), kv_map(v_seq, r)) for r in range(npg)]
    grid_spec = pltpu.PrefetchScalarGridSpec(
        num_scalar_prefetch=1,
        grid=(bsz + 1, n_steps),
        in_specs=in_specs,
        out_specs=pl.BlockSpec((None, t_new, wa), v3),
        scratch_shapes=[
            pltpu.VMEM((LANES, wa), F32),
            pltpu.VMEM((LANES, wa), BF16),
            pltpu.VMEM((page, wa), F32),
            pltpu.VMEM((page, LANES), F32),
            pltpu.VMEM((2, n_pages * page + page, LANES), F32),
            pltpu.VMEM((1, LANES), F32),
            pltpu.VMEM((LANES, wa), F32),
        ],
    )
    return pl.pallas_call(
        functools.partial(_fox_sample_body, npg=npg, n_pages=n_pages, page=page, n_heads=n_heads,
                          t_new=t_new, scale=scale),
        out_shape=jax.ShapeDtypeStruct((bsz, t_new, wa), F32),
        grid_spec=grid_spec,
        compiler_params=_cparams(("arbitrary", "arbitrary")),
        name="fox_sample",
    )(page_table, q, k_new, v_new, cl, g_out, *([cache_k] * npg), *([cache_c] * npg), *([cache_v] * npg))


def _bdot(a, b):
    return lax.dot_general(a, b, (((2,), (1,)), ((0,), (0,))), preferred_element_type=F32)


def _bdot_nt(a, b):
    return lax.dot_general(a, b, (((2,), (2,)), ((0,), (0,))), preferred_element_type=F32)


def _mlstm_chunk(qc, kc, vc, i_col, b_col, c_st, n_st, m_st, n_valid):
    nh, ln, _ = qc.shape
    r = lax.broadcasted_iota(jnp.int32, (ln, ln), 0)
    c = lax.broadcasted_iota(jnp.int32, (ln, ln), 1)
    eye = (r == c)[None]
    tril = (r >= c)[None]

    def col_to_row(col):
        return jnp.sum(jnp.where(eye, col, 0.0), axis=1, keepdims=True)

    i_row = col_to_row(i_col)
    b_row = col_to_row(b_col)
    a_col = b_col + m_st
    dmat = jnp.where(tril, b_col - b_row + i_row, NEG_INF)
    mt = jnp.maximum(a_col, jnp.max(dmat, axis=2, keepdims=True))
    qb = qc.astype(BF16)
    kb = kc.astype(BF16)
    vb = vc.astype(BF16)
    w_intra = jnp.exp(dmat - mt) * _bdot_nt(qb, kb)
    w_inter = jnp.exp(a_col - mt)
    num = w_inter * _bdot(qb, c_st.astype(BF16)) + _bdot(w_intra.astype(BF16), vb)
    den = w_inter * jnp.sum(qc * n_st, axis=2, keepdims=True) + jnp.sum(w_intra, axis=2, keepdims=True)
    h = num / jnp.maximum(jnp.abs(den), jnp.exp(-mt))
    last = n_valid - 1
    m_new = mt[:, last:last + 1, :]
    b_last = b_col[:, last:last + 1, :]
    w_state = jnp.exp(b_last - b_col + i_col - m_new)
    if n_valid < ln:
        row = lax.broadcasted_iota(jnp.int32, (1, ln, 1), 1)
        w_state = jnp.where(row < n_valid, w_state, 0.0)
    decay = jnp.exp(b_last + m_st - m_new)
    ks = w_state * kc
    ks_t = jnp.stack([ks[hd].T for hd in range(nh)]).astype(BF16)
    c_new = decay * c_st + _bdot(ks_t, vb)
    n_new = decay * n_st + jnp.sum(ks, axis=1, keepdims=True)
    return h, c_new, n_new, m_new


def _heads(x, n_heads, offset=0, width=HEAD_DIM):
    return jnp.stack([x[:, offset + h * width:offset + (h + 1) * width] for h in range(n_heads)])


def _mlstm_prompt_body(qk_ref, w_ref, b_ref, v_ref, o_ref, act_ref, cl_ref, g_ref,
                       y_ref, c_out_ref, n_out_ref, m_out_ref,
                       ext_ref, qk_scr, c_ref, n_ref, m_ref, *, rb, chunk, n_heads):
    blk = pl.program_id(1)
    pad = SUBLANES
    wm = n_heads * HEAD_DIM
    kscale = HEAD_DIM ** -0.5

    @pl.when(blk == 0)
    def _():
        ext_ref[0:pad, :] = jnp.zeros((pad, 2 * wm), F32)
        c_ref[...] = jnp.zeros_like(c_ref)
        n_ref[...] = jnp.zeros_like(n_ref)
        m_ref[...] = jnp.zeros_like(m_ref)

    ext_ref[pad:pad + rb, :] = qk_ref[...]
    acc = b_ref[...] + w_ref[CONV_W - 1:CONV_W, :] * qk_ref[...]
    for j in range(CONV_W - 1):
        off = pad - (CONV_W - 1) + j
        acc = acc + w_ref[j:j + 1, :] * ext_ref[off:off + rb, :]
    qk_scr[...] = acc * jax.nn.sigmoid(acc)
    ext_ref[0:pad, :] = qk_ref[rb - pad:rb, :]

    g = g_ref[...][:, None, :]
    for ci in range(rb // chunk):
        rows = slice(ci * chunk, (ci + 1) * chunk)
        qk = qk_scr[rows, :]
        hh, c_new, n_new, m_new = _mlstm_chunk(
            _heads(qk, n_heads), _heads(qk, n_heads, wm) * kscale, _heads(v_ref[rows, :], n_heads),
            _heads(act_ref[rows, :], n_heads, n_heads, 1), _heads(cl_ref[rows, :], n_heads, 2 * n_heads, 1),
            c_ref[...], n_ref[...], m_ref[:, :, 0:1], chunk)
        c_ref[...] = c_new
        n_ref[...] = n_new
        m_ref[...] = jnp.broadcast_to(m_new, m_ref.shape)
        y = jax.nn.sigmoid(_heads(o_ref[rows, :], n_heads)) * _rms(hh, g)
        for h in range(n_heads):
            y_ref[rows, h * HEAD_DIM:(h + 1) * HEAD_DIM] = y[h].astype(y_ref.dtype)

    @pl.when(blk == pl.num_programs(1) - 1)
    def _():
        c_out_ref[...] = c_ref[...]
        n_out_ref[...] = n_ref[:, 0, :]
        m_out_ref[...] = m_ref[:, 0, :]


def _mlstm_prompt(qkm, conv_w, conv_b, vm, om, act, cl, g_out, batch, seq, n_heads):
    n = qkm.shape[0]
    wm = n_heads * HEAD_DIM
    chunk = math.gcd(seq, MLSTM_CHUNK)
    rb = _pick_tile(seq, 4 * chunk)
    assert rb % chunk == 0 and rb >= SUBLANES
    nblk = seq // rb
    row = lambda b, i: (b * nblk + i, 0)
    const = lambda b, i: (0, 0)
    st = lambda b, i: (b, 0, 0)
    out_shape = (
        jax.ShapeDtypeStruct((n, wm), BF16),
        jax.ShapeDtypeStruct((batch, n_heads, HEAD_DIM, HEAD_DIM), F32),
        jax.ShapeDtypeStruct((batch, n_heads, HEAD_DIM), F32),
        jax.ShapeDtypeStruct((batch, n_heads, LANES), F32),
    )
    return pl.pallas_call(
        functools.partial(_mlstm_prompt_body, rb=rb, chunk=chunk, n_heads=n_heads),
        out_shape=out_shape,
        grid=(batch, nblk),
        in_specs=[
            pl.BlockSpec((rb, 2 * wm), row),
            pl.BlockSpec((CONV_W, 2 * wm), const),
            pl.BlockSpec((1, 2 * wm), const),
            pl.BlockSpec((rb, wm), row),
            pl.BlockSpec((rb, wm), row),
            pl.BlockSpec((rb, LANES), row),
            pl.BlockSpec((rb, LANES), row),
            pl.BlockSpec((n_heads, HEAD_DIM), const),
        ],
        out_specs=(
            pl.BlockSpec((rb, wm), row),
            pl.BlockSpec((None, n_heads, HEAD_DIM, HEAD_DIM), lambda b, i: (b, 0, 0, 0)),
            pl.BlockSpec((None, n_heads, HEAD_DIM), st),
            pl.BlockSpec((None, n_heads, LANES), st),
        ),
        scratch_shapes=[
            pltpu.VMEM((rb + SUBLANES, 2 * wm), F32),
            pltpu.VMEM((rb, 2 * wm), F32),
            pltpu.VMEM((n_heads, HEAD_DIM, HEAD_DIM), F32),
            pltpu.VMEM((n_heads, 1, HEAD_DIM), F32),
            pltpu.VMEM((n_heads, 1, LANES), F32),
        ],
        compiler_params=_cparams(("parallel", "arbitrary")),
        name="mlstm_prompt",
    )(qkm, conv_w, conv_b, vm, om, act, cl, g_out)


def _mlstm_sample_body(qk_ref, prev_ref, w_ref, b_ref, v_ref, o_ref, act_ref, cl_ref, g_ref, c0_ref, n0_ref, m0_ref,
                       y_ref, c_out_ref, n_out_ref, m_out_ref, ext_ref, pad_ref, *, t_new, n_heads, sb):
    wm = n_heads * HEAD_DIM
    rows = SUBLANES
    kscale = HEAD_DIM ** -0.5
    valid = lax.broadcasted_iota(jnp.int32, (rows, 1), 0) < t_new

    def padded(val, width):
        pad_ref[:, 0:width] = jnp.zeros((rows, width), F32)
        pad_ref[0:t_new, 0:width] = val
        return pad_ref[:, 0:width]

    qs, ks, vs, os_, i_cols, b_cols, n0s, m0s = [], [], [], [], [], [], [], []
    for s in range(sb):
        tok = slice(s * t_new, (s + 1) * t_new)
        ext_ref[...] = jnp.zeros_like(ext_ref)
        ext_ref[0:CONV_W - 1, :] = prev_ref[s]
        ext_ref[CONV_W - 1:CONV_W - 1 + t_new, :] = qk_ref[tok, :]
        acc = b_ref[...] + w_ref[0:1, :] * ext_ref[0:rows, :]
        for j in range(1, CONV_W):
            acc = acc + w_ref[j:j + 1, :] * ext_ref[j:j + rows, :]
        qk = jnp.where(valid, acc * jax.nn.sigmoid(acc), 0.0)
        qs.append(_heads(qk, n_heads))
        ks.append(_heads(qk, n_heads, wm) * kscale)
        vs.append(_heads(padded(v_ref[tok, :], wm), n_heads))
        os_.append(_heads(padded(o_ref[tok, :], wm), n_heads))
        i_cols.append(jnp.where(valid, _heads(padded(act_ref[tok, :], LANES), n_heads, n_heads, 1), NEG_INF))
        b_col = _heads(padded(cl_ref[tok, :], LANES), n_heads, 2 * n_heads, 1)
        b_cols.append(jnp.where(valid, b_col, b_col[:, t_new - 1:t_new, :]))
        n0s.append(jnp.stack([n0_ref[s, h:h + 1, :] for h in range(n_heads)]))
        m0s.append(jnp.stack([m0_ref[s, h:h + 1, :] for h in range(n_heads)]))

    cat = lambda parts: jnp.concatenate(parts, axis=0)
    c0 = cat([c0_ref[s] for s in range(sb)])
    hh, c_new, n_new, m_new = _mlstm_chunk(cat(qs), cat(ks), cat(vs), cat(i_cols), cat(b_cols),
                                           c0, cat(n0s), cat(m0s), t_new)
    g = jnp.concatenate([g_ref[...]] * sb, axis=0)[:, None, :]
    y = jax.nn.sigmoid(cat(os_)) * _rms(hh, g)
    m_wide = jnp.broadcast_to(m_new, (sb * n_heads, 1, LANES))
    for s in range(sb):
        for h in range(n_heads):
            y_ref[s * t_new:(s + 1) * t_new, h * HEAD_DIM:(h + 1) * HEAD_DIM] = y[s * n_heads + h, 0:t_new, :]
        grp = slice(s * n_heads, (s + 1) * n_heads)
        c_out_ref[s] = c_new[grp]
        n_out_ref[s] = n_new[grp, 0, :]
        m_out_ref[s] = m_wide[grp, 0, :]


def _mlstm_sample(qkm, conv_prev, conv_w, conv_b, vm, om, act, cl, g_out, c0, n0, m0, t_new, n_heads):
    n, wm2 = qkm.shape
    bsz = n // t_new
    wm = wm2 // 2
    assert t_new + CONV_W - 1 <= SUBLANES
    sb = _pick_tile(bsz, 4)
    assert (sb * t_new) % SUBLANES == 0 or sb == bsz
    tok = lambda b: (b, 0)
    seq3 = lambda b: (b, 0, 0)
    const = lambda b: (0, 0)
    out_shape = (
        jax.ShapeDtypeStruct((n, wm), F32),
        jax.ShapeDtypeStruct((bsz, n_heads, HEAD_DIM, HEAD_DIM), F32),
        jax.ShapeDtypeStruct((bsz, n_heads, HEAD_DIM), F32),
        jax.ShapeDtypeStruct((bsz, n_heads, LANES), F32),
    )
    return pl.pallas_call(
        functools.partial(_mlstm_sample_body, t_new=t_new, n_heads=n_heads, sb=sb),
        out_shape=out_shape,
        grid=(bsz // sb,),
        in_specs=[
            pl.BlockSpec((sb * t_new, wm2), tok),
            pl.BlockSpec((sb, CONV_W - 1, wm2), seq3),
            pl.BlockSpec((CONV_W, wm2), const),
            pl.BlockSpec((1, wm2), const),
            pl.BlockSpec((sb * t_new, wm), tok),
            pl.BlockSpec((sb * t_new, wm), tok),
            pl.BlockSpec((sb * t_new, LANES), tok),
            pl.BlockSpec((sb * t_new, LANES), tok),
            pl.BlockSpec((n_heads, HEAD_DIM), const),
            pl.BlockSpec((sb, n_heads, HEAD_DIM, HEAD_DIM), lambda b: (b, 0, 0, 0)),
            pl.BlockSpec((sb, n_heads, HEAD_DIM), seq3),
            pl.BlockSpec((sb, n_heads, 1), seq3),
        ],
        out_specs=(
            pl.BlockSpec((sb * t_new, wm), tok),
            pl.BlockSpec((sb, n_heads, HEAD_DIM, HEAD_DIM), lambda b: (b, 0, 0, 0)),
            pl.BlockSpec((sb, n_heads, HEAD_DIM), seq3),
            pl.BlockSpec((sb, n_heads, LANES), seq3),
        ),
        scratch_shapes=[pltpu.VMEM((2 * SUBLANES, wm2), F32), pltpu.VMEM((SUBLANES, wm), F32)],
        compiler_params=_cparams(("parallel",)),
        name="mlstm_sample",
    )(qkm, conv_prev, conv_w, conv_b, vm, om, act, cl, g_out, c0, n0, m0)


def _outproj_body(ya_ref, ym_ref, w_ref, x_ref, o_ref, *, wa):
    y = _dot(ya_ref[...].astype(BF16), w_ref[0:wa, :]) + _dot(ym_ref[...].astype(BF16), w_ref[wa:, :])
    o_ref[...] = x_ref[...] + y


def _outproj(ya, ym, w, x):
    n, d = x.shape
    wa = ya.shape[1]
    wm = ym.shape[1]
    tm = _pick_tile(n, 512)
    row = lambda i: (i, 0)
    return pl.pallas_call(
        functools.partial(_outproj_body, wa=wa),
        out_shape=jax.ShapeDtypeStruct((n, d), F32),
        grid=(n // tm,),
        in_specs=[
            pl.BlockSpec((tm, wa), row),
            pl.BlockSpec((tm, wm), row),
            pl.BlockSpec((wa + wm, d), lambda i: (0, 0)),
            pl.BlockSpec((tm, d), row),
        ],
        out_specs=pl.BlockSpec((tm, d), row),
        compiler_params=_cparams(("parallel",)),
        name="outproj",
    )(ya, ym, w, x)


def _cast_pad_body(x_ref, o_ref, *, axis, size):
    if axis == 1:
        o_ref[:, 0:size] = x_ref[...].astype(BF16)
        o_ref[:, size:] = jnp.zeros((o_ref.shape[0], o_ref.shape[1] - size), BF16)
    else:
        o_ref[0:size, :] = x_ref[...].astype(BF16)
        o_ref[size:, :] = jnp.zeros((o_ref.shape[0] - size, o_ref.shape[1]), BF16)


def _cast_pad(w, axis, padded):
    r, c = w.shape
    size = w.shape[axis]
    assert padded > size
    if axis == 1:
        t = _pick_tile(r, 256)
        grid, in_blk, out_blk, imap = (r // t,), (t, c), (t, padded), (lambda i: (i, 0))
        out_shape = (r, padded)
    else:
        t = _pick_tile(c, 256)
        grid, in_blk, out_blk, imap = (c // t,), (r, t), (padded, t), (lambda i: (0, i))
        out_shape = (padded, c)
    return pl.pallas_call(
        functools.partial(_cast_pad_body, axis=axis, size=size),
        out_shape=jax.ShapeDtypeStruct(out_shape, BF16),
        grid=grid,
        in_specs=[pl.BlockSpec(in_blk, imap)],
        out_specs=pl.BlockSpec(out_blk, imap),
        compiler_params=_cparams(("parallel",)),
        name="cast_pad",
    )(w)


def _win_prep_body(wt_ref, fa_ref, im_ref, main_ref, gates_ref, *, n_gate_cols):
    main_ref[...] = wt_ref[...].T.astype(BF16)

    @pl.when(pl.program_id(0) == 0)
    def _():
        d = gates_ref.shape[0]
        g = jnp.concatenate([fa_ref[...], im_ref[...], jnp.zeros((LANES - n_gate_cols, d), F32)], axis=0)
        gates_ref[...] = g.T.astype(BF16)


def _win_prep(w_in_t, cuts, n_gate_cols, tile):
    n_in, d = w_in_t.shape
    o_f, o_qk, o_i = cuts
    n_main = o_f + (o_i - o_qk)
    assert o_f % tile == 0 and (o_i - o_qk) % tile == 0 and o_qk % SUBLANES == 0 and o_i % SUBLANES == 0
    n_lead = o_f // tile

    def row_off(j):
        return pl.multiple_of(jnp.where(j < n_lead, j * tile, o_qk + (j - n_lead) * tile), SUBLANES)

    el = pl.Element
    return pl.pallas_call(
        functools.partial(_win_prep_body, n_gate_cols=n_gate_cols),
        out_shape=(jax.ShapeDtypeStruct((d, n_main), BF16), jax.ShapeDtypeStruct((d, LANES), BF16)),
        grid=(n_main // tile,),
        in_specs=[
            pl.BlockSpec((el(tile), el(d)), lambda j: (row_off(j), 0)),
            pl.BlockSpec((el(o_qk - o_f), el(d)), lambda j: (o_f, 0)),
            pl.BlockSpec((el(n_in - o_i), el(d)), lambda j: (o_i, 0)),
        ],
        out_specs=(pl.BlockSpec((d, tile), lambda j: (0, j)), pl.BlockSpec((d, LANES), lambda j: (0, 0))),
        compiler_params=_cparams(("arbitrary",)),
        name="win_prep",
    )(w_in_t, w_in_t, w_in_t)


def _prep_weights(lw, n_heads_a, n_heads_m):
    wa = n_heads_a * HEAD_DIM
    wm = n_heads_m * HEAD_DIM
    f = lw['w1_gate'].shape[1]
    tf = 512
    fp = tf * ((f + tf - 1) // tf)

    def ffn_w(wg, wu, wd):
        if fp == f:
            return wg.astype(BF16), wu.astype(BF16), wd.astype(BF16)
        return _cast_pad(wg, 1, fp), _cast_pad(wu, 1, fp), _cast_pad(wd, 0, fp)

    w_in = lw['w_in']
    o_f = 3 * wa
    o_qk = o_f + n_heads_a
    o_i = o_qk + 2 * wm + 2 * wm
    o_fm = o_i + n_heads_m
    ng = 2 * n_heads_m + n_heads_a
    assert o_fm + n_heads_m == w_in.shape[1]
    w_main, w_gates = _win_prep(jnp.swapaxes(w_in, 0, 1), (o_f, o_qk, o_i), ng, wa)
    bias = jnp.concatenate([lw['b_fox_f'], lw['b_m_i'], lw['b_m_f'], jnp.zeros((LANES - ng,), F32)])[None, :]
    return {
        'ffn1': ffn_w(lw['w1_gate'], lw['w1_up'], lw['w1_down']),
        'ffn_w': ffn_w,
        'tf': tf,
        'fp': fp,
        'w_main': w_main,
        'w_gates': w_gates,
        'gate_bias': bias,
        'w_out': lw['w_out'].astype(BF16),
    }


def kernel(x_prompt, x_sample, cache_k, cache_v, cache_logf, state_conv, state_C, state_n, state_m, page_table,
           g_ffn1, w1_gate, w1_up, w1_down, g_mix, w_in, b_fox_f, b_m_i, b_m_f, conv_w, conv_b, g_q, g_k,
           g_out_a, g_out_m, w_out, g_ffn2, w2_gate, w2_up, w2_down):
    depth = w_in.shape[0]
    bp, seq, d = x_prompt.shape
    bs, t_new, _ = x_sample.shape
    n_heads_a = g_out_a.shape[1]
    n_heads_m = g_out_m.shape[1]
    wa = n_heads_a * HEAD_DIM
    wm = n_heads_m * HEAD_DIM
    n_pool, page = cache_k.shape[1], cache_k.shape[2]

    yp = x_prompt.reshape(bp * seq, d)
    ys = x_sample.reshape(bs * t_new, d)
    outs = [[] for _ in range(14)]
    for l in range(depth):
        lw = {
            'w1_gate': w1_gate[l], 'w1_up': w1_up[l], 'w1_down': w1_down[l], 'w_in': w_in[l],
            'b_fox_f': b_fox_f[l], 'b_m_i': b_m_i[l], 'b_m_f': b_m_f[l], 'w_out': w_out[l],
            'w2_gate': w2_gate[l], 'w2_up': w2_up[l], 'w2_down': w2_down[l],
        }
        pw = _prep_weights(lw, n_heads_a, n_heads_m)
        gf1 = g_ffn1[l][None, :]
        gf2 = g_ffn2[l][None, :]
        gmix = g_mix[l][None, :]
        gq = g_q[l][None, :]
        gk = g_k[l][None, :]
        cw = conv_w[l]
        cb = conv_b[l][None, :]
        goa = g_out_a[l]
        gom = g_out_m[l]

        w2 = (lw['w2_gate'], lw['w2_up'], lw['w2_down'])
        ffn1_steps = (yp.shape[0] // _pick_tile(yp.shape[0], 512)) * (pw['fp'] // pw['tf'])
        if w2[0].shape[1] % LANES == 0 and ffn1_steps >= 3 * (pw['fp'] // LANES):
            (x1, xn), ffn2_w = _ffn(yp, gf1, *pw['ffn1'], pw['tf'], g_next=gmix, prep=w2)
        else:
            ffn2_w = pw['ffn_w'](*w2)
            x1, xn = _ffn(yp, gf1, *pw['ffn1'], pw['tf'], g_next=gmix)
        q, k, kb, v, vt, qkm, vm, om, gt = _inproj(xn, pw['w_main'], pw['w_gates'], gq, gk, n_heads_a)
        act, cg, cl = _gates(gt, pw['gate_bias'], seq, math.gcd(seq, MLSTM_CHUNK), n_heads_a)
        ya = _fox_prompt(q, kb, vt, cg, goa, bp, seq, n_heads_a)
        ym, c_p, n_p, m_p = _mlstm_prompt(qkm, cw, cb, vm, om, act, cl, gom, bp, seq, n_heads_m)
        x2 = _outproj(ya, ym, pw['w_out'], x1)
        yp = _ffn(x2, gf2, *ffn2_w, pw['tf'])
        outs[0].append(k.reshape(bp, seq, n_heads_a, HEAD_DIM))
        outs[1].append(v.reshape(bp, seq, n_heads_a, HEAD_DIM))
        outs[2].append(act[:, :n_heads_a].reshape(bp, seq, n_heads_a))
        outs[3].append(qkm.reshape(bp, seq, 2 * wm)[:, seq - (CONV_W - 1):, :])
        outs[4].append(c_p)
        outs[5].append(n_p)
        outs[6].append(m_p[:, :, 0])

        x1, xn = _ffn(ys, gf1, *pw['ffn1'], pw['tf'], g_next=gmix)
        q, k, kb, v, vt, qkm, vm, om, gt = _inproj(xn, pw['w_main'], pw['w_gates'], gq, gk, n_heads_a)
        act, cg, cl = _gates(gt, pw['gate_bias'], t_new, math.gcd(t_new, MLSTM_CHUNK), n_heads_a)
        r3 = lambda a: a.reshape(bs, t_new, a.shape[-1])
        lf_rows = jnp.swapaxes(cache_logf[l], 1, 2).reshape(n_pool * n_heads_a, page)
        cache_c = _pool_cumsum(lf_rows).reshape(n_pool, n_heads_a, page)
        ya = _fox_sample(page_table, r3(q).astype(F32), r3(k), r3(v), r3(cg), goa,
                         cache_k.reshape(-1, HEAD_DIM), cache_v.reshape(-1, HEAD_DIM),
                         cache_c, page, l * n_pool, n_heads_a)
        ym, c_s, n_s, m_s = _mlstm_sample(qkm, state_conv[l], cw, cb, vm, om, act, cl, gom,
                                          state_C[l], state_n[l], state_m[l][:, :, None], t_new, n_heads_m)
        x2 = _outproj(ya.reshape(bs * t_new, wa), ym, pw['w_out'], x1)
        ys = _ffn(x2, gf2, *ffn2_w, pw['tf'])
        conv_ext = jnp.concatenate([state_conv[l], r3(qkm)], axis=1)
        outs[7].append(k.reshape(bs, t_new, n_heads_a, HEAD_DIM))
        outs[8].append(v.reshape(bs, t_new, n_heads_a, HEAD_DIM))
        outs[9].append(act[:, :n_heads_a].reshape(bs, t_new, n_heads_a))
        outs[10].append(conv_ext[:, t_new:, :])
        outs[11].append(c_s)
        outs[12].append(n_s)
        outs[13].append(m_s[:, :, 0])

    return (yp.reshape(bp, seq, d), ys.reshape(bs, t_new, d)) + tuple(jnp.stack(o) for o in outs)
```

```python
import functools
import math

import jax
import jax.numpy as jnp
from jax import lax
from jax.experimental import pallas as pl
from jax.experimental.pallas import tpu as pltpu

F32 = jnp.float32
BF16 = jnp.bfloat16

EPS = 1e-6
HEAD_DIM = 128
CONV_W = 4
MLSTM_CHUNK = 128
LANES = 128
SUBLANES = 8
VMEM_LIMIT_BYTES = 56 * 1024 * 1024
PAGE_BUFFER_BYTES = 32 * 1024 * 1024
NEG_INF = float("-inf")
LOG2E = math.log2(math.e)


def _cparams(sem):
    return pltpu.CompilerParams(dimension_semantics=sem, vmem_limit_bytes=VMEM_LIMIT_BYTES)


def _rms(x, g):
    y = x * lax.rsqrt(jnp.mean(x * x, axis=-1, keepdims=True) + EPS)
    return y * g


def _log_sigmoid(x):
    return jnp.minimum(x, 0.0) - jnp.log1p(jnp.exp(-jnp.abs(x)))


def _dot(a, b, precision=None):
    return jnp.dot(a, b, preferred_element_type=F32, precision=precision)


def _dot_nt(a, b, precision=None):
    return lax.dot_general(a, b, (((1,), (1,)), ((), ())), preferred_element_type=F32, precision=precision)


def _pick_tile(n, pref):
    t = min(n, pref)
    while n % t:
        t //= 2
    return t


def _log2(n):
    k = int(math.log2(n))
    assert 1 << k == n
    return k


def _ffn_body(x_ref, g_ref, wg_ref, wu_ref, wd_ref, gn_ref, *rest, with_next, n_prep, prep_real, prep_slabs):
    prep_in = rest[:n_prep]
    rest = rest[n_prep:]
    if with_next:
        o_ref, on_ref = rest[:2]
        rest = rest[2:]
    else:
        o_ref = rest[0]
        rest = rest[1:]
    prep_out = rest[:n_prep]
    xn_ref, acc_ref = rest[n_prep:]
    j = pl.program_id(1)

    @pl.when(j == 0)
    def _():
        xn_ref[...] = _rms(x_ref[...], g_ref[...]).astype(BF16)
        acc_ref[...] = jnp.zeros_like(acc_ref)

    xn = xn_ref[...]
    half = wg_ref.shape[1] // 2
    part = None
    for c in range(2):
        cols = slice(c * half, (c + 1) * half)
        gate = _dot(xn, wg_ref[:, cols])
        up = _dot(xn, wu_ref[:, cols])
        h = (gate * jax.nn.sigmoid(gate)) * up
        d = _dot(h.astype(BF16), wd_ref[cols, :])
        part = d if part is None else part + d
    acc_ref[...] += part

    @pl.when(j == pl.num_programs(1) - 1)
    def _():
        y = x_ref[...] + 0.5 * acc_ref[...]
        o_ref[...] = y
        if with_next:
            on_ref[...] = _rms(y, gn_ref[...]).astype(BF16)

    t = pl.program_id(0) * pl.num_programs(1) + j
    for k in range(n_prep):
        t0 = k * prep_slabs

        @pl.when((t >= t0) & (t < t0 + prep_real))
        def _(k=k):
            prep_out[k][...] = prep_in[k][...].astype(BF16)

        @pl.when((t >= t0 + prep_real) & (t < t0 + prep_slabs))
        def _(k=k):
            prep_out[k][...] = jnp.zeros_like(prep_out[k])


def _ffn(x, g, wg, wu, wd, tf, g_next=None, prep=None):
    n, d = x.shape
    fp = wg.shape[1]
    tm = _pick_tile(n, 512)
    ni, nj = n // tm, fp // tf
    with_next = g_next is not None
    row = pl.BlockSpec((tm, d), lambda i, j: (i, 0))
    vec = pl.BlockSpec((1, d), lambda i, j: (0, 0))
    out_shape = [jax.ShapeDtypeStruct((n, d), F32)]
    out_specs = [row]
    if with_next:
        out_shape.append(jax.ShapeDtypeStruct((n, d), BF16))
        out_specs.append(row)
    in_specs = [
        row, vec,
        pl.BlockSpec((d, tf), lambda i, j: (0, j)),
        pl.BlockSpec((d, tf), lambda i, j: (0, j)),
        pl.BlockSpec((tf, d), lambda i, j: (j, 0)),
        vec,
    ]
    operands = [x, g, wg, wu, wd, g_next if with_next else g]
    n_prep = prep_real = prep_slabs = 0
    if prep is not None:
        f = prep[0].shape[1]
        assert f % LANES == 0 and fp % LANES == 0
        n_prep, prep_real, prep_slabs = 3, f // LANES, fp // LANES
        assert ni * nj >= n_prep * prep_slabs

        def slab(k, last):
            return lambda i, j: jnp.clip(i * nj + j - k * prep_slabs, 0, last)

        for k in range(n_prep):
            cols = k < 2
            blk = (d, LANES) if cols else (LANES, d)
            for last, specs in ((prep_real - 1, in_specs), (prep_slabs - 1, out_specs)):
                s = slab(k, last)
                specs.append(pl.BlockSpec(blk, (lambda i, j, s=s: (0, s(i, j))) if cols
                                          else (lambda i, j, s=s: (s(i, j), 0))))
            out_shape.append(jax.ShapeDtypeStruct((d, fp) if cols else (fp, d), BF16))
        operands += list(prep)
    res = pl.pallas_call(
        functools.partial(_ffn_body, with_next=with_next, n_prep=n_prep, prep_real=prep_real,
                          prep_slabs=prep_slabs),
        out_shape=tuple(out_shape),
        grid=(ni, nj),
        in_specs=in_specs,
        out_specs=tuple(out_specs),
        scratch_shapes=[pltpu.VMEM((tm, d), BF16), pltpu.VMEM((tm, d), F32)],
        compiler_params=_cparams(("arbitrary", "arbitrary") if n_prep else ("parallel", "arbitrary")),
        name="ffn",
    )(*operands)
    n_main = 2 if with_next else 1
    main = res[0] if n_main == 1 else tuple(res[:2])
    return (main, tuple(res[n_main:])) if prep is not None else main


def _inproj_body(xn_ref, w_ref, wgt_ref, gq_ref, gk_ref,
                 q_ref, k_ref, kb_ref, v_ref, vt_ref, qkm_ref, vm_ref, om_ref, gt_ref, *, n_heads):
    j = pl.program_id(0)
    xn = xn_ref[...]
    half = w_ref.shape[1] // 2
    heads_per_half = half // HEAD_DIM

    def halves(finish):
        for c in range(2):
            cols = slice(c * half, (c + 1) * half)
            finish(cols, _dot(xn, w_ref[:, cols]))

    def per_head(cols, acc, store):
        for hh in range(heads_per_half):
            loc = slice(hh * HEAD_DIM, (hh + 1) * HEAD_DIM)
            store(slice(cols.start + loc.start, cols.start + loc.stop), acc[:, loc])

    @pl.when(j == 0)
    def _():
        def store(sl, a):
            q_ref[:, sl] = _rms(a, gq_ref[...]).astype(BF16)

        halves(lambda cols, acc: per_head(cols, acc, store))
        gt_ref[...] = _dot(xn, wgt_ref[...])

    @pl.when(j == 1)
    def _():
        def store(sl, a):
            kn = _rms(a, gk_ref[...])
            k_ref[:, sl] = kn
            kb_ref[:, sl] = kn.astype(BF16)

        halves(lambda cols, acc: per_head(cols, acc, store))

    @pl.when(j == 2)
    def _():
        def finish(cols, acc):
            v_ref[:, cols] = acc
            vt_ref[cols, :] = acc.T.astype(BF16)

        halves(finish)

    def plain(ref):
        def finish(cols, acc):
            ref[:, cols] = acc
        return finish

    @pl.when((j == 3) | (j == 4))
    def _():
        halves(plain(qkm_ref))

    @pl.when(j == 5)
    def _():
        halves(plain(vm_ref))

    @pl.when(j == 6)
    def _():
        halves(plain(om_ref))


def _inproj(xn, w_main, w_gates, gq, gk, n_heads):
    n, d = xn.shape
    wa = n_heads * HEAD_DIM
    assert w_main.shape[1] == 7 * wa
    tm = _pick_tile(n, 512)
    ni = n // tm
    const = lambda j, i: (0, 0)

    def rows_at(j, i, j0, j1):
        return jnp.where(j < j0, 0, jnp.where(j > j1, ni - 1, i))

    def row_map(j0, j1=None):
        j1 = j0 if j1 is None else j1
        return lambda j, i: (rows_at(j, i, j0, j1), 0)

    out_shape = (
        jax.ShapeDtypeStruct((n, wa), BF16),
        jax.ShapeDtypeStruct((n, wa), F32),
        jax.ShapeDtypeStruct((n, wa), BF16),
        jax.ShapeDtypeStruct((n, wa), F32),
        jax.ShapeDtypeStruct((wa, n), BF16),
        jax.ShapeDtypeStruct((n, 2 * wa), F32),
        jax.ShapeDtypeStruct((n, wa), F32),
        jax.ShapeDtypeStruct((n, wa), F32),
        jax.ShapeDtypeStruct((n, LANES), F32),
    )
    out_specs = (
        pl.BlockSpec((tm, wa), row_map(0)),
        pl.BlockSpec((tm, wa), row_map(1)),
        pl.BlockSpec((tm, wa), row_map(1)),
        pl.BlockSpec((tm, wa), row_map(2)),
        pl.BlockSpec((wa, tm), lambda j, i: (0, rows_at(j, i, 2, 2))),
        pl.BlockSpec((tm, wa), lambda j, i: (rows_at(j, i, 3, 4), jnp.clip(j - 3, 0, 1))),
        pl.BlockSpec((tm, wa), row_map(5)),
        pl.BlockSpec((tm, wa), row_map(6)),
        pl.BlockSpec((tm, LANES), row_map(0)),
    )
    return pl.pallas_call(
        functools.partial(_inproj_body, n_heads=n_heads),
        out_shape=out_shape,
        grid=(7, ni),
        in_specs=[
            pl.BlockSpec((tm, d), lambda j, i: (i, 0)),
            pl.BlockSpec((d, wa), lambda j, i: (0, j)),
            pl.BlockSpec((d, LANES), const),
            pl.BlockSpec((1, HEAD_DIM), const),
            pl.BlockSpec((1, HEAD_DIM), const),
        ],
        out_specs=out_specs,
        compiler_params=_cparams(("arbitrary", "arbitrary")),
        name="inproj",
    )(xn, w_main, w_gates, gq, gk)


def _split3(x):
    hi = x.astype(BF16)
    r1 = x - hi.astype(F32)
    mid = r1.astype(BF16)
    lo = (r1 - mid.astype(F32)).astype(BF16)
    return hi, mid, lo


def _dot_01(m01, x):
    hi, mid, lo = _split3(x)
    return _dot(m01, hi) + _dot(m01, mid) + _dot(m01, lo)


def _gates_body(g_ref, b_ref, act_ref, cg_ref, cl_ref, *, seg, lc, rows, n_heads):
    ch = min(rows, 256)
    sg = min(seg, ch)
    lane = lax.broadcasted_iota(jnp.int32, (1, LANES), 1)
    is_logsig = (lane < n_heads) | ((lane >= 2 * n_heads) & (lane < 3 * n_heads))
    r = lax.broadcasted_iota(jnp.int32, (ch, ch), 0)
    c = lax.broadcasted_iota(jnp.int32, (ch, ch), 1)
    sh_g = _log2(sg)
    sh_l = _log2(lc)
    tri_g = jnp.where((r >= c) & ((r >> sh_g) == (c >> sh_g)), 1.0, 0.0).astype(BF16)
    tri_l = jnp.where((r >= c) & ((r >> sh_l) == (c >> sh_l)), 1.0, 0.0).astype(BF16)
    carry = jnp.zeros((1, LANES), F32)
    for ci in range(rows // ch):
        sl = slice(ci * ch, (ci + 1) * ch)
        x = g_ref[sl, :] + b_ref[...]
        act = jnp.where(is_logsig, _log_sigmoid(x), x)
        act_ref[sl, :] = act
        cg = _dot_01(tri_g, act)
        if seg > ch:
            cg = cg + carry
            carry = cg[ch - 1:ch, :]
        cg_ref[sl, :] = cg
        cl_ref[sl, :] = _dot_01(tri_l, act)


def _gates(gates, bias, seg, lc, n_heads):
    n = gates.shape[0]
    rows = seg if seg >= 256 else _pick_tile(n, 256)
    blk = pl.BlockSpec((rows, LANES), lambda b: (b, 0))
    return pl.pallas_call(
        functools.partial(_gates_body, seg=seg, lc=lc, rows=rows, n_heads=n_heads),
        out_shape=(jax.ShapeDtypeStruct((n, LANES), F32),) * 3,
        grid=(n // rows,),
        in_specs=[blk, pl.BlockSpec((1, LANES), lambda b: (0, 0))],
        out_specs=(blk, blk, blk),
        compiler_params=_cparams(("parallel",)),
        name="gates",
    )(gates, bias)


def _fox_prompt_body(q_ref, k_ref, vt_ref, c_ref, g_ref, o_ref,
                     qtail_ref, ktail_ref, m_ref, l_ref, acc_ref, *, tile, scale, n_heads):
    qi = pl.program_id(1)
    ki = pl.program_id(2)
    exp2_scale = scale * LOG2E

    def head_cols(h):
        if isinstance(h, int):
            return slice(h * HEAD_DIM, (h + 1) * HEAD_DIM)
        return pl.ds(pl.multiple_of(h * HEAD_DIM, HEAD_DIM), HEAD_DIM)

    @pl.when((qi == 0) & (ki == 0))
    def _():
        lane = lax.broadcasted_iota(jnp.int32, (1, LANES), 1)

        def tails(h, carry):
            c = jnp.sum(jnp.where(lane == h, c_ref[...], 0.0), axis=1, keepdims=True) * (1.0 / scale)
            p1, p2, p3 = (p.astype(F32) for p in _split3(c))
            one = jnp.ones_like(p1)

            def tail(cols):
                t = jnp.zeros((c.shape[0], LANES), F32)
                for j, col in enumerate(cols):
                    t = jnp.where(lane == j, col, t)
                return t.astype(BF16)

            qtail_ref[h] = tail((p1, p2, p3, one, one, one))
            ktail_ref[h] = tail((one, one, one, -p1, -p2, -p3))
            return carry

        lax.fori_loop(0, n_heads, tails, 0)

    @pl.when(ki == 0)
    def _():
        m_ref[...] = jnp.full_like(m_ref, NEG_INF)
        l_ref[...] = jnp.zeros_like(l_ref)
        acc_ref[...] = jnp.zeros_like(acc_ref)

    def block(masked):
        q_rows = pl.ds(pl.multiple_of(qi * tile, tile), tile)
        k_rows = pl.ds(pl.multiple_of(ki * tile, tile), tile)

        def one_head(h, carry):
            q_aug = jnp.concatenate([q_ref[:, head_cols(h)], qtail_ref[h, q_rows, :]], axis=1)
            k_aug = jnp.concatenate([k_ref[:, head_cols(h)], ktail_ref[h, k_rows, :]], axis=1)
            st = _dot_nt(k_aug, q_aug)
            if masked:
                kpos = lax.broadcasted_iota(jnp.int32, (tile, tile), 0)
                qpos = lax.broadcasted_iota(jnp.int32, (tile, tile), 1)
                st = jnp.where(kpos <= qpos, st, NEG_INF)
            m_old = m_ref[h]
            m_new = jnp.maximum(m_old, jnp.max(st, axis=0, keepdims=True))
            alpha = jnp.exp2((m_old - m_new) * exp2_scale)
            p = jnp.exp2((st - m_new) * exp2_scale)
            l_ref[h] = alpha * l_ref[h] + jnp.sum(p, axis=0, keepdims=True)
            acc_ref[h] = alpha * acc_ref[h] + _dot(vt_ref[head_cols(h), :], p.astype(BF16))
            m_ref[h] = m_new
            return carry

        for h in range(n_heads):
            one_head(h, 0)

    @pl.when(ki < qi)
    def _():
        block(False)

    @pl.when(ki == qi)
    def _():
        block(True)

    @pl.when(ki == pl.num_programs(2) - 1)
    def _():
        def finish(h, carry):
            o = (acc_ref[h] / l_ref[h]).T
            o_ref[:, head_cols(h)] = _rms(o, g_ref[pl.ds(h, 1), :]).astype(o_ref.dtype)
            return carry

        lax.fori_loop(0, n_heads, finish, 0)


def _fox_prompt(q, kb, vt, cg, g_out, batch, seq, n_heads):
    n, wa = q.shape
    tile = _pick_tile(seq, 512)
    nt = seq // tile

    def kv_blk(qi, ki):
        return jnp.minimum(ki, qi)

    return pl.pallas_call(
        functools.partial(_fox_prompt_body, tile=tile, scale=HEAD_DIM ** -0.5, n_heads=n_heads),
        out_shape=jax.ShapeDtypeStruct((n, wa), BF16),
        grid=(batch, nt, nt),
        in_specs=[
            pl.BlockSpec((tile, wa), lambda b, qi, ki: (b * nt + qi, 0)),
            pl.BlockSpec((tile, wa), lambda b, qi, ki: (b * nt + kv_blk(qi, ki), 0)),
            pl.BlockSpec((wa, tile), lambda b, qi, ki: (0, b * nt + kv_blk(qi, ki))),
            pl.BlockSpec((seq, LANES), lambda b, qi, ki: (b, 0)),
            pl.BlockSpec((n_heads, HEAD_DIM), lambda b, qi, ki: (0, 0)),
        ],
        out_specs=pl.BlockSpec((tile, wa), lambda b, qi, ki: (b * nt + qi, 0)),
        scratch_shapes=[
            pltpu.VMEM((n_heads, seq, LANES), BF16),
            pltpu.VMEM((n_heads, seq, LANES), BF16),
            pltpu.VMEM((n_heads, 1, tile), F32),
            pltpu.VMEM((n_heads, 1, tile), F32),
            pltpu.VMEM((n_heads, HEAD_DIM, tile), F32),
        ],
        compiler_params=_cparams(("parallel", "arbitrary", "arbitrary")),
        name="fox_prompt",
    )(q, kb, vt, cg, g_out)


def _pool_cumsum_body(x_ref, o_ref):
    page = x_ref.shape[1]
    r = lax.broadcasted_iota(jnp.int32, (page, page), 0)
    c = lax.broadcasted_iota(jnp.int32, (page, page), 1)
    tri = jnp.where(r <= c, 1.0, 0.0).astype(BF16)
    hi, mid, lo = _split3(x_ref[...])
    o_ref[...] = _dot(hi, tri) + _dot(mid, tri) + _dot(lo, tri)


def _pool_cumsum(lf):
    n_rows, page = lf.shape
    rows = _pick_tile(n_rows, 2048)
    blk = pl.BlockSpec((rows, page), lambda i: (i, 0))
    return pl.pallas_call(
        _pool_cumsum_body,
        out_shape=jax.ShapeDtypeStruct((n_rows, page), F32),
        grid=(n_rows // rows,),
        in_specs=[blk],
        out_specs=blk,
        compiler_params=_cparams(("parallel",)),
        name="pool_cumsum",
    )(lf)


def _fox_sample_body(pt_ref, q_ref, kn_ref, vn_ref, cl_ref, g_ref, *rest,
                     npg, n_pages, page, n_heads, t_new, scale):
    k_refs = rest[0:npg]
    c_refs = rest[npg:2 * npg]
    v_refs = rest[2 * npg:3 * npg]
    o_ref = rest[3 * npg]
    qx_ref, qxb_ref, pad_ref, cpad_ref, s2_ref, carry_ref, acc_ref = rest[3 * npg + 1:]
    del pt_ref
    b_id = pl.program_id(0)
    s_id = pl.program_id(1)
    n_seq = pl.num_programs(0) - 1
    n_steps = n_pages // npg
    past = n_pages * page
    wa = n_heads * HEAD_DIM
    s_ref = s2_ref.at[b_id % 2]
    p_ref = s2_ref.at[(b_id + 1) % 2]
    k_phase = b_id < n_seq
    v_phase = b_id >= 1
    row_h = lax.broadcasted_iota(jnp.int32, (n_heads, wa), 0)
    col_h = lax.broadcasted_iota(jnp.int32, (n_heads, wa), 1) >> _log2(HEAD_DIM)
    head_diag = row_h == col_h

    def expand(c):
        out = c
        for t in range(1, t_new):
            out = out + pltpu.roll(c, t * n_heads, axis=1)
        return out

    @pl.when(k_phase & (s_id == 0))
    def _():
        qx_ref[...] = jnp.zeros_like(qx_ref)
        q = q_ref[...]
        for t in range(t_new):
            qx_ref[t * n_heads:(t + 1) * n_heads, :] = jnp.where(head_diag, q[t:t + 1, :], 0.0)
        qxb_ref[...] = qx_ref[...].astype(BF16)
        cpad_ref[...] = jnp.zeros_like(cpad_ref)
        carry_ref[...] = jnp.zeros_like(carry_ref)

    @pl.when(v_phase & (s_id == 0))
    def _():
        acc_ref[...] = jnp.zeros_like(acc_ref)

    def head_major(ref):
        return jnp.concatenate([ref[pl.ds(h, page, stride=n_heads), :] for h in range(n_heads)], axis=1)

    @pl.when(k_phase)
    def _():
        for r in range(npg):
            st = _dot_nt(head_major(k_refs[r]).astype(BF16), qxb_ref[...]) * scale
            cpad_ref[:, 0:n_heads] = c_refs[r][...].T
            cx = expand(cpad_ref[...]) + carry_ref[...]
            carry_ref[...] = cx[page - 1:page, :]
            off = pl.multiple_of((s_id * npg + r) * page, page)
            s_ref[pl.ds(off, page), :] = st - cx

    @pl.when(v_phase)
    def _():
        for r in range(npg):
            off = pl.multiple_of((s_id * npg + r) * page, page)
            p = p_ref[pl.ds(off, page), :].T.astype(BF16)
            acc_ref[...] += _dot(p, head_major(v_refs[r]).astype(BF16))

    @pl.when(v_phase & (s_id == n_steps - 1))
    def _():
        pad_ref[...] = jnp.zeros_like(pad_ref)
        pad_ref[0:t_new, :] = vn_ref[...]
        p = p_ref[past:past + page, :].T.astype(BF16)
        acc = acc_ref[...] + _dot(p, pad_ref[...].astype(BF16))
        for t in range(t_new):
            blk = acc[t * n_heads:(t + 1) * n_heads, :]
            o_t = jnp.sum(jnp.where(head_diag, blk, 0.0), axis=0, keepdims=True)
            for h in range(n_heads):
                sl = slice(h * HEAD_DIM, (h + 1) * HEAD_DIM)
                o_ref[t:t + 1, sl] = _rms(o_t[:, sl], g_ref[h:h + 1, :])

    @pl.when(k_phase & (s_id == n_steps - 1))
    def _():
        pad_ref[...] = jnp.zeros_like(pad_ref)
        pad_ref[0:t_new, :] = kn_ref[...]
        st = _dot_nt(pad_ref[...].astype(BF16), qxb_ref[...]) * scale
        lane = lax.broadcasted_iota(jnp.int32, (1, LANES), 1)
        cpad_ref[...] = jnp.zeros_like(cpad_ref)
        cpad_ref[0:t_new, :] = jnp.where(lane < n_heads, cl_ref[...], 0.0)
        cl_x = expand(cpad_ref[...])
        u = lax.broadcasted_iota(jnp.int32, (page, LANES), 0)
        t = lax.broadcasted_iota(jnp.int32, (page, LANES), 1) >> _log2(n_heads)
        st = jnp.where((u < t_new) & (u <= t), st - cl_x, NEG_INF)
        s_past = s_ref[0:past, :] + carry_ref[...]
        m = jnp.maximum(jnp.max(s_past, axis=0, keepdims=True), jnp.max(st, axis=0, keepdims=True))
        p_past = jnp.exp(s_past - m)
        p_new = jnp.exp(st - m)
        inv = 1.0 / (jnp.sum(p_past, axis=0, keepdims=True) + jnp.sum(p_new, axis=0, keepdims=True))
        s_ref[0:past, :] = p_past * inv
        s_ref[past:past + page, :] = p_new * inv


def _fox_sample(page_table, q, k_new, v_new, cl, g_out, cache_k, cache_v, cache_c, page, pool_base, n_heads):
    bsz, t_new, wa = q.shape
    n_pages = page_table.shape[1]
    assert page == LANES and n_heads == SUBLANES and n_heads * t_new <= LANES
    page_bytes = page * wa * 4
    npg = _pick_tile(n_pages, max(1, PAGE_BUFFER_BYTES // (4 * page_bytes)))
    n_steps = n_pages // npg
    scale = HEAD_DIM ** -0.5

    def k_seq(b):
        return jnp.minimum(b, bsz - 1)

    def v_seq(b):
        return jnp.maximum(b - 1, 0)

    def kv_map(seq_fn, r):
        return lambda b, s, pt: (pool_base + pt[seq_fn(b), s * npg + r], 0)

    def c_map(r):
        return lambda b, s, pt: (pt[k_seq(b), s * npg + r], 0, 0)

    k3 = lambda b, s, pt: (k_seq(b), 0, 0)
    v3 = lambda b, s, pt: (v_seq(b), 0, 0)
    in_specs = [
        pl.BlockSpec((None, t_new, wa), k3),
        pl.BlockSpec((None, t_new, wa), k3),
        pl.BlockSpec((None, t_new, wa), v3),
        pl.BlockSpec((None, t_new, LANES), k3),
        pl.BlockSpec((n_heads, HEAD_DIM), lambda b, s, pt: (0, 0)),
    ]
    in_specs += [pl.BlockSpec((page * n_heads, HEAD_DIM), kv_map(k_seq, r)) for r in range(npg)]
    in_specs += [pl.BlockSpec((None, n_heads, page), c_map(r)) for r in range(npg)]
    in_specs += [pl.BlockSpec((page * n_heads, HEAD_DIM), kv_map(v_seq, r)) for r in range(npg)]
    grid_spec = pltpu.PrefetchScalarGridSpec(
        num_scalar_prefetch=1,
        grid=(bsz + 1, n_steps),
        in_specs=in_specs,
        out_specs=pl.BlockSpec((None, t_new, wa), v3),
        scratch_shapes=[
            pltpu.VMEM((LANES, wa), F32),
            pltpu.VMEM((LANES, wa), BF16),
            pltpu.VMEM((page, wa), F32),
            pltpu.VMEM((page, LANES), F32),
            pltpu.VMEM((2, n_pages * page + page, LANES), F32),
            pltpu.VMEM((1, LANES), F32),
            pltpu.VMEM((LANES, wa), F32),
        ],
    )
    return pl.pallas_call(
        functools.partial(_fox_sample_body, npg=npg, n_pages=n_pages, page=page, n_heads=n_heads,
                          t_new=t_new, scale=scale),
        out_shape=jax.ShapeDtypeStruct((bsz, t_new, wa), F32),
        grid_spec=grid_spec,
        compiler_params=_cparams(("arbitrary", "arbitrary")),
        name="fox_sample",
    )(page_table, q, k_new, v_new, cl, g_out, *([cache_k] * npg), *([cache_c] * npg), *([cache_v] * npg))


def _bdot(a, b):
    return lax.dot_general(a, b, (((2,), (1,)), ((0,), (0,))), preferred_element_type=F32)


def _bdot_nt(a, b):
    return lax.dot_general(a, b, (((2,), (2,)), ((0,), (0,))), preferred_element_type=F32)


def _mlstm_chunk(qc, kc, vc, i_col, b_col, c_st, n_st, m_st, n_valid):
    nh, ln, _ = qc.shape
    r = lax.broadcasted_iota(jnp.int32, (ln, ln), 0)
    c = lax.broadcasted_iota(jnp.int32, (ln, ln), 1)
    eye = (r == c)[None]
    tril = (r >= c)[None]

    def col_to_row(col):
        return jnp.sum(jnp.where(eye, col, 0.0), axis=1, keepdims=True)

    i_row = col_to_row(i_col)
    b_row = col_to_row(b_col)
    a_col = b_col + m_st
    dmat = jnp.where(tril, b_col - b_row + i_row, NEG_INF)
    mt = jnp.maximum(a_col, jnp.max(dmat, axis=2, keepdims=True))
    qb = qc.astype(BF16)
    kb = kc.astype(BF16)
    vb = vc.astype(BF16)
    w_intra = jnp.exp(dmat - mt) * _bdot_nt(qb, kb)
    w_inter = jnp.exp(a_col - mt)
    num = w_inter * _bdot(qb, c_st.astype(BF16)) + _bdot(w_intra.astype(BF16), vb)
    den = w_inter * jnp.sum(qc * n_st, axis=2, keepdims=True) + jnp.sum(w_intra, axis=2, keepdims=True)
    h = num / jnp.maximum(jnp.abs(den), jnp.exp(-mt))
    last = n_valid - 1
    m_new = mt[:, last:last + 1, :]
    b_last = b_col[:, last:last + 1, :]
    w_state = jnp.exp(b_last - b_col + i_col - m_new)
    if n_valid < ln:
        row = lax.broadcasted_iota(jnp.int32, (1, ln, 1), 1)
        w_state = jnp.where(row < n_valid, w_state, 0.0)
    decay = jnp.exp(b_last + m_st - m_new)
    ks = w_state * kc
    ks_t = jnp.stack([ks[hd].T for hd in range(nh)]).astype(BF16)
    c_new = decay * c_st + _bdot(ks_t, vb)
    n_new = decay * n_st + jnp.sum(ks, axis=1, keepdims=True)
    return h, c_new, n_new, m_new


def _heads(x, n_heads, offset=0, width=HEAD_DIM):
    return jnp.stack([x[:, offset + h * width:offset + (h + 1) * width] for h in range(n_heads)])


def _mlstm_prompt_body(qk_ref, w_ref, b_ref, v_ref, o_ref, act_ref, cl_ref, g_ref,
                       y_ref, c_out_ref, n_out_ref, m_out_ref,
                       ext_ref, qk_scr, c_ref, n_ref, m_ref, *, rb, chunk, n_heads):
    blk = pl.program_id(1)
    pad = SUBLANES
    wm = n_heads * HEAD_DIM
    kscale = HEAD_DIM ** -0.5

    @pl.when(blk == 0)
    def _():
        ext_ref[0:pad, :] = jnp.zeros((pad, 2 * wm), F32)
        c_ref[...] = jnp.zeros_like(c_ref)
        n_ref[...] = jnp.zeros_like(n_ref)
        m_ref[...] = jnp.zeros_like(m_ref)

    ext_ref[pad:pad + rb, :] = qk_ref[...]
    acc = b_ref[...] + w_ref[CONV_W - 1:CONV_W, :] * qk_ref[...]
    for j in range(CONV_W - 1):
        off = pad - (CONV_W - 1) + j
        acc = acc + w_ref[j:j + 1, :] * ext_ref[off:off + rb, :]
    qk_scr[...] = acc * jax.nn.sigmoid(acc)
    ext_ref[0:pad, :] = qk_ref[rb - pad:rb, :]

    g = g_ref[...][:, None, :]
    for ci in range(rb // chunk):
        rows = slice(ci * chunk, (ci + 1) * chunk)
        qk = qk_scr[rows, :]
        hh, c_new, n_new, m_new = _mlstm_chunk(
            _heads(qk, n_heads), _heads(qk, n_heads, wm) * kscale, _heads(v_ref[rows, :], n_heads),
            _heads(act_ref[rows, :], n_heads, n_heads, 1), _heads(cl_ref[rows, :], n_heads, 2 * n_heads, 1),
            c_ref[...], n_ref[...], m_ref[:, :, 0:1], chunk)
        c_ref[...] = c_new
        n_ref[...] = n_new
        m_ref[...] = jnp.broadcast_to(m_new, m_ref.shape)
        y = jax.nn.sigmoid(_heads(o_ref[rows, :], n_heads)) * _rms(hh, g)
        for h in range(n_heads):
            y_ref[rows, h * HEAD_DIM:(h + 1) * HEAD_DIM] = y[h].astype(y_ref.dtype)

    @pl.when(blk == pl.num_programs(1) - 1)
    def _():
        c_out_ref[...] = c_ref[...]
        n_out_ref[...] = n_ref[:, 0, :]
        m_out_ref[...] = m_ref[:, 0, :]


def _mlstm_prompt(qkm, conv_w, conv_b, vm, om, act, cl, g_out, batch, seq, n_heads):
    n = qkm.shape[0]
    wm = n_heads * HEAD_DIM
    chunk = math.gcd(seq, MLSTM_CHUNK)
    rb = _pick_tile(seq, 4 * chunk)
    assert rb % chunk == 0 and rb >= SUBLANES
    nblk = seq // rb
    row = lambda b, i: (b * nblk + i, 0)
    const = lambda b, i: (0, 0)
    st = lambda b, i: (b, 0, 0)
    out_shape = (
        jax.ShapeDtypeStruct((n, wm), BF16),
        jax.ShapeDtypeStruct((batch, n_heads, HEAD_DIM, HEAD_DIM), F32),
        jax.ShapeDtypeStruct((batch, n_heads, HEAD_DIM), F32),
        jax.ShapeDtypeStruct((batch, n_heads, LANES), F32),
    )
    return pl.pallas_call(
        functools.partial(_mlstm_prompt_body, rb=rb, chunk=chunk, n_heads=n_heads),
        out_shape=out_shape,
        grid=(batch, nblk),
        in_specs=[
            pl.BlockSpec((rb, 2 * wm), row),
            pl.BlockSpec((CONV_W, 2 * wm), const),
            pl.BlockSpec((1, 2 * wm), const),
            pl.BlockSpec((rb, wm), row),
            pl.BlockSpec((rb, wm), row),
            pl.BlockSpec((rb, LANES), row),
            pl.BlockSpec((rb, LANES), row),
            pl.BlockSpec((n_heads, HEAD_DIM), const),
        ],
        out_specs=(
            pl.BlockSpec((rb, wm), row),
            pl.BlockSpec((None, n_heads, HEAD_DIM, HEAD_DIM), lambda b, i: (b, 0, 0, 0)),
            pl.BlockSpec((None, n_heads, HEAD_DIM), st),
            pl.BlockSpec((None, n_heads, LANES), st),
        ),
        scratch_shapes=[
            pltpu.VMEM((rb + SUBLANES, 2 * wm), F32),
            pltpu.VMEM((rb, 2 * wm), F32),
            pltpu.VMEM((n_heads, HEAD_DIM, HEAD_DIM), F32),
            pltpu.VMEM((n_heads, 1, HEAD_DIM), F32),
            pltpu.VMEM((n_heads, 1, LANES), F32),
        ],
        compiler_params=_cparams(("parallel", "arbitrary")),
        name="mlstm_prompt",
    )(qkm, conv_w, conv_b, vm, om, act, cl, g_out)


def _mlstm_sample_body(qk_ref, prev_ref, w_ref, b_ref, v_ref, o_ref, act_ref, cl_ref, g_ref, c0_ref, n0_ref, m0_ref,
                       y_ref, c_out_ref, n_out_ref, m_out_ref, ext_ref, pad_ref, *, t_new, n_heads, sb):
    wm = n_heads * HEAD_DIM
    rows = SUBLANES
    kscale = HEAD_DIM ** -0.5
    valid = lax.broadcasted_iota(jnp.int32, (rows, 1), 0) < t_new

    def padded(val, width):
        pad_ref[:, 0:width] = jnp.zeros((rows, width), F32)
        pad_ref[0:t_new, 0:width] = val
        return pad_ref[:, 0:width]

    qs, ks, vs, os_, i_cols, b_cols, n0s, m0s = [], [], [], [], [], [], [], []
    for s in range(sb):
        tok = slice(s * t_new, (s + 1) * t_new)
        ext_ref[...] = jnp.zeros_like(ext_ref)
        ext_ref[0:CONV_W - 1, :] = prev_ref[s]
        ext_ref[CONV_W - 1:CONV_W - 1 + t_new, :] = qk_ref[tok, :]
        acc = b_ref[...] + w_ref[0:1, :] * ext_ref[0:rows, :]
        for j in range(1, CONV_W):
            acc = acc + w_ref[j:j + 1, :] * ext_ref[j:j + rows, :]
        qk = jnp.where(valid, acc * jax.nn.sigmoid(acc), 0.0)
        qs.append(_heads(qk, n_heads))
        ks.append(_heads(qk, n_heads, wm) * kscale)
        vs.append(_heads(padded(v_ref[tok, :], wm), n_heads))
        os_.append(_heads(padded(o_ref[tok, :], wm), n_heads))
        i_cols.append(jnp.where(valid, _heads(padded(act_ref[tok, :], LANES), n_heads, n_heads, 1), NEG_INF))
        b_col = _heads(padded(cl_ref[tok, :], LANES), n_heads, 2 * n_heads, 1)
        b_cols.append(jnp.where(valid, b_col, b_col[:, t_new - 1:t_new, :]))
        n0s.append(jnp.stack([n0_ref[s, h:h + 1, :] for h in range(n_heads)]))
        m0s.append(jnp.stack([m0_ref[s, h:h + 1, :] for h in range(n_heads)]))

    cat = lambda parts: jnp.concatenate(parts, axis=0)
    c0 = cat([c0_ref[s] for s in range(sb)])
    hh, c_new, n_new, m_new = _mlstm_chunk(cat(qs), cat(ks), cat(vs), cat(i_cols), cat(b_cols),
                                           c0, cat(n0s), cat(m0s), t_new)
    g = jnp.concatenate([g_ref[...]] * sb, axis=0)[:, None, :]
    y = jax.nn.sigmoid(cat(os_)) * _rms(hh, g)
    m_wide = jnp.broadcast_to(m_new, (sb * n_heads, 1, LANES))
    for s in range(sb):
        for h in range(n_heads):
            y_ref[s * t_new:(s + 1) * t_new, h * HEAD_DIM:(h + 1) * HEAD_DIM] = y[s * n_heads + h, 0:t_new, :]
        grp = slice(s * n_heads, (s + 1) * n_heads)
        c_out_ref[s] = c_new[grp]
        n_out_ref[s] = n_new[grp, 0, :]
        m_out_ref[s] = m_wide[grp, 0, :]


def _mlstm_sample(qkm, conv_prev, conv_w, conv_b, vm, om, act, cl, g_out, c0, n0, m0, t_new, n_heads):
    n, wm2 = qkm.shape
    bsz = n // t_new
    wm = wm2 // 2
    assert t_new + CONV_W - 1 <= SUBLANES
    sb = _pick_tile(bsz, 4)
    assert (sb * t_new) % SUBLANES == 0 or sb == bsz
    tok = lambda b: (b, 0)
    seq3 = lambda b: (b, 0, 0)
    const = lambda b: (0, 0)
    out_shape = (
        jax.ShapeDtypeStruct((n, wm), F32),
        jax.ShapeDtypeStruct((bsz, n_heads, HEAD_DIM, HEAD_DIM), F32),
        jax.ShapeDtypeStruct((bsz, n_heads, HEAD_DIM), F32),
        jax.ShapeDtypeStruct((bsz, n_heads, LANES), F32),
    )
    return pl.pallas_call(
        functools.partial(_mlstm_sample_body, t_new=t_new, n_heads=n_heads, sb=sb),
        out_shape=out_shape,
        grid=(bsz // sb,),
        in_specs=[
            pl.BlockSpec((sb * t_new, wm2), tok),
            pl.BlockSpec((sb, CONV_W - 1, wm2), seq3),
            pl.BlockSpec((CONV_W, wm2), const),
            pl.BlockSpec((1, wm2), const),
            pl.BlockSpec((sb * t_new, wm), tok),
            pl.BlockSpec((sb * t_new, wm), tok),
            pl.BlockSpec((sb * t_new, LANES), tok),
            pl.BlockSpec((sb * t_new, LANES), tok),
            pl.BlockSpec((n_heads, HEAD_DIM), const),
            pl.BlockSpec((sb, n_heads, HEAD_DIM, HEAD_DIM), lambda b: (b, 0, 0, 0)),
            pl.BlockSpec((sb, n_heads, HEAD_DIM), seq3),
            pl.BlockSpec((sb, n_heads, 1), seq3),
        ],
        out_specs=(
            pl.BlockSpec((sb * t_new, wm), tok),
            pl.BlockSpec((sb, n_heads, HEAD_DIM, HEAD_DIM), lambda b: (b, 0, 0, 0)),
            pl.BlockSpec((sb, n_heads, HEAD_DIM), seq3),
            pl.BlockSpec((sb, n_heads, LANES), seq3),
        ),
        scratch_shapes=[pltpu.VMEM((2 * SUBLANES, wm2), F32), pltpu.VMEM((SUBLANES, wm), F32)],
        compiler_params=_cparams(("parallel",)),
        name="mlstm_sample",
    )(qkm, conv_prev, conv_w, conv_b, vm, om, act, cl, g_out, c0, n0, m0)


def _outproj_body(ya_ref, ym_ref, w_ref, x_ref, o_ref, *, wa):
    y = _dot(ya_ref[...].astype(BF16), w_ref[0:wa, :]) + _dot(ym_ref[...].astype(BF16), w_ref[wa:, :])
    o_ref[...] = x_ref[...] + y


def _outproj(ya, ym, w, x):
    n, d = x.shape
    wa = ya.shape[1]
    wm = ym.shape[1]
    tm = _pick_tile(n, 512)
    row = lambda i: (i, 0)
    return pl.pallas_call(
        functools.partial(_outproj_body, wa=wa),
        out_shape=jax.ShapeDtypeStruct((n, d), F32),
        grid=(n // tm,),
        in_specs=[
            pl.BlockSpec((tm, wa), row),
            pl.BlockSpec((tm, wm), row),
            pl.BlockSpec((wa + wm, d), lambda i: (0, 0)),
            pl.BlockSpec((tm, d), row),
        ],
        out_specs=pl.BlockSpec((tm, d), row),
        compiler_params=_cparams(("parallel",)),
        name="outproj",
    )(ya, ym, w, x)


def _cast_pad_body(x_ref, o_ref, *, axis, size):
    if axis == 1:
        o_ref[:, 0:size] = x_ref[...].astype(BF16)
        o_ref[:, size:] = jnp.zeros((o_ref.shape[0], o_ref.shape[1] - size), BF16)
    else:
        o_ref[0:size, :] = x_ref[...].astype(BF16)
        o_ref[size:, :] = jnp.zeros((o_ref.shape[0] - size, o_ref.shape[1]), BF16)


def _cast_pad(w, axis, padded):
    r, c = w.shape
    size = w.shape[axis]
    assert padded > size
    if axis == 1:
        t = _pick_tile(r, 256)
        grid, in_blk, out_blk, imap = (r // t,), (t, c), (t, padded), (lambda i: (i, 0))
        out_shape = (r, padded)
    else:
        t = _pick_tile(c, 256)
        grid, in_blk, out_blk, imap = (c // t,), (r, t), (padded, t), (lambda i: (0, i))
        out_shape = (padded, c)
    return pl.pallas_call(
        functools.partial(_cast_pad_body, axis=axis, size=size),
        out_shape=jax.ShapeDtypeStruct(out_shape, BF16),
        grid=grid,
        in_specs=[pl.BlockSpec(in_blk, imap)],
        out_specs=pl.BlockSpec(out_blk, imap),
        compiler_params=_cparams(("parallel",)),
        name="cast_pad",
    )(w)


def _win_prep_body(wt_ref, fa_ref, im_ref, main_ref, gates_ref, *, n_gate_cols):
    main_ref[...] = wt_ref[...].T.astype(BF16)

    @pl.when(pl.program_id(0) == 0)
    def _():
        d = gates_ref.shape[0]
        g = jnp.concatenate([fa_ref[...], im_ref[...], jnp.zeros((LANES - n_gate_cols, d), F32)], axis=0)
        gates_ref[...] = g.T.astype(BF16)


def _win_prep(w_in_t, cuts, n_gate_cols, tile):
    n_in, d = w_in_t.shape
    o_f, o_qk, o_i = cuts
    n_main = o_f + (o_i - o_qk)
    assert o_f % tile == 0 and (o_i - o_qk) % tile == 0 and o_qk % SUBLANES == 0 and o_i % SUBLANES == 0
    n_lead = o_f // tile

    def row_off(j):
        return pl.multiple_of(jnp.where(j < n_lead, j * tile, o_qk + (j - n_lead) * tile), SUBLANES)

    el = pl.Element
    return pl.pallas_call(
        functools.partial(_win_prep_body, n_gate_cols=n_gate_cols),
        out_shape=(jax.ShapeDtypeStruct((d, n_main), BF16), jax.ShapeDtypeStruct((d, LANES), BF16)),
        grid=(n_main // tile,),
        in_specs=[
            pl.BlockSpec((el(tile), el(d)), lambda j: (row_off(j), 0)),
            pl.BlockSpec((el(o_qk - o_f), el(d)), lambda j: (o_f, 0)),
            pl.BlockSpec((el(n_in - o_i), el(d)), lambda j: (o_i, 0)),
        ],
        out_specs=(pl.BlockSpec((d, tile), lambda j: (0, j)), pl.BlockSpec((d, LANES), lambda j: (0, 0))),
        compiler_params=_cparams(("arbitrary",)),
        name="win_prep",
    )(w_in_t, w_in_t, w_in_t)


def _prep_weights(lw, n_heads_a, n_heads_m):
    wa = n_heads_a * HEAD_DIM
    wm = n_heads_m * HEAD_DIM
    f = lw['w1_gate'].shape[1]
    tf = 512
    fp = tf * ((f + tf - 1) // tf)

    def ffn_w(wg, wu, wd):
        if fp == f:
            return wg.astype(BF16), wu.astype(BF16), wd.astype(BF16)
        return _cast_pad(wg, 1, fp), _cast_pad(wu, 1, fp), _cast_pad(wd, 0, fp)

    w_in = lw['w_in']
    o_f = 3 * wa
    o_qk = o_f + n_heads_a
    o_i = o_qk + 2 * wm + 2 * wm
    o_fm = o_i + n_heads_m
    ng = 2 * n_heads_m + n_heads_a
    assert o_fm + n_heads_m == w_in.shape[1]
    w_main, w_gates = _win_prep(jnp.swapaxes(w_in, 0, 1), (o_f, o_qk, o_i), ng, wa)
    bias = jnp.concatenate([lw['b_fox_f'], lw['b_m_i'], lw['b_m_f'], jnp.zeros((LANES - ng,), F32)])[None, :]
    return {
        'ffn1': ffn_w(lw['w1_gate'], lw['w1_up'], lw['w1_down']),
        'ffn_w': ffn_w,
        'tf': tf,
        'fp': fp,
        'w_main': w_main,
        'w_gates': w_gates,
        'gate_bias': bias,
        'w_out': lw['w_out'].astype(BF16),
    }


def kernel(x_prompt, x_sample, cache_k, cache_v, cache_logf, state_conv, state_C, state_n, state_m, page_table,
           g_ffn1, w1_gate, w1_up, w1_down, g_mix, w_in, b_fox_f, b_m_i, b_m_f, conv_w, conv_b, g_q, g_k,
           g_out_a, g_out_m, w_out, g_ffn2, w2_gate, w2_up, w2_down):
    depth = w_in.shape[0]
    bp, seq, d = x_prompt.shape
    bs, t_new, _ = x_sample.shape
    n_heads_a = g_out_a.shape[1]
    n_heads_m = g_out_m.shape[1]
    wa = n_heads_a * HEAD_DIM
    wm = n_heads_m * HEAD_DIM
    n_pool, page = cache_k.shape[1], cache_k.shape[2]

    yp = x_prompt.reshape(bp * seq, d)
    ys = x_sample.reshape(bs * t_new, d)
    outs = [[] for _ in range(14)]
    for l in range(depth):
        lw = {
            'w1_gate': w1_gate[l], 'w1_up': w1_up[l], 'w1_down': w1_down[l], 'w_in': w_in[l],
            'b_fox_f': b_fox_f[l], 'b_m_i': b_m_i[l], 'b_m_f': b_m_f[l], 'w_out': w_out[l],
            'w2_gate': w2_gate[l], 'w2_up': w2_up[l], 'w2_down': w2_down[l],
        }
        pw = _prep_weights(lw, n_heads_a, n_heads_m)
        gf1 = g_ffn1[l][None, :]
        gf2 = g_ffn2[l][None, :]
        gmix = g_mix[l][None, :]
        gq = g_q[l][None, :]
        gk = g_k[l][None, :]
        cw = conv_w[l]
        cb = conv_b[l][None, :]
        goa = g_out_a[l]
        gom = g_out_m[l]

        w2 = (lw['w2_gate'], lw['w2_up'], lw['w2_down'])
        ffn1_steps = (yp.shape[0] // _pick_tile(yp.shape[0], 512)) * (pw['fp'] // pw['tf'])
        if w2[0].shape[1] % LANES == 0 and ffn1_steps >= 3 * (pw['fp'] // LANES):
            (x1, xn), ffn2_w = _ffn(yp, gf1, *pw['ffn1'], pw['tf'], g_next=gmix, prep=w2)
        else:
            ffn2_w = pw['ffn_w'](*w2)
            x1, xn = _ffn(yp, gf1, *pw['ffn1'], pw['tf'], g_next=gmix)
        q, k, kb, v, vt, qkm, vm, om, gt = _inproj(xn, pw['w_main'], pw['w_gates'], gq, gk, n_heads_a)
        act, cg, cl = _gates(gt, pw['gate_bias'], seq, math.gcd(seq, MLSTM_CHUNK), n_heads_a)
        ya = _fox_prompt(q, kb, vt, cg, goa, bp, seq, n_heads_a)
        ym, c_p, n_p, m_p = _mlstm_prompt(qkm, cw, cb, vm, om, act, cl, gom, bp, seq, n_heads_m)
        x2 = _outproj(ya, ym, pw['w_out'], x1)
        yp = _ffn(x2, gf2, *ffn2_w, pw['tf'])
        outs[0].append(k.reshape(bp, seq, n_heads_a, HEAD_DIM))
        outs[1].append(v.reshape(bp, seq, n_heads_a, HEAD_DIM))
        outs[2].append(act[:, :n_heads_a].reshape(bp, seq, n_heads_a))
        outs[3].append(qkm.reshape(bp, seq, 2 * wm)[:, seq - (CONV_W - 1):, :])
        outs[4].append(c_p)
        outs[5].append(n_p)
        outs[6].append(m_p[:, :, 0])

        x1, xn = _ffn(ys, gf1, *pw['ffn1'], pw['tf'], g_next=gmix)
        q, k, kb, v, vt, qkm, vm, om, gt = _inproj(xn, pw['w_main'], pw['w_gates'], gq, gk, n_heads_a)
        act, cg, cl = _gates(gt, pw['gate_bias'], t_new, math.gcd(t_new, MLSTM_CHUNK), n_heads_a)
        r3 = lambda a: a.reshape(bs, t_new, a.shape[-1])
        lf_rows = jnp.swapaxes(cache_logf[l], 1, 2).reshape(n_pool * n_heads_a, page)
        cache_c = _pool_cumsum(lf_rows).reshape(n_pool, n_heads_a, page)
        ya = _fox_sample(page_table, r3(q).astype(F32), r3(k), r3(v), r3(cg), goa,
                         cache_k.reshape(-1, HEAD_DIM), cache_v.reshape(-1, HEAD_DIM),
                         cache_c, page, l * n_pool, n_heads_a)
        ym, c_s, n_s, m_s = _mlstm_sample(qkm, state_conv[l], cw, cb, vm, om, act, cl, gom,
                                          state_C[l], state_n[l], state_m[l][:, :, None], t_new, n_heads_m)
        x2 = _outproj(ya.reshape(bs * t_new, wa), ym, pw['w_out'], x1)
        ys = _ffn(x2, gf2, *ffn2_w, pw['tf'])
        conv_ext = jnp.concatenate([state_conv[l], r3(qkm)], axis=1)
        outs[7].append(k.reshape(bs, t_new, n_heads_a, HEAD_DIM))
        outs[8].append(v.reshape(bs, t_new, n_heads_a, HEAD_DIM))
        outs[9].append(act[:, :n_heads_a].reshape(bs, t_new, n_heads_a))
        outs[10].append(conv_ext[:, t_new:, :])
        outs[11].append(c_s)
        outs[12].append(n_s)
        outs[13].append(m_s[:, :, 0])

    return (yp.reshape(bp, seq, d), ys.reshape(bs, t_new, d)) + tuple(jnp.stack(o) for o in outs)
```
